```python
import jax, jax.numpy as jnp
from jax import lax
import numpy as np

D_MODEL = 1024
BATCH = 4
SEQ = 4096
DEPTH = 1

MIX_WIDTH = D_MODEL
HGRN_WIDTH = MIX_WIDTH // 2
HGRN_HEADS = 4
HGRN_HEAD_DIM = HGRN_WIDTH // HGRN_HEADS
HGRN_CHUNK = 64
GMLP_WIDTH = MIX_WIDTH - HGRN_WIDTH
GMLP_GROUPS = 4
GMLP_GROUP_DIM = GMLP_WIDTH // GMLP_GROUPS
GMLP_CHUNK = 128
IN_PROJ_WIDTH = 4 * HGRN_WIDTH + 2 * GMLP_WIDTH
N_EXPERTS = 32
TOP_K = 4
D_FF = D_MODEL
SWIGLU_LIMIT = 7.0
SWIGLU_ALPHA = 1.702
MOE_BLOCK = 256
EPS = 1e-6

kernel_name = "hybrid_hgrn2_gmlp_moe_adaln"


def rms_norm(x, g):
    xf = x.astype(jnp.float32)
    y = xf * lax.rsqrt(jnp.mean(xf * xf, axis=-1, keepdims=True) + EPS)
    return (y * g.astype(jnp.float32)).astype(x.dtype)


def layer_norm(x, g, b):
    xf = x.astype(jnp.float32)
    mu = jnp.mean(xf, axis=-1, keepdims=True)
    var = jnp.mean(jnp.square(xf - mu), axis=-1, keepdims=True)
    y = (xf - mu) * lax.rsqrt(var + EPS)
    return (y * g.astype(jnp.float32) + b.astype(jnp.float32)).astype(x.dtype)


def hgrn2_chunk_scan(q, k, v, logf):
    B, S, H, DK = q.shape
    DV = v.shape[-1]
    nc = S // HGRN_CHUNK

    def to_chunks(t):
        return t.reshape(B, nc, HGRN_CHUNK, H, t.shape[-1]).transpose(1, 0, 3, 2, 4)

    causal = jnp.tril(jnp.ones((HGRN_CHUNK, HGRN_CHUNK), dtype=bool))[:, :, None]

    def step(state, inp):
        qc, kc, vc, gc = inp
        b = jnp.cumsum(gc, axis=2)
        o_inter = jnp.einsum('bhtk,bhkv->bhtv', qc * jnp.exp(b), state)
        diff = b[:, :, :, None, :] - b[:, :, None, :, :]
        decay = jnp.exp(jnp.where(causal, diff, -jnp.inf))
        scores = jnp.einsum('bhtk,bhsk,bhtsk->bhts', qc, kc, decay)
        o_intra = jnp.einsum('bhts,bhsv->bhtv', scores, vc)
        b_last = b[:, :, -1:, :]
        k_dec = kc * jnp.exp(b_last - b)
        new_state = jnp.exp(b_last[:, :, 0, :])[..., None] * state + jnp.einsum('bhsk,bhsv->bhkv', k_dec, vc)
        return new_state, o_inter + o_intra

    state0 = jnp.zeros((B, H, DK, DV), jnp.float32)
    _, o = lax.scan(step, state0, (to_chunks(q), to_chunks(k), to_chunks(v), to_chunks(logf)))
    return o.transpose(1, 0, 3, 2, 4).reshape(B, S, H, DV)


def hgrn2_mixer(zq, zf, zi, zg, lb, norm_g):
    B, S, _ = zq.shape
    shp = (B, S, HGRN_HEADS, HGRN_HEAD_DIM)
    lb = lb.astype(jnp.float32)
    f = lb + (1.0 - lb) * jax.nn.sigmoid(zf.astype(jnp.float32))
    logf = jnp.log(f).reshape(shp)
    k = (1.0 - f).reshape(shp)
    q = zq.astype(jnp.float32).reshape(shp)
    v = zi.astype(jnp.float32).reshape(shp)
    o = hgrn2_chunk_scan(q, k, v, logf).astype(zq.dtype)
    o = rms_norm(o, norm_g.reshape(HGRN_HEADS, HGRN_HEAD_DIM))
    o = o * jax.nn.silu(zg).reshape(shp)
    return o.reshape(B, S, HGRN_WIDTH)


def gmlp_mixer(zu, zv, ln_g, ln_b, ws, bs, norm_g):
    B, S, _ = zu.shape
    nc = S // GMLP_CHUNK
    shp = (B, nc, GMLP_CHUNK, GMLP_GROUPS, GMLP_GROUP_DIM)
    u = jax.nn.gelu(zu, approximate=False)
    v = layer_norm(jax.nn.gelu(zv, approximate=False), ln_g, ln_b).reshape(shp)
    ws_causal = ws * jnp.tril(jnp.ones((GMLP_CHUNK, GMLP_CHUNK), ws.dtype))
    sv = jnp.einsum('gts,bnsgd->bntgd', ws_causal, v) + bs.T[None, None, :, :, None]
    y = u.reshape(shp) * sv
    y = rms_norm(y, norm_g.reshape(GMLP_GROUPS, GMLP_GROUP_DIM))
    return y.reshape(B, S, GMLP_WIDTH)


def moe_ffn(h, router_w, router_b, w_gate_up, b_gate_up, w_down, b_down):
    N, D = h.shape
    logits = h.astype(jnp.float32) @ router_w.astype(jnp.float32) + router_b.astype(jnp.float32)
    top_val, top_idx = lax.top_k(logits, TOP_K)
    gates = jax.nn.softmax(top_val, axis=-1)

    NK = N * TOP_K
    flat_e = top_idx.reshape(-1)
    flat_g = gates.reshape(-1)
    order = jnp.argsort(flat_e)
    sorted_e = flat_e[order]
    tok = (order // TOP_K).astype(jnp.int32)
    counts = jnp.bincount(flat_e, length=N_EXPERTS)
    padded = ((counts + MOE_BLOCK - 1) // MOE_BLOCK) * MOE_BLOCK
    start_sorted = jnp.cumsum(counts) - counts
    pad_end = jnp.cumsum(padded)
    start_pad = pad_end - padded
    dest = start_pad[sorted_e] + (jnp.arange(NK) - start_sorted[sorted_e])

    P = ((NK + N_EXPERTS * (MOE_BLOCK - 1) + MOE_BLOCK - 1) // MOE_BLOCK) * MOE_BLOCK
    n_blocks = P // MOE_BLOCK
    row_tok = jnp.full((P,), N, jnp.int32).at[dest].set(tok)
    row_gate = jnp.zeros((P,), jnp.float32).at[dest].set(flat_g[order])
    block_start = jnp.arange(n_blocks) * MOE_BLOCK
    block_expert = jnp.minimum(jnp.searchsorted(pad_end, block_start, side='right'), N_EXPERTS - 1)

    h_pad = jnp.concatenate([h, jnp.zeros((1, D), h.dtype)], axis=0)
    xb = h_pad[row_tok].reshape(n_blocks, MOE_BLOCK, D)

    def expert_block(args):
        xblk, e = args
        gu = xblk @ w_gate_up[e] + b_gate_up[e]
        gate, up = gu[:, :D_FF], gu[:, D_FF:]
        gate = jnp.minimum(gate, SWIGLU_LIMIT)
        up = jnp.clip(up, -SWIGLU_LIMIT, SWIGLU_LIMIT)
        glu = gate * jax.nn.sigmoid(SWIGLU_ALPHA * gate)
        return ((up + 1.0) * glu) @ w_down[e] + b_down[e]

    yb = lax.map(expert_block, (xb, block_expert)).reshape(P, D)
    yb = yb * row_gate.astype(yb.dtype)[:, None]
    out = jnp.zeros((N + 1, D), yb.dtype).at[row_tok].add(yb)
    return out[:N]


def setup_inputs(seed: int = 0) -> dict:
    key = jax.random.key(seed)
    ks = jax.random.split(key, 24)
    f32 = jnp.float32
    L = DEPTH
    nrm = lambda k, shape, s: (jax.random.normal(k, shape, f32) * s)
    return {
        "x": nrm(ks[0], (BATCH, SEQ, D_MODEL), 1.0),
        "c": nrm(ks[1], (BATCH, D_MODEL), 1.0),
        "ada_w": nrm(ks[2], (L, D_MODEL, 6 * D_MODEL), 0.5 * D_MODEL ** -0.5),
        "ada_b": nrm(ks[3], (L, 6 * D_MODEL), 0.02),
        "norm_mix_g": 1.0 + nrm(ks[4], (L, D_MODEL), 0.05),
        "w_in": nrm(ks[5], (L, D_MODEL, IN_PROJ_WIDTH), D_MODEL ** -0.5),
        "lb_params": nrm(ks[6], (L + 1, HGRN_WIDTH), 1.0),
        "hgrn_norm_g": 1.0 + nrm(ks[7], (L, HGRN_WIDTH), 0.05),
        "gmlp_ln_g": 1.0 + nrm(ks[8], (L, GMLP_WIDTH), 0.05),
        "gmlp_ln_b": nrm(ks[9], (L, GMLP_WIDTH), 0.02),
        "gmlp_ws": nrm(ks[10], (L, GMLP_GROUPS, GMLP_CHUNK, GMLP_CHUNK), GMLP_CHUNK ** -0.5),
        "gmlp_bs": 1.0 + nrm(ks[11], (L, GMLP_GROUPS, GMLP_CHUNK), 0.1),
        "gmlp_norm_g": 1.0 + nrm(ks[12], (L, GMLP_WIDTH), 0.05),
        "w_out": nrm(ks[13], (L, MIX_WIDTH, D_MODEL), MIX_WIDTH ** -0.5),
        "norm_ffn_g": 1.0 + nrm(ks[14], (L, D_MODEL), 0.05),
        "router_w": nrm(ks[15], (L, D_MODEL, N_EXPERTS), D_MODEL ** -0.5),
        "router_b": nrm(ks[16], (L, N_EXPERTS), 0.01),
        "w_gate_up": nrm(ks[17], (L, N_EXPERTS, D_MODEL, 2 * D_FF), D_MODEL ** -0.5),
        "b_gate_up": nrm(ks[18], (L, N_EXPERTS, 2 * D_FF), 0.02),
        "w_down": nrm(ks[19], (L, N_EXPERTS, D_FF, D_MODEL), D_FF ** -0.5),
        "b_down": nrm(ks[20], (L, N_EXPERTS, D_MODEL), 0.02),
        "final_g": 1.0 + nrm(ks[21], (D_MODEL,), 0.05),
    }


def reference(x, c, ada_w, ada_b, norm_mix_g, w_in, lb_params, hgrn_norm_g, gmlp_ln_g, gmlp_ln_b,
              gmlp_ws, gmlp_bs, gmlp_norm_g, w_out, norm_ffn_g, router_w, router_b, w_gate_up,
              b_gate_up, w_down, b_down, final_g):
    B, S, D = x.shape
    lb_all = jnp.cumsum(jax.nn.softmax(lb_params.astype(jnp.float32), axis=0), axis=0)
    c_act = jax.nn.silu(c)
    splits = [HGRN_WIDTH, 2 * HGRN_WIDTH, 3 * HGRN_WIDTH, 4 * HGRN_WIDTH, 4 * HGRN_WIDTH + GMLP_WIDTH]
    for l in range(DEPTH):
        mod = (c_act @ ada_w[l] + ada_b[l])[:, None, :]
        shift1, scale1, gate1, shift2, scale2, gate2 = jnp.split(mod, 6, axis=-1)

        h = rms_norm(x, norm_mix_g[l]) * (1.0 + scale1) + shift1
        z = h @ w_in[l]
        zq, zf, zi, zg, zu, zv = jnp.split(z, splits, axis=-1)
        y_hgrn = hgrn2_mixer(zq, zf, zi, zg, lb_all[l], hgrn_norm_g[l])
        y_gmlp = gmlp_mixer(zu, zv, gmlp_ln_g[l], gmlp_ln_b[l], gmlp_ws[l], gmlp_bs[l], gmlp_norm_g[l])
        y = jnp.concatenate([y_hgrn, y_gmlp], axis=-1) @ w_out[l]
        x = x + gate1 * y

        h = rms_norm(x, norm_ffn_g[l]) * (1.0 + scale2) + shift2
        y = moe_ffn(h.reshape(B * S, D), router_w[l], router_b[l], w_gate_up[l], b_gate_up[l],
                    w_down[l], b_down[l]).reshape(B, S, D)
        x = x + gate2 * y
    return rms_norm(x, final_g)
```

```python
import functools

import jax
import jax.numpy as jnp
from jax import lax
from jax.experimental import pallas as pl
from jax.experimental.pallas import tpu as pltpu

F32 = jnp.float32
BF16 = jnp.bfloat16

HGRN_HEADS = 4
HEAD_DIM = 128
HGRN_WIDTH = HGRN_HEADS * HEAD_DIM
HGRN_CHUNK = 64
HGRN_SUB = 16
GMLP_GROUPS = 4
GROUP_DIM = 128
GMLP_WIDTH = GMLP_GROUPS * GROUP_DIM
GMLP_CHUNK = 128
N_EXPERTS = 32
TOP_K = 4
SWIGLU_LIMIT = 7.0
SWIGLU_ALPHA = 1.702
EPS = 1e-6
LANES = 128
DECAY_EXP_CLAMP = 60.0
TM = 256
MOE_BLK = 256
TC = 256
VMEM_LIMIT = 56 * 1024 * 1024


def _dot(a, b):
    return jnp.dot(a, b, preferred_element_type=F32)


def _dot_nt(a, b):
    return lax.dot_general(a, b, (((1,), (1,)), ((), ())), preferred_element_type=F32)


def _dot_tn(a, b):
    return lax.dot_general(a, b, (((0,), (0,)), ((), ())), preferred_element_type=F32)


def _rms(x):
    return x * lax.rsqrt(jnp.mean(x * x, axis=-1, keepdims=True) + EPS)


def _gelu(x):
    return 0.5 * x * (1.0 + lax.erf(x * 0.7071067811865476))


def _mod_kernel(c_ref, w_ref, b_ref, o_ref):
    c = c_ref[...]
    ca = c * jax.nn.sigmoid(c)
    o_ref[...] = jnp.dot(ca, w_ref[...], precision=lax.Precision.HIGHEST,
                         preferred_element_type=F32) + b_ref[...]


def _modulation(c, ada_w, ada_b):
    bsz, d = c.shape
    n_out = ada_w.shape[1]
    rows = 8
    c_pad = jnp.zeros((rows, d), F32).at[:bsz].set(c)
    tn = 1024
    out = pl.pallas_call(
        _mod_kernel,
        grid=(n_out // tn,),
        in_specs=[pl.BlockSpec((rows, d), lambda j: (0, 0)),
                  pl.BlockSpec((d, tn), lambda j: (0, j)),
                  pl.BlockSpec((1, tn), lambda j: (0, j))],
        out_specs=pl.BlockSpec((rows, tn), lambda j: (0, j)),
        out_shape=jax.ShapeDtypeStruct((rows, n_out), F32),
        name="adaln_mod",
    )(c_pad, ada_w, ada_b.reshape(1, n_out))
    return out[:bsz]


def _mixer_kernel(x_ref, mod_ref, g1_ref, win_ref, lbp_ref, hg_ref, lng_ref, lnb_ref, ws_ref, bs_ref,
                  gng_ref, wout_ref, g2_ref, rwh_ref, rwl_ref, rb_ref,
                  x1_ref, h2_ref, route_ref,
                  z_ref, y_ref, st_ref):
    @pl.when(pl.program_id(1) == 0)
    def _():
        st_ref[...] = jnp.zeros_like(st_ref)

    x = x_ref[...]
    mod = mod_ref[...]
    h = _rms(x) * g1_ref[...]
    h = h * (1.0 + mod[1:2]) + mod[0:1]
    z_ref[...] = _dot(h.astype(BF16), win_ref[...])

    lbp = lbp_ref[...]
    lbe = jnp.exp(lbp - jnp.max(lbp, axis=0, keepdims=True))
    lb = lbe[0:1] / jnp.sum(lbe, axis=0, keepdims=True)
    hg = hg_ref[...]
    row = lax.broadcasted_iota(jnp.int32, (HGRN_CHUNK, HGRN_CHUNK), 0)
    col = lax.broadcasted_iota(jnp.int32, (HGRN_CHUNK, HGRN_CHUNK), 1)
    causal = col <= row
    tri = jnp.where(causal, 1.0, 0.0).astype(BF16)
    n_sub = HGRN_CHUNK // HGRN_SUB

    def chunk_body(c, carry):
        rows = pl.ds(pl.multiple_of(c * HGRN_CHUNK, HGRN_CHUNK), HGRN_CHUNK)
        zq = z_ref[rows, 0:HGRN_WIDTH]
        zf = z_ref[rows, HGRN_WIDTH:2 * HGRN_WIDTH]
        zi = z_ref[rows, 2 * HGRN_WIDTH:3 * HGRN_WIDTH]
        zg = z_ref[rows, 3 * HGRN_WIDTH:4 * HGRN_WIDTH]
        f = lb + (1.0 - lb) * jax.nn.sigmoid(zf)
        logf = jnp.log(f)
        kk = 1.0 - f
        p0 = logf.astype(BF16)
        r0 = logf - p0.astype(F32)
        p1 = r0.astype(BF16)
        p2 = (r0 - p1.astype(F32)).astype(BF16)
        b = (_dot(tri, p0) + _dot(tri, p1)) + _dot(tri, p2)
        b_last = b[HGRN_CHUNK - 1:HGRN_CHUNK, :]
        qe = (zq * jnp.exp(b)).astype(BF16)
        kdec = (kk * jnp.exp(b_last - b)).astype(BF16)
        v = zi.astype(BF16)
        dec_last = jnp.exp(b_last)
        a_sub, k_sub = [], []
        for i in range(n_sub):
            lo, hi = i * HGRN_SUB, (i + 1) * HGRN_SUB
            bref = b[lo - 1:lo, :] if i > 0 else jnp.zeros((1, HGRN_WIDTH), F32)
            a_sub.append((zq[lo:hi] * jnp.exp(b[lo:hi] - bref)).astype(BF16))
            k_sub.append((kk * jnp.exp(jnp.minimum(bref - b, DECAY_EXP_CLAMP))).astype(BF16))
        silu_g = zg * jax.nn.sigmoid(zg)
        for hd in range(HGRN_HEADS):
            ls = slice(hd * HEAD_DIM, (hd + 1) * HEAD_DIM)
            sc = jnp.concatenate([_dot_nt(a_sub[i][:, ls], k_sub[i][:, ls]) for i in range(n_sub)], axis=0)
            sc = jnp.where(causal, sc, 0.0).astype(BF16)
            st = st_ref[hd]
            o = _dot_nt(qe[:, ls], st.astype(BF16)) + _dot(sc, v[:, ls])
            st_ref[hd] = st * dec_last[:, ls] + _dot_tn(v[:, ls], kdec[:, ls])
            o = _rms(o) * hg[:, ls]
            y_ref[rows, ls] = (o * silu_g[:, ls]).astype(BF16)
        return carry

    lax.fori_loop(0, TM // HGRN_CHUNK, chunk_body, 0)

    u = _gelu(z_ref[:, 4 * HGRN_WIDTH:4 * HGRN_WIDTH + GMLP_WIDTH])
    gv = _gelu(z_ref[:, 4 * HGRN_WIDTH + GMLP_WIDTH:])
    mu = jnp.mean(gv, axis=-1, keepdims=True)
    gc = gv - mu
    var = jnp.mean(gc * gc, axis=-1, keepdims=True)
    vn = (gc * lax.rsqrt(var + EPS) * lng_ref[...] + lnb_ref[...]).astype(BF16)
    row_g = lax.broadcasted_iota(jnp.int32, (GMLP_CHUNK, GMLP_CHUNK), 0)
    col_g = lax.broadcasted_iota(jnp.int32, (GMLP_CHUNK, GMLP_CHUNK), 1)
    gng = gng_ref[...]
    for g in range(GMLP_GROUPS):
        ls = slice(g * GROUP_DIM, (g + 1) * GROUP_DIM)
        ws_c = jnp.where(col_g <= row_g, ws_ref[g], 0.0).astype(BF16)
        bias = bs_ref[g]
        for n in range(TM // GMLP_CHUNK):
            rs = slice(n * GMLP_CHUNK, (n + 1) * GMLP_CHUNK)
            sv = _dot(ws_c, vn[rs, ls]) + bias
            yy = _rms(u[rs, ls] * sv) * gng[:, ls]
            y_ref[rs, HGRN_WIDTH + g * GROUP_DIM:HGRN_WIDTH + (g + 1) * GROUP_DIM] = yy.astype(BF16)

    x1 = x + mod[2:3] * _dot(y_ref[...], wout_ref[...])
    x1_ref[...] = x1
    h2 = _rms(x1) * g2_ref[...]
    h2 = h2 * (1.0 + mod[4:5]) + mod[3:4]
    h2_ref[...] = h2

    hh = h2.astype(BF16)
    hl = (h2 - hh.astype(F32)).astype(BF16)
    logits = _dot(hh, rwh_ref[...]) + (_dot(hl, rwh_ref[...]) + _dot(hh, rwl_ref[...])) + rb_ref[...]
    lane = lax.broadcasted_iota(jnp.int32, (TM, LANES), 1)
    vals, idxs = [], []
    for _ in range(TOP_K):
        m = jnp.max(logits, axis=-1, keepdims=True)
        idx = jnp.min(jnp.where(logits == m, lane, LANES), axis=-1, keepdims=True)
        vals.append(m)
        idxs.append(idx)
        logits = jnp.where(lane == idx, -jnp.inf, logits)
    es = [jnp.exp(v - vals[0]) for v in vals]
    tot = (es[0] + es[1]) + (es[2] + es[3])
    route = jnp.zeros((TM, LANES), F32)
    for k in range(TOP_K):
        route = jnp.where(lane == k, idxs[k].astype(F32), route)
        route = jnp.where(lane == TOP_K + k, es[k] / tot, route)
    route_ref[...] = route


def _mixer(x, mod8, norm_mix_g, w_in, lb_params, hgrn_norm_g, gmlp_ln_g, gmlp_ln_b, gmlp_ws, gmlp_bs,
           gmlp_norm_g, w_out, norm_ffn_g, router_w, router_b):
    bsz, seq, d = x.shape
    n_in = w_in.shape[1]
    rw = jnp.zeros((d, LANES), F32).at[:, :N_EXPERTS].set(router_w)
    rwh = rw.astype(BF16)
    rwl = (rw - rwh.astype(F32)).astype(BF16)
    rb = jnp.full((1, LANES), -1e30, F32).at[0, :N_EXPERTS].set(router_b)
    const = lambda *shape: pl.BlockSpec(shape, lambda b, i: (0,) * len(shape))
    tile = lambda w: pl.BlockSpec((None, TM, w), lambda b, i: (b, i, 0))
    return pl.pallas_call(
        _mixer_kernel,
        grid=(bsz, seq // TM),
        in_specs=[tile(d),
                  pl.BlockSpec((None, 8, d), lambda b, i: (b, 0, 0)),
                  const(1, d), const(d, n_in), const(2, HGRN_WIDTH), const(1, HGRN_WIDTH),
                  const(1, GMLP_WIDTH), const(1, GMLP_WIDTH),
                  const(GMLP_GROUPS, GMLP_CHUNK, GMLP_CHUNK), const(GMLP_GROUPS, GMLP_CHUNK, 1),
                  const(1, GMLP_WIDTH), const(d, d), const(1, d),
                  const(d, LANES), const(d, LANES), const(1, LANES)],
        out_specs=[tile(d), tile(d), tile(LANES)],
        out_shape=[jax.ShapeDtypeStruct((bsz, seq, d), F32),
                   jax.ShapeDtypeStruct((bsz, seq, d), F32),
                   jax.ShapeDtypeStruct((bsz, seq, LANES), F32)],
        scratch_shapes=[pltpu.VMEM((TM, n_in), F32),
                        pltpu.VMEM((TM, d), BF16),
                        pltpu.VMEM((HGRN_HEADS, HEAD_DIM, HEAD_DIM), F32)],
        compiler_params=pltpu.CompilerParams(dimension_semantics=("arbitrary", "arbitrary"),
                                             vmem_limit_bytes=VMEM_LIMIT),
        name="mixer",
    )(x, mod8, norm_mix_g.reshape(1, d), w_in.astype(BF16), lb_params, hgrn_norm_g.reshape(1, -1),
      gmlp_ln_g.reshape(1, -1), gmlp_ln_b.reshape(1, -1), gmlp_ws, gmlp_bs[:, :, None],
      gmlp_norm_g.reshape(1, -1), w_out.astype(BF16), norm_ffn_g.reshape(1, d), rwh, rwl, rb)


def _moe_kernel(n_slots,
                be_ref, first_ref, nused_ref,
                tokc_ref, tokn_ref, dst_ref, h2_hbm, rg_ref, wgu_ref, bgu_ref, wd_ref, bd_ref,
                ytk_hbm,
                xbuf, ybuf, wgu_bf, wd_bf, gsem, ssem):
    i = pl.program_id(0)
    n_used = nused_ref[0]
    slot = i % 2
    d_ff = wd_ref.shape[0]

    def gather_row(tok_ref, r, s):
        return pltpu.make_async_copy(h2_hbm.at[pl.ds(tok_ref[0, 0, r], 1)], xbuf.at[s, pl.ds(r, 1)], gsem.at[s])

    def scatter_row(r, s):
        return pltpu.make_async_copy(ybuf.at[s, pl.ds(r, 1)], ytk_hbm.at[pl.ds(dst_ref[0, 0, r], 1)], ssem.at[s])

    def wait_gather(s):
        pltpu.make_async_copy(h2_hbm.at[pl.ds(0, MOE_BLK)], xbuf.at[s], gsem.at[s]).wait()

    def wait_scatter(s):
        pltpu.make_async_copy(ybuf.at[s], ytk_hbm.at[pl.ds(0, MOE_BLK)], ssem.at[s]).wait()

    @pl.when(i == 0)
    def _():
        for r in range(MOE_BLK):
            gather_row(tokc_ref, r, 0).start()
        ybuf[...] = jnp.zeros_like(ybuf)
        for s in range(2):
            cp = pltpu.make_async_copy(ybuf.at[s], ytk_hbm.at[pl.ds(n_slots + s * MOE_BLK, MOE_BLK)], ssem.at[s])
            cp.start()
            cp.wait()

    @pl.when(i + 1 < n_used)
    def _():
        for r in range(MOE_BLK):
            gather_row(tokn_ref, r, 1 - slot).start()

    @pl.when(i < n_used)
    def _():
        @pl.when(first_ref[i] == 1)
        def _():
            wgu_bf[...] = wgu_ref[...].astype(BF16)
            wd_bf[...] = wd_ref[...].astype(BF16)

        wait_gather(slot)

        @pl.when(i >= 2)
        def _():
            wait_scatter(slot)

        xb = xbuf[slot].astype(BF16)
        gu = _dot(xb, wgu_bf[...]) + bgu_ref[...]
        gate = jnp.minimum(gu[:, :d_ff], SWIGLU_LIMIT)
        up = jnp.clip(gu[:, d_ff:], -SWIGLU_LIMIT, SWIGLU_LIMIT)
        glu = gate * jax.nn.sigmoid(SWIGLU_ALPHA * gate)
        yb = _dot(((up + 1.0) * glu).astype(BF16), wd_bf[...]) + bd_ref[...]
        ybuf[slot] = yb * rg_ref[...]
        for r in range(MOE_BLK):
            scatter_row(r, slot).start()

        @pl.when(i == n_used - 1)
        def _():
            wait_scatter(slot)

            @pl.when(i >= 1)
            def _():
                wait_scatter(1 - slot)


def _moe(h2, row_tok, row_dst, row_gate, block_expert, block_first, n_used, w_gate_up, b_gate_up, w_down,
         b_down, n_out_rows):
    n_tok, d = h2.shape
    n_blocks = row_tok.shape[0] // MOE_BLK
    d_ff = w_down.shape[1]
    smem_blk = lambda f: pl.BlockSpec((1, 1, MOE_BLK), f, memory_space=pltpu.SMEM)
    grid_spec = pltpu.PrefetchScalarGridSpec(
        num_scalar_prefetch=3,
        grid=(n_blocks,),
        in_specs=[smem_blk(lambda i, be, fi, nu: (i, 0, 0)),
                  smem_blk(lambda i, be, fi, nu: (jnp.minimum(i + 1, n_blocks - 1), 0, 0)),
                  smem_blk(lambda i, be, fi, nu: (i, 0, 0)),
                  pl.BlockSpec(memory_space=pl.ANY),
                  pl.BlockSpec((None, MOE_BLK, 1), lambda i, be, fi, nu: (i, 0, 0)),
                  pl.BlockSpec((None, d, 2 * d_ff), lambda i, be, fi, nu: (be[i], 0, 0)),
                  pl.BlockSpec((None, 1, 2 * d_ff), lambda i, be, fi, nu: (be[i], 0, 0)),
                  pl.BlockSpec((None, d_ff, d), lambda i, be, fi, nu: (be[i], 0, 0)),
                  pl.BlockSpec((None, 1, d), lambda i, be, fi, nu: (be[i], 0, 0))],
        out_specs=pl.BlockSpec(memory_space=pl.ANY),
        scratch_shapes=[pltpu.VMEM((2, MOE_BLK, d), F32),
                        pltpu.VMEM((2, MOE_BLK, d), F32),
                        pltpu.VMEM((d, 2 * d_ff), BF16),
                        pltpu.VMEM((d_ff, d), BF16),
                        pltpu.SemaphoreType.DMA((2,)),
                        pltpu.SemaphoreType.DMA((2,))],
    )
    tok3 = row_tok.reshape(n_blocks, 1, MOE_BLK)
    return pl.pallas_call(
        functools.partial(_moe_kernel, n_out_rows - 2 * MOE_BLK),
        grid_spec=grid_spec,
        out_shape=jax.ShapeDtypeStruct((n_out_rows, d), F32),
        compiler_params=pltpu.CompilerParams(dimension_semantics=("arbitrary",),
                                             vmem_limit_bytes=VMEM_LIMIT),
        name="moe_ffn",
    )(block_expert, block_first, n_used,
      tok3, tok3, row_dst.reshape(n_blocks, 1, MOE_BLK), h2, row_gate.reshape(n_blocks, MOE_BLK, 1),
      w_gate_up, b_gate_up[:, None, :], w_down, b_down[:, None, :])


def _combine_kernel(x1_ref, mod_ref, y0_ref, y1_ref, y2_ref, y3_ref, fg_ref, o_ref):
    y = (y0_ref[...] + y1_ref[...]) + (y2_ref[...] + y3_ref[...])
    x2 = x1_ref[...] + mod_ref[5:6, :] * y
    o_ref[...] = _rms(x2) * fg_ref[...]


def _combine(x1, mod8, ytk, final_g):
    bsz, seq, d = x1.shape
    n_tok = bsz * seq
    tiles_per_seq = seq // TC
    y_spec = lambda k: pl.BlockSpec((TC, d), lambda b, i: (k * (n_tok // TC) + b * tiles_per_seq + i, 0))
    return pl.pallas_call(
        _combine_kernel,
        grid=(bsz, tiles_per_seq),
        in_specs=[pl.BlockSpec((None, TC, d), lambda b, i: (b, i, 0)),
                  pl.BlockSpec((None, 8, d), lambda b, i: (b, 0, 0)),
                  y_spec(0), y_spec(1), y_spec(2), y_spec(3),
                  pl.BlockSpec((1, d), lambda b, i: (0, 0))],
        out_specs=pl.BlockSpec((None, TC, d), lambda b, i: (b, i, 0)),
        out_shape=jax.ShapeDtypeStruct((bsz, seq, d), F32),
        compiler_params=pltpu.CompilerParams(dimension_semantics=("arbitrary", "arbitrary")),
        name="combine",
    )(x1, mod8, ytk, ytk, ytk, ytk, final_g.reshape(1, d))


def _routing(route, n_tok):
    topi = route[:, :TOP_K].astype(jnp.int32)
    gates = route[:, TOP_K:2 * TOP_K]
    nk = n_tok * TOP_K
    flat_e = topi.reshape(-1)
    onehot = (flat_e[:, None] == jnp.arange(N_EXPERTS, dtype=jnp.int32)[None, :]).astype(jnp.int32)
    csum = jnp.cumsum(onehot, axis=0)
    rank = jnp.sum(onehot * csum, axis=1) - 1
    counts = csum[-1]
    padded = ((counts + MOE_BLK - 1) // MOE_BLK) * MOE_BLK
    pad_end = jnp.cumsum(padded)
    start_pad = pad_end - padded
    dest = start_pad[flat_e] + rank
    p_rows = ((nk + N_EXPERTS * (MOE_BLK - 1) + MOE_BLK - 1) // MOE_BLK) * MOE_BLK
    n_blocks = p_rows // MOE_BLK
    flat = jnp.arange(nk, dtype=jnp.int32)
    tok = flat // TOP_K
    slot_id = (flat % TOP_K) * n_tok + tok
    prow = jnp.arange(p_rows, dtype=jnp.int32)
    dump = nk + ((prow // MOE_BLK) % 2) * MOE_BLK + prow % MOE_BLK
    row_dst = dump.at[dest].set(slot_id)
    row_tok = jnp.zeros((p_rows,), jnp.int32).at[dest].set(tok)
    row_gate = jnp.zeros((p_rows,), F32).at[dest].set(gates.reshape(-1))
    n_used = (pad_end[-1] // MOE_BLK).astype(jnp.int32)
    blk = jnp.arange(n_blocks, dtype=jnp.int32)
    be = jnp.minimum(jnp.searchsorted(pad_end, blk * MOE_BLK, side="right"), N_EXPERTS - 1).astype(jnp.int32)
    be = jnp.where(blk < n_used, be, be[n_used - 1])
    first = jnp.concatenate([jnp.ones((1,), jnp.int32), (be[1:] != be[:-1]).astype(jnp.int32)])
    return row_tok, row_dst, row_gate, be, first, n_used.reshape(1), nk + 2 * MOE_BLK


def kernel(x, c, ada_w, ada_b, norm_mix_g, w_in, lb_params, hgrn_norm_g, gmlp_ln_g, gmlp_ln_b, gmlp_ws, gmlp_bs,
           gmlp_norm_g, w_out, norm_ffn_g, router_w, router_b, w_gate_up, b_gate_up, w_down, b_down, final_g):
    assert ada_w.shape[0] == 1, "single-layer block"
    bsz, seq, d = x.shape
    n_tok = bsz * seq
    mod = _modulation(c, ada_w[0], ada_b[0])
    mod8 = jnp.zeros((bsz, 8, d), F32).at[:, :6].set(mod.reshape(bsz, 6, d))
    x1, h2, route = _mixer(x, mod8, norm_mix_g[0], w_in[0], lb_params, hgrn_norm_g[0], gmlp_ln_g[0],
                           gmlp_ln_b[0], gmlp_ws[0], gmlp_bs[0], gmlp_norm_g[0], w_out[0], norm_ffn_g[0],
                           router_w[0], router_b[0])
    row_tok, row_dst, row_gate, be, first, n_used, n_out_rows = _routing(route.reshape(n_tok, LANES), n_tok)
    ytk = _moe(h2.reshape(n_tok, d), row_tok, row_dst, row_gate, be, first, n_used,
               w_gate_up[0], b_gate_up[0], w_down[0], b_down[0], n_out_rows)
    return _combine(x1, mod8, ytk, final_g)
```

```python
import functools

import jax
import jax.numpy as jnp
from jax import lax
from jax.experimental import pallas as pl
from jax.experimental.pallas import tpu as pltpu

F32 = jnp.float32
BF16 = jnp.bfloat16

HGRN_HEADS = 4
HEAD_DIM = 128
HGRN_WIDTH = HGRN_HEADS * HEAD_DIM
HGRN_CHUNK = 64
HGRN_SUB = 16
GMLP_GROUPS = 4
GROUP_DIM = 128
GMLP_WIDTH = GMLP_GROUPS * GROUP_DIM
GMLP_CHUNK = 128
N_EXPERTS = 32
TOP_K = 4
SWIGLU_LIMIT = 7.0
SWIGLU_ALPHA = 1.702
EPS = 1e-6
LANES = 128
SUBLANES = 8
DECAY_EXP_CLAMP = 60.0
TM = 256
MOE_BLK = 256
TD = 256
VMEM_LIMIT = 56 * 1024 * 1024


def _dot(a, b):
    return jnp.dot(a, b, preferred_element_type=F32)


def _dot_nt(a, b):
    return lax.dot_general(a, b, (((1,), (1,)), ((), ())), preferred_element_type=F32)


def _dot_tn(a, b):
    return lax.dot_general(a, b, (((0,), (0,)), ((), ())), preferred_element_type=F32)


def _rms(x):
    return x * lax.rsqrt(jnp.mean(x * x, axis=-1, keepdims=True) + EPS)


def _gelu(x):
    return 0.5 * x * (1.0 + lax.erf(x * 0.7071067811865476))


def _by_parity(i, fn):
    @pl.when(i % 2 == 0)
    def _():
        fn(0)

    @pl.when(i % 2 == 1)
    def _():
        fn(1)


def _rows_to_tiles(x2d, ref3d_set):
    for j in range(SUBLANES):
        ref3d_set(j, x2d[:, j * LANES:(j + 1) * LANES])


def _mod_kernel(c_ref, w_ref, b_ref, o_ref):
    c = c_ref[...]
    ca = c * jax.nn.sigmoid(c)
    o_ref[...] = jnp.dot(ca, w_ref[...], precision=lax.Precision.HIGHEST,
                         preferred_element_type=F32) + b_ref[...]


def _modulation(c, ada_w, ada_b):
    bsz, d = c.shape
    n_out = ada_w.shape[1]
    rows = 8
    c_pad = jnp.zeros((rows, d), F32).at[:bsz].set(c)
    tn = 1024
    out = pl.pallas_call(
        _mod_kernel,
        grid=(n_out // tn,),
        in_specs=[pl.BlockSpec((rows, d), lambda j: (0, 0)),
                  pl.BlockSpec((d, tn), lambda j: (0, j)),
                  pl.BlockSpec((1, tn), lambda j: (0, j))],
        out_specs=pl.BlockSpec((rows, tn), lambda j: (0, j)),
        out_shape=jax.ShapeDtypeStruct((rows, n_out), F32),
        name="adaln_mod",
    )(c_pad, ada_w, ada_b.reshape(1, n_out))
    return out[:bsz]


def _mixer_kernel(x_ref, mod_ref, g1_ref, win_ref, lbp_ref, hg_ref, lng_ref, lnb_ref, ws_ref, bs_ref,
                  gng_ref, wout_ref, g2_ref, rwh_ref, rwl_ref, rb_ref,
                  x1_ref, h2_ref, route_ref, cnt_out_ref,
                  z_ref, y_ref, st_ref, cnt_ref):
    @pl.when(pl.program_id(1) == 0)
    def _():
        st_ref[...] = jnp.zeros_like(st_ref)

    x = x_ref[...]
    mod = mod_ref[...]
    h = _rms(x) * g1_ref[...]
    h = h * (1.0 + mod[1:2]) + mod[0:1]
    z_ref[...] = _dot(h.astype(BF16), win_ref[...])

    lbp = lbp_ref[...]
    lbe = jnp.exp(lbp - jnp.max(lbp, axis=0, keepdims=True))
    lb = lbe[0:1] / jnp.sum(lbe, axis=0, keepdims=True)
    hg = hg_ref[...]
    row = lax.broadcasted_iota(jnp.int32, (HGRN_CHUNK, HGRN_CHUNK), 0)
    col = lax.broadcasted_iota(jnp.int32, (HGRN_CHUNK, HGRN_CHUNK), 1)
    causal = col <= row
    tri = jnp.where(causal, 1.0, 0.0).astype(BF16)
    n_sub = HGRN_CHUNK // HGRN_SUB

    def chunk_body(c, carry):
        rows = pl.ds(pl.multiple_of(c * HGRN_CHUNK, HGRN_CHUNK), HGRN_CHUNK)
        zq = z_ref[rows, 0:HGRN_WIDTH]
        zf = z_ref[rows, HGRN_WIDTH:2 * HGRN_WIDTH]
        zi = z_ref[rows, 2 * HGRN_WIDTH:3 * HGRN_WIDTH]
        zg = z_ref[rows, 3 * HGRN_WIDTH:4 * HGRN_WIDTH]
        f = lb + (1.0 - lb) * jax.nn.sigmoid(zf)
        logf = jnp.log(f)
        kk = 1.0 - f
        p0 = logf.astype(BF16)
        r0 = logf - p0.astype(F32)
        p1 = r0.astype(BF16)
        p2 = (r0 - p1.astype(F32)).astype(BF16)
        b = (_dot(tri, p0) + _dot(tri, p1)) + _dot(tri, p2)
        b_last = b[HGRN_CHUNK - 1:HGRN_CHUNK, :]
        qe = (zq * jnp.exp(b)).astype(BF16)
        kdec = (kk * jnp.exp(b_last - b)).astype(BF16)
        v = zi.astype(BF16)
        dec_last = jnp.exp(b_last)
        a_sub, k_sub = [], []
        for i in range(n_sub):
            lo, hi = i * HGRN_SUB, (i + 1) * HGRN_SUB
            bref = b[lo - 1:lo, :] if i > 0 else jnp.zeros((1, HGRN_WIDTH), F32)
            a_sub.append((zq[lo:hi] * jnp.exp(b[lo:hi] - bref)).astype(BF16))
            k_sub.append((kk * jnp.exp(jnp.minimum(bref - b, DECAY_EXP_CLAMP))).astype(BF16))
        silu_g = zg * jax.nn.sigmoid(zg)
        for hd in range(HGRN_HEADS):
            ls = slice(hd * HEAD_DIM, (hd + 1) * HEAD_DIM)
            sc = jnp.concatenate([_dot_nt(a_sub[i][:, ls], k_sub[i][:, ls]) for i in range(n_sub)], axis=0)
            sc = jnp.where(causal, sc, 0.0).astype(BF16)
            st = st_ref[hd]
            o = _dot_nt(qe[:, ls], st.astype(BF16)) + _dot(sc, v[:, ls])
            st_ref[hd] = st * dec_last[:, ls] + _dot_tn(v[:, ls], kdec[:, ls])
            o = _rms(o) * hg[:, ls]
            y_ref[rows, ls] = (o * silu_g[:, ls]).astype(BF16)
        return carry

    lax.fori_loop(0, TM // HGRN_CHUNK, chunk_body, 0)

    u = _gelu(z_ref[:, 4 * HGRN_WIDTH:4 * HGRN_WIDTH + GMLP_WIDTH])
    gv = _gelu(z_ref[:, 4 * HGRN_WIDTH + GMLP_WIDTH:])
    mu = jnp.mean(gv, axis=-1, keepdims=True)
    gc = gv - mu
    var = jnp.mean(gc * gc, axis=-1, keepdims=True)
    vn = (gc * lax.rsqrt(var + EPS) * lng_ref[...] + lnb_ref[...]).astype(BF16)
    row_g = lax.broadcasted_iota(jnp.int32, (GMLP_CHUNK, GMLP_CHUNK), 0)
    col_g = lax.broadcasted_iota(jnp.int32, (GMLP_CHUNK, GMLP_CHUNK), 1)
    gng = gng_ref[...]
    for g in range(GMLP_GROUPS):
        ls = slice(g * GROUP_DIM, (g + 1) * GROUP_DIM)
        ws_c = jnp.where(col_g <= row_g, ws_ref[g], 0.0).astype(BF16)
        bias = bs_ref[g]
        for n in range(TM // GMLP_CHUNK):
            rs = slice(n * GMLP_CHUNK, (n + 1) * GMLP_CHUNK)
            sv = _dot(ws_c, vn[rs, ls]) + bias
            yy = _rms(u[rs, ls] * sv) * gng[:, ls]
            y_ref[rs, HGRN_WIDTH + g * GROUP_DIM:HGRN_WIDTH + (g + 1) * GROUP_DIM] = yy.astype(BF16)

    x1 = x + mod[2:3] * _dot(y_ref[...], wout_ref[...])
    x1_ref[...] = x1
    h2 = _rms(x1) * g2_ref[...]
    h2 = h2 * (1.0 + mod[4:5]) + mod[3:4]
    h2_ref[...] = h2

    hh = h2.astype(BF16)
    hl = (h2 - hh.astype(F32)).astype(BF16)
    logits = _dot(hh, rwh_ref[...]) + (_dot(hl, rwh_ref[...]) + _dot(hh, rwl_ref[...])) + rb_ref[...]
    lane = lax.broadcasted_iota(jnp.int32, (TM, LANES), 1)
    vals, idxs = [], []
    for _ in range(TOP_K):
        m = jnp.max(logits, axis=-1, keepdims=True)
        idx = jnp.min(jnp.where(logits == m, lane, LANES), axis=-1, keepdims=True)
        vals.append(m)
        idxs.append(idx)
        logits = jnp.where(lane == idx, -jnp.inf, logits)
    es = [jnp.exp(v - vals[0]) for v in vals]
    tot = (es[0] + es[1]) + (es[2] + es[3])

    @pl.when((pl.program_id(0) == 0) & (pl.program_id(1) == 0))
    def _():
        cnt_ref[...] = jnp.zeros_like(cnt_ref)

    hot = [lane == idxs[k] for k in range(TOP_K)]
    picked = jnp.where((hot[0] | hot[1]) | (hot[2] | hot[3]), 1.0, 0.0)
    row_t = lax.broadcasted_iota(jnp.int32, (TM, TM), 0)
    col_t = lax.broadcasted_iota(jnp.int32, (TM, TM), 1)
    before = jnp.where(col_t < row_t, 1.0, 0.0).astype(BF16)
    seen = _dot(before, picked.astype(BF16)) + cnt_ref[0:1, :]
    cnt_ref[...] = cnt_ref[...] + jnp.sum(picked, axis=0, keepdims=True)
    cnt_out_ref[...] = cnt_ref[...]
    route = jnp.zeros((TM, LANES), F32)
    for k in range(TOP_K):
        rank = jnp.sum(jnp.where(hot[k], seen, 0.0), axis=-1, keepdims=True)
        route = jnp.where(lane == k, idxs[k].astype(F32), route)
        route = jnp.where(lane == TOP_K + k, es[k] / tot, route)
        route = jnp.where(lane == 2 * TOP_K + k, rank, route)
    route_ref[...] = route


def _mixer(x, mod8, norm_mix_g, w_in, lb_params, hgrn_norm_g, gmlp_ln_g, gmlp_ln_b, gmlp_ws, gmlp_bs,
           gmlp_norm_g, w_out, norm_ffn_g, router_w, router_b):
    bsz, seq, d = x.shape
    n_in = w_in.shape[1]
    rw = jnp.zeros((d, LANES), F32).at[:, :N_EXPERTS].set(router_w)
    rwh = rw.astype(BF16)
    rwl = (rw - rwh.astype(F32)).astype(BF16)
    rb = jnp.full((1, LANES), -1e30, F32).at[0, :N_EXPERTS].set(router_b)
    const = lambda *shape: pl.BlockSpec(shape, lambda b, i: (0,) * len(shape))
    tile = lambda w: pl.BlockSpec((None, TM, w), lambda b, i: (b, i, 0))
    return pl.pallas_call(
        _mixer_kernel,
        grid=(bsz, seq // TM),
        in_specs=[tile(d),
                  pl.BlockSpec((None, 8, d), lambda b, i: (b, 0, 0)),
                  const(1, d), const(d, n_in), const(2, HGRN_WIDTH), const(1, HGRN_WIDTH),
                  const(1, GMLP_WIDTH), const(1, GMLP_WIDTH),
                  const(GMLP_GROUPS, GMLP_CHUNK, GMLP_CHUNK), const(GMLP_GROUPS, GMLP_CHUNK, 1),
                  const(1, GMLP_WIDTH), const(d, d), const(1, d),
                  const(d, LANES), const(d, LANES), const(1, LANES)],
        out_specs=[tile(d), tile(d), tile(LANES), const(SUBLANES, LANES)],
        out_shape=[jax.ShapeDtypeStruct((bsz, seq, d), F32),
                   jax.ShapeDtypeStruct((bsz, seq, d), F32),
                   jax.ShapeDtypeStruct((bsz, seq, LANES), F32),
                   jax.ShapeDtypeStruct((SUBLANES, LANES), F32)],
        scratch_shapes=[pltpu.VMEM((TM, n_in), F32),
                        pltpu.VMEM((TM, d), BF16),
                        pltpu.VMEM((HGRN_HEADS, HEAD_DIM, HEAD_DIM), F32),
                        pltpu.VMEM((SUBLANES, LANES), F32)],
        compiler_params=pltpu.CompilerParams(dimension_semantics=("arbitrary", "arbitrary"),
                                             vmem_limit_bytes=VMEM_LIMIT),
        name="mixer",
    )(x, mod8, norm_mix_g.reshape(1, d), w_in.astype(BF16), lb_params, hgrn_norm_g.reshape(1, -1),
      gmlp_ln_g.reshape(1, -1), gmlp_ln_b.reshape(1, -1), gmlp_ws, gmlp_bs[:, :, None],
      gmlp_norm_g.reshape(1, -1), w_out.astype(BF16), norm_ffn_g.reshape(1, d), rwh, rwl, rb)


def _dispatch_kernel(n_pad_step, dest_ref, pad_ref, h2_ref, xs_hbm, rows, zero_tile, sem):
    i = pl.program_id(0)
    n_steps = pl.num_programs(0)

    def wait_step(s):
        for _ in range(TOP_K):
            pltpu.make_async_copy(rows.at[s], xs_hbm.at[pl.ds(0, TD)], sem.at[s]).wait()
        pltpu.make_async_copy(rows.at[s, pl.ds(0, n_pad_step)], xs_hbm.at[pl.ds(0, n_pad_step)], sem.at[s]).wait()

    @pl.when(i == 0)
    def _():
        zero_tile[...] = jnp.zeros_like(zero_tile)

    def step(s):
        @pl.when(i >= 2)
        def _():
            wait_step(s)

        h2 = h2_ref[...]
        for j in range(SUBLANES):
            rows[s, :, j, :] = h2[:, j * LANES:(j + 1) * LANES]
        for t in range(TD):
            for k in range(TOP_K):
                pltpu.make_async_copy(rows.at[s, t], xs_hbm.at[dest_ref[0, 0, t * TOP_K + k]], sem.at[s]).start()
        for q in range(n_pad_step):
            pltpu.make_async_copy(zero_tile.at[0], xs_hbm.at[pad_ref[0, 0, q]], sem.at[s]).start()

        @pl.when(i == n_steps - 1)
        def _():
            wait_step(s)

            @pl.when(i >= 1)
            def _():
                wait_step(1 - s)

    _by_parity(i, step)


def _dispatch(h2, dest, pad_dest, p_rows):
    n_tok, d = h2.shape
    n_steps = n_tok // TD
    n_pad_step = pad_dest.shape[0] // n_steps
    return pl.pallas_call(
        functools.partial(_dispatch_kernel, n_pad_step),
        grid=(n_steps,),
        in_specs=[pl.BlockSpec((1, 1, TD * TOP_K), lambda i: (i, 0, 0), memory_space=pltpu.SMEM),
                  pl.BlockSpec((1, 1, n_pad_step), lambda i: (i, 0, 0), memory_space=pltpu.SMEM),
                  pl.BlockSpec((TD, d), lambda i: (i, 0))],
        out_specs=pl.BlockSpec(memory_space=pl.ANY),
        out_shape=jax.ShapeDtypeStruct((p_rows, SUBLANES, LANES), F32),
        scratch_shapes=[pltpu.VMEM((2, TD, SUBLANES, LANES), F32),
                        pltpu.VMEM((1, SUBLANES, LANES), F32),
                        pltpu.SemaphoreType.DMA((2,))],
        compiler_params=pltpu.CompilerParams(dimension_semantics=("arbitrary",)),
        name="dispatch",
    )(dest.reshape(n_steps, 1, TD * TOP_K), pad_dest.reshape(n_steps, 1, n_pad_step), h2)


def _moe_kernel(be_ref, first_ref, nused_ref, xs_ref, wgu_ref, bgu_ref, wd_ref, bd_ref, ys_ref, wgu_bf, wd_bf):
    i = pl.program_id(0)
    d_ff = wd_ref.shape[0]

    @pl.when(i >= nused_ref[0])
    def _():
        ys_ref[...] = jnp.zeros_like(ys_ref)

    @pl.when(i < nused_ref[0])
    def _():
        @pl.when(first_ref[i] == 1)
        def _():
            wgu_bf[...] = wgu_ref[...].astype(BF16)
            wd_bf[...] = wd_ref[...].astype(BF16)

        xb = jnp.concatenate([xs_ref[:, j, :] for j in range(SUBLANES)], axis=-1).astype(BF16)
        gu = _dot(xb, wgu_bf[...]) + bgu_ref[...]
        gate = jnp.minimum(gu[:, :d_ff], SWIGLU_LIMIT)
        up = jnp.clip(gu[:, d_ff:], -SWIGLU_LIMIT, SWIGLU_LIMIT)
        glu = gate * jax.nn.sigmoid(SWIGLU_ALPHA * gate)
        yb = _dot(((up + 1.0) * glu).astype(BF16), wd_bf[...]) + bd_ref[...]
        for j in range(SUBLANES):
            ys_ref[:, j, :] = yb[:, j * LANES:(j + 1) * LANES]


def _moe(xs, block_expert, block_first, n_used, w_gate_up, b_gate_up, w_down, b_down):
    p_rows = xs.shape[0]
    n_blocks = p_rows // MOE_BLK
    d_ff, d = w_down.shape[1], w_down.shape[2]
    grid_spec = pltpu.PrefetchScalarGridSpec(
        num_scalar_prefetch=3,
        grid=(n_blocks,),
        in_specs=[pl.BlockSpec((MOE_BLK, SUBLANES, LANES),
                               lambda i, be, fi, nu: (jnp.minimum(i, nu[0] - 1), 0, 0)),
                  pl.BlockSpec((None, d, 2 * d_ff), lambda i, be, fi, nu: (be[i], 0, 0)),
                  pl.BlockSpec((None, 1, 2 * d_ff), lambda i, be, fi, nu: (be[i], 0, 0)),
                  pl.BlockSpec((None, d_ff, d), lambda i, be, fi, nu: (be[i], 0, 0)),
                  pl.BlockSpec((None, 1, d), lambda i, be, fi, nu: (be[i], 0, 0))],
        out_specs=pl.BlockSpec((MOE_BLK, SUBLANES, LANES), lambda i, be, fi, nu: (i, 0, 0)),
        scratch_shapes=[pltpu.VMEM((d, 2 * d_ff), BF16),
                        pltpu.VMEM((d_ff, d), BF16)],
    )
    return pl.pallas_call(
        _moe_kernel,
        grid_spec=grid_spec,
        out_shape=jax.ShapeDtypeStruct((p_rows, SUBLANES, LANES), F32),
        compiler_params=pltpu.CompilerParams(dimension_semantics=("arbitrary",),
                                             vmem_limit_bytes=VMEM_LIMIT),
        name="moe_ffn",
    )(block_expert, block_first, n_used, xs, w_gate_up, b_gate_up[:, None, :], w_down, b_down[:, None, :])


def _combine_kernel(destc_ref, destn_ref, x1_ref, mod_ref, route_ref, fg_ref, ys_hbm, o_ref, buf, sem):
    i = pl.program_id(0)
    n_steps = pl.num_programs(0)
    d = x1_ref.shape[-1]

    def start_gather(dest_ref, s):
        for t in range(TD):
            for k in range(TOP_K):
                pltpu.make_async_copy(ys_hbm.at[dest_ref[0, 0, t * TOP_K + k]], buf.at[s, k, t], sem.at[s]).start()

    def wait_gather(s):
        for k in range(TOP_K):
            pltpu.make_async_copy(ys_hbm.at[pl.ds(0, TD)], buf.at[s, k], sem.at[s]).wait()

    @pl.when(i == 0)
    def _():
        start_gather(destc_ref, 0)

    def step(s):
        @pl.when(i + 1 < n_steps)
        def _():
            start_gather(destn_ref, 1 - s)

        wait_gather(s)
        gates = route_ref[...]
        gk = [gates[:, TOP_K + k:TOP_K + k + 1] for k in range(TOP_K)]
        gate2 = mod_ref[5:6, :]
        x2, ss = [], jnp.zeros((TD, 1), F32)
        for j in range(SUBLANES):
            ls = slice(j * LANES, (j + 1) * LANES)
            y = (gk[0] * buf[s, 0, :, j, :] + gk[1] * buf[s, 1, :, j, :]) + \
                (gk[2] * buf[s, 2, :, j, :] + gk[3] * buf[s, 3, :, j, :])
            xj = x1_ref[:, ls] + gate2[:, ls] * y
            x2.append(xj)
            ss = ss + jnp.sum(xj * xj, axis=-1, keepdims=True)
        inv = lax.rsqrt(ss / d + EPS)
        for j in range(SUBLANES):
            ls = slice(j * LANES, (j + 1) * LANES)
            o_ref[:, ls] = x2[j] * inv * fg_ref[:, ls]

    _by_parity(i, step)


def _combine(x1, mod8, route, dest, ys, final_g):
    bsz, seq, d = x1.shape
    n_tok = bsz * seq
    n_steps = n_tok // TD
    per_seq = seq // TD
    dest3 = dest.reshape(n_steps, 1, TD * TOP_K)
    smem_blk = lambda f: pl.BlockSpec((1, 1, TD * TOP_K), f, memory_space=pltpu.SMEM)
    return pl.pallas_call(
        _combine_kernel,
        grid=(n_steps,),
        in_specs=[smem_blk(lambda i: (i, 0, 0)),
                  smem_blk(lambda i: (jnp.minimum(i + 1, n_steps - 1), 0, 0)),
                  pl.BlockSpec((TD, d), lambda i: (i, 0)),
                  pl.BlockSpec((None, 8, d), lambda i: (i // per_seq, 0, 0)),
                  pl.BlockSpec((TD, LANES), lambda i: (i, 0)),
                  pl.BlockSpec((1, d), lambda i: (0, 0)),
                  pl.BlockSpec(memory_space=pl.ANY)],
        out_specs=pl.BlockSpec((TD, d), lambda i: (i, 0)),
        out_shape=jax.ShapeDtypeStruct((n_tok, d), F32),
        scratch_shapes=[pltpu.VMEM((2, TOP_K, TD, SUBLANES, LANES), F32),
                        pltpu.SemaphoreType.DMA((2,))],
        compiler_params=pltpu.CompilerParams(dimension_semantics=("arbitrary",),
                                             vmem_limit_bytes=VMEM_LIMIT),
        name="combine",
    )(dest3, dest3, x1.reshape(n_tok, d), mod8, route, final_g.reshape(1, d), ys).reshape(bsz, seq, d)


def _routing(route, counts_f, n_tok):
    nk = n_tok * TOP_K
    experts = jnp.arange(N_EXPERTS, dtype=jnp.int32)
    topi = route[:, :TOP_K].astype(jnp.int32)
    rank = route[:, 2 * TOP_K:3 * TOP_K].astype(jnp.int32)
    counts = counts_f[0, :N_EXPERTS].astype(jnp.int32)
    padded = ((counts + MOE_BLK - 1) // MOE_BLK) * MOE_BLK
    pad_end = jnp.cumsum(padded)
    start_pad = pad_end - padded
    dest = rank + jnp.sum(jnp.where(topi[:, :, None] == experts[None, None, :], start_pad[None, None, :], 0), axis=-1)
    p_rows = ((nk + N_EXPERTS * (MOE_BLK - 1) + MOE_BLK - 1) // MOE_BLK) * MOE_BLK
    n_blocks = p_rows // MOE_BLK
    n_pad = padded - counts
    pad_cum = jnp.cumsum(n_pad)
    j = jnp.arange(p_rows - nk, dtype=jnp.int32)
    e_j = jnp.minimum(jnp.searchsorted(pad_cum, j, side="right"), N_EXPERTS - 1).astype(jnp.int32)
    in_expert = j < pad_cum[-1]
    pad_dest = jnp.where(in_expert,
                         start_pad[e_j] + counts[e_j] + (j - (pad_cum[e_j] - n_pad[e_j])),
                         pad_end[-1] + (j - pad_cum[-1])).astype(jnp.int32)
    n_used = (pad_end[-1] // MOE_BLK).astype(jnp.int32)
    blk = jnp.arange(n_blocks, dtype=jnp.int32)
    be = jnp.minimum(jnp.searchsorted(pad_end, blk * MOE_BLK, side="right"), N_EXPERTS - 1).astype(jnp.int32)
    be = jnp.where(blk < n_used, be, be[n_used - 1])
    first = jnp.concatenate([jnp.ones((1,), jnp.int32), (be[1:] != be[:-1]).astype(jnp.int32)])
    return dest.reshape(-1), pad_dest, be, first, n_used.reshape(1), p_rows


def kernel(x, c, ada_w, ada_b, norm_mix_g, w_in, lb_params, hgrn_norm_g, gmlp_ln_g, gmlp_ln_b, gmlp_ws, gmlp_bs,
           gmlp_norm_g, w_out, norm_ffn_g, router_w, router_b, w_gate_up, b_gate_up, w_down, b_down, final_g):
    assert ada_w.shape[0] == 1, "single-layer block"
    bsz, seq, d = x.shape
    assert d == SUBLANES * LANES and seq % TM == 0 and (bsz * seq) % TD == 0
    n_tok = bsz * seq
    mod = _modulation(c, ada_w[0], ada_b[0])
    mod8 = jnp.zeros((bsz, 8, d), F32).at[:, :6].set(mod.reshape(bsz, 6, d))
    x1, h2, route, counts = _mixer(x, mod8, norm_mix_g[0], w_in[0], lb_params, hgrn_norm_g[0], gmlp_ln_g[0],
                                   gmlp_ln_b[0], gmlp_ws[0], gmlp_bs[0], gmlp_norm_g[0], w_out[0],
                                   norm_ffn_g[0], router_w[0], router_b[0])
    route = route.reshape(n_tok, LANES)
    dest, pad_dest, be, first, n_used, p_rows = _routing(route, counts, n_tok)
    xs = _dispatch(h2.reshape(n_tok, d), dest, pad_dest, p_rows)
    ys = _moe(xs, be, first, n_used, w_gate_up[0], b_gate_up[0], w_down[0], b_down[0])
    return _combine(x1, mod8, route, dest, ys, final_g)
```

```python
import functools

import jax
import jax.numpy as jnp
from jax import lax
from jax.experimental import pallas as pl
from jax.experimental.pallas import tpu as pltpu

F32 = jnp.float32
BF16 = jnp.bfloat16

HGRN_HEADS = 4
HEAD_DIM = 128
HGRN_WIDTH = HGRN_HEADS * HEAD_DIM
HGRN_CHUNK = 64
HGRN_SUB = 16
GMLP_GROUPS = 4
GROUP_DIM = 128
GMLP_WIDTH = GMLP_GROUPS * GROUP_DIM
GMLP_CHUNK = 128
N_EXPERTS = 32
TOP_K = 4
SWIGLU_LIMIT = 7.0
SWIGLU_ALPHA = 1.702
EPS = 1e-6
LANES = 128
SUBLANES = 8
N_DMA_QUEUES = 2
DECAY_EXP_CLAMP = 60.0
TM = 256
MOE_BLK = 256
TD = 256
VMEM_LIMIT = 56 * 1024 * 1024


def _dot(a, b):
    return jnp.dot(a, b, preferred_element_type=F32)


def _dot_nt(a, b):
    return lax.dot_general(a, b, (((1,), (1,)), ((), ())), preferred_element_type=F32)


def _dot_tn(a, b):
    return lax.dot_general(a, b, (((0,), (0,)), ((), ())), preferred_element_type=F32)


def _rms(x):
    return x * lax.rsqrt(jnp.mean(x * x, axis=-1, keepdims=True) + EPS)


def _gelu(x):
    return 0.5 * x * (1.0 + lax.erf(x * 0.7071067811865476))


def _by_parity(i, fn):
    @pl.when(i % 2 == 0)
    def _():
        fn(0)

    @pl.when(i % 2 == 1)
    def _():
        fn(1)


def _mod_kernel(c_ref, w_ref, b_ref, o_ref):
    c = c_ref[...]
    ca = c * jax.nn.sigmoid(c)
    o_ref[...] = jnp.dot(ca, w_ref[...], precision=lax.Precision.HIGHEST,
                         preferred_element_type=F32) + b_ref[...]


def _modulation(c, ada_w, ada_b):
    bsz, d = c.shape
    n_out = ada_w.shape[1]
    rows = 8
    c_pad = jnp.zeros((rows, d), F32).at[:bsz].set(c)
    tn = 1024
    out = pl.pallas_call(
        _mod_kernel,
        grid=(n_out // tn,),
        in_specs=[pl.BlockSpec((rows, d), lambda j: (0, 0)),
                  pl.BlockSpec((d, tn), lambda j: (0, j)),
                  pl.BlockSpec((1, tn), lambda j: (0, j))],
        out_specs=pl.BlockSpec((rows, tn), lambda j: (0, j)),
        out_shape=jax.ShapeDtypeStruct((rows, n_out), F32),
        name="adaln_mod",
    )(c_pad, ada_w, ada_b.reshape(1, n_out))
    return out[:bsz]


def _mixer_kernel(x_ref, mod_ref, g1_ref, win_ref, lbp_ref, hg_ref, lng_ref, lnb_ref, ws_ref, bs_ref,
                  gng_ref, wout_ref, g2_ref, rwh_ref, rwl_ref, rb_ref,
                  x1_ref, h2_ref, route_ref, cnt_out_ref,
                  z_ref, y_ref, st_ref, cnt_ref):
    @pl.when(pl.program_id(1) == 0)
    def _():
        st_ref[...] = jnp.zeros_like(st_ref)

    x = x_ref[...]
    mod = mod_ref[...]
    h = _rms(x) * g1_ref[...]
    h = h * (1.0 + mod[1:2]) + mod[0:1]
    z_ref[...] = _dot(h.astype(BF16), win_ref[...])

    lbp = lbp_ref[...]
    lbe = jnp.exp(lbp - jnp.max(lbp, axis=0, keepdims=True))
    lb = lbe[0:1] / jnp.sum(lbe, axis=0, keepdims=True)
    hg = hg_ref[...]
    row = lax.broadcasted_iota(jnp.int32, (HGRN_CHUNK, HGRN_CHUNK), 0)
    col = lax.broadcasted_iota(jnp.int32, (HGRN_CHUNK, HGRN_CHUNK), 1)
    causal = col <= row
    tri = jnp.where(causal, 1.0, 0.0).astype(BF16)
    n_sub = HGRN_CHUNK // HGRN_SUB

    def chunk_body(c, carry):
        rows = pl.ds(pl.multiple_of(c * HGRN_CHUNK, HGRN_CHUNK), HGRN_CHUNK)
        zq = z_ref[rows, 0:HGRN_WIDTH]
        zf = z_ref[rows, HGRN_WIDTH:2 * HGRN_WIDTH]
        zi = z_ref[rows, 2 * HGRN_WIDTH:3 * HGRN_WIDTH]
        zg = z_ref[rows, 3 * HGRN_WIDTH:4 * HGRN_WIDTH]
        f = lb + (1.0 - lb) * jax.nn.sigmoid(zf)
        logf = jnp.log(f)
        kk = 1.0 - f
        p0 = logf.astype(BF16)
        r0 = logf - p0.astype(F32)
        p1 = r0.astype(BF16)
        p2 = (r0 - p1.astype(F32)).astype(BF16)
        b = (_dot(tri, p0) + _dot(tri, p1)) + _dot(tri, p2)
        b_last = b[HGRN_CHUNK - 1:HGRN_CHUNK, :]
        qe = (zq * jnp.exp(b)).astype(BF16)
        kdec = (kk * jnp.exp(b_last - b)).astype(BF16)
        v = zi.astype(BF16)
        dec_last = jnp.exp(b_last)
        a_sub, k_sub = [], []
        for i in range(n_sub):
            lo, hi = i * HGRN_SUB, (i + 1) * HGRN_SUB
            bref = b[lo - 1:lo, :] if i > 0 else jnp.zeros((1, HGRN_WIDTH), F32)
            a_sub.append((zq[lo:hi] * jnp.exp(b[lo:hi] - bref)).astype(BF16))
            k_sub.append((kk * jnp.exp(jnp.minimum(bref - b, DECAY_EXP_CLAMP))).astype(BF16))
        silu_g = zg * jax.nn.sigmoid(zg)
        for hd in range(HGRN_HEADS):
            ls = slice(hd * HEAD_DIM, (hd + 1) * HEAD_DIM)
            sc = jnp.concatenate([_dot_nt(a_sub[i][:, ls], k_sub[i][:, ls]) for i in range(n_sub)], axis=0)
            sc = jnp.where(causal, sc, 0.0).astype(BF16)
            st = st_ref[hd]
            o = _dot_nt(qe[:, ls], st.astype(BF16)) + _dot(sc, v[:, ls])
            st_ref[hd] = st * dec_last[:, ls] + _dot_tn(v[:, ls], kdec[:, ls])
            o = _rms(o) * hg[:, ls]
            y_ref[rows, ls] = (o * silu_g[:, ls]).astype(BF16)
        return carry

    lax.fori_loop(0, TM // HGRN_CHUNK, chunk_body, 0)

    u = _gelu(z_ref[:, 4 * HGRN_WIDTH:4 * HGRN_WIDTH + GMLP_WIDTH])
    gv = _gelu(z_ref[:, 4 * HGRN_WIDTH + GMLP_WIDTH:])
    mu = jnp.mean(gv, axis=-1, keepdims=True)
    gc = gv - mu
    var = jnp.mean(gc * gc, axis=-1, keepdims=True)
    vn = (gc * lax.rsqrt(var + EPS) * lng_ref[...] + lnb_ref[...]).astype(BF16)
    row_g = lax.broadcasted_iota(jnp.int32, (GMLP_CHUNK, GMLP_CHUNK), 0)
    col_g = lax.broadcasted_iota(jnp.int32, (GMLP_CHUNK, GMLP_CHUNK), 1)
    gng = gng_ref[...]
    for g in range(GMLP_GROUPS):
        ls = slice(g * GROUP_DIM, (g + 1) * GROUP_DIM)
        ws_c = jnp.where(col_g <= row_g, ws_ref[g], 0.0).astype(BF16)
        bias = bs_ref[g]
        for n in range(TM // GMLP_CHUNK):
            rs = slice(n * GMLP_CHUNK, (n + 1) * GMLP_CHUNK)
            sv = _dot(ws_c, vn[rs, ls]) + bias
            yy = _rms(u[rs, ls] * sv) * gng[:, ls]
            y_ref[rs, HGRN_WIDTH + g * GROUP_DIM:HGRN_WIDTH + (g + 1) * GROUP_DIM] = yy.astype(BF16)

    x1 = x + mod[2:3] * _dot(y_ref[...], wout_ref[...])
    x1_ref[...] = x1
    h2 = _rms(x1) * g2_ref[...]
    h2 = h2 * (1.0 + mod[4:5]) + mod[3:4]
    h2_ref[...] = h2

    hh = h2.astype(BF16)
    hl = (h2 - hh.astype(F32)).astype(BF16)
    logits = _dot(hh, rwh_ref[...]) + (_dot(hl, rwh_ref[...]) + _dot(hh, rwl_ref[...])) + rb_ref[...]
    lane = lax.broadcasted_iota(jnp.int32, (TM, LANES), 1)
    vals, idxs = [], []
    for _ in range(TOP_K):
        m = jnp.max(logits, axis=-1, keepdims=True)
        idx = jnp.min(jnp.where(logits == m, lane, LANES), axis=-1, keepdims=True)
        vals.append(m)
        idxs.append(idx)
        logits = jnp.where(lane == idx, -jnp.inf, logits)
    es = [jnp.exp(v - vals[0]) for v in vals]
    tot = (es[0] + es[1]) + (es[2] + es[3])

    @pl.when((pl.program_id(0) == 0) & (pl.program_id(1) == 0))
    def _():
        cnt_ref[...] = jnp.zeros_like(cnt_ref)

    hot = [lane == idxs[k] for k in range(TOP_K)]
    picked = jnp.where((hot[0] | hot[1]) | (hot[2] | hot[3]), 1.0, 0.0)
    row_t = lax.broadcasted_iota(jnp.int32, (TM, TM), 0)
    col_t = lax.broadcasted_iota(jnp.int32, (TM, TM), 1)
    before = jnp.where(col_t < row_t, 1.0, 0.0).astype(BF16)
    seen = _dot(before, picked.astype(BF16)) + cnt_ref[0:1, :]
    cnt_ref[...] = cnt_ref[...] + jnp.sum(picked, axis=0, keepdims=True)
    cnt_out_ref[...] = cnt_ref[...]
    route = jnp.zeros((TM, LANES), F32)
    for k in range(TOP_K):
        rank = jnp.sum(jnp.where(hot[k], seen, 0.0), axis=-1, keepdims=True)
        route = jnp.where(lane == k, idxs[k].astype(F32), route)
        route = jnp.where(lane == TOP_K + k, es[k] / tot, route)
        route = jnp.where(lane == 2 * TOP_K + k, rank, route)
    route_ref[...] = route


def _mixer(x, mod8, norm_mix_g, w_in, lb_params, hgrn_norm_g, gmlp_ln_g, gmlp_ln_b, gmlp_ws, gmlp_bs,
           gmlp_norm_g, w_out, norm_ffn_g, router_w, router_b):
    bsz, seq, d = x.shape
    n_in = w_in.shape[1]
    rw = jnp.zeros((d, LANES), F32).at[:, :N_EXPERTS].set(router_w)
    rwh = rw.astype(BF16)
    rwl = (rw - rwh.astype(F32)).astype(BF16)
    rb = jnp.full((1, LANES), -1e30, F32).at[0, :N_EXPERTS].set(router_b)
    const = lambda *shape: pl.BlockSpec(shape, lambda b, i: (0,) * len(shape))
    tile = lambda w: pl.BlockSpec((None, TM, w), lambda b, i: (b, i, 0))
    return pl.pallas_call(
        _mixer_kernel,
        grid=(bsz, seq // TM),
        in_specs=[tile(d),
                  pl.BlockSpec((None, 8, d), lambda b, i: (b, 0, 0)),
                  const(1, d), const(d, n_in), const(2, HGRN_WIDTH), const(1, HGRN_WIDTH),
                  const(1, GMLP_WIDTH), const(1, GMLP_WIDTH),
                  const(GMLP_GROUPS, GMLP_CHUNK, GMLP_CHUNK), const(GMLP_GROUPS, GMLP_CHUNK, 1),
                  const(1, GMLP_WIDTH), const(d, d), const(1, d),
                  const(d, LANES), const(d, LANES), const(1, LANES)],
        out_specs=[tile(d), tile(d), tile(LANES), const(SUBLANES, LANES)],
        out_shape=[jax.ShapeDtypeStruct((bsz, seq, d), F32),
                   jax.ShapeDtypeStruct((bsz, seq, d), F32),
                   jax.ShapeDtypeStruct((bsz, seq, LANES), F32),
                   jax.ShapeDtypeStruct((SUBLANES, LANES), F32)],
        scratch_shapes=[pltpu.VMEM((TM, n_in), F32),
                        pltpu.VMEM((TM, d), BF16),
                        pltpu.VMEM((HGRN_HEADS, HEAD_DIM, HEAD_DIM), F32),
                        pltpu.VMEM((SUBLANES, LANES), F32)],
        compiler_params=pltpu.CompilerParams(dimension_semantics=("arbitrary", "arbitrary"),
                                             vmem_limit_bytes=VMEM_LIMIT),
        name="mixer",
    )(x, mod8, norm_mix_g.reshape(1, d), w_in.astype(BF16), lb_params, hgrn_norm_g.reshape(1, -1),
      gmlp_ln_g.reshape(1, -1), gmlp_ln_b.reshape(1, -1), gmlp_ws, gmlp_bs[:, :, None],
      gmlp_norm_g.reshape(1, -1), w_out.astype(BF16), norm_ffn_g.reshape(1, d), rwh, rwl, rb)


def _tile(r):
    return pl.ds(r * SUBLANES, SUBLANES)


def _tile8(r8):
    return pl.ds(pl.multiple_of(r8, SUBLANES), SUBLANES)


def _chunk(j, n):
    return pl.ds(j, n, stride=SUBLANES)


def _dispatch_kernel(n_pad_step, dest_ref, pad_ref, h2_ref, xs_hbm, rows, zero_tile, sem):
    i = pl.program_id(0)
    n_steps = pl.num_programs(0)

    def wait_step(s):
        for _ in range(TOP_K):
            pltpu.make_async_copy(rows.at[s], xs_hbm.at[pl.ds(0, TD * SUBLANES)], sem.at[s]).wait()
        n = n_pad_step * SUBLANES
        pltpu.make_async_copy(rows.at[s, pl.ds(0, n)], xs_hbm.at[pl.ds(0, n)], sem.at[s]).wait()

    @pl.when(i == 0)
    def _():
        zero_tile[...] = jnp.zeros_like(zero_tile)

    def step(s):
        @pl.when(i >= 2)
        def _():
            wait_step(s)

        for j in range(SUBLANES):
            rows[s, _chunk(j, TD), :] = h2_ref[:, j * LANES:(j + 1) * LANES]
        for t in range(TD):
            for k in range(TOP_K):
                pltpu.make_async_copy(rows.at[s, _tile(t)], xs_hbm.at[_tile8(dest_ref[0, 0, t * TOP_K + k])],
                                      sem.at[s]).start(priority=k % N_DMA_QUEUES)
        for q in range(n_pad_step):
            pltpu.make_async_copy(zero_tile, xs_hbm.at[_tile8(pad_ref[0, 0, q])],
                                  sem.at[s]).start(priority=q % N_DMA_QUEUES)

        @pl.when(i == n_steps - 1)
        def _():
            wait_step(s)

            @pl.when(i >= 1)
            def _():
                wait_step(1 - s)

    _by_parity(i, step)


def _dispatch(h2, dest, pad_dest, p_rows):
    n_tok, d = h2.shape
    n_steps = n_tok // TD
    n_pad_step = pad_dest.shape[0] // n_steps
    return pl.pallas_call(
        functools.partial(_dispatch_kernel, n_pad_step),
        grid=(n_steps,),
        in_specs=[pl.BlockSpec((1, 1, TD * TOP_K), lambda i: (i, 0, 0), memory_space=pltpu.SMEM),
                  pl.BlockSpec((1, 1, n_pad_step), lambda i: (i, 0, 0), memory_space=pltpu.SMEM),
                  pl.BlockSpec((TD, d), lambda i: (i, 0))],
        out_specs=pl.BlockSpec(memory_space=pl.ANY),
        out_shape=jax.ShapeDtypeStruct((p_rows * SUBLANES, LANES), F32),
        scratch_shapes=[pltpu.VMEM((2, TD * SUBLANES, LANES), F32),
                        pltpu.VMEM((SUBLANES, LANES), F32),
                        pltpu.SemaphoreType.DMA((2,))],
        compiler_params=pltpu.CompilerParams(dimension_semantics=("arbitrary",)),
        name="dispatch",
    )(dest.reshape(n_steps, 1, TD * TOP_K), pad_dest.reshape(n_steps, 1, n_pad_step), h2)


def _moe_kernel(be_ref, first_ref, nused_ref, xs_ref, wgu_ref, bgu_ref, wd_ref, bd_ref, ys_ref, wgu_bf, wd_bf):
    i = pl.program_id(0)
    d_ff = wd_ref.shape[0]

    @pl.when(i >= nused_ref[0])
    def _():
        ys_ref[...] = jnp.zeros_like(ys_ref)

    @pl.when(i < nused_ref[0])
    def _():
        @pl.when(first_ref[i] == 1)
        def _():
            wgu_bf[...] = wgu_ref[...].astype(BF16)
            wd_bf[...] = wd_ref[...].astype(BF16)

        xb = jnp.concatenate([xs_ref[_chunk(j, MOE_BLK), :] for j in range(SUBLANES)], axis=-1).astype(BF16)
        gu = _dot(xb, wgu_bf[...]) + bgu_ref[...]
        gate = jnp.minimum(gu[:, :d_ff], SWIGLU_LIMIT)
        up = jnp.clip(gu[:, d_ff:], -SWIGLU_LIMIT, SWIGLU_LIMIT)
        glu = gate * jax.nn.sigmoid(SWIGLU_ALPHA * gate)
        yb = _dot(((up + 1.0) * glu).astype(BF16), wd_bf[...]) + bd_ref[...]
        for j in range(SUBLANES):
            ys_ref[_chunk(j, MOE_BLK), :] = yb[:, j * LANES:(j + 1) * LANES]


def _moe(xs, block_expert, block_first, n_used, w_gate_up, b_gate_up, w_down, b_down):
    n_blocks = xs.shape[0] // (MOE_BLK * SUBLANES)
    d_ff, d = w_down.shape[1], w_down.shape[2]
    grid_spec = pltpu.PrefetchScalarGridSpec(
        num_scalar_prefetch=3,
        grid=(n_blocks,),
        in_specs=[pl.BlockSpec((MOE_BLK * SUBLANES, LANES),
                               lambda i, be, fi, nu: (jnp.minimum(i, nu[0] - 1), 0)),
                  pl.BlockSpec((None, d, 2 * d_ff), lambda i, be, fi, nu: (be[i], 0, 0)),
                  pl.BlockSpec((None, 1, 2 * d_ff), lambda i, be, fi, nu: (be[i], 0, 0)),
                  pl.BlockSpec((None, d_ff, d), lambda i, be, fi, nu: (be[i], 0, 0)),
                  pl.BlockSpec((None, 1, d), lambda i, be, fi, nu: (be[i], 0, 0))],
        out_specs=pl.BlockSpec((MOE_BLK * SUBLANES, LANES), lambda i, be, fi, nu: (i, 0)),
        scratch_shapes=[pltpu.VMEM((d, 2 * d_ff), BF16),
                        pltpu.VMEM((d_ff, d), BF16)],
    )
    return pl.pallas_call(
        _moe_kernel,
        grid_spec=grid_spec,
        out_shape=jax.ShapeDtypeStruct(xs.shape, F32),
        compiler_params=pltpu.CompilerParams(dimension_semantics=("arbitrary",),
                                             vmem_limit_bytes=VMEM_LIMIT),
        name="moe_ffn",
    )(block_expert, block_first, n_used, xs, w_gate_up, b_gate_up[:, None, :], w_down, b_down[:, None, :])


def _combine_kernel(destc_ref, destn_ref, x1_ref, mod_ref, route_ref, fg_ref, ys_hbm, o_ref, buf, sem):
    i = pl.program_id(0)
    n_steps = pl.num_programs(0)
    d = x1_ref.shape[-1]

    def start_gather(dest_ref, s):
        for t in range(TD):
            for k in range(TOP_K):
                pltpu.make_async_copy(ys_hbm.at[_tile8(dest_ref[0, 0, t * TOP_K + k])], buf.at[s, k, _tile(t)],
                                      sem.at[s]).start(priority=k % N_DMA_QUEUES)

    def wait_gather(s):
        for k in range(TOP_K):
            pltpu.make_async_copy(ys_hbm.at[pl.ds(0, TD * SUBLANES)], buf.at[s, k], sem.at[s]).wait()

    @pl.when(i == 0)
    def _():
        start_gather(destc_ref, 0)

    def step(s):
        @pl.when(i + 1 < n_steps)
        def _():
            start_gather(destn_ref, 1 - s)

        wait_gather(s)
        gates = route_ref[...]
        gk = [gates[:, TOP_K + k:TOP_K + k + 1] for k in range(TOP_K)]
        gate2 = mod_ref[5:6, :]
        x2, ss = [], jnp.zeros((TD, 1), F32)
        for j in range(SUBLANES):
            ls = slice(j * LANES, (j + 1) * LANES)
            rj = _chunk(j, TD)
            y = (gk[0] * buf[s, 0, rj, :] + gk[1] * buf[s, 1, rj, :]) + \
                (gk[2] * buf[s, 2, rj, :] + gk[3] * buf[s, 3, rj, :])
            xj = x1_ref[:, ls] + gate2[:, ls] * y
            x2.append(xj)
            ss = ss + jnp.sum(xj * xj, axis=-1, keepdims=True)
        inv = lax.rsqrt(ss / d + EPS)
        for j in range(SUBLANES):
            ls = slice(j * LANES, (j + 1) * LANES)
            o_ref[:, ls] = x2[j] * inv * fg_ref[:, ls]

    _by_parity(i, step)


def _combine(x1, mod8, route, dest, ys, final_g):
    bsz, seq, d = x1.shape
    n_tok = bsz * seq
    n_steps = n_tok // TD
    per_seq = seq // TD
    dest3 = dest.reshape(n_steps, 1, TD * TOP_K)
    smem_blk = lambda f: pl.BlockSpec((1, 1, TD * TOP_K), f, memory_space=pltpu.SMEM)
    return pl.pallas_call(
        _combine_kernel,
        grid=(n_steps,),
        in_specs=[smem_blk(lambda i: (i, 0, 0)),
                  smem_blk(lambda i: (jnp.minimum(i + 1, n_steps - 1), 0, 0)),
                  pl.BlockSpec((TD, d), lambda i: (i, 0)),
                  pl.BlockSpec((None, 8, d), lambda i: (i // per_seq, 0, 0)),
                  pl.BlockSpec((TD, LANES), lambda i: (i, 0)),
                  pl.BlockSpec((1, d), lambda i: (0, 0)),
                  pl.BlockSpec(memory_space=pl.ANY)],
        out_specs=pl.BlockSpec((TD, d), lambda i: (i, 0)),
        out_shape=jax.ShapeDtypeStruct((n_tok, d), F32),
        scratch_shapes=[pltpu.VMEM((2, TOP_K, TD * SUBLANES, LANES), F32),
                        pltpu.SemaphoreType.DMA((2,))],
        compiler_params=pltpu.CompilerParams(dimension_semantics=("arbitrary",),
                                             vmem_limit_bytes=VMEM_LIMIT),
        name="combine",
    )(dest3, dest3, x1.reshape(n_tok, d), mod8, route, final_g.reshape(1, d), ys).reshape(bsz, seq, d)


def _routing(route, counts_f, n_tok):
    nk = n_tok * TOP_K
    experts = jnp.arange(N_EXPERTS, dtype=jnp.int32)
    topi = route[:, :TOP_K].astype(jnp.int32)
    rank = route[:, 2 * TOP_K:3 * TOP_K].astype(jnp.int32)
    counts = counts_f[0, :N_EXPERTS].astype(jnp.int32)
    padded = ((counts + MOE_BLK - 1) // MOE_BLK) * MOE_BLK
    pad_end = jnp.cumsum(padded)
    start_pad = pad_end - padded
    dest = rank + jnp.sum(jnp.where(topi[:, :, None] == experts[None, None, :], start_pad[None, None, :], 0), axis=-1)
    p_rows = ((nk + N_EXPERTS * (MOE_BLK - 1) + MOE_BLK - 1) // MOE_BLK) * MOE_BLK
    n_blocks = p_rows // MOE_BLK
    n_pad = padded - counts
    pad_cum = jnp.cumsum(n_pad)
    j = jnp.arange(p_rows - nk, dtype=jnp.int32)
    e_j = jnp.minimum(jnp.sum((pad_cum[None, :] <= j[:, None]).astype(jnp.int32), axis=1), N_EXPERTS - 1)
    in_expert = j < pad_cum[-1]
    pad_dest = jnp.where(in_expert,
                         start_pad[e_j] + counts[e_j] + (j - (pad_cum[e_j] - n_pad[e_j])),
                         pad_end[-1] + (j - pad_cum[-1])).astype(jnp.int32)
    n_used = (pad_end[-1] // MOE_BLK).astype(jnp.int32)
    blk = jnp.arange(n_blocks, dtype=jnp.int32)
    be = jnp.minimum(jnp.sum((pad_end[None, :] <= (blk * MOE_BLK)[:, None]).astype(jnp.int32), axis=1),
                     N_EXPERTS - 1)
    be = jnp.where(blk < n_used, be, be[n_used - 1])
    first = jnp.concatenate([jnp.ones((1,), jnp.int32), (be[1:] != be[:-1]).astype(jnp.int32)])
    return dest.reshape(-1) * SUBLANES, pad_dest * SUBLANES, be, first, n_used.reshape(1), p_rows


def kernel(x, c, ada_w, ada_b, norm_mix_g, w_in, lb_params, hgrn_norm_g, gmlp_ln_g, gmlp_ln_b, gmlp_ws, gmlp_bs,
           gmlp_norm_g, w_out, norm_ffn_g, router_w, router_b, w_gate_up, b_gate_up, w_down, b_down, final_g):
    assert ada_w.shape[0] == 1, "single-layer block"
    bsz, seq, d = x.shape
    assert d == SUBLANES * LANES and seq % TM == 0 and (bsz * seq) % TD == 0
    n_tok = bsz * seq
    mod = _modulation(c, ada_w[0], ada_b[0])
    mod8 = jnp.zeros((bsz, 8, d), F32).at[:, :6].set(mod.reshape(bsz, 6, d))
    x1, h2, route, counts = _mixer(x, mod8, norm_mix_g[0], w_in[0], lb_params, hgrn_norm_g[0], gmlp_ln_g[0],
                                   gmlp_ln_b[0], gmlp_ws[0], gmlp_bs[0], gmlp_norm_g[0], w_out[0],
                                   norm_ffn_g[0], router_w[0], router_b[0])
    route = route.reshape(n_tok, LANES)
    dest, pad_dest, be, first, n_used, p_rows = _routing(route, counts, n_tok)
    xs = _dispatch(h2.reshape(n_tok, d), dest, pad_dest, p_rows)
    ys = _moe(xs, be, first, n_used, w_gate_up[0], b_gate_up[0], w_down[0], b_down[0])
    return _combine(x1, mod8, route, dest, ys, final_g)
```

```python
import functools

import jax
import jax.numpy as jnp
from jax import lax
from jax.experimental import pallas as pl
from jax.experimental.pallas import tpu as pltpu

F32 = jnp.float32
BF16 = jnp.bfloat16

HGRN_HEADS = 4
HEAD_DIM = 128
HGRN_WIDTH = HGRN_HEADS * HEAD_DIM
HGRN_CHUNK = 64
HGRN_SUB = 16
GMLP_GROUPS = 4
GROUP_DIM = 128
GMLP_WIDTH = GMLP_GROUPS * GROUP_DIM
GMLP_CHUNK = 128
N_EXPERTS = 32
TOP_K = 4
SWIGLU_LIMIT = 7.0
SWIGLU_ALPHA = 1.702
EPS = 1e-6
LANES = 128
SUBLANES = 8
N_DMA_QUEUES = 2
DECAY_EXP_CLAMP = 60.0
TM = 256
MOE_BLK = 256
TD = 256
VMEM_LIMIT = 56 * 1024 * 1024


def _dot(a, b):
    return jnp.dot(a, b, preferred_element_type=F32)


def _dot_nt(a, b):
    return lax.dot_general(a, b, (((1,), (1,)), ((), ())), preferred_element_type=F32)


def _dot_tn(a, b):
    return lax.dot_general(a, b, (((0,), (0,)), ((), ())), preferred_element_type=F32)


def _rms(x):
    return x * lax.rsqrt(jnp.mean(x * x, axis=-1, keepdims=True) + EPS)


def _gelu(x):
    return 0.5 * x * (1.0 + lax.erf(x * 0.7071067811865476))


def _by_parity(i, fn):
    @pl.when(i % 2 == 0)
    def _():
        fn(0)

    @pl.when(i % 2 == 1)
    def _():
        fn(1)


def _mod_kernel(c_ref, w_ref, b_ref, o_ref):
    c = c_ref[...]
    ca = c * jax.nn.sigmoid(c)
    o_ref[...] = jnp.dot(ca, w_ref[...], precision=lax.Precision.HIGHEST,
                         preferred_element_type=F32) + b_ref[...]


def _modulation(c, ada_w, ada_b):
    bsz, d = c.shape
    n_out = ada_w.shape[1]
    rows = 8
    c_pad = jnp.zeros((rows, d), F32).at[:bsz].set(c)
    tn = 1024
    out = pl.pallas_call(
        _mod_kernel,
        grid=(n_out // tn,),
        in_specs=[pl.BlockSpec((rows, d), lambda j: (0, 0)),
                  pl.BlockSpec((d, tn), lambda j: (0, j)),
                  pl.BlockSpec((1, tn), lambda j: (0, j))],
        out_specs=pl.BlockSpec((rows, tn), lambda j: (0, j)),
        out_shape=jax.ShapeDtypeStruct((rows, n_out), F32),
        name="adaln_mod",
    )(c_pad, ada_w, ada_b.reshape(1, n_out))
    return out[:bsz]


def _mixer_kernel(x_ref, mod_ref, g1_ref, win_ref, lbp_ref, hg_ref, lng_ref, lnb_ref, ws_ref, bs_ref,
                  gng_ref, wout_ref, g2_ref, rwh_ref, rwl_ref, rb_ref,
                  x1_ref, h2_ref, route_ref, cnt_out_ref,
                  z_ref, y_ref, st_ref, cnt_ref):
    @pl.when(pl.program_id(1) == 0)
    def _():
        st_ref[...] = jnp.zeros_like(st_ref)

    x = x_ref[...]
    mod = mod_ref[...]
    h = _rms(x) * g1_ref[...]
    h = h * (1.0 + mod[1:2]) + mod[0:1]
    z_ref[...] = _dot(h.astype(BF16), win_ref[...])

    lbp = lbp_ref[...]
    lbe = jnp.exp(lbp - jnp.max(lbp, axis=0, keepdims=True))
    lb = lbe[0:1] / jnp.sum(lbe, axis=0, keepdims=True)
    hg = hg_ref[...]
    row = lax.broadcasted_iota(jnp.int32, (HGRN_CHUNK, HGRN_CHUNK), 0)
    col = lax.broadcasted_iota(jnp.int32, (HGRN_CHUNK, HGRN_CHUNK), 1)
    causal = col <= row
    tri = jnp.where(causal, 1.0, 0.0).astype(BF16)
    n_sub = HGRN_CHUNK // HGRN_SUB

    def chunk_body(c):
        rows = slice(c * HGRN_CHUNK, (c + 1) * HGRN_CHUNK)
        zq = z_ref[rows, 0:HGRN_WIDTH]
        zf = z_ref[rows, HGRN_WIDTH:2 * HGRN_WIDTH]
        zi = z_ref[rows, 2 * HGRN_WIDTH:3 * HGRN_WIDTH]
        zg = z_ref[rows, 3 * HGRN_WIDTH:4 * HGRN_WIDTH]
        f = lb + (1.0 - lb) * jax.nn.sigmoid(zf)
        logf = jnp.log(f)
        kk = 1.0 - f
        p0 = logf.astype(BF16)
        r0 = logf - p0.astype(F32)
        p1 = r0.astype(BF16)
        p2 = (r0 - p1.astype(F32)).astype(BF16)
        b = (_dot(tri, p0) + _dot(tri, p1)) + _dot(tri, p2)
        b_last = b[HGRN_CHUNK - 1:HGRN_CHUNK, :]
        qe = (zq * jnp.exp(b)).astype(BF16)
        kdec = (kk * jnp.exp(b_last - b)).astype(BF16)
        v = zi.astype(BF16)
        dec_last = jnp.exp(b_last)
        a_sub, k_sub = [], []
        for i in range(n_sub):
            lo, hi = i * HGRN_SUB, (i + 1) * HGRN_SUB
            bref = b[lo - 1:lo, :] if i > 0 else jnp.zeros((1, HGRN_WIDTH), F32)
            a_sub.append((zq[lo:hi] * jnp.exp(b[lo:hi] - bref)).astype(BF16))
            k_sub.append((kk * jnp.exp(jnp.minimum(bref - b, DECAY_EXP_CLAMP))).astype(BF16))
        silu_g = zg * jax.nn.sigmoid(zg)
        for hd in range(HGRN_HEADS):
            ls = slice(hd * HEAD_DIM, (hd + 1) * HEAD_DIM)
            sc = jnp.concatenate([_dot_nt(a_sub[i][:, ls], k_sub[i][:, ls]) for i in range(n_sub)], axis=0)
            sc = jnp.where(causal, sc, 0.0).astype(BF16)
            st = st_ref[hd]
            o = _dot_nt(qe[:, ls], st.astype(BF16)) + _dot(sc, v[:, ls])
            st_ref[hd] = st * dec_last[:, ls] + _dot_tn(v[:, ls], kdec[:, ls])
            o = _rms(o) * hg[:, ls]
            y_ref[rows, ls] = (o * silu_g[:, ls]).astype(BF16)

    for c in range(TM // HGRN_CHUNK):
        chunk_body(c)

    u = _gelu(z_ref[:, 4 * HGRN_WIDTH:4 * HGRN_WIDTH + GMLP_WIDTH])
    gv = _gelu(z_ref[:, 4 * HGRN_WIDTH + GMLP_WIDTH:])
    mu = jnp.mean(gv, axis=-1, keepdims=True)
    gc = gv - mu
    var = jnp.mean(gc * gc, axis=-1, keepdims=True)
    vn = (gc * lax.rsqrt(var + EPS) * lng_ref[...] + lnb_ref[...]).astype(BF16)
    row_g = lax.broadcasted_iota(jnp.int32, (GMLP_CHUNK, GMLP_CHUNK), 0)
    col_g = lax.broadcasted_iota(jnp.int32, (GMLP_CHUNK, GMLP_CHUNK), 1)
    gng = gng_ref[...]
    for g in range(GMLP_GROUPS):
        ls = slice(g * GROUP_DIM, (g + 1) * GROUP_DIM)
        ws_c = jnp.where(col_g <= row_g, ws_ref[g], 0.0).astype(BF16)
        bias = bs_ref[g]
        for n in range(TM // GMLP_CHUNK):
            rs = slice(n * GMLP_CHUNK, (n + 1) * GMLP_CHUNK)
            sv = _dot(ws_c, vn[rs, ls]) + bias
            yy = _rms(u[rs, ls] * sv) * gng[:, ls]
            y_ref[rs, HGRN_WIDTH + g * GROUP_DIM:HGRN_WIDTH + (g + 1) * GROUP_DIM] = yy.astype(BF16)

    x1 = x + mod[2:3] * _dot(y_ref[...], wout_ref[...])
    x1_ref[...] = x1
    h2 = _rms(x1) * g2_ref[...]
    h2 = h2 * (1.0 + mod[4:5]) + mod[3:4]
    h2_ref[...] = h2

    hh = h2.astype(BF16)
    hl = (h2 - hh.astype(F32)).astype(BF16)
    logits = _dot(hh, rwh_ref[...]) + (_dot(hl, rwh_ref[...]) + _dot(hh, rwl_ref[...])) + rb_ref[...]
    lane = lax.broadcasted_iota(jnp.int32, (TM, LANES), 1)
    vals, idxs = [], []
    for _ in range(TOP_K):
        m = jnp.max(logits, axis=-1, keepdims=True)
        idx = jnp.min(jnp.where(logits == m, lane, LANES), axis=-1, keepdims=True)
        vals.append(m)
        idxs.append(idx)
        logits = jnp.where(lane == idx, -jnp.inf, logits)
    es = [jnp.exp(v - vals[0]) for v in vals]
    tot = (es[0] + es[1]) + (es[2] + es[3])

    @pl.when((pl.program_id(0) == 0) & (pl.program_id(1) == 0))
    def _():
        cnt_ref[...] = jnp.zeros_like(cnt_ref)

    hot = [lane == idxs[k] for k in range(TOP_K)]
    picked = jnp.where((hot[0] | hot[1]) | (hot[2] | hot[3]), 1.0, 0.0)
    row_t = lax.broadcasted_iota(jnp.int32, (TM, TM), 0)
    col_t = lax.broadcasted_iota(jnp.int32, (TM, TM), 1)
    before = jnp.where(col_t < row_t, 1.0, 0.0).astype(BF16)
    seen = _dot(before, picked.astype(BF16)) + cnt_ref[0:1, :]
    cnt_ref[...] = cnt_ref[...] + jnp.sum(picked, axis=0, keepdims=True)
    cnt_out_ref[...] = cnt_ref[...]
    route = jnp.zeros((TM, LANES), F32)
    for k in range(TOP_K):
        rank = jnp.sum(jnp.where(hot[k], seen, 0.0), axis=-1, keepdims=True)
        route = jnp.where(lane == k, idxs[k].astype(F32), route)
        route = jnp.where(lane == TOP_K + k, es[k] / tot, route)
        route = jnp.where(lane == 2 * TOP_K + k, rank, route)
    route_ref[...] = route


def _mixer(x, mod8, norm_mix_g, w_in, lb_params, hgrn_norm_g, gmlp_ln_g, gmlp_ln_b, gmlp_ws, gmlp_bs,
           gmlp_norm_g, w_out, norm_ffn_g, router_w, router_b):
    bsz, seq, d = x.shape
    n_in = w_in.shape[1]
    rw = jnp.zeros((d, LANES), F32).at[:, :N_EXPERTS].set(router_w)
    rwh = rw.astype(BF16)
    rwl = (rw - rwh.astype(F32)).astype(BF16)
    rb = jnp.full((1, LANES), -1e30, F32).at[0, :N_EXPERTS].set(router_b)
    const = lambda *shape: pl.BlockSpec(shape, lambda b, i: (0,) * len(shape))
    tile = lambda w: pl.BlockSpec((None, TM, w), lambda b, i: (b, i, 0))
    return pl.pallas_call(
        _mixer_kernel,
        grid=(bsz, seq // TM),
        in_specs=[tile(d),
                  pl.BlockSpec((None, 8, d), lambda b, i: (b, 0, 0)),
                  const(1, d), const(d, n_in), const(2, HGRN_WIDTH), const(1, HGRN_WIDTH),
                  const(1, GMLP_WIDTH), const(1, GMLP_WIDTH),
                  const(GMLP_GROUPS, GMLP_CHUNK, GMLP_CHUNK), const(GMLP_GROUPS, GMLP_CHUNK, 1),
                  const(1, GMLP_WIDTH), const(d, d), const(1, d),
                  const(d, LANES), const(d, LANES), const(1, LANES)],
        out_specs=[tile(d), tile(d), tile(LANES), const(SUBLANES, LANES)],
        out_shape=[jax.ShapeDtypeStruct((bsz, seq, d), F32),
                   jax.ShapeDtypeStruct((bsz, seq, d), F32),
                   jax.ShapeDtypeStruct((bsz, seq, LANES), F32),
                   jax.ShapeDtypeStruct((SUBLANES, LANES), F32)],
        scratch_shapes=[pltpu.VMEM((TM, n_in), F32),
                        pltpu.VMEM((TM, d), BF16),
                        pltpu.VMEM((HGRN_HEADS, HEAD_DIM, HEAD_DIM), F32),
                        pltpu.VMEM((SUBLANES, LANES), F32)],
        compiler_params=pltpu.CompilerParams(dimension_semantics=("arbitrary", "arbitrary"),
                                             vmem_limit_bytes=VMEM_LIMIT),
        name="mixer",
    )(x, mod8, norm_mix_g.reshape(1, d), w_in.astype(BF16), lb_params, hgrn_norm_g.reshape(1, -1),
      gmlp_ln_g.reshape(1, -1), gmlp_ln_b.reshape(1, -1), gmlp_ws, gmlp_bs[:, :, None],
      gmlp_norm_g.reshape(1, -1), w_out.astype(BF16), norm_ffn_g.reshape(1, d), rwh, rwl, rb)


def _tile(r):
    return pl.ds(r * SUBLANES, SUBLANES)


def _tile8(r8):
    return pl.ds(pl.multiple_of(r8, SUBLANES), SUBLANES)


def _chunk(j, n):
    return pl.ds(j, n, stride=SUBLANES)


def _dispatch_kernel(n_pad_step, dest_ref, pad_ref, h2_ref, xs_hbm, rows, zero_tile, sem):
    i = pl.program_id(0)
    n_steps = pl.num_programs(0)

    def wait_step(s):
        for _ in range(TOP_K):
            pltpu.make_async_copy(rows.at[s], xs_hbm.at[pl.ds(0, TD * SUBLANES)], sem.at[s]).wait()
        n = n_pad_step * SUBLANES
        pltpu.make_async_copy(rows.at[s, pl.ds(0, n)], xs_hbm.at[pl.ds(0, n)], sem.at[s]).wait()

    @pl.when(i == 0)
    def _():
        zero_tile[...] = jnp.zeros_like(zero_tile)

    def step(s):
        @pl.when(i >= 2)
        def _():
            wait_step(s)

        for j in range(SUBLANES):
            rows[s, _chunk(j, TD), :] = h2_ref[:, j * LANES:(j + 1) * LANES]
        for t in range(TD):
            for k in range(TOP_K):
                pltpu.make_async_copy(rows.at[s, _tile(t)], xs_hbm.at[_tile8(dest_ref[0, 0, t * TOP_K + k])],
                                      sem.at[s]).start(priority=k % N_DMA_QUEUES)
        for q in range(n_pad_step):
            pltpu.make_async_copy(zero_tile, xs_hbm.at[_tile8(pad_ref[0, 0, q])],
                                  sem.at[s]).start(priority=q % N_DMA_QUEUES)

        @pl.when(i == n_steps - 1)
        def _():
            wait_step(s)

            @pl.when(i >= 1)
            def _():
                wait_step(1 - s)

    _by_parity(i, step)


def _dispatch(h2, dest, pad_dest, p_rows):
    n_tok, d = h2.shape
    n_steps = n_tok // TD
    n_pad_step = pad_dest.shape[0] // n_steps
    return pl.pallas_call(
        functools.partial(_dispatch_kernel, n_pad_step),
        grid=(n_steps,),
        in_specs=[pl.BlockSpec((1, 1, TD * TOP_K), lambda i: (i, 0, 0), memory_space=pltpu.SMEM),
                  pl.BlockSpec((1, 1, n_pad_step), lambda i: (i, 0, 0), memory_space=pltpu.SMEM),
                  pl.BlockSpec((TD, d), lambda i: (i, 0))],
        out_specs=pl.BlockSpec(memory_space=pl.ANY),
        out_shape=jax.ShapeDtypeStruct((p_rows * SUBLANES, LANES), F32),
        scratch_shapes=[pltpu.VMEM((2, TD * SUBLANES, LANES), F32),
                        pltpu.VMEM((SUBLANES, LANES), F32),
                        pltpu.SemaphoreType.DMA((2,))],
        compiler_params=pltpu.CompilerParams(dimension_semantics=("arbitrary",)),
        name="dispatch",
    )(dest.reshape(n_steps, 1, TD * TOP_K), pad_dest.reshape(n_steps, 1, n_pad_step), h2)


def _moe_kernel(n_blocks, blk0_ref, nblk_ref, xs_hbm, wgu_ref, bgu_ref, wd_ref, bd_ref, ys_hbm,
                xbuf, ybuf, wgu_bf, wd_bf, isem, osem):
    e = pl.program_id(0)
    nb = nblk_ref[e]
    b0 = blk0_ref[e]
    d_ff = wd_ref.shape[0]
    blk_rows = MOE_BLK * SUBLANES

    def block(b):
        return pl.ds(pl.multiple_of((b0 + b) * blk_rows, blk_rows), blk_rows)

    def x_copy(b, s):
        return pltpu.make_async_copy(xs_hbm.at[block(b)], xbuf.at[s], isem.at[s])

    def y_copy(b, s):
        return pltpu.make_async_copy(ybuf.at[s], ys_hbm.at[block(b)], osem.at[s])

    @pl.when(nb > 0)
    def _():
        x_copy(0, 0).start()
        wgu_bf[...] = wgu_ref[...].astype(BF16)
        wd_bf[...] = wd_ref[...].astype(BF16)

    def body(b, carry):
        s = b % 2

        @pl.when(b + 1 < nb)
        def _():
            x_copy(b + 1, 1 - s).start()

        x_copy(b, s).wait()

        @pl.when(b >= 2)
        def _():
            y_copy(b - 2, s).wait()

        xb = jnp.concatenate([xbuf[s, _chunk(j, MOE_BLK), :] for j in range(SUBLANES)], axis=-1).astype(BF16)
        gu = _dot(xb, wgu_bf[...]) + bgu_ref[...]
        gate = jnp.minimum(gu[:, :d_ff], SWIGLU_LIMIT)
        up = jnp.clip(gu[:, d_ff:], -SWIGLU_LIMIT, SWIGLU_LIMIT)
        glu = gate * jax.nn.sigmoid(SWIGLU_ALPHA * gate)
        yb = _dot(((up + 1.0) * glu).astype(BF16), wd_bf[...]) + bd_ref[...]
        for j in range(SUBLANES):
            ybuf[s, _chunk(j, MOE_BLK), :] = yb[:, j * LANES:(j + 1) * LANES]
        y_copy(b, s).start()
        return carry

    lax.fori_loop(0, nb, body, 0)

    @pl.when(nb >= 1)
    def _():
        y_copy(nb - 1, (nb - 1) % 2).wait()

    @pl.when(nb >= 2)
    def _():
        y_copy(nb - 2, nb % 2).wait()

    @pl.when(e == pl.num_programs(0) - 1)
    def _():
        ybuf[0] = jnp.zeros_like(ybuf[0])

        def fill(b, carry):
            cp = y_copy(b, 0)
            cp.start()
            cp.wait()
            return carry

        lax.fori_loop(nb, n_blocks - b0, fill, 0)


def _moe(xs, blk0, nblk, w_gate_up, b_gate_up, w_down, b_down):
    n_blocks = xs.shape[0] // (MOE_BLK * SUBLANES)
    n_exp, d_ff, d = w_down.shape
    grid_spec = pltpu.PrefetchScalarGridSpec(
        num_scalar_prefetch=2,
        grid=(n_exp,),
        in_specs=[pl.BlockSpec(memory_space=pl.ANY),
                  pl.BlockSpec((None, d, 2 * d_ff), lambda e, b0, nb: (e, 0, 0)),
                  pl.BlockSpec((None, 1, 2 * d_ff), lambda e, b0, nb: (e, 0, 0)),
                  pl.BlockSpec((None, d_ff, d), lambda e, b0, nb: (e, 0, 0)),
                  pl.BlockSpec((None, 1, d), lambda e, b0, nb: (e, 0, 0))],
        out_specs=pl.BlockSpec(memory_space=pl.ANY),
        scratch_shapes=[pltpu.VMEM((2, MOE_BLK * SUBLANES, LANES), F32),
                        pltpu.VMEM((2, MOE_BLK * SUBLANES, LANES), F32),
                        pltpu.VMEM((d, 2 * d_ff), BF16),
                        pltpu.VMEM((d_ff, d), BF16),
                        pltpu.SemaphoreType.DMA((2,)),
                        pltpu.SemaphoreType.DMA((2,))],
    )
    return pl.pallas_call(
        functools.partial(_moe_kernel, n_blocks),
        grid_spec=grid_spec,
        out_shape=jax.ShapeDtypeStruct(xs.shape, F32),
        compiler_params=pltpu.CompilerParams(dimension_semantics=("arbitrary",),
                                             vmem_limit_bytes=VMEM_LIMIT),
        name="moe_ffn",
    )(blk0, nblk, xs, w_gate_up, b_gate_up[:, None, :], w_down, b_down[:, None, :])


def _combine_kernel(destc_ref, destn_ref, x1_ref, mod_ref, route_ref, fg_ref, ys_hbm, o_ref, buf, sem):
    i = pl.program_id(0)
    n_steps = pl.num_programs(0)
    d = x1_ref.shape[-1]

    def start_gather(dest_ref, s):
        for t in range(TD):
            for k in range(TOP_K):
                pltpu.make_async_copy(ys_hbm.at[_tile8(dest_ref[0, 0, t * TOP_K + k])], buf.at[s, k, _tile(t)],
                                      sem.at[s]).start(priority=k % N_DMA_QUEUES)

    def wait_gather(s):
        for k in range(TOP_K):
            pltpu.make_async_copy(ys_hbm.at[pl.ds(0, TD * SUBLANES)], buf.at[s, k], sem.at[s]).wait()

    @pl.when(i == 0)
    def _():
        start_gather(destc_ref, 0)

    def step(s):
        @pl.when(i + 1 < n_steps)
        def _():
            start_gather(destn_ref, 1 - s)

        wait_gather(s)
        gates = route_ref[...]
        gk = [gates[:, TOP_K + k:TOP_K + k + 1] for k in range(TOP_K)]
        gate2 = mod_ref[5:6, :]
        x2, ss = [], jnp.zeros((TD, 1), F32)
        for j in range(SUBLANES):
            ls = slice(j * LANES, (j + 1) * LANES)
            rj = _chunk(j, TD)
            y = (gk[0] * buf[s, 0, rj, :] + gk[1] * buf[s, 1, rj, :]) + \
                (gk[2] * buf[s, 2, rj, :] + gk[3] * buf[s, 3, rj, :])
            xj = x1_ref[:, ls] + gate2[:, ls] * y
            x2.append(xj)
            ss = ss + jnp.sum(xj * xj, axis=-1, keepdims=True)
        inv = lax.rsqrt(ss / d + EPS)
        for j in range(SUBLANES):
            ls = slice(j * LANES, (j + 1) * LANES)
            o_ref[:, ls] = x2[j] * inv * fg_ref[:, ls]

    _by_parity(i, step)


def _combine(x1, mod8, route, dest, ys, final_g):
    bsz, seq, d = x1.shape
    n_tok = bsz * seq
    n_steps = n_tok // TD
    per_seq = seq // TD
    dest3 = dest.reshape(n_steps, 1, TD * TOP_K)
    smem_blk = lambda f: pl.BlockSpec((1, 1, TD * TOP_K), f, memory_space=pltpu.SMEM)
    return pl.pallas_call(
        _combine_kernel,
        grid=(n_steps,),
        in_specs=[smem_blk(lambda i: (i, 0, 0)),
                  smem_blk(lambda i: (jnp.minimum(i + 1, n_steps - 1), 0, 0)),
                  pl.BlockSpec((TD, d), lambda i: (i, 0)),
                  pl.BlockSpec((None, 8, d), lambda i: (i // per_seq, 0, 0)),
                  pl.BlockSpec((TD, LANES), lambda i: (i, 0)),
                  pl.BlockSpec((1, d), lambda i: (0, 0)),
                  pl.BlockSpec(memory_space=pl.ANY)],
        out_specs=pl.BlockSpec((TD, d), lambda i: (i, 0)),
        out_shape=jax.ShapeDtypeStruct((n_tok, d), F32),
        scratch_shapes=[pltpu.VMEM((2, TOP_K, TD * SUBLANES, LANES), F32),
                        pltpu.SemaphoreType.DMA((2,))],
        compiler_params=pltpu.CompilerParams(dimension_semantics=("arbitrary",),
                                             vmem_limit_bytes=VMEM_LIMIT),
        name="combine",
    )(dest3, dest3, x1.reshape(n_tok, d), mod8, route, final_g.reshape(1, d), ys).reshape(bsz, seq, d)


def _routing(route, counts_f, n_tok):
    nk = n_tok * TOP_K
    experts = jnp.arange(N_EXPERTS, dtype=jnp.int32)
    topi = route[:, :TOP_K].astype(jnp.int32)
    rank = route[:, 2 * TOP_K:3 * TOP_K].astype(jnp.int32)
    counts = counts_f[0, :N_EXPERTS].astype(jnp.int32)
    padded = ((counts + MOE_BLK - 1) // MOE_BLK) * MOE_BLK
    pad_end = jnp.cumsum(padded)
    start_pad = pad_end - padded
    dest = rank + jnp.sum(jnp.where(topi[:, :, None] == experts[None, None, :], start_pad[None, None, :], 0), axis=-1)
    p_rows = ((nk + N_EXPERTS * (MOE_BLK - 1) + MOE_BLK - 1) // MOE_BLK) * MOE_BLK
    n_pad = padded - counts
    pad_cum = jnp.cumsum(n_pad)
    j = jnp.arange(p_rows - nk, dtype=jnp.int32)
    e_j = jnp.minimum(jnp.sum((pad_cum[None, :] <= j[:, None]).astype(jnp.int32), axis=1), N_EXPERTS - 1)
    in_expert = j < pad_cum[-1]
    pad_dest = jnp.where(in_expert,
                         start_pad[e_j] + counts[e_j] + (j - (pad_cum[e_j] - n_pad[e_j])),
                         pad_end[-1] + (j - pad_cum[-1])).astype(jnp.int32)
    return (dest.reshape(-1) * SUBLANES, pad_dest * SUBLANES, (start_pad // MOE_BLK).astype(jnp.int32),
            (padded // MOE_BLK).astype(jnp.int32), p_rows)


def kernel(x, c, ada_w, ada_b, norm_mix_g, w_in, lb_params, hgrn_norm_g, gmlp_ln_g, gmlp_ln_b, gmlp_ws, gmlp_bs,
           gmlp_norm_g, w_out, norm_ffn_g, router_w, router_b, w_gate_up, b_gate_up, w_down, b_down, final_g):
    assert ada_w.shape[0] == 1, "single-layer block"
    bsz, seq, d = x.shape
    assert d == SUBLANES * LANES and seq % TM == 0 and (bsz * seq) % TD == 0
    n_tok = bsz * seq
    mod = _modulation(c, ada_w[0], ada_b[0])
    mod8 = jnp.zeros((bsz, 8, d), F32).at[:, :6].set(mod.reshape(bsz, 6, d))
    x1, h2, route, counts = _mixer(x, mod8, norm_mix_g[0], w_in[0], lb_params, hgrn_norm_g[0], gmlp_ln_g[0],
                                   gmlp_ln_b[0], gmlp_ws[0], gmlp_bs[0], gmlp_norm_g[0], w_out[0],
                                   norm_ffn_g[0], router_w[0], router_b[0])
    route = route.reshape(n_tok, LANES)
    dest, pad_dest, blk0, nblk, p_rows = _routing(route, counts, n_tok)
    xs = _dispatch(h2.reshape(n_tok, d), dest, pad_dest, p_rows)
    ys = _moe(xs, blk0, nblk, w_gate_up[0], b_gate_up[0], w_down[0], b_down[0])
    return _combine(x1, mod8, route, dest, ys, final_g)
```

```python
import functools

import jax
import jax.numpy as jnp
from jax import lax
from jax.experimental import pallas as pl
from jax.experimental.pallas import tpu as pltpu

F32 = jnp.float32
BF16 = jnp.bfloat16

HGRN_HEADS = 4
HEAD_DIM = 128
HGRN_WIDTH = HGRN_HEADS * HEAD_DIM
HGRN_CHUNK = 64
HGRN_SUB = 16
GMLP_GROUPS = 4
GROUP_DIM = 128
GMLP_WIDTH = GMLP_GROUPS * GROUP_DIM
GMLP_CHUNK = 128
N_EXPERTS = 32
TOP_K = 4
SWIGLU_LIMIT = 7.0
SWIGLU_ALPHA = 1.702
EPS = 1e-6
LANES = 128
SUBLANES = 8
N_DMA_QUEUES = 2
BLOCK_DMA_PRIORITY = 1
DECAY_EXP_CLAMP = 60.0
TM = 256
MOE_BLK = 256
TD = 256
VMEM_LIMIT = 56 * 1024 * 1024


def _dot(a, b):
    return jnp.dot(a, b, preferred_element_type=F32)


def _dot_nt(a, b):
    return lax.dot_general(a, b, (((1,), (1,)), ((), ())), preferred_element_type=F32)


def _dot_tn(a, b):
    return lax.dot_general(a, b, (((0,), (0,)), ((), ())), preferred_element_type=F32)


def _rms(x):
    return x * lax.rsqrt(jnp.mean(x * x, axis=-1, keepdims=True) + EPS)


def _gelu(x):
    return 0.5 * x * (1.0 + lax.erf(x * 0.7071067811865476))


def _by_parity(i, fn):
    @pl.when(i % 2 == 0)
    def _():
        fn(0)

    @pl.when(i % 2 == 1)
    def _():
        fn(1)


def _mod_kernel(c_ref, w_ref, b_ref, o_ref):
    c = c_ref[...]
    ca = c * jax.nn.sigmoid(c)
    o_ref[...] = jnp.dot(ca, w_ref[...], precision=lax.Precision.HIGHEST,
                         preferred_element_type=F32) + b_ref[...]


def _modulation(c, ada_w, ada_b):
    bsz, d = c.shape
    n_out = ada_w.shape[1]
    rows = 8
    c_pad = jnp.zeros((rows, d), F32).at[:bsz].set(c)
    tn = 1024
    out = pl.pallas_call(
        _mod_kernel,
        grid=(n_out // tn,),
        in_specs=[pl.BlockSpec((rows, d), lambda j: (0, 0)),
                  pl.BlockSpec((d, tn), lambda j: (0, j)),
                  pl.BlockSpec((1, tn), lambda j: (0, j))],
        out_specs=pl.BlockSpec((rows, tn), lambda j: (0, j)),
        out_shape=jax.ShapeDtypeStruct((rows, n_out), F32),
        name="adaln_mod",
    )(c_pad, ada_w, ada_b.reshape(1, n_out))
    return out[:bsz]


def _mixer_kernel(x_ref, mod_ref, g1_ref, win_ref, lbp_ref, hg_ref, lng_ref, lnb_ref, ws_ref, bs_ref,
                  gng_ref, wout_ref, g2_ref, rwh_ref, rwl_ref, rb_ref,
                  x1_ref, h2_ref, route_ref, cnt_out_ref,
                  z_ref, y_ref, st_ref, cnt_ref):
    @pl.when(pl.program_id(1) == 0)
    def _():
        st_ref[...] = jnp.zeros_like(st_ref)

    x = x_ref[...]
    mod = mod_ref[...]
    h = _rms(x) * g1_ref[...]
    h = h * (1.0 + mod[1:2]) + mod[0:1]
    z_ref[...] = _dot(h.astype(BF16), win_ref[...])

    lbp = lbp_ref[...]
    lbe = jnp.exp(lbp - jnp.max(lbp, axis=0, keepdims=True))
    lb = lbe[0:1] / jnp.sum(lbe, axis=0, keepdims=True)
    hg = hg_ref[...]
    row = lax.broadcasted_iota(jnp.int32, (HGRN_CHUNK, HGRN_CHUNK), 0)
    col = lax.broadcasted_iota(jnp.int32, (HGRN_CHUNK, HGRN_CHUNK), 1)
    causal = col <= row
    tri = jnp.where(causal, 1.0, 0.0).astype(BF16)
    n_sub = HGRN_CHUNK // HGRN_SUB

    def chunk_body(c):
        rows = slice(c * HGRN_CHUNK, (c + 1) * HGRN_CHUNK)
        zq = z_ref[rows, 0:HGRN_WIDTH]
        zf = z_ref[rows, HGRN_WIDTH:2 * HGRN_WIDTH]
        zi = z_ref[rows, 2 * HGRN_WIDTH:3 * HGRN_WIDTH]
        zg = z_ref[rows, 3 * HGRN_WIDTH:4 * HGRN_WIDTH]
        f = lb + (1.0 - lb) * jax.nn.sigmoid(zf)
        logf = jnp.log(f)
        kk = 1.0 - f
        p0 = logf.astype(BF16)
        r0 = logf - p0.astype(F32)
        p1 = r0.astype(BF16)
        p2 = (r0 - p1.astype(F32)).astype(BF16)
        b = (_dot(tri, p0) + _dot(tri, p1)) + _dot(tri, p2)
        b_last = b[HGRN_CHUNK - 1:HGRN_CHUNK, :]
        qe = (zq * jnp.exp(b)).astype(BF16)
        kdec = (kk * jnp.exp(b_last - b)).astype(BF16)
        v = zi.astype(BF16)
        dec_last = jnp.exp(b_last)
        a_sub, k_sub = [], []
        for i in range(n_sub):
            lo, hi = i * HGRN_SUB, (i + 1) * HGRN_SUB
            bref = b[lo - 1:lo, :] if i > 0 else jnp.zeros((1, HGRN_WIDTH), F32)
            a_sub.append((zq[lo:hi] * jnp.exp(b[lo:hi] - bref)).astype(BF16))
            k_sub.append((kk * jnp.exp(jnp.minimum(bref - b, DECAY_EXP_CLAMP))).astype(BF16))
        silu_g = zg * jax.nn.sigmoid(zg)
        for hd in range(HGRN_HEADS):
            ls = slice(hd * HEAD_DIM, (hd + 1) * HEAD_DIM)
            sc = jnp.concatenate([_dot_nt(a_sub[i][:, ls], k_sub[i][:, ls]) for i in range(n_sub)], axis=0)
            sc = jnp.where(causal, sc, 0.0).astype(BF16)
            st = st_ref[hd]
            o = _dot_nt(qe[:, ls], st.astype(BF16)) + _dot(sc, v[:, ls])
            st_ref[hd] = st * dec_last[:, ls] + _dot_tn(v[:, ls], kdec[:, ls])
            o = _rms(o) * hg[:, ls]
            y_ref[rows, ls] = (o * silu_g[:, ls]).astype(BF16)

    for c in range(TM // HGRN_CHUNK):
        chunk_body(c)

    u = _gelu(z_ref[:, 4 * HGRN_WIDTH:4 * HGRN_WIDTH + GMLP_WIDTH])
    gv = _gelu(z_ref[:, 4 * HGRN_WIDTH + GMLP_WIDTH:])
    mu = jnp.mean(gv, axis=-1, keepdims=True)
    gc = gv - mu
    var = jnp.mean(gc * gc, axis=-1, keepdims=True)
    vn = (gc * lax.rsqrt(var + EPS) * lng_ref[...] + lnb_ref[...]).astype(BF16)
    row_g = lax.broadcasted_iota(jnp.int32, (GMLP_CHUNK, GMLP_CHUNK), 0)
    col_g = lax.broadcasted_iota(jnp.int32, (GMLP_CHUNK, GMLP_CHUNK), 1)
    gng = gng_ref[...]
    for g in range(GMLP_GROUPS):
        ls = slice(g * GROUP_DIM, (g + 1) * GROUP_DIM)
        ws_c = jnp.where(col_g <= row_g, ws_ref[g], 0.0).astype(BF16)
        bias = bs_ref[g]
        for n in range(TM // GMLP_CHUNK):
            rs = slice(n * GMLP_CHUNK, (n + 1) * GMLP_CHUNK)
            sv = _dot(ws_c, vn[rs, ls]) + bias
            yy = _rms(u[rs, ls] * sv) * gng[:, ls]
            y_ref[rs, HGRN_WIDTH + g * GROUP_DIM:HGRN_WIDTH + (g + 1) * GROUP_DIM] = yy.astype(BF16)

    x1 = x + mod[2:3] * _dot(y_ref[...], wout_ref[...])
    x1_ref[...] = x1
    h2 = _rms(x1) * g2_ref[...]
    h2 = h2 * (1.0 + mod[4:5]) + mod[3:4]
    h2_ref[...] = h2

    hh = h2.astype(BF16)
    hl = (h2 - hh.astype(F32)).astype(BF16)
    logits = _dot(hh, rwh_ref[...]) + (_dot(hl, rwh_ref[...]) + _dot(hh, rwl_ref[...])) + rb_ref[...]
    lane = lax.broadcasted_iota(jnp.int32, (TM, LANES), 1)
    vals, idxs = [], []
    for _ in range(TOP_K):
        m = jnp.max(logits, axis=-1, keepdims=True)
        idx = jnp.min(jnp.where(logits == m, lane, LANES), axis=-1, keepdims=True)
        vals.append(m)
        idxs.append(idx)
        logits = jnp.where(lane == idx, -jnp.inf, logits)
    es = [jnp.exp(v - vals[0]) for v in vals]
    tot = (es[0] + es[1]) + (es[2] + es[3])

    @pl.when((pl.program_id(0) == 0) & (pl.program_id(1) == 0))
    def _():
        cnt_ref[...] = jnp.zeros_like(cnt_ref)

    hot = [lane == idxs[k] for k in range(TOP_K)]
    picked = jnp.where((hot[0] | hot[1]) | (hot[2] | hot[3]), 1.0, 0.0)
    row_t = lax.broadcasted_iota(jnp.int32, (TM, TM), 0)
    col_t = lax.broadcasted_iota(jnp.int32, (TM, TM), 1)
    before = jnp.where(col_t < row_t, 1.0, 0.0).astype(BF16)
    seen = _dot(before, picked.astype(BF16)) + cnt_ref[0:1, :]
    cnt_ref[...] = cnt_ref[...] + jnp.sum(picked, axis=0, keepdims=True)
    cnt_out_ref[...] = cnt_ref[...]
    route = jnp.zeros((TM, LANES), F32)
    for k in range(TOP_K):
        rank = jnp.sum(jnp.where(hot[k], seen, 0.0), axis=-1, keepdims=True)
        route = jnp.where(lane == k, idxs[k].astype(F32), route)
        route = jnp.where(lane == TOP_K + k, es[k] / tot, route)
        route = jnp.where(lane == 2 * TOP_K + k, rank, route)
    route_ref[...] = route


def _mixer(x, mod8, norm_mix_g, w_in, lb_params, hgrn_norm_g, gmlp_ln_g, gmlp_ln_b, gmlp_ws, gmlp_bs,
           gmlp_norm_g, w_out, norm_ffn_g, router_w, router_b):
    bsz, seq, d = x.shape
    n_in = w_in.shape[1]
    rw = jnp.zeros((d, LANES), F32).at[:, :N_EXPERTS].set(router_w)
    rwh = rw.astype(BF16)
    rwl = (rw - rwh.astype(F32)).astype(BF16)
    rb = jnp.full((1, LANES), -1e30, F32).at[0, :N_EXPERTS].set(router_b)
    const = lambda *shape: pl.BlockSpec(shape, lambda b, i: (0,) * len(shape))
    tile = lambda w: pl.BlockSpec((None, TM, w), lambda b, i: (b, i, 0))
    return pl.pallas_call(
        _mixer_kernel,
        grid=(bsz, seq // TM),
        in_specs=[tile(d),
                  pl.BlockSpec((None, 8, d), lambda b, i: (b, 0, 0)),
                  const(1, d), const(d, n_in), const(2, HGRN_WIDTH), const(1, HGRN_WIDTH),
                  const(1, GMLP_WIDTH), const(1, GMLP_WIDTH),
                  const(GMLP_GROUPS, GMLP_CHUNK, GMLP_CHUNK), const(GMLP_GROUPS, GMLP_CHUNK, 1),
                  const(1, GMLP_WIDTH), const(d, d), const(1, d),
                  const(d, LANES), const(d, LANES), const(1, LANES)],
        out_specs=[tile(d), tile(d), tile(LANES), const(SUBLANES, LANES)],
        out_shape=[jax.ShapeDtypeStruct((bsz, seq, d), F32),
                   jax.ShapeDtypeStruct((bsz, seq, d), F32),
                   jax.ShapeDtypeStruct((bsz, seq, LANES), F32),
                   jax.ShapeDtypeStruct((SUBLANES, LANES), F32)],
        scratch_shapes=[pltpu.VMEM((TM, n_in), F32),
                        pltpu.VMEM((TM, d), BF16),
                        pltpu.VMEM((HGRN_HEADS, HEAD_DIM, HEAD_DIM), F32),
                        pltpu.VMEM((SUBLANES, LANES), F32)],
        compiler_params=pltpu.CompilerParams(dimension_semantics=("arbitrary", "arbitrary"),
                                             vmem_limit_bytes=VMEM_LIMIT),
        name="mixer",
    )(x, mod8, norm_mix_g.reshape(1, d), w_in.astype(BF16), lb_params, hgrn_norm_g.reshape(1, -1),
      gmlp_ln_g.reshape(1, -1), gmlp_ln_b.reshape(1, -1), gmlp_ws, gmlp_bs[:, :, None],
      gmlp_norm_g.reshape(1, -1), w_out.astype(BF16), norm_ffn_g.reshape(1, d), rwh, rwl, rb)


def _tile(r):
    return pl.ds(r * SUBLANES, SUBLANES)


def _tile8(r8):
    return pl.ds(pl.multiple_of(r8, SUBLANES), SUBLANES)


def _chunk(j, n):
    return pl.ds(j, n, stride=SUBLANES)


def _dispatch_kernel(n_pad_step, dest_ref, pad_ref, h2_ref, xs_hbm, rows, zero_tile, sem):
    i = pl.program_id(0)
    n_steps = pl.num_programs(0)

    def wait_step(s):
        for _ in range(TOP_K):
            pltpu.make_async_copy(rows.at[s], xs_hbm.at[pl.ds(0, TD * SUBLANES)], sem.at[s]).wait()
        n = n_pad_step * SUBLANES
        pltpu.make_async_copy(rows.at[s, pl.ds(0, n)], xs_hbm.at[pl.ds(0, n)], sem.at[s]).wait()

    @pl.when(i == 0)
    def _():
        zero_tile[...] = jnp.zeros_like(zero_tile)

    def step(s):
        @pl.when(i >= 2)
        def _():
            wait_step(s)

        for j in range(SUBLANES):
            rows[s, _chunk(j, TD), :] = h2_ref[:, j * LANES:(j + 1) * LANES]
        for t in range(TD):
            for k in range(TOP_K):
                pltpu.make_async_copy(rows.at[s, _tile(t)], xs_hbm.at[_tile8(dest_ref[0, 0, t * TOP_K + k])],
                                      sem.at[s]).start(priority=k % N_DMA_QUEUES)
        for q in range(n_pad_step):
            pltpu.make_async_copy(zero_tile, xs_hbm.at[_tile8(pad_ref[0, 0, q])],
                                  sem.at[s]).start(priority=q % N_DMA_QUEUES)

        @pl.when(i == n_steps - 1)
        def _():
            wait_step(s)

            @pl.when(i >= 1)
            def _():
                wait_step(1 - s)

    _by_parity(i, step)


def _dispatch(h2, dest, pad_dest, p_rows):
    n_tok, d = h2.shape
    n_steps = n_tok // TD
    n_pad_step = pad_dest.shape[0] // n_steps
    return pl.pallas_call(
        functools.partial(_dispatch_kernel, n_pad_step),
        grid=(n_steps,),
        in_specs=[pl.BlockSpec((1, 1, TD * TOP_K), lambda i: (i, 0, 0), memory_space=pltpu.SMEM),
                  pl.BlockSpec((1, 1, n_pad_step), lambda i: (i, 0, 0), memory_space=pltpu.SMEM),
                  pl.BlockSpec((TD, d), lambda i: (i, 0))],
        out_specs=pl.BlockSpec(memory_space=pl.ANY),
        out_shape=jax.ShapeDtypeStruct((p_rows * SUBLANES, LANES), F32),
        scratch_shapes=[pltpu.VMEM((2, TD * SUBLANES, LANES), F32),
                        pltpu.VMEM((SUBLANES, LANES), F32),
                        pltpu.SemaphoreType.DMA((2,))],
        compiler_params=pltpu.CompilerParams(dimension_semantics=("arbitrary",)),
        name="dispatch",
    )(dest.reshape(n_steps, 1, TD * TOP_K), pad_dest.reshape(n_steps, 1, n_pad_step), h2)


def _moe_kernel(n_blocks, blk0_ref, nblk_ref, xs_hbm, wgu_ref, bgu_ref, wd_ref, bd_ref, ys_hbm,
                xbuf, ybuf, wgu_bf, wd_bf, isem, osem):
    e = pl.program_id(0)
    nb = nblk_ref[e]
    b0 = blk0_ref[e]
    d_ff = wd_ref.shape[0]
    blk_rows = MOE_BLK * SUBLANES

    def block(b):
        return pl.ds(pl.multiple_of((b0 + b) * blk_rows, blk_rows), blk_rows)

    def x_copy(b, s):
        return pltpu.make_async_copy(xs_hbm.at[block(b)], xbuf.at[s], isem.at[s])

    def y_copy(b, s):
        return pltpu.make_async_copy(ybuf.at[s], ys_hbm.at[block(b)], osem.at[s])

    @pl.when(nb > 0)
    def _():
        x_copy(0, 0).start(priority=BLOCK_DMA_PRIORITY)
        wgu_bf[...] = wgu_ref[...].astype(BF16)
        wd_bf[...] = wd_ref[...].astype(BF16)

    def body(b, carry):
        s = b % 2

        @pl.when(b + 1 < nb)
        def _():
            x_copy(b + 1, 1 - s).start(priority=BLOCK_DMA_PRIORITY)

        x_copy(b, s).wait()

        @pl.when(b >= 2)
        def _():
            y_copy(b - 2, s).wait()

        xb = jnp.concatenate([xbuf[s, _chunk(j, MOE_BLK), :] for j in range(SUBLANES)], axis=-1).astype(BF16)
        gu = _dot(xb, wgu_bf[...]) + bgu_ref[...]
        gate = jnp.minimum(gu[:, :d_ff], SWIGLU_LIMIT)
        up = jnp.clip(gu[:, d_ff:], -SWIGLU_LIMIT, SWIGLU_LIMIT)
        glu = gate * jax.nn.sigmoid(SWIGLU_ALPHA * gate)
        yb = _dot(((up + 1.0) * glu).astype(BF16), wd_bf[...]) + bd_ref[...]
        for j in range(SUBLANES):
            ybuf[s, _chunk(j, MOE_BLK), :] = yb[:, j * LANES:(j + 1) * LANES]
        y_copy(b, s).start(priority=BLOCK_DMA_PRIORITY)
        return carry

    lax.fori_loop(0, nb, body, 0)

    @pl.when(nb >= 1)
    def _():
        y_copy(nb - 1, (nb - 1) % 2).wait()

    @pl.when(nb >= 2)
    def _():
        y_copy(nb - 2, nb % 2).wait()

    @pl.when(e == pl.num_programs(0) - 1)
    def _():
        ybuf[0] = jnp.zeros_like(ybuf[0])

        def fill(b, carry):
            cp = y_copy(b, 0)
            cp.start()
            cp.wait()
            return carry

        lax.fori_loop(nb, n_blocks - b0, fill, 0)


def _moe(xs, blk0, nblk, w_gate_up, b_gate_up, w_down, b_down):
    n_blocks = xs.shape[0] // (MOE_BLK * SUBLANES)
    n_exp, d_ff, d = w_down.shape
    grid_spec = pltpu.PrefetchScalarGridSpec(
        num_scalar_prefetch=2,
        grid=(n_exp,),
        in_specs=[pl.BlockSpec(memory_space=pl.ANY),
                  pl.BlockSpec((None, d, 2 * d_ff), lambda e, b0, nb: (e, 0, 0)),
                  pl.BlockSpec((None, 1, 2 * d_ff), lambda e, b0, nb: (e, 0, 0)),
                  pl.BlockSpec((None, d_ff, d), lambda e, b0, nb: (e, 0, 0)),
                  pl.BlockSpec((None, 1, d), lambda e, b0, nb: (e, 0, 0))],
        out_specs=pl.BlockSpec(memory_space=pl.ANY),
        scratch_shapes=[pltpu.VMEM((2, MOE_BLK * SUBLANES, LANES), F32),
                        pltpu.VMEM((2, MOE_BLK * SUBLANES, LANES), F32),
                        pltpu.VMEM((d, 2 * d_ff), BF16),
                        pltpu.VMEM((d_ff, d), BF16),
                        pltpu.SemaphoreType.DMA((2,)),
                        pltpu.SemaphoreType.DMA((2,))],
    )
    return pl.pallas_call(
        functools.partial(_moe_kernel, n_blocks),
        grid_spec=grid_spec,
        out_shape=jax.ShapeDtypeStruct(xs.shape, F32),
        compiler_params=pltpu.CompilerParams(dimension_semantics=("arbitrary",),
                                             vmem_limit_bytes=VMEM_LIMIT),
        name="moe_ffn",
    )(blk0, nblk, xs, w_gate_up, b_gate_up[:, None, :], w_down, b_down[:, None, :])


def _combine_kernel(destc_ref, destn_ref, x1_ref, mod_ref, route_ref, fg_ref, ys_hbm, o_ref, buf, sem):
    i = pl.program_id(0)
    n_steps = pl.num_programs(0)
    d = x1_ref.shape[-1]

    def start_gather(dest_ref, s):
        for t in range(TD):
            for k in range(TOP_K):
                pltpu.make_async_copy(ys_hbm.at[_tile8(dest_ref[0, 0, t * TOP_K + k])], buf.at[s, k, _tile(t)],
                                      sem.at[s]).start(priority=k % N_DMA_QUEUES)

    def wait_gather(s):
        for k in range(TOP_K):
            pltpu.make_async_copy(ys_hbm.at[pl.ds(0, TD * SUBLANES)], buf.at[s, k], sem.at[s]).wait()

    @pl.when(i == 0)
    def _():
        start_gather(destc_ref, 0)

    def step(s):
        @pl.when(i + 1 < n_steps)
        def _():
            start_gather(destn_ref, 1 - s)

        wait_gather(s)
        gates = route_ref[...]
        gk = [gates[:, TOP_K + k:TOP_K + k + 1] for k in range(TOP_K)]
        gate2 = mod_ref[5:6, :]
        x2, ss = [], jnp.zeros((TD, 1), F32)
        for j in range(SUBLANES):
            ls = slice(j * LANES, (j + 1) * LANES)
            rj = _chunk(j, TD)
            y = (gk[0] * buf[s, 0, rj, :] + gk[1] * buf[s, 1, rj, :]) + \
                (gk[2] * buf[s, 2, rj, :] + gk[3] * buf[s, 3, rj, :])
            xj = x1_ref[:, ls] + gate2[:, ls] * y
            x2.append(xj)
            ss = ss + jnp.sum(xj * xj, axis=-1, keepdims=True)
        inv = lax.rsqrt(ss / d + EPS)
        for j in range(SUBLANES):
            ls = slice(j * LANES, (j + 1) * LANES)
            o_ref[:, ls] = x2[j] * inv * fg_ref[:, ls]

    _by_parity(i, step)


def _combine(x1, mod8, route, dest, ys, final_g):
    bsz, seq, d = x1.shape
    n_tok = bsz * seq
    n_steps = n_tok // TD
    per_seq = seq // TD
    dest3 = dest.reshape(n_steps, 1, TD * TOP_K)
    smem_blk = lambda f: pl.BlockSpec((1, 1, TD * TOP_K), f, memory_space=pltpu.SMEM)
    return pl.pallas_call(
        _combine_kernel,
        grid=(n_steps,),
        in_specs=[smem_blk(lambda i: (i, 0, 0)),
                  smem_blk(lambda i: (jnp.minimum(i + 1, n_steps - 1), 0, 0)),
                  pl.BlockSpec((TD, d), lambda i: (i, 0)),
                  pl.BlockSpec((None, 8, d), lambda i: (i // per_seq, 0, 0)),
                  pl.BlockSpec((TD, LANES), lambda i: (i, 0)),
                  pl.BlockSpec((1, d), lambda i: (0, 0)),
                  pl.BlockSpec(memory_space=pl.ANY)],
        out_specs=pl.BlockSpec((TD, d), lambda i: (i, 0)),
        out_shape=jax.ShapeDtypeStruct((n_tok, d), F32),
        scratch_shapes=[pltpu.VMEM((2, TOP_K, TD * SUBLANES, LANES), F32),
                        pltpu.SemaphoreType.DMA((2,))],
        compiler_params=pltpu.CompilerParams(dimension_semantics=("arbitrary",),
                                             vmem_limit_bytes=VMEM_LIMIT),
        name="combine",
    )(dest3, dest3, x1.reshape(n_tok, d), mod8, route, final_g.reshape(1, d), ys).reshape(bsz, seq, d)


def _routing(route, counts_f, n_tok):
    nk = n_tok * TOP_K
    experts = jnp.arange(N_EXPERTS, dtype=jnp.int32)
    topi = route[:, :TOP_K].astype(jnp.int32)
    rank = route[:, 2 * TOP_K:3 * TOP_K].astype(jnp.int32)
    counts = counts_f[0, :N_EXPERTS].astype(jnp.int32)
    padded = ((counts + MOE_BLK - 1) // MOE_BLK) * MOE_BLK
    pad_end = jnp.cumsum(padded)
    start_pad = pad_end - padded
    dest = rank + jnp.sum(jnp.where(topi[:, :, None] == experts[None, None, :], start_pad[None, None, :], 0), axis=-1)
    p_rows = ((nk + N_EXPERTS * (MOE_BLK - 1) + MOE_BLK - 1) // MOE_BLK) * MOE_BLK
    n_pad = padded - counts
    pad_cum = jnp.cumsum(n_pad)
    j = jnp.arange(p_rows - nk, dtype=jnp.int32)
    e_j = jnp.minimum(jnp.sum((pad_cum[None, :] <= j[:, None]).astype(jnp.int32), axis=1), N_EXPERTS - 1)
    in_expert = j < pad_cum[-1]
    pad_dest = jnp.where(in_expert,
                         start_pad[e_j] + counts[e_j] + (j - (pad_cum[e_j] - n_pad[e_j])),
                         pad_end[-1] + (j - pad_cum[-1])).astype(jnp.int32)
    return (dest.reshape(-1) * SUBLANES, pad_dest * SUBLANES, (start_pad // MOE_BLK).astype(jnp.int32),
            (padded // MOE_BLK).astype(jnp.int32), p_rows)


def kernel(x, c, ada_w, ada_b, norm_mix_g, w_in, lb_params, hgrn_norm_g, gmlp_ln_g, gmlp_ln_b, gmlp_ws, gmlp_bs,
           gmlp_norm_g, w_out, norm_ffn_g, router_w, router_b, w_gate_up, b_gate_up, w_down, b_down, final_g):
    assert ada_w.shape[0] == 1, "single-layer block"
    bsz, seq, d = x.shape
    assert d == SUBLANES * LANES and seq % TM == 0 and (bsz * seq) % TD == 0
    n_tok = bsz * seq
    mod = _modulation(c, ada_w[0], ada_b[0])
    mod8 = jnp.zeros((bsz, 8, d), F32).at[:, :6].set(mod.reshape(bsz, 6, d))
    x1, h2, route, counts = _mixer(x, mod8, norm_mix_g[0], w_in[0], lb_params, hgrn_norm_g[0], gmlp_ln_g[0],
                                   gmlp_ln_b[0], gmlp_ws[0], gmlp_bs[0], gmlp_norm_g[0], w_out[0],
                                   norm_ffn_g[0], router_w[0], router_b[0])
    route = route.reshape(n_tok, LANES)
    dest, pad_dest, blk0, nblk, p_rows = _routing(route, counts, n_tok)
    xs = _dispatch(h2.reshape(n_tok, d), dest, pad_dest, p_rows)
    ys = _moe(xs, blk0, nblk, w_gate_up[0], b_gate_up[0], w_down[0], b_down[0])
    return _combine(x1, mod8, route, dest, ys, final_g)
```

```python
import functools

import jax
import jax.numpy as jnp
from jax import lax
from jax.experimental import pallas as pl
from jax.experimental.pallas import tpu as pltpu

F32 = jnp.float32
BF16 = jnp.bfloat16

HGRN_HEADS = 4
HEAD_DIM = 128
HGRN_WIDTH = HGRN_HEADS * HEAD_DIM
HGRN_CHUNK = 64
HGRN_SUB = 16
GMLP_GROUPS = 4
GROUP_DIM = 128
GMLP_WIDTH = GMLP_GROUPS * GROUP_DIM
GMLP_CHUNK = 128
N_EXPERTS = 32
TOP_K = 4
SWIGLU_LIMIT = 7.0
SWIGLU_ALPHA = 1.702
EPS = 1e-6
LANES = 128
SUBLANES = 8
N_DMA_QUEUES = 2
BLOCK_DMA_PRIORITY = 1
W_CHUNKS = 8
DECAY_EXP_CLAMP = 60.0
TM = 256
MOE_BLK = 256
TD = 256
VMEM_LIMIT = 56 * 1024 * 1024


def _dot(a, b):
    return jnp.dot(a, b, preferred_element_type=F32)


def _dot_nt(a, b):
    return lax.dot_general(a, b, (((1,), (1,)), ((), ())), preferred_element_type=F32)


def _dot_tn(a, b):
    return lax.dot_general(a, b, (((0,), (0,)), ((), ())), preferred_element_type=F32)


def _rms(x):
    return x * lax.rsqrt(jnp.mean(x * x, axis=-1, keepdims=True) + EPS)


def _gelu(x):
    return 0.5 * x * (1.0 + lax.erf(x * 0.7071067811865476))


def _by_parity(i, fn):
    @pl.when(i % 2 == 0)
    def _():
        fn(0)

    @pl.when(i % 2 == 1)
    def _():
        fn(1)


def _mod_kernel(c_ref, w_ref, b_ref, o_ref):
    c = c_ref[...]
    ca = c * jax.nn.sigmoid(c)
    o_ref[...] = jnp.dot(ca, w_ref[...], precision=lax.Precision.HIGHEST,
                         preferred_element_type=F32) + b_ref[...]


def _modulation(c, ada_w, ada_b):
    bsz, d = c.shape
    n_out = ada_w.shape[1]
    rows = 8
    c_pad = jnp.zeros((rows, d), F32).at[:bsz].set(c)
    tn = 1024
    out = pl.pallas_call(
        _mod_kernel,
        grid=(n_out // tn,),
        in_specs=[pl.BlockSpec((rows, d), lambda j: (0, 0)),
                  pl.BlockSpec((d, tn), lambda j: (0, j)),
                  pl.BlockSpec((1, tn), lambda j: (0, j))],
        out_specs=pl.BlockSpec((rows, tn), lambda j: (0, j)),
        out_shape=jax.ShapeDtypeStruct((rows, n_out), F32),
        name="adaln_mod",
    )(c_pad, ada_w, ada_b.reshape(1, n_out))
    return out[:bsz]


def _mixer_kernel(x_ref, mod_ref, g1_ref, win_ref, lbp_ref, hg_ref, lng_ref, lnb_ref, ws_ref, bs_ref,
                  gng_ref, wout_ref, g2_ref, rwh_ref, rwl_ref, rb_ref,
                  x1_ref, h2_ref, route_ref, cnt_out_ref,
                  z_ref, y_ref, st_ref, cnt_ref):
    @pl.when(pl.program_id(1) == 0)
    def _():
        st_ref[...] = jnp.zeros_like(st_ref)

    x = x_ref[...]
    mod = mod_ref[...]
    h = _rms(x) * g1_ref[...]
    h = h * (1.0 + mod[1:2]) + mod[0:1]
    z_ref[...] = _dot(h.astype(BF16), win_ref[...])

    lbp = lbp_ref[...]
    lbe = jnp.exp(lbp - jnp.max(lbp, axis=0, keepdims=True))
    lb = lbe[0:1] / jnp.sum(lbe, axis=0, keepdims=True)
    hg = hg_ref[...]
    row = lax.broadcasted_iota(jnp.int32, (HGRN_CHUNK, HGRN_CHUNK), 0)
    col = lax.broadcasted_iota(jnp.int32, (HGRN_CHUNK, HGRN_CHUNK), 1)
    causal = col <= row
    tri = jnp.where(causal, 1.0, 0.0).astype(BF16)
    n_sub = HGRN_CHUNK // HGRN_SUB

    def chunk_body(c):
        rows = slice(c * HGRN_CHUNK, (c + 1) * HGRN_CHUNK)
        zq = z_ref[rows, 0:HGRN_WIDTH]
        zf = z_ref[rows, HGRN_WIDTH:2 * HGRN_WIDTH]
        zi = z_ref[rows, 2 * HGRN_WIDTH:3 * HGRN_WIDTH]
        zg = z_ref[rows, 3 * HGRN_WIDTH:4 * HGRN_WIDTH]
        f = lb + (1.0 - lb) * jax.nn.sigmoid(zf)
        logf = jnp.log(f)
        kk = 1.0 - f
        p0 = logf.astype(BF16)
        r0 = logf - p0.astype(F32)
        p1 = r0.astype(BF16)
        p2 = (r0 - p1.astype(F32)).astype(BF16)
        b = (_dot(tri, p0) + _dot(tri, p1)) + _dot(tri, p2)
        b_last = b[HGRN_CHUNK - 1:HGRN_CHUNK, :]
        qe = (zq * jnp.exp(b)).astype(BF16)
        kdec = (kk * jnp.exp(b_last - b)).astype(BF16)
        v = zi.astype(BF16)
        dec_last = jnp.exp(b_last)
        a_sub, k_sub = [], []
        for i in range(n_sub):
            lo, hi = i * HGRN_SUB, (i + 1) * HGRN_SUB
            bref = b[lo - 1:lo, :] if i > 0 else jnp.zeros((1, HGRN_WIDTH), F32)
            a_sub.append((zq[lo:hi] * jnp.exp(b[lo:hi] - bref)).astype(BF16))
            k_sub.append((kk * jnp.exp(jnp.minimum(bref - b, DECAY_EXP_CLAMP))).astype(BF16))
        silu_g = zg * jax.nn.sigmoid(zg)
        for hd in range(HGRN_HEADS):
            ls = slice(hd * HEAD_DIM, (hd + 1) * HEAD_DIM)
            sc = jnp.concatenate([_dot_nt(a_sub[i][:, ls], k_sub[i][:, ls]) for i in range(n_sub)], axis=0)
            sc = jnp.where(causal, sc, 0.0).astype(BF16)
            st = st_ref[hd]
            o = _dot_nt(qe[:, ls], st.astype(BF16)) + _dot(sc, v[:, ls])
            st_ref[hd] = st * dec_last[:, ls] + _dot_tn(v[:, ls], kdec[:, ls])
            o = _rms(o) * hg[:, ls]
            y_ref[rows, ls] = (o * silu_g[:, ls]).astype(BF16)

    for c in range(TM // HGRN_CHUNK):
        chunk_body(c)

    u = _gelu(z_ref[:, 4 * HGRN_WIDTH:4 * HGRN_WIDTH + GMLP_WIDTH])
    gv = _gelu(z_ref[:, 4 * HGRN_WIDTH + GMLP_WIDTH:])
    mu = jnp.mean(gv, axis=-1, keepdims=True)
    gc = gv - mu
    var = jnp.mean(gc * gc, axis=-1, keepdims=True)
    vn = (gc * lax.rsqrt(var + EPS) * lng_ref[...] + lnb_ref[...]).astype(BF16)
    row_g = lax.broadcasted_iota(jnp.int32, (GMLP_CHUNK, GMLP_CHUNK), 0)
    col_g = lax.broadcasted_iota(jnp.int32, (GMLP_CHUNK, GMLP_CHUNK), 1)
    gng = gng_ref[...]
    for g in range(GMLP_GROUPS):
        ls = slice(g * GROUP_DIM, (g + 1) * GROUP_DIM)
        ws_c = jnp.where(col_g <= row_g, ws_ref[g], 0.0).astype(BF16)
        bias = bs_ref[g]
        for n in range(TM // GMLP_CHUNK):
            rs = slice(n * GMLP_CHUNK, (n + 1) * GMLP_CHUNK)
            sv = _dot(ws_c, vn[rs, ls]) + bias
            yy = _rms(u[rs, ls] * sv) * gng[:, ls]
            y_ref[rs, HGRN_WIDTH + g * GROUP_DIM:HGRN_WIDTH + (g + 1) * GROUP_DIM] = yy.astype(BF16)

    x1 = x + mod[2:3] * _dot(y_ref[...], wout_ref[...])
    x1_ref[...] = x1
    h2 = _rms(x1) * g2_ref[...]
    h2 = h2 * (1.0 + mod[4:5]) + mod[3:4]
    h2_ref[...] = h2

    hh = h2.astype(BF16)
    hl = (h2 - hh.astype(F32)).astype(BF16)
    logits = _dot(hh, rwh_ref[...]) + (_dot(hl, rwh_ref[...]) + _dot(hh, rwl_ref[...])) + rb_ref[...]
    lane = lax.broadcasted_iota(jnp.int32, (TM, LANES), 1)
    vals, idxs = [], []
    for _ in range(TOP_K):
        m = jnp.max(logits, axis=-1, keepdims=True)
        idx = jnp.min(jnp.where(logits == m, lane, LANES), axis=-1, keepdims=True)
        vals.append(m)
        idxs.append(idx)
        logits = jnp.where(lane == idx, -jnp.inf, logits)
    es = [jnp.exp(v - vals[0]) for v in vals]
    tot = (es[0] + es[1]) + (es[2] + es[3])

    @pl.when((pl.program_id(0) == 0) & (pl.program_id(1) == 0))
    def _():
        cnt_ref[...] = jnp.zeros_like(cnt_ref)

    hot = [lane == idxs[k] for k in range(TOP_K)]
    picked = jnp.where((hot[0] | hot[1]) | (hot[2] | hot[3]), 1.0, 0.0)
    row_t = lax.broadcasted_iota(jnp.int32, (TM, TM), 0)
    col_t = lax.broadcasted_iota(jnp.int32, (TM, TM), 1)
    before = jnp.where(col_t < row_t, 1.0, 0.0).astype(BF16)
    seen = _dot(before, picked.astype(BF16)) + cnt_ref[0:1, :]
    cnt_ref[...] = cnt_ref[...] + jnp.sum(picked, axis=0, keepdims=True)
    cnt_out_ref[...] = cnt_ref[...]
    route = jnp.zeros((TM, LANES), F32)
    for k in range(TOP_K):
        rank = jnp.sum(jnp.where(hot[k], seen, 0.0), axis=-1, keepdims=True)
        route = jnp.where(lane == k, idxs[k].astype(F32), route)
        route = jnp.where(lane == TOP_K + k, es[k] / tot, route)
        route = jnp.where(lane == 2 * TOP_K + k, rank, route)
    route_ref[...] = route


def _mixer(x, mod8, norm_mix_g, w_in, lb_params, hgrn_norm_g, gmlp_ln_g, gmlp_ln_b, gmlp_ws, gmlp_bs,
           gmlp_norm_g, w_out, norm_ffn_g, router_w, router_b):
    bsz, seq, d = x.shape
    n_in = w_in.shape[1]
    rw = jnp.zeros((d, LANES), F32).at[:, :N_EXPERTS].set(router_w)
    rwh = rw.astype(BF16)
    rwl = (rw - rwh.astype(F32)).astype(BF16)
    rb = jnp.full((1, LANES), -1e30, F32).at[0, :N_EXPERTS].set(router_b)
    const = lambda *shape: pl.BlockSpec(shape, lambda b, i: (0,) * len(shape))
    tile = lambda w: pl.BlockSpec((None, TM, w), lambda b, i: (b, i, 0))
    return pl.pallas_call(
        _mixer_kernel,
        grid=(bsz, seq // TM),
        in_specs=[tile(d),
                  pl.BlockSpec((None, 8, d), lambda b, i: (b, 0, 0)),
                  const(1, d), const(d, n_in), const(2, HGRN_WIDTH), const(1, HGRN_WIDTH),
                  const(1, GMLP_WIDTH), const(1, GMLP_WIDTH),
                  const(GMLP_GROUPS, GMLP_CHUNK, GMLP_CHUNK), const(GMLP_GROUPS, GMLP_CHUNK, 1),
                  const(1, GMLP_WIDTH), const(d, d), const(1, d),
                  const(d, LANES), const(d, LANES), const(1, LANES)],
        out_specs=[tile(d), tile(d), tile(LANES), const(SUBLANES, LANES)],
        out_shape=[jax.ShapeDtypeStruct((bsz, seq, d), F32),
                   jax.ShapeDtypeStruct((bsz, seq, d), F32),
                   jax.ShapeDtypeStruct((bsz, seq, LANES), F32),
                   jax.ShapeDtypeStruct((SUBLANES, LANES), F32)],
        scratch_shapes=[pltpu.VMEM((TM, n_in), F32),
                        pltpu.VMEM((TM, d), BF16),
                        pltpu.VMEM((HGRN_HEADS, HEAD_DIM, HEAD_DIM), F32),
                        pltpu.VMEM((SUBLANES, LANES), F32)],
        compiler_params=pltpu.CompilerParams(dimension_semantics=("arbitrary", "arbitrary"),
                                             vmem_limit_bytes=VMEM_LIMIT),
        name="mixer",
    )(x, mod8, norm_mix_g.reshape(1, d), w_in.astype(BF16), lb_params, hgrn_norm_g.reshape(1, -1),
      gmlp_ln_g.reshape(1, -1), gmlp_ln_b.reshape(1, -1), gmlp_ws, gmlp_bs[:, :, None],
      gmlp_norm_g.reshape(1, -1), w_out.astype(BF16), norm_ffn_g.reshape(1, d), rwh, rwl, rb)


def _tile(r):
    return pl.ds(r * SUBLANES, SUBLANES)


def _tile8(r8):
    return pl.ds(pl.multiple_of(r8, SUBLANES), SUBLANES)


def _chunk(j, n):
    return pl.ds(j, n, stride=SUBLANES)


def _dispatch_kernel(n_pad_step, dest_ref, pad_ref, h2_ref, xs_hbm, rows, zero_tile, sem):
    i = pl.program_id(0)
    n_steps = pl.num_programs(0)

    def wait_step(s):
        for _ in range(TOP_K):
            pltpu.make_async_copy(rows.at[s], xs_hbm.at[pl.ds(0, TD * SUBLANES)], sem.at[s]).wait()
        n = n_pad_step * SUBLANES
        pltpu.make_async_copy(rows.at[s, pl.ds(0, n)], xs_hbm.at[pl.ds(0, n)], sem.at[s]).wait()

    @pl.when(i == 0)
    def _():
        zero_tile[...] = jnp.zeros_like(zero_tile)

    def step(s):
        @pl.when(i >= 2)
        def _():
            wait_step(s)

        for j in range(SUBLANES):
            rows[s, _chunk(j, TD), :] = h2_ref[:, j * LANES:(j + 1) * LANES]
        for t in range(TD):
            for k in range(TOP_K):
                pltpu.make_async_copy(rows.at[s, _tile(t)], xs_hbm.at[_tile8(dest_ref[0, 0, t * TOP_K + k])],
                                      sem.at[s]).start(priority=k % N_DMA_QUEUES)
        for q in range(n_pad_step):
            pltpu.make_async_copy(zero_tile, xs_hbm.at[_tile8(pad_ref[0, 0, q])],
                                  sem.at[s]).start(priority=q % N_DMA_QUEUES)

        @pl.when(i == n_steps - 1)
        def _():
            wait_step(s)

            @pl.when(i >= 1)
            def _():
                wait_step(1 - s)

    _by_parity(i, step)


def _dispatch(h2, dest, pad_dest, p_rows):
    n_tok, d = h2.shape
    n_steps = n_tok // TD
    n_pad_step = pad_dest.shape[0] // n_steps
    return pl.pallas_call(
        functools.partial(_dispatch_kernel, n_pad_step),
        grid=(n_steps,),
        in_specs=[pl.BlockSpec((1, 1, TD * TOP_K), lambda i: (i, 0, 0), memory_space=pltpu.SMEM),
                  pl.BlockSpec((1, 1, n_pad_step), lambda i: (i, 0, 0), memory_space=pltpu.SMEM),
                  pl.BlockSpec((TD, d), lambda i: (i, 0))],
        out_specs=pl.BlockSpec(memory_space=pl.ANY),
        out_shape=jax.ShapeDtypeStruct((p_rows * SUBLANES, LANES), F32),
        scratch_shapes=[pltpu.VMEM((2, TD * SUBLANES, LANES), F32),
                        pltpu.VMEM((SUBLANES, LANES), F32),
                        pltpu.SemaphoreType.DMA((2,))],
        compiler_params=pltpu.CompilerParams(dimension_semantics=("arbitrary",)),
        name="dispatch",
    )(dest.reshape(n_steps, 1, TD * TOP_K), pad_dest.reshape(n_steps, 1, n_pad_step), h2)


def _moe_kernel(n_blocks, blk0_ref, nblk_ref, xs_hbm, wgu_hbm, bgu_ref, wd_hbm, bd_ref, ys_hbm,
                xbuf, ybuf, wgu_f, wd_f, wgu_bf, wd_bf, isem, osem, wsem):
    e = pl.program_id(0)
    n_exp = pl.num_programs(0)
    nb = nblk_ref[e]
    b0 = blk0_ref[e]
    d, d_ff = wd_f.shape[2], wd_f.shape[1]
    blk_rows = MOE_BLK * SUBLANES
    ws = e % 2
    gu_rows, d_rows = d // W_CHUNKS, d_ff // W_CHUNKS

    def block(b):
        return pl.ds(pl.multiple_of((b0 + b) * blk_rows, blk_rows), blk_rows)

    def x_copy(b, s):
        return pltpu.make_async_copy(xs_hbm.at[block(b)], xbuf.at[s], isem.at[s])

    def y_copy(b, s):
        return pltpu.make_async_copy(ybuf.at[s], ys_hbm.at[block(b)], osem.at[s])

    def w_start(ex, slot, c):
        r_gu = pl.ds(pl.multiple_of(c * gu_rows, gu_rows), gu_rows)
        r_d = pl.ds(pl.multiple_of(c * d_rows, d_rows), d_rows)
        pltpu.make_async_copy(wgu_hbm.at[ex, r_gu], wgu_f.at[slot, r_gu], wsem.at[slot]).start()
        pltpu.make_async_copy(wd_hbm.at[ex, r_d], wd_f.at[slot, r_d], wsem.at[slot]).start()

    def w_wait(slot):
        pltpu.make_async_copy(wgu_hbm.at[0], wgu_f.at[slot], wsem.at[slot]).wait()
        pltpu.make_async_copy(wd_hbm.at[0], wd_f.at[slot], wsem.at[slot]).wait()

    @pl.when(e == 0)
    def _():
        for c in range(W_CHUNKS):
            w_start(0, 0, c)

    @pl.when(nb > 0)
    def _():
        x_copy(0, 0).start(priority=BLOCK_DMA_PRIORITY)

    w_wait(ws)
    wgu_bf[...] = wgu_f[ws].astype(BF16)
    wd_bf[...] = wd_f[ws].astype(BF16)
    has_next = e + 1 < n_exp

    def body(b, carry):
        s = b % 2

        @pl.when(b + 1 < nb)
        def _():
            x_copy(b + 1, 1 - s).start(priority=BLOCK_DMA_PRIORITY)

        @pl.when(has_next & (b < W_CHUNKS))
        def _():
            w_start(e + 1, 1 - ws, b)

        x_copy(b, s).wait()

        @pl.when(b >= 2)
        def _():
            y_copy(b - 2, s).wait()

        xb = jnp.concatenate([xbuf[s, _chunk(j, MOE_BLK), :] for j in range(SUBLANES)], axis=-1).astype(BF16)
        gu = _dot(xb, wgu_bf[...]) + bgu_ref[...]
        gate = jnp.minimum(gu[:, :d_ff], SWIGLU_LIMIT)
        up = jnp.clip(gu[:, d_ff:], -SWIGLU_LIMIT, SWIGLU_LIMIT)
        glu = gate * jax.nn.sigmoid(SWIGLU_ALPHA * gate)
        yb = _dot(((up + 1.0) * glu).astype(BF16), wd_bf[...]) + bd_ref[...]
        for j in range(SUBLANES):
            ybuf[s, _chunk(j, MOE_BLK), :] = yb[:, j * LANES:(j + 1) * LANES]
        y_copy(b, s).start(priority=BLOCK_DMA_PRIORITY)
        return carry

    lax.fori_loop(0, nb, body, 0)

    @pl.when(has_next)
    def _():
        def rest(c, carry):
            w_start(e + 1, 1 - ws, c)
            return carry

        lax.fori_loop(jnp.minimum(nb, W_CHUNKS), W_CHUNKS, rest, 0)

    @pl.when(nb >= 1)
    def _():
        y_copy(nb - 1, (nb - 1) % 2).wait()

    @pl.when(nb >= 2)
    def _():
        y_copy(nb - 2, nb % 2).wait()

    @pl.when(e == pl.num_programs(0) - 1)
    def _():
        ybuf[0] = jnp.zeros_like(ybuf[0])

        def fill(b, carry):
            cp = y_copy(b, 0)
            cp.start()
            cp.wait()
            return carry

        lax.fori_loop(nb, n_blocks - b0, fill, 0)


def _moe(xs, blk0, nblk, w_gate_up, b_gate_up, w_down, b_down):
    n_blocks = xs.shape[0] // (MOE_BLK * SUBLANES)
    n_exp, d_ff, d = w_down.shape
    grid_spec = pltpu.PrefetchScalarGridSpec(
        num_scalar_prefetch=2,
        grid=(n_exp,),
        in_specs=[pl.BlockSpec(memory_space=pl.ANY),
                  pl.BlockSpec(memory_space=pl.ANY),
                  pl.BlockSpec((None, 1, 2 * d_ff), lambda e, b0, nb: (e, 0, 0)),
                  pl.BlockSpec(memory_space=pl.ANY),
                  pl.BlockSpec((None, 1, d), lambda e, b0, nb: (e, 0, 0))],
        out_specs=pl.BlockSpec(memory_space=pl.ANY),
        scratch_shapes=[pltpu.VMEM((2, MOE_BLK * SUBLANES, LANES), F32),
                        pltpu.VMEM((2, MOE_BLK * SUBLANES, LANES), F32),
                        pltpu.VMEM((2, d, 2 * d_ff), F32),
                        pltpu.VMEM((2, d_ff, d), F32),
                        pltpu.VMEM((d, 2 * d_ff), BF16),
                        pltpu.VMEM((d_ff, d), BF16),
                        pltpu.SemaphoreType.DMA((2,)),
                        pltpu.SemaphoreType.DMA((2,)),
                        pltpu.SemaphoreType.DMA((2,))],
    )
    return pl.pallas_call(
        functools.partial(_moe_kernel, n_blocks),
        grid_spec=grid_spec,
        out_shape=jax.ShapeDtypeStruct(xs.shape, F32),
        compiler_params=pltpu.CompilerParams(dimension_semantics=("arbitrary",),
                                             vmem_limit_bytes=VMEM_LIMIT),
        name="moe_ffn",
    )(blk0, nblk, xs, w_gate_up, b_gate_up[:, None, :], w_down, b_down[:, None, :])


def _combine_kernel(destc_ref, destn_ref, x1_ref, mod_ref, route_ref, fg_ref, ys_hbm, o_ref, buf0, buf1, sem):
    buf = (buf0, buf1)
    i = pl.program_id(0)
    n_steps = pl.num_programs(0)
    d = x1_ref.shape[-1]

    def start_gather(dest_ref, s):
        for t in range(TD):
            for k in range(TOP_K):
                pltpu.make_async_copy(ys_hbm.at[_tile8(dest_ref[0, 0, t * TOP_K + k])], buf[s].at[k, _tile(t)],
                                      sem.at[s]).start(priority=k % N_DMA_QUEUES)

    def wait_gather(s):
        for k in range(TOP_K):
            pltpu.make_async_copy(ys_hbm.at[pl.ds(0, TD * SUBLANES)], buf[s].at[k], sem.at[s]).wait()

    @pl.when(i == 0)
    def _():
        start_gather(destc_ref, 0)

    def step(s):
        start_gather(destn_ref, 1 - s)
        wait_gather(s)
        gates = route_ref[...]
        gk = [gates[:, TOP_K + k:TOP_K + k + 1] for k in range(TOP_K)]
        gate2 = mod_ref[5:6, :]
        x2, ss = [], jnp.zeros((TD, 1), F32)
        for j in range(SUBLANES):
            ls = slice(j * LANES, (j + 1) * LANES)
            rj = _chunk(j, TD)
            y = (gk[0] * buf[s][0, rj, :] + gk[1] * buf[s][1, rj, :]) + \
                (gk[2] * buf[s][2, rj, :] + gk[3] * buf[s][3, rj, :])
            xj = x1_ref[:, ls] + gate2[:, ls] * y
            x2.append(xj)
            ss = ss + jnp.sum(xj * xj, axis=-1, keepdims=True)
        inv = lax.rsqrt(ss / d + EPS)
        for j in range(SUBLANES):
            ls = slice(j * LANES, (j + 1) * LANES)
            o_ref[:, ls] = x2[j] * inv * fg_ref[:, ls]

        @pl.when(i == n_steps - 1)
        def _():
            wait_gather(1 - s)

    _by_parity(i, step)


def _combine(x1, mod8, route, dest, ys, final_g):
    bsz, seq, d = x1.shape
    n_tok = bsz * seq
    n_steps = n_tok // TD
    per_seq = seq // TD
    dest3 = dest.reshape(n_steps, 1, TD * TOP_K)
    smem_blk = lambda f: pl.BlockSpec((1, 1, TD * TOP_K), f, memory_space=pltpu.SMEM)
    return pl.pallas_call(
        _combine_kernel,
        grid=(n_steps,),
        in_specs=[smem_blk(lambda i: (i, 0, 0)),
                  smem_blk(lambda i: (jnp.minimum(i + 1, n_steps - 1), 0, 0)),
                  pl.BlockSpec((TD, d), lambda i: (i, 0)),
                  pl.BlockSpec((None, 8, d), lambda i: (i // per_seq, 0, 0)),
                  pl.BlockSpec((TD, LANES), lambda i: (i, 0)),
                  pl.BlockSpec((1, d), lambda i: (0, 0)),
                  pl.BlockSpec(memory_space=pl.ANY)],
        out_specs=pl.BlockSpec((TD, d), lambda i: (i, 0)),
        out_shape=jax.ShapeDtypeStruct((n_tok, d), F32),
        scratch_shapes=[pltpu.VMEM((TOP_K, TD * SUBLANES, LANES), F32),
                        pltpu.VMEM((TOP_K, TD * SUBLANES, LANES), F32),
                        pltpu.SemaphoreType.DMA((2,))],
        compiler_params=pltpu.CompilerParams(dimension_semantics=("arbitrary",),
                                             vmem_limit_bytes=VMEM_LIMIT),
        name="combine",
    )(dest3, dest3, x1.reshape(n_tok, d), mod8, route, final_g.reshape(1, d), ys).reshape(bsz, seq, d)


def _routing(route, counts_f, n_tok):
    nk = n_tok * TOP_K
    experts = jnp.arange(N_EXPERTS, dtype=jnp.int32)
    topi = route[:, :TOP_K].astype(jnp.int32)
    rank = route[:, 2 * TOP_K:3 * TOP_K].astype(jnp.int32)
    counts = counts_f[0, :N_EXPERTS].astype(jnp.int32)
    padded = ((counts + MOE_BLK - 1) // MOE_BLK) * MOE_BLK
    pad_end = jnp.cumsum(padded)
    start_pad = pad_end - padded
    dest = rank + jnp.take(start_pad, topi, axis=0)
    p_rows = ((nk + N_EXPERTS * (MOE_BLK - 1) + MOE_BLK - 1) // MOE_BLK) * MOE_BLK
    n_pad = padded - counts
    pad_cum = jnp.cumsum(n_pad)
    j = jnp.arange(p_rows - nk, dtype=jnp.int32)
    e_j = jnp.minimum(jnp.sum((pad_cum[None, :] <= j[:, None]).astype(jnp.int32), axis=1), N_EXPERTS - 1)
    in_expert = j < pad_cum[-1]
    pad_dest = jnp.where(in_expert,
                         start_pad[e_j] + counts[e_j] + (j - (pad_cum[e_j] - n_pad[e_j])),
                         pad_end[-1] + (j - pad_cum[-1])).astype(jnp.int32)
    return (dest.reshape(-1) * SUBLANES, pad_dest * SUBLANES, (start_pad // MOE_BLK).astype(jnp.int32),
            (padded // MOE_BLK).astype(jnp.int32), p_rows)


def kernel(x, c, ada_w, ada_b, norm_mix_g, w_in, lb_params, hgrn_norm_g, gmlp_ln_g, gmlp_ln_b, gmlp_ws, gmlp_bs,
           gmlp_norm_g, w_out, norm_ffn_g, router_w, router_b, w_gate_up, b_gate_up, w_down, b_down, final_g):
    assert ada_w.shape[0] == 1, "single-layer block"
    bsz, seq, d = x.shape
    assert d == SUBLANES * LANES and seq % TM == 0 and (bsz * seq) % TD == 0
    n_tok = bsz * seq
    mod = _modulation(c, ada_w[0], ada_b[0])
    mod8 = jnp.zeros((bsz, 8, d), F32).at[:, :6].set(mod.reshape(bsz, 6, d))
    x1, h2, route, counts = _mixer(x, mod8, norm_mix_g[0], w_in[0], lb_params, hgrn_norm_g[0], gmlp_ln_g[0],
                                   gmlp_ln_b[0], gmlp_ws[0], gmlp_bs[0], gmlp_norm_g[0], w_out[0],
                                   norm_ffn_g[0], router_w[0], router_b[0])
    route = route.reshape(n_tok, LANES)
    dest, pad_dest, blk0, nblk, p_rows = _routing(route, counts, n_tok)
    xs = _dispatch(h2.reshape(n_tok, d), dest, pad_dest, p_rows)
    ys = _moe(xs, blk0, nblk, w_gate_up[0], b_gate_up[0], w_down[0], b_down[0])
    return _combine(x1, mod8, route, dest, ys, final_g)
```

```python
import functools

import jax
import jax.numpy as jnp
from jax import lax
from jax.experimental import pallas as pl
from jax.experimental.pallas import tpu as pltpu

F32 = jnp.float32
BF16 = jnp.bfloat16

HGRN_HEADS = 4
HEAD_DIM = 128
HGRN_WIDTH = HGRN_HEADS * HEAD_DIM
HGRN_CHUNK = 64
HGRN_SUB = 16
GMLP_GROUPS = 4
GROUP_DIM = 128
GMLP_WIDTH = GMLP_GROUPS * GROUP_DIM
GMLP_CHUNK = 128
N_EXPERTS = 32
TOP_K = 4
SWIGLU_LIMIT = 7.0
SWIGLU_ALPHA = 1.702
EPS = 1e-6
LANES = 128
SUBLANES = 8
N_DMA_QUEUES = 2
BLOCK_DMA_PRIORITY = 1
ROUTE_ROWS = 16
W_CHUNKS = 8
DECAY_EXP_CLAMP = 60.0
TM = 256
MOE_BLK = 256
TD = 256
VMEM_LIMIT = 56 * 1024 * 1024


def _dot(a, b):
    return jnp.dot(a, b, preferred_element_type=F32)


def _dot_nt(a, b):
    return lax.dot_general(a, b, (((1,), (1,)), ((), ())), preferred_element_type=F32)


def _dot_tn(a, b):
    return lax.dot_general(a, b, (((0,), (0,)), ((), ())), preferred_element_type=F32)


def _rms(x):
    return x * lax.rsqrt(jnp.mean(x * x, axis=-1, keepdims=True) + EPS)


def _gelu(x):
    return 0.5 * x * (1.0 + lax.erf(x * 0.7071067811865476))


def _by_parity(i, fn):
    @pl.when(i % 2 == 0)
    def _():
        fn(0)

    @pl.when(i % 2 == 1)
    def _():
        fn(1)


def _mod_kernel(c_ref, w_ref, b_ref, o_ref):
    c = c_ref[...]
    ca = c * jax.nn.sigmoid(c)
    o_ref[...] = jnp.dot(ca, w_ref[...], precision=lax.Precision.HIGHEST,
                         preferred_element_type=F32) + b_ref[...]


def _modulation(c, ada_w, ada_b):
    bsz, d = c.shape
    n_out = ada_w.shape[1]
    rows = 8
    c_pad = jnp.zeros((rows, d), F32).at[:bsz].set(c)
    tn = 1024
    out = pl.pallas_call(
        _mod_kernel,
        grid=(n_out // tn,),
        in_specs=[pl.BlockSpec((rows, d), lambda j: (0, 0)),
                  pl.BlockSpec((d, tn), lambda j: (0, j)),
                  pl.BlockSpec((1, tn), lambda j: (0, j))],
        out_specs=pl.BlockSpec((rows, tn), lambda j: (0, j)),
        out_shape=jax.ShapeDtypeStruct((rows, n_out), F32),
        name="adaln_mod",
    )(c_pad, ada_w, ada_b.reshape(1, n_out))
    return out[:bsz]


def _mixer_kernel(x_ref, mod_ref, g1_ref, win_ref, lbp_ref, hg_ref, lng_ref, lnb_ref, ws_ref, bs_ref,
                  gng_ref, wout_ref, g2_ref, rwh_ref, rwl_ref, rb_ref,
                  x1_ref, h2_ref, route_ref, cnt_out_ref,
                  z_ref, y_ref, st_ref, cnt_ref):
    @pl.when(pl.program_id(1) == 0)
    def _():
        st_ref[...] = jnp.zeros_like(st_ref)

    x = x_ref[...]
    mod = mod_ref[...]
    h = _rms(x) * g1_ref[...]
    h = h * (1.0 + mod[1:2]) + mod[0:1]
    z_ref[...] = _dot(h.astype(BF16), win_ref[...])

    lbp = lbp_ref[...]
    lbe = jnp.exp(lbp - jnp.max(lbp, axis=0, keepdims=True))
    lb = lbe[0:1] / jnp.sum(lbe, axis=0, keepdims=True)
    hg = hg_ref[...]
    row = lax.broadcasted_iota(jnp.int32, (HGRN_CHUNK, HGRN_CHUNK), 0)
    col = lax.broadcasted_iota(jnp.int32, (HGRN_CHUNK, HGRN_CHUNK), 1)
    causal = col <= row
    tri = jnp.where(causal, 1.0, 0.0).astype(BF16)
    n_sub = HGRN_CHUNK // HGRN_SUB

    def chunk_body(c):
        rows = slice(c * HGRN_CHUNK, (c + 1) * HGRN_CHUNK)
        zq = z_ref[rows, 0:HGRN_WIDTH]
        zf = z_ref[rows, HGRN_WIDTH:2 * HGRN_WIDTH]
        zi = z_ref[rows, 2 * HGRN_WIDTH:3 * HGRN_WIDTH]
        zg = z_ref[rows, 3 * HGRN_WIDTH:4 * HGRN_WIDTH]
        f = lb + (1.0 - lb) * jax.nn.sigmoid(zf)
        logf = jnp.log(f)
        kk = 1.0 - f
        p0 = logf.astype(BF16)
        r0 = logf - p0.astype(F32)
        p1 = r0.astype(BF16)
        p2 = (r0 - p1.astype(F32)).astype(BF16)
        b = (_dot(tri, p0) + _dot(tri, p1)) + _dot(tri, p2)
        b_last = b[HGRN_CHUNK - 1:HGRN_CHUNK, :]
        qe = (zq * jnp.exp(b)).astype(BF16)
        kdec = (kk * jnp.exp(b_last - b)).astype(BF16)
        v = zi.astype(BF16)
        dec_last = jnp.exp(b_last)
        a_sub, k_sub = [], []
        for i in range(n_sub):
            lo, hi = i * HGRN_SUB, (i + 1) * HGRN_SUB
            bref = b[lo - 1:lo, :] if i > 0 else jnp.zeros((1, HGRN_WIDTH), F32)
            a_sub.append((zq[lo:hi] * jnp.exp(b[lo:hi] - bref)).astype(BF16))
            k_sub.append((kk * jnp.exp(jnp.minimum(bref - b, DECAY_EXP_CLAMP))).astype(BF16))
        silu_g = zg * jax.nn.sigmoid(zg)
        for hd in range(HGRN_HEADS):
            ls = slice(hd * HEAD_DIM, (hd + 1) * HEAD_DIM)
            sc = jnp.concatenate([_dot_nt(a_sub[i][:, ls], k_sub[i][:, ls]) for i in range(n_sub)], axis=0)
            sc = jnp.where(causal, sc, 0.0).astype(BF16)
            st = st_ref[hd]
            o = _dot_nt(qe[:, ls], st.astype(BF16)) + _dot(sc, v[:, ls])
            st_ref[hd] = st * dec_last[:, ls] + _dot_tn(v[:, ls], kdec[:, ls])
            o = _rms(o) * hg[:, ls]
            y_ref[rows, ls] = (o * silu_g[:, ls]).astype(BF16)

    for c in range(TM // HGRN_CHUNK):
        chunk_body(c)

    u = _gelu(z_ref[:, 4 * HGRN_WIDTH:4 * HGRN_WIDTH + GMLP_WIDTH])
    gv = _gelu(z_ref[:, 4 * HGRN_WIDTH + GMLP_WIDTH:])
    mu = jnp.mean(gv, axis=-1, keepdims=True)
    gc = gv - mu
    var = jnp.mean(gc * gc, axis=-1, keepdims=True)
    vn = (gc * lax.rsqrt(var + EPS) * lng_ref[...] + lnb_ref[...]).astype(BF16)
    row_g = lax.broadcasted_iota(jnp.int32, (GMLP_CHUNK, GMLP_CHUNK), 0)
    col_g = lax.broadcasted_iota(jnp.int32, (GMLP_CHUNK, GMLP_CHUNK), 1)
    gng = gng_ref[...]
    for g in range(GMLP_GROUPS):
        ls = slice(g * GROUP_DIM, (g + 1) * GROUP_DIM)
        ws_c = jnp.where(col_g <= row_g, ws_ref[g], 0.0).astype(BF16)
        bias = bs_ref[g]
        for n in range(TM // GMLP_CHUNK):
            rs = slice(n * GMLP_CHUNK, (n + 1) * GMLP_CHUNK)
            sv = _dot(ws_c, vn[rs, ls]) + bias
            yy = _rms(u[rs, ls] * sv) * gng[:, ls]
            y_ref[rs, HGRN_WIDTH + g * GROUP_DIM:HGRN_WIDTH + (g + 1) * GROUP_DIM] = yy.astype(BF16)

    x1 = x + mod[2:3] * _dot(y_ref[...], wout_ref[...])
    x1_ref[...] = x1
    h2 = _rms(x1) * g2_ref[...]
    h2 = h2 * (1.0 + mod[4:5]) + mod[3:4]
    h2_ref[...] = h2

    hh = h2.astype(BF16)
    hl = (h2 - hh.astype(F32)).astype(BF16)
    rwh, rwl = rwh_ref[...], rwl_ref[...]
    logits = _dot_nt(rwh, hh) + (_dot_nt(rwh, hl) + _dot_nt(rwl, hh)) + rb_ref[...]
    erow = lax.broadcasted_iota(jnp.int32, (N_EXPERTS, TM), 0)
    vals, idxs = [], []
    for _ in range(TOP_K):
        m = jnp.max(logits, axis=0, keepdims=True)
        idx = jnp.min(jnp.where(logits == m, erow, N_EXPERTS), axis=0, keepdims=True)
        vals.append(m)
        idxs.append(idx)
        logits = jnp.where(erow == idx, -jnp.inf, logits)
    es = [jnp.exp(v - vals[0]) for v in vals]
    tot = (es[0] + es[1]) + (es[2] + es[3])

    @pl.when((pl.program_id(0) == 0) & (pl.program_id(1) == 0))
    def _():
        cnt_ref[...] = jnp.zeros_like(cnt_ref)

    hot = [erow == idxs[k] for k in range(TOP_K)]
    picked = jnp.where((hot[0] | hot[1]) | (hot[2] | hot[3]), 1.0, 0.0)
    row_t = lax.broadcasted_iota(jnp.int32, (TM, TM), 0)
    col_t = lax.broadcasted_iota(jnp.int32, (TM, TM), 1)
    earlier = jnp.where(row_t < col_t, 1.0, 0.0).astype(BF16)
    seen = _dot(picked.astype(BF16), earlier) + cnt_ref[:, 0:1]
    cnt_ref[...] = cnt_ref[...] + jnp.sum(picked, axis=1, keepdims=True)
    cnt_out_ref[...] = cnt_ref[...]
    ranks = [jnp.sum(jnp.where(hot[k], seen, 0.0), axis=0, keepdims=True) for k in range(TOP_K)]
    route_ref[...] = jnp.concatenate([i.astype(F32) for i in idxs] + [e / tot for e in es] + ranks +
                                     [jnp.zeros((ROUTE_ROWS - 3 * TOP_K, TM), F32)], axis=0)


def _mixer(x, mod8, norm_mix_g, w_in, lb_params, hgrn_norm_g, gmlp_ln_g, gmlp_ln_b, gmlp_ws, gmlp_bs,
           gmlp_norm_g, w_out, norm_ffn_g, router_w, router_b):
    bsz, seq, d = x.shape
    n_in = w_in.shape[1]
    rw = router_w.T
    rwh = rw.astype(BF16)
    rwl = (rw - rwh.astype(F32)).astype(BF16)
    rb = router_b.reshape(N_EXPERTS, 1)
    const = lambda *shape: pl.BlockSpec(shape, lambda b, i: (0,) * len(shape))
    tile = lambda w: pl.BlockSpec((None, TM, w), lambda b, i: (b, i, 0))
    per_seq = seq // TM
    return pl.pallas_call(
        _mixer_kernel,
        grid=(bsz, seq // TM),
        in_specs=[tile(d),
                  pl.BlockSpec((None, 8, d), lambda b, i: (b, 0, 0)),
                  const(1, d), const(d, n_in), const(2, HGRN_WIDTH), const(1, HGRN_WIDTH),
                  const(1, GMLP_WIDTH), const(1, GMLP_WIDTH),
                  const(GMLP_GROUPS, GMLP_CHUNK, GMLP_CHUNK), const(GMLP_GROUPS, GMLP_CHUNK, 1),
                  const(1, GMLP_WIDTH), const(d, d), const(1, d),
                  const(N_EXPERTS, d), const(N_EXPERTS, d), const(N_EXPERTS, 1)],
        out_specs=[tile(d), tile(d),
                   pl.BlockSpec((None, ROUTE_ROWS, TM), lambda b, i: (b * per_seq + i, 0, 0)),
                   const(N_EXPERTS, LANES)],
        out_shape=[jax.ShapeDtypeStruct((bsz, seq, d), F32),
                   jax.ShapeDtypeStruct((bsz, seq, d), F32),
                   jax.ShapeDtypeStruct((bsz * per_seq, ROUTE_ROWS, TM), F32),
                   jax.ShapeDtypeStruct((N_EXPERTS, LANES), F32)],
        scratch_shapes=[pltpu.VMEM((TM, n_in), F32),
                        pltpu.VMEM((TM, d), BF16),
                        pltpu.VMEM((HGRN_HEADS, HEAD_DIM, HEAD_DIM), F32),
                        pltpu.VMEM((N_EXPERTS, LANES), F32)],
        compiler_params=pltpu.CompilerParams(dimension_semantics=("arbitrary", "arbitrary"),
                                             vmem_limit_bytes=VMEM_LIMIT),
        name="mixer",
    )(x, mod8, norm_mix_g.reshape(1, d), w_in.astype(BF16), lb_params, hgrn_norm_g.reshape(1, -1),
      gmlp_ln_g.reshape(1, -1), gmlp_ln_b.reshape(1, -1), gmlp_ws, gmlp_bs[:, :, None],
      gmlp_norm_g.reshape(1, -1), w_out.astype(BF16), norm_ffn_g.reshape(1, d), rwh, rwl, rb)


def _tile(r):
    return pl.ds(r * SUBLANES, SUBLANES)


def _tile8(r8):
    return pl.ds(pl.multiple_of(r8, SUBLANES), SUBLANES)


def _chunk(j, n):
    return pl.ds(j, n, stride=SUBLANES)


def _dispatch_kernel(n_pad_step, dest_ref, pad_ref, h2_ref, xs_hbm, rows, zero_tile, sem):
    i = pl.program_id(0)
    n_steps = pl.num_programs(0)

    def wait_step(s):
        for _ in range(TOP_K):
            pltpu.make_async_copy(rows.at[s], xs_hbm.at[pl.ds(0, TD * SUBLANES)], sem.at[s]).wait()
        n = n_pad_step * SUBLANES
        pltpu.make_async_copy(rows.at[s, pl.ds(0, n)], xs_hbm.at[pl.ds(0, n)], sem.at[s]).wait()

    @pl.when(i == 0)
    def _():
        zero_tile[...] = jnp.zeros_like(zero_tile)

    def step(s):
        @pl.when(i >= 2)
        def _():
            wait_step(s)

        for j in range(SUBLANES):
            rows[s, _chunk(j, TD), :] = h2_ref[:, j * LANES:(j + 1) * LANES]
        for t in range(TD):
            for k in range(TOP_K):
                pltpu.make_async_copy(rows.at[s, _tile(t)], xs_hbm.at[_tile8(dest_ref[0, 0, t * TOP_K + k])],
                                      sem.at[s]).start(priority=k % N_DMA_QUEUES)
        for q in range(n_pad_step):
            pltpu.make_async_copy(zero_tile, xs_hbm.at[_tile8(pad_ref[0, 0, q])],
                                  sem.at[s]).start(priority=q % N_DMA_QUEUES)

        @pl.when(i == n_steps - 1)
        def _():
            wait_step(s)

            @pl.when(i >= 1)
            def _():
                wait_step(1 - s)

    _by_parity(i, step)


def _dispatch(h2, dest, pad_dest, p_rows):
    n_tok, d = h2.shape
    n_steps = n_tok // TD
    n_pad_step = pad_dest.shape[0] // n_steps
    return pl.pallas_call(
        functools.partial(_dispatch_kernel, n_pad_step),
        grid=(n_steps,),
        in_specs=[pl.BlockSpec((1, 1, TD * TOP_K), lambda i: (i, 0, 0), memory_space=pltpu.SMEM),
                  pl.BlockSpec((1, 1, n_pad_step), lambda i: (i, 0, 0), memory_space=pltpu.SMEM),
                  pl.BlockSpec((TD, d), lambda i: (i, 0))],
        out_specs=pl.BlockSpec(memory_space=pl.ANY),
        out_shape=jax.ShapeDtypeStruct((p_rows * SUBLANES, LANES), F32),
        scratch_shapes=[pltpu.VMEM((2, TD * SUBLANES, LANES), F32),
                        pltpu.VMEM((SUBLANES, LANES), F32),
                        pltpu.SemaphoreType.DMA((2,))],
        compiler_params=pltpu.CompilerParams(dimension_semantics=("arbitrary",)),
        name="dispatch",
    )(dest.reshape(n_steps, 1, TD * TOP_K), pad_dest.reshape(n_steps, 1, n_pad_step), h2)


def _moe_kernel(n_blocks, blk0_ref, nblk_ref, xs_hbm, wgu_hbm, bgu_ref, wd_hbm, bd_ref, ys_hbm,
                xbuf, ybuf, wgu_f, wd_f, wgu_bf, wd_bf, isem, osem, wsem):
    e = pl.program_id(0)
    n_exp = pl.num_programs(0)
    nb = nblk_ref[e]
    b0 = blk0_ref[e]
    d, d_ff = wd_f.shape[2], wd_f.shape[1]
    blk_rows = MOE_BLK * SUBLANES
    ws = e % 2
    gu_rows, d_rows = d // W_CHUNKS, d_ff // W_CHUNKS

    def block(b):
        return pl.ds(pl.multiple_of((b0 + b) * blk_rows, blk_rows), blk_rows)

    def x_copy(b, s):
        return pltpu.make_async_copy(xs_hbm.at[block(b)], xbuf.at[s], isem.at[s])

    def y_copy(b, s):
        return pltpu.make_async_copy(ybuf.at[s], ys_hbm.at[block(b)], osem.at[s])

    def w_start(ex, slot, c):
        r_gu = pl.ds(pl.multiple_of(c * gu_rows, gu_rows), gu_rows)
        r_d = pl.ds(pl.multiple_of(c * d_rows, d_rows), d_rows)
        pltpu.make_async_copy(wgu_hbm.at[ex, r_gu], wgu_f.at[slot, r_gu], wsem.at[slot]).start()
        pltpu.make_async_copy(wd_hbm.at[ex, r_d], wd_f.at[slot, r_d], wsem.at[slot]).start()

    def w_wait(slot):
        pltpu.make_async_copy(wgu_hbm.at[0], wgu_f.at[slot], wsem.at[slot]).wait()
        pltpu.make_async_copy(wd_hbm.at[0], wd_f.at[slot], wsem.at[slot]).wait()

    @pl.when(e == 0)
    def _():
        for c in range(W_CHUNKS):
            w_start(0, 0, c)

    @pl.when(nb > 0)
    def _():
        x_copy(0, 0).start(priority=BLOCK_DMA_PRIORITY)

    w_wait(ws)
    wgu_bf[...] = wgu_f[ws].astype(BF16)
    wd_bf[...] = wd_f[ws].astype(BF16)
    has_next = e + 1 < n_exp

    def body(b, carry):
        s = b % 2

        @pl.when(b + 1 < nb)
        def _():
            x_copy(b + 1, 1 - s).start(priority=BLOCK_DMA_PRIORITY)

        @pl.when(has_next & (b < W_CHUNKS))
        def _():
            w_start(e + 1, 1 - ws, b)

        x_copy(b, s).wait()

        @pl.when(b >= 2)
        def _():
            y_copy(b - 2, s).wait()

        xb = jnp.concatenate([xbuf[s, _chunk(j, MOE_BLK), :] for j in range(SUBLANES)], axis=-1).astype(BF16)
        gu = _dot(xb, wgu_bf[...]) + bgu_ref[...]
        gate = jnp.minimum(gu[:, :d_ff], SWIGLU_LIMIT)
        up = jnp.clip(gu[:, d_ff:], -SWIGLU_LIMIT, SWIGLU_LIMIT)
        glu = gate * jax.nn.sigmoid(SWIGLU_ALPHA * gate)
        yb = _dot(((up + 1.0) * glu).astype(BF16), wd_bf[...]) + bd_ref[...]
        for j in range(SUBLANES):
            ybuf[s, _chunk(j, MOE_BLK), :] = yb[:, j * LANES:(j + 1) * LANES]
        y_copy(b, s).start(priority=BLOCK_DMA_PRIORITY)
        return carry

    lax.fori_loop(0, nb, body, 0)

    @pl.when(has_next)
    def _():
        def rest(c, carry):
            w_start(e + 1, 1 - ws, c)
            return carry

        lax.fori_loop(jnp.minimum(nb, W_CHUNKS), W_CHUNKS, rest, 0)

    @pl.when(nb >= 1)
    def _():
        y_copy(nb - 1, (nb - 1) % 2).wait()

    @pl.when(nb >= 2)
    def _():
        y_copy(nb - 2, nb % 2).wait()

    @pl.when(e == pl.num_programs(0) - 1)
    def _():
        ybuf[0] = jnp.zeros_like(ybuf[0])

        def fill(b, carry):
            cp = y_copy(b, 0)
            cp.start()
            cp.wait()
            return carry

        lax.fori_loop(nb, n_blocks - b0, fill, 0)


def _moe(xs, blk0, nblk, w_gate_up, b_gate_up, w_down, b_down):
    n_blocks = xs.shape[0] // (MOE_BLK * SUBLANES)
    n_exp, d_ff, d = w_down.shape
    grid_spec = pltpu.PrefetchScalarGridSpec(
        num_scalar_prefetch=2,
        grid=(n_exp,),
        in_specs=[pl.BlockSpec(memory_space=pl.ANY),
                  pl.BlockSpec(memory_space=pl.ANY),
                  pl.BlockSpec((None, 1, 2 * d_ff), lambda e, b0, nb: (e, 0, 0)),
                  pl.BlockSpec(memory_space=pl.ANY),
                  pl.BlockSpec((None, 1, d), lambda e, b0, nb: (e, 0, 0))],
        out_specs=pl.BlockSpec(memory_space=pl.ANY),
        scratch_shapes=[pltpu.VMEM((2, MOE_BLK * SUBLANES, LANES), F32),
                        pltpu.VMEM((2, MOE_BLK * SUBLANES, LANES), F32),
                        pltpu.VMEM((2, d, 2 * d_ff), F32),
                        pltpu.VMEM((2, d_ff, d), F32),
                        pltpu.VMEM((d, 2 * d_ff), BF16),
                        pltpu.VMEM((d_ff, d), BF16),
                        pltpu.SemaphoreType.DMA((2,)),
                        pltpu.SemaphoreType.DMA((2,)),
                        pltpu.SemaphoreType.DMA((2,))],
    )
    return pl.pallas_call(
        functools.partial(_moe_kernel, n_blocks),
        grid_spec=grid_spec,
        out_shape=jax.ShapeDtypeStruct(xs.shape, F32),
        compiler_params=pltpu.CompilerParams(dimension_semantics=("arbitrary",),
                                             vmem_limit_bytes=VMEM_LIMIT),
        name="moe_ffn",
    )(blk0, nblk, xs, w_gate_up, b_gate_up[:, None, :], w_down, b_down[:, None, :])


def _combine_kernel(destc_ref, destn_ref, x1_ref, mod_ref, route_ref, fg_ref, ys_hbm, o_ref, buf0, buf1, sem):
    buf = (buf0, buf1)
    i = pl.program_id(0)
    n_steps = pl.num_programs(0)
    d = x1_ref.shape[-1]

    def start_gather(dest_ref, s, t0=0, t1=TD):
        for t in range(t0, t1):
            for k in range(TOP_K):
                pltpu.make_async_copy(ys_hbm.at[_tile8(dest_ref[0, 0, t * TOP_K + k])], buf[s].at[k, _tile(t)],
                                      sem.at[s]).start(priority=k % N_DMA_QUEUES)

    def wait_gather(s):
        for k in range(TOP_K):
            pltpu.make_async_copy(ys_hbm.at[pl.ds(0, TD * SUBLANES)], buf[s].at[k], sem.at[s]).wait()

    @pl.when(i == 0)
    def _():
        start_gather(destc_ref, 0)

    def step(s):
        wait_gather(s)
        gates = route_ref[...]
        gk = [gates[:, TOP_K + k:TOP_K + k + 1] for k in range(TOP_K)]
        gate2 = mod_ref[5:6, :]

        def residual(j):
            ls = slice(j * LANES, (j + 1) * LANES)
            rj = _chunk(j, TD)
            y = (gk[0] * buf[s][0, rj, :] + gk[1] * buf[s][1, rj, :]) + \
                (gk[2] * buf[s][2, rj, :] + gk[3] * buf[s][3, rj, :])
            return x1_ref[:, ls] + gate2[:, ls] * y

        ss = jnp.zeros((TD, 1), F32)
        per = TD // SUBLANES
        for j in range(SUBLANES):
            start_gather(destn_ref, 1 - s, j * per, (j + 1) * per)
            xj = residual(j)
            ss = ss + jnp.sum(xj * xj, axis=-1, keepdims=True)
        inv = lax.rsqrt(ss / d + EPS)
        for j in range(SUBLANES):
            ls = slice(j * LANES, (j + 1) * LANES)
            o_ref[:, ls] = residual(j) * inv * fg_ref[:, ls]

        @pl.when(i == n_steps - 1)
        def _():
            wait_gather(1 - s)

    _by_parity(i, step)


def _combine(x1, mod8, route, dest, ys, final_g):
    bsz, seq, d = x1.shape
    n_tok = bsz * seq
    n_steps = n_tok // TD
    per_seq = seq // TD
    dest3 = dest.reshape(n_steps, 1, TD * TOP_K)
    smem_blk = lambda f: pl.BlockSpec((1, 1, TD * TOP_K), f, memory_space=pltpu.SMEM)
    return pl.pallas_call(
        _combine_kernel,
        grid=(n_steps,),
        in_specs=[smem_blk(lambda i: (i, 0, 0)),
                  smem_blk(lambda i: (jnp.minimum(i + 1, n_steps - 1), 0, 0)),
                  pl.BlockSpec((TD, d), lambda i: (i, 0)),
                  pl.BlockSpec((None, 8, d), lambda i: (i // per_seq, 0, 0)),
                  pl.BlockSpec((TD, ROUTE_ROWS), lambda i: (i, 0)),
                  pl.BlockSpec((1, d), lambda i: (0, 0)),
                  pl.BlockSpec(memory_space=pl.ANY)],
        out_specs=pl.BlockSpec((TD, d), lambda i: (i, 0)),
        out_shape=jax.ShapeDtypeStruct((n_tok, d), F32),
        scratch_shapes=[pltpu.VMEM((TOP_K, TD * SUBLANES, LANES), F32),
                        pltpu.VMEM((TOP_K, TD * SUBLANES, LANES), F32),
                        pltpu.SemaphoreType.DMA((2,))],
        compiler_params=pltpu.CompilerParams(dimension_semantics=("arbitrary",),
                                             vmem_limit_bytes=VMEM_LIMIT),
        name="combine",
    )(dest3, dest3, x1.reshape(n_tok, d), mod8, route, final_g.reshape(1, d), ys).reshape(bsz, seq, d)


def _routing(route, counts_f, n_tok):
    nk = n_tok * TOP_K
    experts = jnp.arange(N_EXPERTS, dtype=jnp.int32)
    topi = route[:, :TOP_K].astype(jnp.int32)
    rank = route[:, 2 * TOP_K:3 * TOP_K].astype(jnp.int32)
    counts = counts_f[:, 0].astype(jnp.int32)
    padded = ((counts + MOE_BLK - 1) // MOE_BLK) * MOE_BLK
    pad_end = jnp.cumsum(padded)
    start_pad = pad_end - padded
    dest = rank + jnp.sum(jnp.where(topi[:, :, None] == experts[None, None, :], start_pad[None, None, :], 0), axis=-1)
    p_rows = ((nk + N_EXPERTS * (MOE_BLK - 1) + MOE_BLK - 1) // MOE_BLK) * MOE_BLK
    n_pad = padded - counts
    pad_cum = jnp.cumsum(n_pad)
    j = jnp.arange(p_rows - nk, dtype=jnp.int32)
    first_pad = pad_cum - n_pad
    mine = (j[:, None] >= first_pad[None, :]) & (j[:, None] < pad_cum[None, :])
    in_tail = jnp.sum(jnp.where(mine, (start_pad + counts - first_pad)[None, :], 0), axis=1) + j
    pad_dest = jnp.where(j < pad_cum[-1], in_tail, pad_end[-1] + (j - pad_cum[-1])).astype(jnp.int32)
    return (dest.reshape(-1) * SUBLANES, pad_dest * SUBLANES, (start_pad // MOE_BLK).astype(jnp.int32),
            (padded // MOE_BLK).astype(jnp.int32), p_rows)


def kernel(x, c, ada_w, ada_b, norm_mix_g, w_in, lb_params, hgrn_norm_g, gmlp_ln_g, gmlp_ln_b, gmlp_ws, gmlp_bs,
           gmlp_norm_g, w_out, norm_ffn_g, router_w, router_b, w_gate_up, b_gate_up, w_down, b_down, final_g):
    assert ada_w.shape[0] == 1, "single-layer block"
    bsz, seq, d = x.shape
    assert d == SUBLANES * LANES and seq % TM == 0 and (bsz * seq) % TD == 0
    n_tok = bsz * seq
    mod = _modulation(c, ada_w[0], ada_b[0])
    mod8 = jnp.zeros((bsz, 8, d), F32).at[:, :6].set(mod.reshape(bsz, 6, d))
    x1, h2, route, counts = _mixer(x, mod8, norm_mix_g[0], w_in[0], lb_params, hgrn_norm_g[0], gmlp_ln_g[0],
                                   gmlp_ln_b[0], gmlp_ws[0], gmlp_bs[0], gmlp_norm_g[0], w_out[0],
                                   norm_ffn_g[0], router_w[0], router_b[0])
    route = route.transpose(0, 2, 1).reshape(n_tok, ROUTE_ROWS)
    dest, pad_dest, blk0, nblk, p_rows = _routing(route, counts, n_tok)
    xs = _dispatch(h2.reshape(n_tok, d), dest, pad_dest, p_rows)
    ys = _moe(xs, blk0, nblk, w_gate_up[0], b_gate_up[0], w_down[0], b_down[0])
    return _combine(x1, mod8, route, dest, ys, final_g)
```

```python
import functools

import jax
import jax.numpy as jnp
from jax import lax
from jax.experimental import pallas as pl
from jax.experimental.pallas import tpu as pltpu

F32 = jnp.float32
BF16 = jnp.bfloat16

HGRN_HEADS = 4
HEAD_DIM = 128
HGRN_WIDTH = HGRN_HEADS * HEAD_DIM
HGRN_CHUNK = 64
HGRN_SUB = 16
GMLP_GROUPS = 4
GROUP_DIM = 128
GMLP_WIDTH = GMLP_GROUPS * GROUP_DIM
GMLP_CHUNK = 128
N_EXPERTS = 32
TOP_K = 4
SWIGLU_LIMIT = 7.0
SWIGLU_ALPHA = 1.702
EPS = 1e-6
LANES = 128
SUBLANES = 8
N_DMA_QUEUES = 2
BLOCK_DMA_PRIORITY = 1
ROUTE_ROWS = 16
W_CHUNKS = 8
DECAY_EXP_CLAMP = 60.0
TM = 512
MOE_BLK = 256
TD = 256
VMEM_LIMIT = 56 * 1024 * 1024


def _dot(a, b):
    return jnp.dot(a, b, preferred_element_type=F32)


def _dot_nt(a, b):
    return lax.dot_general(a, b, (((1,), (1,)), ((), ())), preferred_element_type=F32)


def _dot_tn(a, b):
    return lax.dot_general(a, b, (((0,), (0,)), ((), ())), preferred_element_type=F32)


def _rms(x):
    return x * lax.rsqrt(jnp.mean(x * x, axis=-1, keepdims=True) + EPS)


def _gelu(x):
    return 0.5 * x * (1.0 + lax.erf(x * 0.7071067811865476))


def _by_parity(i, fn):
    @pl.when(i % 2 == 0)
    def _():
        fn(0)

    @pl.when(i % 2 == 1)
    def _():
        fn(1)


def _mod_kernel(c_ref, w_ref, b_ref, o_ref):
    c = c_ref[...]
    ca = c * jax.nn.sigmoid(c)
    o_ref[...] = jnp.dot(ca, w_ref[...], precision=lax.Precision.HIGHEST,
                         preferred_element_type=F32) + b_ref[...]


def _modulation(c, ada_w, ada_b):
    bsz, d = c.shape
    n_out = ada_w.shape[1]
    rows = 8
    c_pad = jnp.zeros((rows, d), F32).at[:bsz].set(c)
    tn = 1024
    out = pl.pallas_call(
        _mod_kernel,
        grid=(n_out // tn,),
        in_specs=[pl.BlockSpec((rows, d), lambda j: (0, 0)),
                  pl.BlockSpec((d, tn), lambda j: (0, j)),
                  pl.BlockSpec((1, tn), lambda j: (0, j))],
        out_specs=pl.BlockSpec((rows, tn), lambda j: (0, j)),
        out_shape=jax.ShapeDtypeStruct((rows, n_out), F32),
        name="adaln_mod",
    )(c_pad, ada_w, ada_b.reshape(1, n_out))
    return out[:bsz]


def _mixer_kernel(x_ref, mod_ref, g1_ref, win_ref, lbp_ref, hg_ref, lng_ref, lnb_ref, ws_ref, bs_ref,
                  gng_ref, wout_ref, g2_ref, rwh_ref, rwl_ref, rb_ref,
                  x1_ref, h2_ref, route_ref, cnt_out_ref,
                  z_ref, y_ref, st_ref, cnt_ref):
    @pl.when(pl.program_id(1) == 0)
    def _():
        st_ref[...] = jnp.zeros_like(st_ref)

    x = x_ref[...]
    mod = mod_ref[...]
    h = _rms(x) * g1_ref[...]
    h = h * (1.0 + mod[1:2]) + mod[0:1]
    z_ref[...] = _dot(h.astype(BF16), win_ref[...])

    lbp = lbp_ref[...]
    lbe = jnp.exp(lbp - jnp.max(lbp, axis=0, keepdims=True))
    lb = lbe[0:1] / jnp.sum(lbe, axis=0, keepdims=True)
    hg = hg_ref[...]
    row = lax.broadcasted_iota(jnp.int32, (HGRN_CHUNK, HGRN_CHUNK), 0)
    col = lax.broadcasted_iota(jnp.int32, (HGRN_CHUNK, HGRN_CHUNK), 1)
    causal = col <= row
    tri = jnp.where(causal, 1.0, 0.0).astype(BF16)
    n_sub = HGRN_CHUNK // HGRN_SUB

    def chunk_body(c):
        rows = slice(c * HGRN_CHUNK, (c + 1) * HGRN_CHUNK)
        zq = z_ref[rows, 0:HGRN_WIDTH]
        zf = z_ref[rows, HGRN_WIDTH:2 * HGRN_WIDTH]
        zi = z_ref[rows, 2 * HGRN_WIDTH:3 * HGRN_WIDTH]
        zg = z_ref[rows, 3 * HGRN_WIDTH:4 * HGRN_WIDTH]
        f = lb + (1.0 - lb) * jax.nn.sigmoid(zf)
        logf = jnp.log(f)
        kk = 1.0 - f
        p0 = logf.astype(BF16)
        r0 = logf - p0.astype(F32)
        p1 = r0.astype(BF16)
        p2 = (r0 - p1.astype(F32)).astype(BF16)
        b = (_dot(tri, p0) + _dot(tri, p1)) + _dot(tri, p2)
        b_last = b[HGRN_CHUNK - 1:HGRN_CHUNK, :]
        qe = (zq * jnp.exp(b)).astype(BF16)
        kdec = (kk * jnp.exp(b_last - b)).astype(BF16)
        v = zi.astype(BF16)
        dec_last = jnp.exp(b_last)
        a_sub, k_sub = [], []
        for i in range(n_sub):
            lo, hi = i * HGRN_SUB, (i + 1) * HGRN_SUB
            bref = b[lo - 1:lo, :] if i > 0 else jnp.zeros((1, HGRN_WIDTH), F32)
            a_sub.append((zq[lo:hi] * jnp.exp(b[lo:hi] - bref)).astype(BF16))
            k_sub.append((kk * jnp.exp(jnp.minimum(bref - b, DECAY_EXP_CLAMP))).astype(BF16))
        silu_g = zg * jax.nn.sigmoid(zg)
        for hd in range(HGRN_HEADS):
            ls = slice(hd * HEAD_DIM, (hd + 1) * HEAD_DIM)
            sc = jnp.concatenate([_dot_nt(a_sub[i][:, ls], k_sub[i][:, ls]) for i in range(n_sub)], axis=0)
            sc = jnp.where(causal, sc, 0.0).astype(BF16)
            st = st_ref[hd]
            o = _dot_nt(qe[:, ls], st.astype(BF16)) + _dot(sc, v[:, ls])
            st_ref[hd] = st * dec_last[:, ls] + _dot_tn(v[:, ls], kdec[:, ls])
            o = _rms(o) * hg[:, ls]
            y_ref[rows, ls] = (o * silu_g[:, ls]).astype(BF16)

    for c in range(TM // HGRN_CHUNK):
        chunk_body(c)

    u = _gelu(z_ref[:, 4 * HGRN_WIDTH:4 * HGRN_WIDTH + GMLP_WIDTH])
    gv = _gelu(z_ref[:, 4 * HGRN_WIDTH + GMLP_WIDTH:])
    mu = jnp.mean(gv, axis=-1, keepdims=True)
    gc = gv - mu
    var = jnp.mean(gc * gc, axis=-1, keepdims=True)
    vn = (gc * lax.rsqrt(var + EPS) * lng_ref[...] + lnb_ref[...]).astype(BF16)
    row_g = lax.broadcasted_iota(jnp.int32, (GMLP_CHUNK, GMLP_CHUNK), 0)
    col_g = lax.broadcasted_iota(jnp.int32, (GMLP_CHUNK, GMLP_CHUNK), 1)
    gng = gng_ref[...]
    for g in range(GMLP_GROUPS):
        ls = slice(g * GROUP_DIM, (g + 1) * GROUP_DIM)
        ws_c = jnp.where(col_g <= row_g, ws_ref[g], 0.0).astype(BF16)
        bias = bs_ref[g]
        for n in range(TM // GMLP_CHUNK):
            rs = slice(n * GMLP_CHUNK, (n + 1) * GMLP_CHUNK)
            sv = _dot(ws_c, vn[rs, ls]) + bias
            yy = _rms(u[rs, ls] * sv) * gng[:, ls]
            y_ref[rs, HGRN_WIDTH + g * GROUP_DIM:HGRN_WIDTH + (g + 1) * GROUP_DIM] = yy.astype(BF16)

    x1 = x + mod[2:3] * _dot(y_ref[...], wout_ref[...])
    x1_ref[...] = x1
    h2 = _rms(x1) * g2_ref[...]
    h2 = h2 * (1.0 + mod[4:5]) + mod[3:4]
    h2_ref[...] = h2

    hh = h2.astype(BF16)
    hl = (h2 - hh.astype(F32)).astype(BF16)
    rwh, rwl = rwh_ref[...], rwl_ref[...]
    logits = _dot_nt(rwh, hh) + (_dot_nt(rwh, hl) + _dot_nt(rwl, hh)) + rb_ref[...]
    erow = lax.broadcasted_iota(jnp.int32, (N_EXPERTS, TM), 0)
    vals, idxs = [], []
    for _ in range(TOP_K):
        m = jnp.max(logits, axis=0, keepdims=True)
        idx = jnp.min(jnp.where(logits == m, erow, N_EXPERTS), axis=0, keepdims=True)
        vals.append(m)
        idxs.append(idx)
        logits = jnp.where(erow == idx, -jnp.inf, logits)
    es = [jnp.exp(v - vals[0]) for v in vals]
    tot = (es[0] + es[1]) + (es[2] + es[3])

    @pl.when((pl.program_id(0) == 0) & (pl.program_id(1) == 0))
    def _():
        cnt_ref[...] = jnp.zeros_like(cnt_ref)

    hot = [erow == idxs[k] for k in range(TOP_K)]
    picked = jnp.where((hot[0] | hot[1]) | (hot[2] | hot[3]), 1.0, 0.0)
    row_t = lax.broadcasted_iota(jnp.int32, (TM, TM), 0)
    col_t = lax.broadcasted_iota(jnp.int32, (TM, TM), 1)
    earlier = jnp.where(row_t < col_t, 1.0, 0.0).astype(BF16)
    seen = _dot(picked.astype(BF16), earlier) + cnt_ref[:, 0:1]
    cnt_ref[...] = cnt_ref[...] + jnp.sum(picked, axis=1, keepdims=True)
    cnt_out_ref[...] = cnt_ref[...]
    ranks = [jnp.sum(jnp.where(hot[k], seen, 0.0), axis=0, keepdims=True) for k in range(TOP_K)]
    route_ref[...] = jnp.concatenate([i.astype(F32) for i in idxs] + [e / tot for e in es] + ranks +
                                     [jnp.zeros((ROUTE_ROWS - 3 * TOP_K, TM), F32)], axis=0)


def _mixer(x, mod8, norm_mix_g, w_in, lb_params, hgrn_norm_g, gmlp_ln_g, gmlp_ln_b, gmlp_ws, gmlp_bs,
           gmlp_norm_g, w_out, norm_ffn_g, router_w, router_b):
    bsz, seq, d = x.shape
    n_in = w_in.shape[1]
    rw = router_w.T
    rwh = rw.astype(BF16)
    rwl = (rw - rwh.astype(F32)).astype(BF16)
    rb = router_b.reshape(N_EXPERTS, 1)
    const = lambda *shape: pl.BlockSpec(shape, lambda b, i: (0,) * len(shape))
    tile = lambda w: pl.BlockSpec((None, TM, w), lambda b, i: (b, i, 0))
    per_seq = seq // TM
    return pl.pallas_call(
        _mixer_kernel,
        grid=(bsz, seq // TM),
        in_specs=[tile(d),
                  pl.BlockSpec((None, 8, d), lambda b, i: (b, 0, 0)),
                  const(1, d), const(d, n_in), const(2, HGRN_WIDTH), const(1, HGRN_WIDTH),
                  const(1, GMLP_WIDTH), const(1, GMLP_WIDTH),
                  const(GMLP_GROUPS, GMLP_CHUNK, GMLP_CHUNK), const(GMLP_GROUPS, GMLP_CHUNK, 1),
                  const(1, GMLP_WIDTH), const(d, d), const(1, d),
                  const(N_EXPERTS, d), const(N_EXPERTS, d), const(N_EXPERTS, 1)],
        out_specs=[tile(d), tile(d),
                   pl.BlockSpec((None, ROUTE_ROWS, TM), lambda b, i: (b * per_seq + i, 0, 0)),
                   const(N_EXPERTS, LANES)],
        out_shape=[jax.ShapeDtypeStruct((bsz, seq, d), F32),
                   jax.ShapeDtypeStruct((bsz, seq, d), F32),
                   jax.ShapeDtypeStruct((bsz * per_seq, ROUTE_ROWS, TM), F32),
                   jax.ShapeDtypeStruct((N_EXPERTS, LANES), F32)],
        scratch_shapes=[pltpu.VMEM((TM, n_in), F32),
                        pltpu.VMEM((TM, d), BF16),
                        pltpu.VMEM((HGRN_HEADS, HEAD_DIM, HEAD_DIM), F32),
                        pltpu.VMEM((N_EXPERTS, LANES), F32)],
        compiler_params=pltpu.CompilerParams(dimension_semantics=("arbitrary", "arbitrary"),
                                             vmem_limit_bytes=VMEM_LIMIT),
        name="mixer",
    )(x, mod8, norm_mix_g.reshape(1, d), w_in.astype(BF16), lb_params, hgrn_norm_g.reshape(1, -1),
      gmlp_ln_g.reshape(1, -1), gmlp_ln_b.reshape(1, -1), gmlp_ws, gmlp_bs[:, :, None],
      gmlp_norm_g.reshape(1, -1), w_out.astype(BF16), norm_ffn_g.reshape(1, d), rwh, rwl, rb)


def _tile(r):
    return pl.ds(r * SUBLANES, SUBLANES)


def _tile8(r8):
    return pl.ds(pl.multiple_of(r8, SUBLANES), SUBLANES)


def _chunk(j, n):
    return pl.ds(j, n, stride=SUBLANES)


def _dispatch_kernel(n_pad_step, dest_ref, pad_ref, h2_ref, xs_hbm, rows, zero_tile, sem):
    i = pl.program_id(0)
    n_steps = pl.num_programs(0)

    def wait_step(s):
        for _ in range(TOP_K):
            pltpu.make_async_copy(rows.at[s], xs_hbm.at[pl.ds(0, TD * SUBLANES)], sem.at[s]).wait()
        n = n_pad_step * SUBLANES
        pltpu.make_async_copy(rows.at[s, pl.ds(0, n)], xs_hbm.at[pl.ds(0, n)], sem.at[s]).wait()

    @pl.when(i == 0)
    def _():
        zero_tile[...] = jnp.zeros_like(zero_tile)

    def step(s):
        @pl.when(i >= 2)
        def _():
            wait_step(s)

        for j in range(SUBLANES):
            rows[s, _chunk(j, TD), :] = h2_ref[:, j * LANES:(j + 1) * LANES]
        for t in range(TD):
            for k in range(TOP_K):
                pltpu.make_async_copy(rows.at[s, _tile(t)], xs_hbm.at[_tile8(dest_ref[0, 0, t * TOP_K + k])],
                                      sem.at[s]).start(priority=k % N_DMA_QUEUES)
        for q in range(n_pad_step):
            pltpu.make_async_copy(zero_tile, xs_hbm.at[_tile8(pad_ref[0, 0, q])],
                                  sem.at[s]).start(priority=q % N_DMA_QUEUES)

        @pl.when(i == n_steps - 1)
        def _():
            wait_step(s)

            @pl.when(i >= 1)
            def _():
                wait_step(1 - s)

    _by_parity(i, step)


def _dispatch(h2, dest, pad_dest, p_rows):
    n_tok, d = h2.shape
    n_steps = n_tok // TD
    n_pad_step = pad_dest.shape[0] // n_steps
    return pl.pallas_call(
        functools.partial(_dispatch_kernel, n_pad_step),
        grid=(n_steps,),
        in_specs=[pl.BlockSpec((1, 1, TD * TOP_K), lambda i: (i, 0, 0), memory_space=pltpu.SMEM),
                  pl.BlockSpec((1, 1, n_pad_step), lambda i: (i, 0, 0), memory_space=pltpu.SMEM),
                  pl.BlockSpec((TD, d), lambda i: (i, 0))],
        out_specs=pl.BlockSpec(memory_space=pl.ANY),
        out_shape=jax.ShapeDtypeStruct((p_rows * SUBLANES, LANES), F32),
        scratch_shapes=[pltpu.VMEM((2, TD * SUBLANES, LANES), F32),
                        pltpu.VMEM((SUBLANES, LANES), F32),
                        pltpu.SemaphoreType.DMA((2,))],
        compiler_params=pltpu.CompilerParams(dimension_semantics=("arbitrary",)),
        name="dispatch",
    )(dest.reshape(n_steps, 1, TD * TOP_K), pad_dest.reshape(n_steps, 1, n_pad_step), h2)


def _moe_kernel(n_blocks, blk0_ref, nblk_ref, xs_hbm, wgu_hbm, bgu_ref, wd_hbm, bd_ref, ys_hbm,
                xbuf, ybuf, wgu_f, wd_f, wgu_bf, wd_bf, isem, osem, wsem):
    e = pl.program_id(0)
    n_exp = pl.num_programs(0)
    nb = nblk_ref[e]
    b0 = blk0_ref[e]
    d, d_ff = wd_f.shape[2], wd_f.shape[1]
    blk_rows = MOE_BLK * SUBLANES
    ws = e % 2
    gu_rows, d_rows = d // W_CHUNKS, d_ff // W_CHUNKS

    def block(b):
        return pl.ds(pl.multiple_of((b0 + b) * blk_rows, blk_rows), blk_rows)

    def x_copy(b, s):
        return pltpu.make_async_copy(xs_hbm.at[block(b)], xbuf.at[s], isem.at[s])

    def y_copy(b, s):
        return pltpu.make_async_copy(ybuf.at[s], ys_hbm.at[block(b)], osem.at[s])

    def w_start(ex, slot, c):
        r_gu = pl.ds(pl.multiple_of(c * gu_rows, gu_rows), gu_rows)
        r_d = pl.ds(pl.multiple_of(c * d_rows, d_rows), d_rows)
        pltpu.make_async_copy(wgu_hbm.at[ex, r_gu], wgu_f.at[slot, r_gu], wsem.at[slot]).start()
        pltpu.make_async_copy(wd_hbm.at[ex, r_d], wd_f.at[slot, r_d], wsem.at[slot]).start()

    def w_wait(slot):
        pltpu.make_async_copy(wgu_hbm.at[0], wgu_f.at[slot], wsem.at[slot]).wait()
        pltpu.make_async_copy(wd_hbm.at[0], wd_f.at[slot], wsem.at[slot]).wait()

    @pl.when(e == 0)
    def _():
        for c in range(W_CHUNKS):
            w_start(0, 0, c)

    @pl.when(nb > 0)
    def _():
        x_copy(0, 0).start(priority=BLOCK_DMA_PRIORITY)

    w_wait(ws)
    wgu_bf[...] = wgu_f[ws].astype(BF16)
    wd_bf[...] = wd_f[ws].astype(BF16)
    has_next = e + 1 < n_exp

    def body(b, carry):
        s = b % 2

        @pl.when(b + 1 < nb)
        def _():
            x_copy(b + 1, 1 - s).start(priority=BLOCK_DMA_PRIORITY)

        @pl.when(has_next & (b < W_CHUNKS))
        def _():
            w_start(e + 1, 1 - ws, b)

        x_copy(b, s).wait()

        @pl.when(b >= 2)
        def _():
            y_copy(b - 2, s).wait()

        xb = jnp.concatenate([xbuf[s, _chunk(j, MOE_BLK), :] for j in range(SUBLANES)], axis=-1).astype(BF16)
        gu = _dot(xb, wgu_bf[...]) + bgu_ref[...]
        gate = jnp.minimum(gu[:, :d_ff], SWIGLU_LIMIT)
        up = jnp.clip(gu[:, d_ff:], -SWIGLU_LIMIT, SWIGLU_LIMIT)
        glu = gate * jax.nn.sigmoid(SWIGLU_ALPHA * gate)
        yb = _dot(((up + 1.0) * glu).astype(BF16), wd_bf[...]) + bd_ref[...]
        for j in range(SUBLANES):
            ybuf[s, _chunk(j, MOE_BLK), :] = yb[:, j * LANES:(j + 1) * LANES]
        y_copy(b, s).start(priority=BLOCK_DMA_PRIORITY)
        return carry

    lax.fori_loop(0, nb, body, 0)

    @pl.when(has_next)
    def _():
        def rest(c, carry):
            w_start(e + 1, 1 - ws, c)
            return carry

        lax.fori_loop(jnp.minimum(nb, W_CHUNKS), W_CHUNKS, rest, 0)

    @pl.when(nb >= 1)
    def _():
        y_copy(nb - 1, (nb - 1) % 2).wait()

    @pl.when(nb >= 2)
    def _():
        y_copy(nb - 2, nb % 2).wait()

    @pl.when(e == pl.num_programs(0) - 1)
    def _():
        ybuf[0] = jnp.zeros_like(ybuf[0])

        def fill(b, carry):
            cp = y_copy(b, 0)
            cp.start()
            cp.wait()
            return carry

        lax.fori_loop(nb, n_blocks - b0, fill, 0)


def _moe(xs, blk0, nblk, w_gate_up, b_gate_up, w_down, b_down):
    n_blocks = xs.shape[0] // (MOE_BLK * SUBLANES)
    n_exp, d_ff, d = w_down.shape
    grid_spec = pltpu.PrefetchScalarGridSpec(
        num_scalar_prefetch=2,
        grid=(n_exp,),
        in_specs=[pl.BlockSpec(memory_space=pl.ANY),
                  pl.BlockSpec(memory_space=pl.ANY),
                  pl.BlockSpec((None, 1, 2 * d_ff), lambda e, b0, nb: (e, 0, 0)),
                  pl.BlockSpec(memory_space=pl.ANY),
                  pl.BlockSpec((None, 1, d), lambda e, b0, nb: (e, 0, 0))],
        out_specs=pl.BlockSpec(memory_space=pl.ANY),
        scratch_shapes=[pltpu.VMEM((2, MOE_BLK * SUBLANES, LANES), F32),
                        pltpu.VMEM((2, MOE_BLK * SUBLANES, LANES), F32),
                        pltpu.VMEM((2, d, 2 * d_ff), F32),
                        pltpu.VMEM((2, d_ff, d), F32),
                        pltpu.VMEM((d, 2 * d_ff), BF16),
                        pltpu.VMEM((d_ff, d), BF16),
                        pltpu.SemaphoreType.DMA((2,)),
                        pltpu.SemaphoreType.DMA((2,)),
                        pltpu.SemaphoreType.DMA((2,))],
    )
    return pl.pallas_call(
        functools.partial(_moe_kernel, n_blocks),
        grid_spec=grid_spec,
        out_shape=jax.ShapeDtypeStruct(xs.shape, F32),
        compiler_params=pltpu.CompilerParams(dimension_semantics=("arbitrary",),
                                             vmem_limit_bytes=VMEM_LIMIT),
        name="moe_ffn",
    )(blk0, nblk, xs, w_gate_up, b_gate_up[:, None, :], w_down, b_down[:, None, :])


def _combine_kernel(destc_ref, destn_ref, x1_ref, mod_ref, route_ref, fg_ref, ys_hbm, o_ref, buf0, buf1, sem):
    buf = (buf0, buf1)
    i = pl.program_id(0)
    n_steps = pl.num_programs(0)
    d = x1_ref.shape[-1]

    def start_gather(dest_ref, s, t0=0, t1=TD):
        for t in range(t0, t1):
            for k in range(TOP_K):
                pltpu.make_async_copy(ys_hbm.at[_tile8(dest_ref[0, 0, t * TOP_K + k])], buf[s].at[k, _tile(t)],
                                      sem.at[s]).start(priority=k % N_DMA_QUEUES)

    def wait_gather(s):
        for k in range(TOP_K):
            pltpu.make_async_copy(ys_hbm.at[pl.ds(0, TD * SUBLANES)], buf[s].at[k], sem.at[s]).wait()

    @pl.when(i == 0)
    def _():
        start_gather(destc_ref, 0)

    def step(s):
        wait_gather(s)
        gates = route_ref[...]
        gk = [gates[:, TOP_K + k:TOP_K + k + 1] for k in range(TOP_K)]
        gate2 = mod_ref[5:6, :]

        def residual(j):
            ls = slice(j * LANES, (j + 1) * LANES)
            rj = _chunk(j, TD)
            y = (gk[0] * buf[s][0, rj, :] + gk[1] * buf[s][1, rj, :]) + \
                (gk[2] * buf[s][2, rj, :] + gk[3] * buf[s][3, rj, :])
            return x1_ref[:, ls] + gate2[:, ls] * y

        ss = jnp.zeros((TD, 1), F32)
        per = TD // SUBLANES
        for j in range(SUBLANES):
            start_gather(destn_ref, 1 - s, j * per, (j + 1) * per)
            xj = residual(j)
            ss = ss + jnp.sum(xj * xj, axis=-1, keepdims=True)
        inv = lax.rsqrt(ss / d + EPS)
        for j in range(SUBLANES):
            ls = slice(j * LANES, (j + 1) * LANES)
            o_ref[:, ls] = residual(j) * inv * fg_ref[:, ls]

        @pl.when(i == n_steps - 1)
        def _():
            wait_gather(1 - s)

    _by_parity(i, step)


def _combine(x1, mod8, route, dest, ys, final_g):
    bsz, seq, d = x1.shape
    n_tok = bsz * seq
    n_steps = n_tok // TD
    per_seq = seq // TD
    dest3 = dest.reshape(n_steps, 1, TD * TOP_K)
    smem_blk = lambda f: pl.BlockSpec((1, 1, TD * TOP_K), f, memory_space=pltpu.SMEM)
    return pl.pallas_call(
        _combine_kernel,
        grid=(n_steps,),
        in_specs=[smem_blk(lambda i: (i, 0, 0)),
                  smem_blk(lambda i: (jnp.minimum(i + 1, n_steps - 1), 0, 0)),
                  pl.BlockSpec((TD, d), lambda i: (i, 0)),
                  pl.BlockSpec((None, 8, d), lambda i: (i // per_seq, 0, 0)),
                  pl.BlockSpec((TD, ROUTE_ROWS), lambda i: (i, 0)),
                  pl.BlockSpec((1, d), lambda i: (0, 0)),
                  pl.BlockSpec(memory_space=pl.ANY)],
        out_specs=pl.BlockSpec((TD, d), lambda i: (i, 0)),
        out_shape=jax.ShapeDtypeStruct((n_tok, d), F32),
        scratch_shapes=[pltpu.VMEM((TOP_K, TD * SUBLANES, LANES), F32),
                        pltpu.VMEM((TOP_K, TD * SUBLANES, LANES), F32),
                        pltpu.SemaphoreType.DMA((2,))],
        compiler_params=pltpu.CompilerParams(dimension_semantics=("arbitrary",),
                                             vmem_limit_bytes=VMEM_LIMIT),
        name="combine",
    )(dest3, dest3, x1.reshape(n_tok, d), mod8, route, final_g.reshape(1, d), ys).reshape(bsz, seq, d)


def _routing(route, counts_f, n_tok):
    nk = n_tok * TOP_K
    experts = jnp.arange(N_EXPERTS, dtype=jnp.int32)
    topi = route[:, :TOP_K].astype(jnp.int32)
    rank = route[:, 2 * TOP_K:3 * TOP_K].astype(jnp.int32)
    counts = counts_f[:, 0].astype(jnp.int32)
    padded = ((counts + MOE_BLK - 1) // MOE_BLK) * MOE_BLK
    pad_end = jnp.cumsum(padded)
    start_pad = pad_end - padded
    dest = rank + jnp.sum(jnp.where(topi[:, :, None] == experts[None, None, :], start_pad[None, None, :], 0), axis=-1)
    p_rows = ((nk + N_EXPERTS * (MOE_BLK - 1) + MOE_BLK - 1) // MOE_BLK) * MOE_BLK
    n_pad = padded - counts
    pad_cum = jnp.cumsum(n_pad)
    j = jnp.arange(p_rows - nk, dtype=jnp.int32)
    first_pad = pad_cum - n_pad
    mine = (j[:, None] >= first_pad[None, :]) & (j[:, None] < pad_cum[None, :])
    in_tail = jnp.sum(jnp.where(mine, (start_pad + counts - first_pad)[None, :], 0), axis=1) + j
    pad_dest = jnp.where(j < pad_cum[-1], in_tail, pad_end[-1] + (j - pad_cum[-1])).astype(jnp.int32)
    return (dest.reshape(-1) * SUBLANES, pad_dest * SUBLANES, (start_pad // MOE_BLK).astype(jnp.int32),
            (padded // MOE_BLK).astype(jnp.int32), p_rows)


def kernel(x, c, ada_w, ada_b, norm_mix_g, w_in, lb_params, hgrn_norm_g, gmlp_ln_g, gmlp_ln_b, gmlp_ws, gmlp_bs,
           gmlp_norm_g, w_out, norm_ffn_g, router_w, router_b, w_gate_up, b_gate_up, w_down, b_down, final_g):
    assert ada_w.shape[0] == 1, "single-layer block"
    bsz, seq, d = x.shape
    assert d == SUBLANES * LANES and seq % TM == 0 and (bsz * seq) % TD == 0
    n_tok = bsz * seq
    mod = _modulation(c, ada_w[0], ada_b[0])
    mod8 = jnp.zeros((bsz, 8, d), F32).at[:, :6].set(mod.reshape(bsz, 6, d))
    x1, h2, route, counts = _mixer(x, mod8, norm_mix_g[0], w_in[0], lb_params, hgrn_norm_g[0], gmlp_ln_g[0],
                                   gmlp_ln_b[0], gmlp_ws[0], gmlp_bs[0], gmlp_norm_g[0], w_out[0],
                                   norm_ffn_g[0], router_w[0], router_b[0])
    route = route.transpose(0, 2, 1).reshape(n_tok, ROUTE_ROWS)
    dest, pad_dest, blk0, nblk, p_rows = _routing(route, counts, n_tok)
    xs = _dispatch(h2.reshape(n_tok, d), dest, pad_dest, p_rows)
    ys = _moe(xs, blk0, nblk, w_gate_up[0], b_gate_up[0], w_down[0], b_down[0])
    return _combine(x1, mod8, route, dest, ys, final_g)
```

```python
import functools

import jax
import jax.numpy as jnp
from jax import lax
from jax.experimental import pallas as pl
from jax.experimental.pallas import tpu as pltpu

F32 = jnp.float32
BF16 = jnp.bfloat16

HGRN_HEADS = 4
HEAD_DIM = 128
HGRN_WIDTH = HGRN_HEADS * HEAD_DIM
HGRN_CHUNK = 64
HGRN_SUB = 16
GMLP_GROUPS = 4
GROUP_DIM = 128
GMLP_WIDTH = GMLP_GROUPS * GROUP_DIM
GMLP_CHUNK = 128
N_EXPERTS = 32
TOP_K = 4
SWIGLU_LIMIT = 7.0
SWIGLU_ALPHA = 1.702
EPS = 1e-6
LANES = 128
SUBLANES = 8
N_DMA_QUEUES = 2
BLOCK_DMA_PRIORITY = 1
ROUTE_ROWS = 16
W_CHUNKS = 8
DECAY_EXP_CLAMP = 60.0
TM = 512
MOE_BLK = 256
TD = 256
VMEM_LIMIT = 56 * 1024 * 1024


def _dot(a, b):
    return jnp.dot(a, b, preferred_element_type=F32)


def _dot_nt(a, b):
    return lax.dot_general(a, b, (((1,), (1,)), ((), ())), preferred_element_type=F32)


def _dot_tn(a, b):
    return lax.dot_general(a, b, (((0,), (0,)), ((), ())), preferred_element_type=F32)


def _rms(x):
    return x * lax.rsqrt(jnp.mean(x * x, axis=-1, keepdims=True) + EPS)


def _gelu(x):
    return 0.5 * x * (1.0 + lax.erf(x * 0.7071067811865476))


def _by_parity(i, fn):
    @pl.when(i % 2 == 0)
    def _():
        fn(0)

    @pl.when(i % 2 == 1)
    def _():
        fn(1)


def _mod_kernel(c_ref, w_ref, b_ref, o_ref):
    c = c_ref[...]
    ca = c * jax.nn.sigmoid(c)
    o_ref[...] = jnp.dot(ca, w_ref[...], precision=lax.Precision.HIGHEST,
                         preferred_element_type=F32) + b_ref[...]


def _modulation(c, ada_w, ada_b):
    bsz, d = c.shape
    n_out = ada_w.shape[1]
    rows = 8
    c_pad = jnp.zeros((rows, d), F32).at[:bsz].set(c)
    tn = 1024
    out = pl.pallas_call(
        _mod_kernel,
        grid=(n_out // tn,),
        in_specs=[pl.BlockSpec((rows, d), lambda j: (0, 0)),
                  pl.BlockSpec((d, tn), lambda j: (0, j)),
                  pl.BlockSpec((1, tn), lambda j: (0, j))],
        out_specs=pl.BlockSpec((rows, tn), lambda j: (0, j)),
        out_shape=jax.ShapeDtypeStruct((rows, n_out), F32),
        name="adaln_mod",
    )(c_pad, ada_w, ada_b.reshape(1, n_out))
    return out[:bsz]


def _mixer_kernel(x_ref, mod_ref, g1_ref, win_ref, lbp_ref, hg_ref, lng_ref, lnb_ref, ws_ref, bs_ref,
                  gng_ref, wout_ref, g2_ref, rwh_ref, rwl_ref, rb_ref,
                  x1_ref, h2_ref, route_ref, cnt_out_ref,
                  z_ref, y_ref, st_ref, cnt_ref):
    @pl.when(pl.program_id(1) == 0)
    def _():
        st_ref[...] = jnp.zeros_like(st_ref)

    x = x_ref[...]
    mod = mod_ref[...]
    h = _rms(x) * g1_ref[...]
    h = h * (1.0 + mod[1:2]) + mod[0:1]
    z_ref[...] = _dot(h.astype(BF16), win_ref[...])

    lbp = lbp_ref[...]
    lbe = jnp.exp(lbp - jnp.max(lbp, axis=0, keepdims=True))
    lb = lbe[0:1] / jnp.sum(lbe, axis=0, keepdims=True)
    hg = hg_ref[...]
    row = lax.broadcasted_iota(jnp.int32, (HGRN_CHUNK, HGRN_CHUNK), 0)
    col = lax.broadcasted_iota(jnp.int32, (HGRN_CHUNK, HGRN_CHUNK), 1)
    causal = col <= row
    tri = jnp.where(causal, 1.0, 0.0).astype(BF16)
    n_sub = HGRN_CHUNK // HGRN_SUB

    def chunk_body(c):
        rows = slice(c * HGRN_CHUNK, (c + 1) * HGRN_CHUNK)
        zq = z_ref[rows, 0:HGRN_WIDTH]
        zf = z_ref[rows, HGRN_WIDTH:2 * HGRN_WIDTH]
        zi = z_ref[rows, 2 * HGRN_WIDTH:3 * HGRN_WIDTH]
        zg = z_ref[rows, 3 * HGRN_WIDTH:4 * HGRN_WIDTH]
        f = lb + (1.0 - lb) * jax.nn.sigmoid(zf)
        logf = jnp.log(f)
        kk = 1.0 - f
        p0 = logf.astype(BF16)
        r0 = logf - p0.astype(F32)
        p1 = r0.astype(BF16)
        p2 = (r0 - p1.astype(F32)).astype(BF16)
        b = (_dot(tri, p0) + _dot(tri, p1)) + _dot(tri, p2)
        b_last = b[HGRN_CHUNK - 1:HGRN_CHUNK, :]
        qe = (zq * jnp.exp(b)).astype(BF16)
        kdec = (kk * jnp.exp(b_last - b)).astype(BF16)
        v = zi.astype(BF16)
        dec_last = jnp.exp(b_last)
        a_sub, k_sub = [], []
        for i in range(n_sub):
            lo, hi = i * HGRN_SUB, (i + 1) * HGRN_SUB
            bref = b[lo - 1:lo, :] if i > 0 else jnp.zeros((1, HGRN_WIDTH), F32)
            a_sub.append((zq[lo:hi] * jnp.exp(b[lo:hi] - bref)).astype(BF16))
            k_sub.append((kk * jnp.exp(jnp.minimum(bref - b, DECAY_EXP_CLAMP))).astype(BF16))
        silu_g = zg * jax.nn.sigmoid(zg)
        for hd in range(HGRN_HEADS):
            ls = slice(hd * HEAD_DIM, (hd + 1) * HEAD_DIM)
            sc = jnp.concatenate([_dot_nt(a_sub[i][:, ls], k_sub[i][:, ls]) for i in range(n_sub)], axis=0)
            sc = jnp.where(causal, sc, 0.0).astype(BF16)
            st = st_ref[hd]
            o = _dot_nt(qe[:, ls], st.astype(BF16)) + _dot(sc, v[:, ls])
            st_ref[hd] = st * dec_last[:, ls] + _dot_tn(v[:, ls], kdec[:, ls])
            o = _rms(o) * hg[:, ls]
            y_ref[rows, ls] = (o * silu_g[:, ls]).astype(BF16)

    for c in range(TM // HGRN_CHUNK):
        chunk_body(c)

    u = _gelu(z_ref[:, 4 * HGRN_WIDTH:4 * HGRN_WIDTH + GMLP_WIDTH])
    gv = _gelu(z_ref[:, 4 * HGRN_WIDTH + GMLP_WIDTH:])
    mu = jnp.mean(gv, axis=-1, keepdims=True)
    gc = gv - mu
    var = jnp.mean(gc * gc, axis=-1, keepdims=True)
    vn = (gc * lax.rsqrt(var + EPS) * lng_ref[...] + lnb_ref[...]).astype(BF16)
    row_g = lax.broadcasted_iota(jnp.int32, (GMLP_CHUNK, GMLP_CHUNK), 0)
    col_g = lax.broadcasted_iota(jnp.int32, (GMLP_CHUNK, GMLP_CHUNK), 1)
    gng = gng_ref[...]
    for g in range(GMLP_GROUPS):
        ls = slice(g * GROUP_DIM, (g + 1) * GROUP_DIM)
        ws_c = jnp.where(col_g <= row_g, ws_ref[g], 0.0).astype(BF16)
        bias = bs_ref[g]
        for n in range(TM // GMLP_CHUNK):
            rs = slice(n * GMLP_CHUNK, (n + 1) * GMLP_CHUNK)
            sv = _dot(ws_c, vn[rs, ls]) + bias
            yy = _rms(u[rs, ls] * sv) * gng[:, ls]
            y_ref[rs, HGRN_WIDTH + g * GROUP_DIM:HGRN_WIDTH + (g + 1) * GROUP_DIM] = yy.astype(BF16)

    x1 = x + mod[2:3] * _dot(y_ref[...], wout_ref[...])
    x1_ref[...] = x1
    h2 = _rms(x1) * g2_ref[...]
    h2 = h2 * (1.0 + mod[4:5]) + mod[3:4]
    h2_ref[...] = h2

    hh = h2.astype(BF16)
    hl = (h2 - hh.astype(F32)).astype(BF16)
    rwh, rwl = rwh_ref[...], rwl_ref[...]
    logits = _dot_nt(rwh, hh) + (_dot_nt(rwh, hl) + _dot_nt(rwl, hh)) + rb_ref[...]
    erow = lax.broadcasted_iota(jnp.int32, (N_EXPERTS, TM), 0)
    vals, idxs = [], []
    for _ in range(TOP_K):
        m = jnp.max(logits, axis=0, keepdims=True)
        idx = jnp.min(jnp.where(logits == m, erow, N_EXPERTS), axis=0, keepdims=True)
        vals.append(m)
        idxs.append(idx)
        logits = jnp.where(erow == idx, -jnp.inf, logits)
    es = [jnp.exp(v - vals[0]) for v in vals]
    tot = (es[0] + es[1]) + (es[2] + es[3])

    @pl.when((pl.program_id(0) == 0) & (pl.program_id(1) == 0))
    def _():
        cnt_ref[...] = jnp.zeros_like(cnt_ref)

    hot = [erow == idxs[k] for k in range(TOP_K)]
    picked = jnp.where((hot[0] | hot[1]) | (hot[2] | hot[3]), 1.0, 0.0)
    row_t = lax.broadcasted_iota(jnp.int32, (TM, TM), 0)
    col_t = lax.broadcasted_iota(jnp.int32, (TM, TM), 1)
    earlier = jnp.where(row_t < col_t, 1.0, 0.0).astype(BF16)
    seen = _dot(picked.astype(BF16), earlier) + cnt_ref[:, 0:1]
    cnt_ref[...] = cnt_ref[...] + jnp.sum(picked, axis=1, keepdims=True)
    cnt_out_ref[...] = cnt_ref[...]
    ranks = [jnp.sum(jnp.where(hot[k], seen, 0.0), axis=0, keepdims=True) for k in range(TOP_K)]
    route_ref[...] = jnp.concatenate([i.astype(F32) for i in idxs] + [e / tot for e in es] + ranks +
                                     [jnp.zeros((ROUTE_ROWS - 3 * TOP_K, TM), F32)], axis=0)


def _mixer(x, mod8, norm_mix_g, w_in, lb_params, hgrn_norm_g, gmlp_ln_g, gmlp_ln_b, gmlp_ws, gmlp_bs,
           gmlp_norm_g, w_out, norm_ffn_g, router_w, router_b):
    bsz, seq, d = x.shape
    n_in = w_in.shape[1]
    rw = router_w.T
    rwh = rw.astype(BF16)
    rwl = (rw - rwh.astype(F32)).astype(BF16)
    rb = router_b.reshape(N_EXPERTS, 1)
    const = lambda *shape: pl.BlockSpec(shape, lambda b, i: (0,) * len(shape))
    tile = lambda w: pl.BlockSpec((None, TM, w), lambda b, i: (b, i, 0))
    per_seq = seq // TM
    return pl.pallas_call(
        _mixer_kernel,
        grid=(bsz, seq // TM),
        in_specs=[tile(d),
                  pl.BlockSpec((None, 8, d), lambda b, i: (b, 0, 0)),
                  const(1, d), const(d, n_in), const(2, HGRN_WIDTH), const(1, HGRN_WIDTH),
                  const(1, GMLP_WIDTH), const(1, GMLP_WIDTH),
                  const(GMLP_GROUPS, GMLP_CHUNK, GMLP_CHUNK), const(GMLP_GROUPS, GMLP_CHUNK, 1),
                  const(1, GMLP_WIDTH), const(d, d), const(1, d),
                  const(N_EXPERTS, d), const(N_EXPERTS, d), const(N_EXPERTS, 1)],
        out_specs=[tile(d), tile(d),
                   pl.BlockSpec((None, ROUTE_ROWS, TM), lambda b, i: (b * per_seq + i, 0, 0)),
                   const(N_EXPERTS, LANES)],
        out_shape=[jax.ShapeDtypeStruct((bsz, seq, d), F32),
                   jax.ShapeDtypeStruct((bsz, seq, d), F32),
                   jax.ShapeDtypeStruct((bsz * per_seq, ROUTE_ROWS, TM), F32),
                   jax.ShapeDtypeStruct((N_EXPERTS, LANES), F32)],
        scratch_shapes=[pltpu.VMEM((TM, n_in), F32),
                        pltpu.VMEM((TM, d), BF16),
                        pltpu.VMEM((HGRN_HEADS, HEAD_DIM, HEAD_DIM), F32),
                        pltpu.VMEM((N_EXPERTS, LANES), F32)],
        compiler_params=pltpu.CompilerParams(dimension_semantics=("arbitrary", "arbitrary"),
                                             vmem_limit_bytes=VMEM_LIMIT),
        name="mixer",
    )(x, mod8, norm_mix_g.reshape(1, d), w_in.astype(BF16), lb_params, hgrn_norm_g.reshape(1, -1),
      gmlp_ln_g.reshape(1, -1), gmlp_ln_b.reshape(1, -1), gmlp_ws, gmlp_bs[:, :, None],
      gmlp_norm_g.reshape(1, -1), w_out.astype(BF16), norm_ffn_g.reshape(1, d), rwh, rwl, rb)


def _tile(r):
    return pl.ds(r * SUBLANES, SUBLANES)


def _tile8(r8):
    return pl.ds(pl.multiple_of(r8, SUBLANES), SUBLANES)


def _chunk(j, n):
    return pl.ds(j, n, stride=SUBLANES)


def _dispatch_kernel(n_pad_step, dest_ref, pad_ref, h2_ref, xs_hbm, rows, zero_tile, sem):
    i = pl.program_id(0)
    n_steps = pl.num_programs(0)

    def wait_step(s):
        for _ in range(TOP_K):
            pltpu.make_async_copy(rows.at[s], xs_hbm.at[pl.ds(0, TD * SUBLANES)], sem.at[s]).wait()
        n = n_pad_step * SUBLANES
        pltpu.make_async_copy(rows.at[s, pl.ds(0, n)], xs_hbm.at[pl.ds(0, n)], sem.at[s]).wait()

    @pl.when(i == 0)
    def _():
        zero_tile[...] = jnp.zeros_like(zero_tile)

    def step(s):
        @pl.when(i >= 2)
        def _():
            wait_step(s)

        for j in range(SUBLANES):
            rows[s, _chunk(j, TD), :] = h2_ref[:, j * LANES:(j + 1) * LANES]
        for t in range(TD):
            for k in range(TOP_K):
                pltpu.make_async_copy(rows.at[s, _tile(t)], xs_hbm.at[_tile8(dest_ref[0, 0, t * TOP_K + k])],
                                      sem.at[s]).start(priority=k % N_DMA_QUEUES)
        for q in range(n_pad_step):
            pltpu.make_async_copy(zero_tile, xs_hbm.at[_tile8(pad_ref[0, 0, q])],
                                  sem.at[s]).start(priority=q % N_DMA_QUEUES)

        @pl.when(i == n_steps - 1)
        def _():
            wait_step(s)

            @pl.when(i >= 1)
            def _():
                wait_step(1 - s)

    _by_parity(i, step)


def _dispatch(h2, dest, pad_dest, p_rows):
    n_tok, d = h2.shape
    n_steps = n_tok // TD
    n_pad_step = pad_dest.shape[0] // n_steps
    return pl.pallas_call(
        functools.partial(_dispatch_kernel, n_pad_step),
        grid=(n_steps,),
        in_specs=[pl.BlockSpec((1, 1, TD * TOP_K), lambda i: (i, 0, 0), memory_space=pltpu.SMEM),
                  pl.BlockSpec((1, 1, n_pad_step), lambda i: (i, 0, 0), memory_space=pltpu.SMEM),
                  pl.BlockSpec((TD, d), lambda i: (i, 0))],
        out_specs=pl.BlockSpec(memory_space=pl.ANY),
        out_shape=jax.ShapeDtypeStruct((p_rows * SUBLANES, LANES), F32),
        scratch_shapes=[pltpu.VMEM((2, TD * SUBLANES, LANES), F32),
                        pltpu.VMEM((SUBLANES, LANES), F32),
                        pltpu.SemaphoreType.DMA((2,))],
        compiler_params=pltpu.CompilerParams(dimension_semantics=("arbitrary",)),
        name="dispatch",
    )(dest.reshape(n_steps, 1, TD * TOP_K), pad_dest.reshape(n_steps, 1, n_pad_step), h2)


def _moe_kernel(n_blocks, blk0_ref, nblk_ref, xs_hbm, wgu_hbm, bgu_ref, wd_hbm, bd_ref, ys_hbm,
                xbuf, ybuf, wgu_f, wd_f, wgu_bf, wd_bf, isem, osem, wsem):
    e = pl.program_id(0)
    n_exp = pl.num_programs(0)
    nb = nblk_ref[e]
    b0 = blk0_ref[e]
    d, d_ff = wd_f.shape[2], wd_f.shape[1]
    blk_rows = MOE_BLK * SUBLANES
    ws = e % 2
    gu_rows, d_rows = d // W_CHUNKS, d_ff // W_CHUNKS

    npairs = nb // 2
    odd = nb - 2 * npairs

    def rows_of(b, n):
        return pl.ds(pl.multiple_of((b0 + b) * blk_rows, blk_rows), n * blk_rows)

    def x_copy(b, n, s):
        return pltpu.make_async_copy(xs_hbm.at[rows_of(b, n)], xbuf.at[s, pl.ds(0, n * blk_rows)], isem.at[s])

    def y_copy(b, n, s):
        return pltpu.make_async_copy(ybuf.at[s, pl.ds(0, n * blk_rows)], ys_hbm.at[rows_of(b, n)], osem.at[s])

    def w_start(ex, slot, c):
        r_gu = pl.ds(pl.multiple_of(c * gu_rows, gu_rows), gu_rows)
        r_d = pl.ds(pl.multiple_of(c * d_rows, d_rows), d_rows)
        pltpu.make_async_copy(wgu_hbm.at[ex, r_gu], wgu_f.at[slot, r_gu], wsem.at[slot]).start()
        pltpu.make_async_copy(wd_hbm.at[ex, r_d], wd_f.at[slot, r_d], wsem.at[slot]).start()

    def w_wait(slot):
        pltpu.make_async_copy(wgu_hbm.at[0], wgu_f.at[slot], wsem.at[slot]).wait()
        pltpu.make_async_copy(wd_hbm.at[0], wd_f.at[slot], wsem.at[slot]).wait()

    @pl.when(e == 0)
    def _():
        for c in range(W_CHUNKS):
            w_start(0, 0, c)

    @pl.when(npairs > 0)
    def _():
        x_copy(0, 2, 0).start(priority=BLOCK_DMA_PRIORITY)

    @pl.when((npairs == 0) & (odd == 1))
    def _():
        x_copy(0, 1, 0).start(priority=BLOCK_DMA_PRIORITY)

    w_wait(ws)
    wgu_bf[...] = wgu_f[ws].astype(BF16)
    wd_bf[...] = wd_f[ws].astype(BF16)
    has_next = e + 1 < n_exp

    def ffn(n, s):
        rows = n * MOE_BLK
        xb = jnp.concatenate([xbuf[s, _chunk(j, rows), :] for j in range(SUBLANES)], axis=-1).astype(BF16)
        gu = _dot(xb, wgu_bf[...]) + bgu_ref[...]
        gate = jnp.minimum(gu[:, :d_ff], SWIGLU_LIMIT)
        up = jnp.clip(gu[:, d_ff:], -SWIGLU_LIMIT, SWIGLU_LIMIT)
        glu = gate * jax.nn.sigmoid(SWIGLU_ALPHA * gate)
        yb = _dot(((up + 1.0) * glu).astype(BF16), wd_bf[...]) + bd_ref[...]
        for j in range(SUBLANES):
            ybuf[s, _chunk(j, rows), :] = yb[:, j * LANES:(j + 1) * LANES]

    def pair_body(p, carry):
        s = p % 2

        @pl.when(p + 1 < npairs)
        def _():
            x_copy(2 * p + 2, 2, 1 - s).start(priority=BLOCK_DMA_PRIORITY)

        @pl.when((p + 1 == npairs) & (odd == 1))
        def _():
            x_copy(2 * p + 2, 1, 1 - s).start(priority=BLOCK_DMA_PRIORITY)

        @pl.when(has_next & (2 * p < W_CHUNKS))
        def _():
            w_start(e + 1, 1 - ws, 2 * p)
            w_start(e + 1, 1 - ws, 2 * p + 1)

        x_copy(2 * p, 2, s).wait()

        @pl.when(p >= 2)
        def _():
            y_copy(2 * p - 4, 2, s).wait()

        ffn(2, s)
        y_copy(2 * p, 2, s).start(priority=BLOCK_DMA_PRIORITY)
        return carry

    lax.fori_loop(0, npairs, pair_body, 0)

    @pl.when(odd == 1)
    def _():
        s = npairs % 2
        x_copy(2 * npairs, 1, s).wait()

        @pl.when(npairs >= 2)
        def _():
            y_copy(2 * npairs - 4, 2, s).wait()

        ffn(1, s)
        cp = y_copy(2 * npairs, 1, s)
        cp.start(priority=BLOCK_DMA_PRIORITY)
        cp.wait()

    @pl.when(has_next)
    def _():
        def rest(c, carry):
            w_start(e + 1, 1 - ws, c)
            return carry

        lax.fori_loop(jnp.minimum(2 * npairs, W_CHUNKS), W_CHUNKS, rest, 0)

    @pl.when(npairs >= 1)
    def _():
        y_copy(2 * npairs - 2, 2, (npairs - 1) % 2).wait()

    @pl.when((npairs >= 2) & (odd == 0))
    def _():
        y_copy(2 * npairs - 4, 2, npairs % 2).wait()

    @pl.when(e == pl.num_programs(0) - 1)
    def _():
        ybuf[0, pl.ds(0, blk_rows), :] = jnp.zeros((blk_rows, LANES), F32)

        def fill(b, carry):
            cp = y_copy(b, 1, 0)
            cp.start()
            cp.wait()
            return carry

        lax.fori_loop(nb, n_blocks - b0, fill, 0)


def _moe(xs, blk0, nblk, w_gate_up, b_gate_up, w_down, b_down):
    n_blocks = xs.shape[0] // (MOE_BLK * SUBLANES)
    n_exp, d_ff, d = w_down.shape
    grid_spec = pltpu.PrefetchScalarGridSpec(
        num_scalar_prefetch=2,
        grid=(n_exp,),
        in_specs=[pl.BlockSpec(memory_space=pl.ANY),
                  pl.BlockSpec(memory_space=pl.ANY),
                  pl.BlockSpec((None, 1, 2 * d_ff), lambda e, b0, nb: (e, 0, 0)),
                  pl.BlockSpec(memory_space=pl.ANY),
                  pl.BlockSpec((None, 1, d), lambda e, b0, nb: (e, 0, 0))],
        out_specs=pl.BlockSpec(memory_space=pl.ANY),
        scratch_shapes=[pltpu.VMEM((2, 2 * MOE_BLK * SUBLANES, LANES), F32),
                        pltpu.VMEM((2, 2 * MOE_BLK * SUBLANES, LANES), F32),
                        pltpu.VMEM((2, d, 2 * d_ff), F32),
                        pltpu.VMEM((2, d_ff, d), F32),
                        pltpu.VMEM((d, 2 * d_ff), BF16),
                        pltpu.VMEM((d_ff, d), BF16),
                        pltpu.SemaphoreType.DMA((2,)),
                        pltpu.SemaphoreType.DMA((2,)),
                        pltpu.SemaphoreType.DMA((2,))],
    )
    return pl.pallas_call(
        functools.partial(_moe_kernel, n_blocks),
        grid_spec=grid_spec,
        out_shape=jax.ShapeDtypeStruct(xs.shape, F32),
        compiler_params=pltpu.CompilerParams(dimension_semantics=("arbitrary",),
                                             vmem_limit_bytes=VMEM_LIMIT),
        name="moe_ffn",
    )(blk0, nblk, xs, w_gate_up, b_gate_up[:, None, :], w_down, b_down[:, None, :])


def _combine_kernel(destc_ref, destn_ref, x1_ref, mod_ref, route_ref, fg_ref, ys_hbm, o_ref, buf0, buf1, sem):
    buf = (buf0, buf1)
    i = pl.program_id(0)
    n_steps = pl.num_programs(0)
    d = x1_ref.shape[-1]

    def start_gather(dest_ref, s, t0=0, t1=TD):
        for t in range(t0, t1):
            for k in range(TOP_K):
                pltpu.make_async_copy(ys_hbm.at[_tile8(dest_ref[0, 0, t * TOP_K + k])], buf[s].at[k, _tile(t)],
                                      sem.at[s]).start(priority=k % N_DMA_QUEUES)

    def wait_gather(s):
        for k in range(TOP_K):
            pltpu.make_async_copy(ys_hbm.at[pl.ds(0, TD * SUBLANES)], buf[s].at[k], sem.at[s]).wait()

    @pl.when(i == 0)
    def _():
        start_gather(destc_ref, 0)

    def step(s):
        wait_gather(s)
        gates = route_ref[...]
        gk = [gates[:, TOP_K + k:TOP_K + k + 1] for k in range(TOP_K)]
        gate2 = mod_ref[5:6, :]

        def residual(j):
            ls = slice(j * LANES, (j + 1) * LANES)
            rj = _chunk(j, TD)
            y = (gk[0] * buf[s][0, rj, :] + gk[1] * buf[s][1, rj, :]) + \
                (gk[2] * buf[s][2, rj, :] + gk[3] * buf[s][3, rj, :])
            return x1_ref[:, ls] + gate2[:, ls] * y

        ss = jnp.zeros((TD, 1), F32)
        per = TD // SUBLANES
        for j in range(SUBLANES):
            start_gather(destn_ref, 1 - s, j * per, (j + 1) * per)
            xj = residual(j)
            ss = ss + jnp.sum(xj * xj, axis=-1, keepdims=True)
        inv = lax.rsqrt(ss / d + EPS)
        for j in range(SUBLANES):
            ls = slice(j * LANES, (j + 1) * LANES)
            o_ref[:, ls] = residual(j) * inv * fg_ref[:, ls]

        @pl.when(i == n_steps - 1)
        def _():
            wait_gather(1 - s)

    _by_parity(i, step)


def _combine(x1, mod8, route, dest, ys, final_g):
    bsz, seq, d = x1.shape
    n_tok = bsz * seq
    n_steps = n_tok // TD
    per_seq = seq // TD
    dest3 = dest.reshape(n_steps, 1, TD * TOP_K)
    smem_blk = lambda f: pl.BlockSpec((1, 1, TD * TOP_K), f, memory_space=pltpu.SMEM)
    return pl.pallas_call(
        _combine_kernel,
        grid=(n_steps,),
        in_specs=[smem_blk(lambda i: (i, 0, 0)),
                  smem_blk(lambda i: (jnp.minimum(i + 1, n_steps - 1), 0, 0)),
                  pl.BlockSpec((TD, d), lambda i: (i, 0)),
                  pl.BlockSpec((None, 8, d), lambda i: (i // per_seq, 0, 0)),
                  pl.BlockSpec((TD, ROUTE_ROWS), lambda i: (i, 0)),
                  pl.BlockSpec((1, d), lambda i: (0, 0)),
                  pl.BlockSpec(memory_space=pl.ANY)],
        out_specs=pl.BlockSpec((TD, d), lambda i: (i, 0)),
        out_shape=jax.ShapeDtypeStruct((n_tok, d), F32),
        scratch_shapes=[pltpu.VMEM((TOP_K, TD * SUBLANES, LANES), F32),
                        pltpu.VMEM((TOP_K, TD * SUBLANES, LANES), F32),
                        pltpu.SemaphoreType.DMA((2,))],
        compiler_params=pltpu.CompilerParams(dimension_semantics=("arbitrary",),
                                             vmem_limit_bytes=VMEM_LIMIT),
        name="combine",
    )(dest3, dest3, x1.reshape(n_tok, d), mod8, route, final_g.reshape(1, d), ys).reshape(bsz, seq, d)


def _routing(route, counts_f, n_tok):
    nk = n_tok * TOP_K
    experts = jnp.arange(N_EXPERTS, dtype=jnp.int32)
    topi = route[:, :TOP_K].astype(jnp.int32)
    rank = route[:, 2 * TOP_K:3 * TOP_K].astype(jnp.int32)
    counts = counts_f[:, 0].astype(jnp.int32)
    padded = ((counts + MOE_BLK - 1) // MOE_BLK) * MOE_BLK
    pad_end = jnp.cumsum(padded)
    start_pad = pad_end - padded
    dest = rank + jnp.sum(jnp.where(topi[:, :, None] == experts[None, None, :], start_pad[None, None, :], 0), axis=-1)
    p_rows = ((nk + N_EXPERTS * (MOE_BLK - 1) + MOE_BLK - 1) // MOE_BLK) * MOE_BLK
    n_pad = padded - counts
    pad_cum = jnp.cumsum(n_pad)
    j = jnp.arange(p_rows - nk, dtype=jnp.int32)
    first_pad = pad_cum - n_pad
    mine = (j[:, None] >= first_pad[None, :]) & (j[:, None] < pad_cum[None, :])
    in_tail = jnp.sum(jnp.where(mine, (start_pad + counts - first_pad)[None, :], 0), axis=1) + j
    pad_dest = jnp.where(j < pad_cum[-1], in_tail, pad_end[-1] + (j - pad_cum[-1])).astype(jnp.int32)
    return (dest.reshape(-1) * SUBLANES, pad_dest * SUBLANES, (start_pad // MOE_BLK).astype(jnp.int32),
            (padded // MOE_BLK).astype(jnp.int32), p_rows)


def kernel(x, c, ada_w, ada_b, norm_mix_g, w_in, lb_params, hgrn_norm_g, gmlp_ln_g, gmlp_ln_b, gmlp_ws, gmlp_bs,
           gmlp_norm_g, w_out, norm_ffn_g, router_w, router_b, w_gate_up, b_gate_up, w_down, b_down, final_g):
    assert ada_w.shape[0] == 1, "single-layer block"
    bsz, seq, d = x.shape
    assert d == SUBLANES * LANES and seq % TM == 0 and (bsz * seq) % TD == 0
    n_tok = bsz * seq
    mod = _modulation(c, ada_w[0], ada_b[0])
    mod8 = jnp.zeros((bsz, 8, d), F32).at[:, :6].set(mod.reshape(bsz, 6, d))
    x1, h2, route, counts = _mixer(x, mod8, norm_mix_g[0], w_in[0], lb_params, hgrn_norm_g[0], gmlp_ln_g[0],
                                   gmlp_ln_b[0], gmlp_ws[0], gmlp_bs[0], gmlp_norm_g[0], w_out[0],
                                   norm_ffn_g[0], router_w[0], router_b[0])
    route = route.transpose(0, 2, 1).reshape(n_tok, ROUTE_ROWS)
    dest, pad_dest, blk0, nblk, p_rows = _routing(route, counts, n_tok)
    xs = _dispatch(h2.reshape(n_tok, d), dest, pad_dest, p_rows)
    ys = _moe(xs, blk0, nblk, w_gate_up[0], b_gate_up[0], w_down[0], b_down[0])
    return _combine(x1, mod8, route, dest, ys, final_g)
```

```python
import functools

import jax
import jax.numpy as jnp
from jax import lax
from jax.experimental import pallas as pl
from jax.experimental.pallas import tpu as pltpu

F32 = jnp.float32
BF16 = jnp.bfloat16

HGRN_HEADS = 4
HEAD_DIM = 128
HGRN_WIDTH = HGRN_HEADS * HEAD_DIM
HGRN_CHUNK = 64
HGRN_SUB = 16
GMLP_GROUPS = 4
GROUP_DIM = 128
GMLP_WIDTH = GMLP_GROUPS * GROUP_DIM
GMLP_CHUNK = 128
N_EXPERTS = 32
TOP_K = 4
SWIGLU_LIMIT = 7.0
SWIGLU_ALPHA = 1.702
EPS = 1e-6
LANES = 128
SUBLANES = 8
N_DMA_QUEUES = 2
BLOCK_DMA_PRIORITY = 1
ROUTE_ROWS = 16
W_CHUNKS = 8
DECAY_EXP_CLAMP = 60.0
TM = 512
MOE_BLK = 256
TD = 256
VMEM_LIMIT = 56 * 1024 * 1024


def _dot(a, b):
    return jnp.dot(a, b, preferred_element_type=F32)


def _dot_nt(a, b):
    return lax.dot_general(a, b, (((1,), (1,)), ((), ())), preferred_element_type=F32)


def _dot_tn(a, b):
    return lax.dot_general(a, b, (((0,), (0,)), ((), ())), preferred_element_type=F32)


def _rms(x):
    return x * lax.rsqrt(jnp.mean(x * x, axis=-1, keepdims=True) + EPS)


def _gelu(x):
    return 0.5 * x * (1.0 + lax.erf(x * 0.7071067811865476))


def _by_parity(i, fn):
    @pl.when(i % 2 == 0)
    def _():
        fn(0)

    @pl.when(i % 2 == 1)
    def _():
        fn(1)


def _mod_kernel(c_ref, w_ref, b_ref, o_ref):
    c = c_ref[...]
    ca = c * jax.nn.sigmoid(c)
    o_ref[...] = jnp.dot(ca, w_ref[...], precision=lax.Precision.HIGHEST,
                         preferred_element_type=F32) + b_ref[...]


def _modulation(c, ada_w, ada_b):
    bsz, d = c.shape
    n_out = ada_w.shape[1]
    rows = 8
    c_pad = jnp.zeros((rows, d), F32).at[:bsz].set(c)
    tn = 1024
    out = pl.pallas_call(
        _mod_kernel,
        grid=(n_out // tn,),
        in_specs=[pl.BlockSpec((rows, d), lambda j: (0, 0)),
                  pl.BlockSpec((d, tn), lambda j: (0, j)),
                  pl.BlockSpec((1, tn), lambda j: (0, j))],
        out_specs=pl.BlockSpec((rows, tn), lambda j: (0, j)),
        out_shape=jax.ShapeDtypeStruct((rows, n_out), F32),
        name="adaln_mod",
    )(c_pad, ada_w, ada_b.reshape(1, n_out))
    return out[:bsz]


def _mixer_kernel(x_ref, mod_ref, g1_ref, win_ref, lbp_ref, hg_ref, lng_ref, lnb_ref, ws_ref, bs_ref,
                  gng_ref, wout_ref, g2_ref, rwh_ref, rwl_ref, rb_ref,
                  x1_ref, h2_ref, route_ref, cnt_out_ref,
                  z_ref, y_ref, st_ref, cnt_ref):
    @pl.when(pl.program_id(1) == 0)
    def _():
        st_ref[...] = jnp.zeros_like(st_ref)

    x = x_ref[...]
    mod = mod_ref[...]
    h = _rms(x) * g1_ref[...]
    h = h * (1.0 + mod[1:2]) + mod[0:1]
    z_ref[...] = _dot(h.astype(BF16), win_ref[...])

    lbp = lbp_ref[...]
    lbe = jnp.exp(lbp - jnp.max(lbp, axis=0, keepdims=True))
    lb = lbe[0:1] / jnp.sum(lbe, axis=0, keepdims=True)
    hg = hg_ref[...]
    row = lax.broadcasted_iota(jnp.int32, (HGRN_CHUNK, HGRN_CHUNK), 0)
    col = lax.broadcasted_iota(jnp.int32, (HGRN_CHUNK, HGRN_CHUNK), 1)
    causal = col <= row
    crow = lax.broadcasted_iota(jnp.int32, (HGRN_CHUNK, HGRN_WIDTH), 0)
    n_sub = HGRN_CHUNK // HGRN_SUB

    def chunk_body(c):
        rows = slice(c * HGRN_CHUNK, (c + 1) * HGRN_CHUNK)
        zq = z_ref[rows, 0:HGRN_WIDTH]
        zf = z_ref[rows, HGRN_WIDTH:2 * HGRN_WIDTH]
        zi = z_ref[rows, 2 * HGRN_WIDTH:3 * HGRN_WIDTH]
        zg = z_ref[rows, 3 * HGRN_WIDTH:4 * HGRN_WIDTH]
        f = lb + (1.0 - lb) * jax.nn.sigmoid(zf)
        logf = jnp.log(f)
        kk = 1.0 - f
        b = logf
        for sh in (1, 2, 4, 8, 16, 32):
            b = b + jnp.where(crow >= sh, pltpu.roll(b, sh, axis=0), 0.0)
        b_last = b[HGRN_CHUNK - 1:HGRN_CHUNK, :]
        qe = (zq * jnp.exp(b)).astype(BF16)
        kdec = (kk * jnp.exp(b_last - b)).astype(BF16)
        v = zi.astype(BF16)
        dec_last = jnp.exp(b_last)
        a_sub, k_sub = [], []
        for i in range(n_sub):
            lo, hi = i * HGRN_SUB, (i + 1) * HGRN_SUB
            bref = b[lo - 1:lo, :] if i > 0 else jnp.zeros((1, HGRN_WIDTH), F32)
            a_sub.append((zq[lo:hi] * jnp.exp(b[lo:hi] - bref)).astype(BF16))
            k_sub.append((kk * jnp.exp(jnp.minimum(bref - b, DECAY_EXP_CLAMP))).astype(BF16))
        silu_g = zg * jax.nn.sigmoid(zg)
        for hd in range(HGRN_HEADS):
            ls = slice(hd * HEAD_DIM, (hd + 1) * HEAD_DIM)
            sc = jnp.concatenate([_dot_nt(a_sub[i][:, ls], k_sub[i][:, ls]) for i in range(n_sub)], axis=0)
            sc = jnp.where(causal, sc, 0.0).astype(BF16)
            st = st_ref[hd]
            o = _dot_nt(qe[:, ls], st.astype(BF16)) + _dot(sc, v[:, ls])
            st_ref[hd] = st * dec_last[:, ls] + _dot_tn(v[:, ls], kdec[:, ls])
            o = _rms(o) * hg[:, ls]
            y_ref[rows, ls] = (o * silu_g[:, ls]).astype(BF16)

    for c in range(TM // HGRN_CHUNK):
        chunk_body(c)

    u = _gelu(z_ref[:, 4 * HGRN_WIDTH:4 * HGRN_WIDTH + GMLP_WIDTH])
    gv = _gelu(z_ref[:, 4 * HGRN_WIDTH + GMLP_WIDTH:])
    mu = jnp.mean(gv, axis=-1, keepdims=True)
    gc = gv - mu
    var = jnp.mean(gc * gc, axis=-1, keepdims=True)
    vn = (gc * lax.rsqrt(var + EPS) * lng_ref[...] + lnb_ref[...]).astype(BF16)
    row_g = lax.broadcasted_iota(jnp.int32, (GMLP_CHUNK, GMLP_CHUNK), 0)
    col_g = lax.broadcasted_iota(jnp.int32, (GMLP_CHUNK, GMLP_CHUNK), 1)
    gng = gng_ref[...]
    for g in range(GMLP_GROUPS):
        ls = slice(g * GROUP_DIM, (g + 1) * GROUP_DIM)
        ws_c = jnp.where(col_g <= row_g, ws_ref[g], 0.0).astype(BF16)
        bias = bs_ref[g]
        for n in range(TM // GMLP_CHUNK):
            rs = slice(n * GMLP_CHUNK, (n + 1) * GMLP_CHUNK)
            sv = _dot(ws_c, vn[rs, ls]) + bias
            yy = _rms(u[rs, ls] * sv) * gng[:, ls]
            y_ref[rs, HGRN_WIDTH + g * GROUP_DIM:HGRN_WIDTH + (g + 1) * GROUP_DIM] = yy.astype(BF16)

    x1 = x + mod[2:3] * _dot(y_ref[...], wout_ref[...])
    x1_ref[...] = x1
    h2 = _rms(x1) * g2_ref[...]
    h2 = h2 * (1.0 + mod[4:5]) + mod[3:4]
    h2_ref[...] = h2

    hh = h2.astype(BF16)
    hl = (h2 - hh.astype(F32)).astype(BF16)
    rwh, rwl = rwh_ref[...], rwl_ref[...]
    logits = _dot_nt(rwh, hh) + (_dot_nt(rwh, hl) + _dot_nt(rwl, hh)) + rb_ref[...]
    erow = lax.broadcasted_iota(jnp.int32, (N_EXPERTS, TM), 0)
    vals, idxs = [], []
    for _ in range(TOP_K):
        m = jnp.max(logits, axis=0, keepdims=True)
        idx = jnp.min(jnp.where(logits == m, erow, N_EXPERTS), axis=0, keepdims=True)
        vals.append(m)
        idxs.append(idx)
        logits = jnp.where(erow == idx, -jnp.inf, logits)
    es = [jnp.exp(v - vals[0]) for v in vals]
    tot = (es[0] + es[1]) + (es[2] + es[3])

    @pl.when((pl.program_id(0) == 0) & (pl.program_id(1) == 0))
    def _():
        cnt_ref[...] = jnp.zeros_like(cnt_ref)

    hot = [erow == idxs[k] for k in range(TOP_K)]
    picked = jnp.where((hot[0] | hot[1]) | (hot[2] | hot[3]), 1.0, 0.0)
    row_t = lax.broadcasted_iota(jnp.int32, (TM, TM), 0)
    col_t = lax.broadcasted_iota(jnp.int32, (TM, TM), 1)
    earlier = jnp.where(row_t < col_t, 1.0, 0.0).astype(BF16)
    seen = _dot(picked.astype(BF16), earlier) + cnt_ref[:, 0:1]
    cnt_ref[...] = cnt_ref[...] + jnp.sum(picked, axis=1, keepdims=True)
    cnt_out_ref[...] = cnt_ref[...]
    ranks = [jnp.sum(jnp.where(hot[k], seen, 0.0), axis=0, keepdims=True) for k in range(TOP_K)]
    route_ref[...] = jnp.concatenate([i.astype(F32) for i in idxs] + [e / tot for e in es] + ranks +
                                     [jnp.zeros((ROUTE_ROWS - 3 * TOP_K, TM), F32)], axis=0)


def _mixer(x, mod8, norm_mix_g, w_in, lb_params, hgrn_norm_g, gmlp_ln_g, gmlp_ln_b, gmlp_ws, gmlp_bs,
           gmlp_norm_g, w_out, norm_ffn_g, router_w, router_b):
    bsz, seq, d = x.shape
    n_in = w_in.shape[1]
    rw = router_w.T
    rwh = rw.astype(BF16)
    rwl = (rw - rwh.astype(F32)).astype(BF16)
    rb = router_b.reshape(N_EXPERTS, 1)
    const = lambda *shape: pl.BlockSpec(shape, lambda b, i: (0,) * len(shape))
    tile = lambda w: pl.BlockSpec((None, TM, w), lambda b, i: (b, i, 0))
    per_seq = seq // TM
    return pl.pallas_call(
        _mixer_kernel,
        grid=(bsz, seq // TM),
        in_specs=[tile(d),
                  pl.BlockSpec((None, 8, d), lambda b, i: (b, 0, 0)),
                  const(1, d), const(d, n_in), const(2, HGRN_WIDTH), const(1, HGRN_WIDTH),
                  const(1, GMLP_WIDTH), const(1, GMLP_WIDTH),
                  const(GMLP_GROUPS, GMLP_CHUNK, GMLP_CHUNK), const(GMLP_GROUPS, GMLP_CHUNK, 1),
                  const(1, GMLP_WIDTH), const(d, d), const(1, d),
                  const(N_EXPERTS, d), const(N_EXPERTS, d), const(N_EXPERTS, 1)],
        out_specs=[tile(d), tile(d),
                   pl.BlockSpec((None, ROUTE_ROWS, TM), lambda b, i: (b * per_seq + i, 0, 0)),
                   const(N_EXPERTS, LANES)],
        out_shape=[jax.ShapeDtypeStruct((bsz, seq, d), F32),
                   jax.ShapeDtypeStruct((bsz, seq, d), F32),
                   jax.ShapeDtypeStruct((bsz * per_seq, ROUTE_ROWS, TM), F32),
                   jax.ShapeDtypeStruct((N_EXPERTS, LANES), F32)],
        scratch_shapes=[pltpu.VMEM((TM, n_in), F32),
                        pltpu.VMEM((TM, d), BF16),
                        pltpu.VMEM((HGRN_HEADS, HEAD_DIM, HEAD_DIM), F32),
                        pltpu.VMEM((N_EXPERTS, LANES), F32)],
        compiler_params=pltpu.CompilerParams(dimension_semantics=("arbitrary", "arbitrary"),
                                             vmem_limit_bytes=VMEM_LIMIT),
        name="mixer",
    )(x, mod8, norm_mix_g.reshape(1, d), w_in.astype(BF16), lb_params, hgrn_norm_g.reshape(1, -1),
      gmlp_ln_g.reshape(1, -1), gmlp_ln_b.reshape(1, -1), gmlp_ws, gmlp_bs[:, :, None],
      gmlp_norm_g.reshape(1, -1), w_out.astype(BF16), norm_ffn_g.reshape(1, d), rwh, rwl, rb)


def _tile(r):
    return pl.ds(r * SUBLANES, SUBLANES)


def _tile8(r8):
    return pl.ds(pl.multiple_of(r8, SUBLANES), SUBLANES)


def _chunk(j, n):
    return pl.ds(j, n, stride=SUBLANES)


def _dispatch_kernel(n_pad_step, dest_ref, pad_ref, h2_ref, xs_hbm, rows, zero_tile, sem):
    i = pl.program_id(0)
    n_steps = pl.num_programs(0)

    def wait_step(s):
        for _ in range(TOP_K):
            pltpu.make_async_copy(rows.at[s], xs_hbm.at[pl.ds(0, TD * SUBLANES)], sem.at[s]).wait()
        n = n_pad_step * SUBLANES
        pltpu.make_async_copy(rows.at[s, pl.ds(0, n)], xs_hbm.at[pl.ds(0, n)], sem.at[s]).wait()

    @pl.when(i == 0)
    def _():
        zero_tile[...] = jnp.zeros_like(zero_tile)

    def step(s):
        @pl.when(i >= 2)
        def _():
            wait_step(s)

        for j in range(SUBLANES):
            rows[s, _chunk(j, TD), :] = h2_ref[:, j * LANES:(j + 1) * LANES]
        for t in range(TD):
            for k in range(TOP_K):
                pltpu.make_async_copy(rows.at[s, _tile(t)], xs_hbm.at[_tile8(dest_ref[0, 0, t * TOP_K + k])],
                                      sem.at[s]).start(priority=k % N_DMA_QUEUES)
        for q in range(n_pad_step):
            pltpu.make_async_copy(zero_tile, xs_hbm.at[_tile8(pad_ref[0, 0, q])],
                                  sem.at[s]).start(priority=q % N_DMA_QUEUES)

        @pl.when(i == n_steps - 1)
        def _():
            wait_step(s)

            @pl.when(i >= 1)
            def _():
                wait_step(1 - s)

    _by_parity(i, step)


def _dispatch(h2, dest, pad_dest, p_rows):
    n_tok, d = h2.shape
    n_steps = n_tok // TD
    n_pad_step = pad_dest.shape[0] // n_steps
    return pl.pallas_call(
        functools.partial(_dispatch_kernel, n_pad_step),
        grid=(n_steps,),
        in_specs=[pl.BlockSpec((1, 1, TD * TOP_K), lambda i: (i, 0, 0), memory_space=pltpu.SMEM),
                  pl.BlockSpec((1, 1, n_pad_step), lambda i: (i, 0, 0), memory_space=pltpu.SMEM),
                  pl.BlockSpec((TD, d), lambda i: (i, 0))],
        out_specs=pl.BlockSpec(memory_space=pl.ANY),
        out_shape=jax.ShapeDtypeStruct((p_rows * SUBLANES, LANES), F32),
        scratch_shapes=[pltpu.VMEM((2, TD * SUBLANES, LANES), F32),
                        pltpu.VMEM((SUBLANES, LANES), F32),
                        pltpu.SemaphoreType.DMA((2,))],
        compiler_params=pltpu.CompilerParams(dimension_semantics=("arbitrary",)),
        name="dispatch",
    )(dest.reshape(n_steps, 1, TD * TOP_K), pad_dest.reshape(n_steps, 1, n_pad_step), h2)


def _moe_kernel(n_blocks, blk0_ref, nblk_ref, xs_hbm, wgu_hbm, bgu_ref, wd_hbm, bd_ref, ys_hbm,
                xbuf, ybuf, wgu_f, wd_f, wgu_bf, wd_bf, isem, osem, wsem):
    e = pl.program_id(0)
    n_exp = pl.num_programs(0)
    nb = nblk_ref[e]
    b0 = blk0_ref[e]
    d, d_ff = wd_f.shape[2], wd_f.shape[1]
    blk_rows = MOE_BLK * SUBLANES
    ws = e % 2
    gu_rows, d_rows = d // W_CHUNKS, d_ff // W_CHUNKS

    npairs = nb // 2
    odd = nb - 2 * npairs

    def rows_of(b, n):
        return pl.ds(pl.multiple_of((b0 + b) * blk_rows, blk_rows), n * blk_rows)

    def x_copy(b, n, s):
        return pltpu.make_async_copy(xs_hbm.at[rows_of(b, n)], xbuf.at[s, pl.ds(0, n * blk_rows)], isem.at[s])

    def y_copy(b, n, s):
        return pltpu.make_async_copy(ybuf.at[s, pl.ds(0, n * blk_rows)], ys_hbm.at[rows_of(b, n)], osem.at[s])

    def w_start(ex, slot, c):
        r_gu = pl.ds(pl.multiple_of(c * gu_rows, gu_rows), gu_rows)
        r_d = pl.ds(pl.multiple_of(c * d_rows, d_rows), d_rows)
        pltpu.make_async_copy(wgu_hbm.at[ex, r_gu], wgu_f.at[slot, r_gu], wsem.at[slot]).start()
        pltpu.make_async_copy(wd_hbm.at[ex, r_d], wd_f.at[slot, r_d], wsem.at[slot]).start()

    def w_wait(slot):
        pltpu.make_async_copy(wgu_hbm.at[0], wgu_f.at[slot], wsem.at[slot]).wait()
        pltpu.make_async_copy(wd_hbm.at[0], wd_f.at[slot], wsem.at[slot]).wait()

    @pl.when(e == 0)
    def _():
        for c in range(W_CHUNKS):
            w_start(0, 0, c)

    @pl.when(npairs > 0)
    def _():
        x_copy(0, 2, 0).start(priority=BLOCK_DMA_PRIORITY)

    @pl.when((npairs == 0) & (odd == 1))
    def _():
        x_copy(0, 1, 0).start(priority=BLOCK_DMA_PRIORITY)

    w_wait(ws)
    wgu_bf[...] = wgu_f[ws].astype(BF16)
    wd_bf[...] = wd_f[ws].astype(BF16)
    has_next = e + 1 < n_exp

    def ffn(n, s):
        rows = n * MOE_BLK
        xb = jnp.concatenate([xbuf[s, _chunk(j, rows), :] for j in range(SUBLANES)], axis=-1).astype(BF16)
        gu = _dot(xb, wgu_bf[...]) + bgu_ref[...]
        gate = jnp.minimum(gu[:, :d_ff], SWIGLU_LIMIT)
        up = jnp.clip(gu[:, d_ff:], -SWIGLU_LIMIT, SWIGLU_LIMIT)
        glu = gate * jax.nn.sigmoid(SWIGLU_ALPHA * gate)
        yb = _dot(((up + 1.0) * glu).astype(BF16), wd_bf[...]) + bd_ref[...]
        for j in range(SUBLANES):
            ybuf[s, _chunk(j, rows), :] = yb[:, j * LANES:(j + 1) * LANES]

    def pair_body(p, carry):
        s = p % 2

        @pl.when(p + 1 < npairs)
        def _():
            x_copy(2 * p + 2, 2, 1 - s).start(priority=BLOCK_DMA_PRIORITY)

        @pl.when((p + 1 == npairs) & (odd == 1))
        def _():
            x_copy(2 * p + 2, 1, 1 - s).start(priority=BLOCK_DMA_PRIORITY)

        @pl.when(has_next & (2 * p < W_CHUNKS))
        def _():
            w_start(e + 1, 1 - ws, 2 * p)
            w_start(e + 1, 1 - ws, 2 * p + 1)

        x_copy(2 * p, 2, s).wait()

        @pl.when(p >= 2)
        def _():
            y_copy(2 * p - 4, 2, s).wait()

        ffn(2, s)
        y_copy(2 * p, 2, s).start(priority=BLOCK_DMA_PRIORITY)
        return carry

    lax.fori_loop(0, npairs, pair_body, 0)

    @pl.when(odd == 1)
    def _():
        s = npairs % 2
        x_copy(2 * npairs, 1, s).wait()

        @pl.when(npairs >= 2)
        def _():
            y_copy(2 * npairs - 4, 2, s).wait()

        ffn(1, s)
        cp = y_copy(2 * npairs, 1, s)
        cp.start(priority=BLOCK_DMA_PRIORITY)
        cp.wait()

    @pl.when(has_next)
    def _():
        def rest(c, carry):
            w_start(e + 1, 1 - ws, c)
            return carry

        lax.fori_loop(jnp.minimum(2 * npairs, W_CHUNKS), W_CHUNKS, rest, 0)

    @pl.when(npairs >= 1)
    def _():
        y_copy(2 * npairs - 2, 2, (npairs - 1) % 2).wait()

    @pl.when((npairs >= 2) & (odd == 0))
    def _():
        y_copy(2 * npairs - 4, 2, npairs % 2).wait()

    @pl.when(e == pl.num_programs(0) - 1)
    def _():
        ybuf[0, pl.ds(0, blk_rows), :] = jnp.zeros((blk_rows, LANES), F32)

        def fill(b, carry):
            cp = y_copy(b, 1, 0)
            cp.start()
            cp.wait()
            return carry

        lax.fori_loop(nb, n_blocks - b0, fill, 0)


def _moe(xs, blk0, nblk, w_gate_up, b_gate_up, w_down, b_down):
    n_blocks = xs.shape[0] // (MOE_BLK * SUBLANES)
    n_exp, d_ff, d = w_down.shape
    grid_spec = pltpu.PrefetchScalarGridSpec(
        num_scalar_prefetch=2,
        grid=(n_exp,),
        in_specs=[pl.BlockSpec(memory_space=pl.ANY),
                  pl.BlockSpec(memory_space=pl.ANY),
                  pl.BlockSpec((None, 1, 2 * d_ff), lambda e, b0, nb: (e, 0, 0)),
                  pl.BlockSpec(memory_space=pl.ANY),
                  pl.BlockSpec((None, 1, d), lambda e, b0, nb: (e, 0, 0))],
        out_specs=pl.BlockSpec(memory_space=pl.ANY),
        scratch_shapes=[pltpu.VMEM((2, 2 * MOE_BLK * SUBLANES, LANES), F32),
                        pltpu.VMEM((2, 2 * MOE_BLK * SUBLANES, LANES), F32),
                        pltpu.VMEM((2, d, 2 * d_ff), F32),
                        pltpu.VMEM((2, d_ff, d), F32),
                        pltpu.VMEM((d, 2 * d_ff), BF16),
                        pltpu.VMEM((d_ff, d), BF16),
                        pltpu.SemaphoreType.DMA((2,)),
                        pltpu.SemaphoreType.DMA((2,)),
                        pltpu.SemaphoreType.DMA((2,))],
    )
    return pl.pallas_call(
        functools.partial(_moe_kernel, n_blocks),
        grid_spec=grid_spec,
        out_shape=jax.ShapeDtypeStruct(xs.shape, F32),
        compiler_params=pltpu.CompilerParams(dimension_semantics=("arbitrary",),
                                             vmem_limit_bytes=VMEM_LIMIT),
        name="moe_ffn",
    )(blk0, nblk, xs, w_gate_up, b_gate_up[:, None, :], w_down, b_down[:, None, :])


def _combine_kernel(destc_ref, destn_ref, x1_ref, mod_ref, route_ref, fg_ref, ys_hbm, o_ref, buf0, buf1, sem):
    buf = (buf0, buf1)
    i = pl.program_id(0)
    n_steps = pl.num_programs(0)
    d = x1_ref.shape[-1]

    def start_gather(dest_ref, s, t0=0, t1=TD):
        for t in range(t0, t1):
            for k in range(TOP_K):
                pltpu.make_async_copy(ys_hbm.at[_tile8(dest_ref[0, 0, t * TOP_K + k])], buf[s].at[k, _tile(t)],
                                      sem.at[s]).start(priority=k % N_DMA_QUEUES)

    def wait_gather(s):
        for k in range(TOP_K):
            pltpu.make_async_copy(ys_hbm.at[pl.ds(0, TD * SUBLANES)], buf[s].at[k], sem.at[s]).wait()

    @pl.when(i == 0)
    def _():
        start_gather(destc_ref, 0)

    def step(s):
        wait_gather(s)
        gates = route_ref[...]
        gk = [gates[:, TOP_K + k:TOP_K + k + 1] for k in range(TOP_K)]
        gate2 = mod_ref[5:6, :]

        def residual(j):
            ls = slice(j * LANES, (j + 1) * LANES)
            rj = _chunk(j, TD)
            y = (gk[0] * buf[s][0, rj, :] + gk[1] * buf[s][1, rj, :]) + \
                (gk[2] * buf[s][2, rj, :] + gk[3] * buf[s][3, rj, :])
            return x1_ref[:, ls] + gate2[:, ls] * y

        ss = jnp.zeros((TD, 1), F32)
        per = TD // SUBLANES
        for j in range(SUBLANES):
            start_gather(destn_ref, 1 - s, j * per, (j + 1) * per)
            xj = residual(j)
            ss = ss + jnp.sum(xj * xj, axis=-1, keepdims=True)
        inv = lax.rsqrt(ss / d + EPS)
        for j in range(SUBLANES):
            ls = slice(j * LANES, (j + 1) * LANES)
            o_ref[:, ls] = residual(j) * inv * fg_ref[:, ls]

        @pl.when(i == n_steps - 1)
        def _():
            wait_gather(1 - s)

    _by_parity(i, step)


def _combine(x1, mod8, route, dest, ys, final_g):
    bsz, seq, d = x1.shape
    n_tok = bsz * seq
    n_steps = n_tok // TD
    per_seq = seq // TD
    dest3 = dest.reshape(n_steps, 1, TD * TOP_K)
    smem_blk = lambda f: pl.BlockSpec((1, 1, TD * TOP_K), f, memory_space=pltpu.SMEM)
    return pl.pallas_call(
        _combine_kernel,
        grid=(n_steps,),
        in_specs=[smem_blk(lambda i: (i, 0, 0)),
                  smem_blk(lambda i: (jnp.minimum(i + 1, n_steps - 1), 0, 0)),
                  pl.BlockSpec((TD, d), lambda i: (i, 0)),
                  pl.BlockSpec((None, 8, d), lambda i: (i // per_seq, 0, 0)),
                  pl.BlockSpec((TD, ROUTE_ROWS), lambda i: (i, 0)),
                  pl.BlockSpec((1, d), lambda i: (0, 0)),
                  pl.BlockSpec(memory_space=pl.ANY)],
        out_specs=pl.BlockSpec((TD, d), lambda i: (i, 0)),
        out_shape=jax.ShapeDtypeStruct((n_tok, d), F32),
        scratch_shapes=[pltpu.VMEM((TOP_K, TD * SUBLANES, LANES), F32),
                        pltpu.VMEM((TOP_K, TD * SUBLANES, LANES), F32),
                        pltpu.SemaphoreType.DMA((2,))],
        compiler_params=pltpu.CompilerParams(dimension_semantics=("arbitrary",),
                                             vmem_limit_bytes=VMEM_LIMIT),
        name="combine",
    )(dest3, dest3, x1.reshape(n_tok, d), mod8, route, final_g.reshape(1, d), ys).reshape(bsz, seq, d)


def _routing(route, counts_f, n_tok):
    nk = n_tok * TOP_K
    experts = jnp.arange(N_EXPERTS, dtype=jnp.int32)
    topi = route[:, :TOP_K].astype(jnp.int32)
    rank = route[:, 2 * TOP_K:3 * TOP_K].astype(jnp.int32)
    counts = counts_f[:, 0].astype(jnp.int32)
    padded = ((counts + MOE_BLK - 1) // MOE_BLK) * MOE_BLK
    pad_end = jnp.cumsum(padded)
    start_pad = pad_end - padded
    dest = rank + jnp.sum(jnp.where(topi[:, :, None] == experts[None, None, :], start_pad[None, None, :], 0), axis=-1)
    p_rows = ((nk + N_EXPERTS * (MOE_BLK - 1) + MOE_BLK - 1) // MOE_BLK) * MOE_BLK
    n_pad = padded - counts
    pad_cum = jnp.cumsum(n_pad)
    j = jnp.arange(p_rows - nk, dtype=jnp.int32)
    first_pad = pad_cum - n_pad
    mine = (j[:, None] >= first_pad[None, :]) & (j[:, None] < pad_cum[None, :])
    in_tail = jnp.sum(jnp.where(mine, (start_pad + counts - first_pad)[None, :], 0), axis=1) + j
    pad_dest = jnp.where(j < pad_cum[-1], in_tail, pad_end[-1] + (j - pad_cum[-1])).astype(jnp.int32)
    return (dest.reshape(-1) * SUBLANES, pad_dest * SUBLANES, (start_pad // MOE_BLK).astype(jnp.int32),
            (padded // MOE_BLK).astype(jnp.int32), p_rows)


def kernel(x, c, ada_w, ada_b, norm_mix_g, w_in, lb_params, hgrn_norm_g, gmlp_ln_g, gmlp_ln_b, gmlp_ws, gmlp_bs,
           gmlp_norm_g, w_out, norm_ffn_g, router_w, router_b, w_gate_up, b_gate_up, w_down, b_down, final_g):
    assert ada_w.shape[0] == 1, "single-layer block"
    bsz, seq, d = x.shape
    assert d == SUBLANES * LANES and seq % TM == 0 and (bsz * seq) % TD == 0
    n_tok = bsz * seq
    mod = _modulation(c, ada_w[0], ada_b[0])
    mod8 = jnp.zeros((bsz, 8, d), F32).at[:, :6].set(mod.reshape(bsz, 6, d))
    x1, h2, route, counts = _mixer(x, mod8, norm_mix_g[0], w_in[0], lb_params, hgrn_norm_g[0], gmlp_ln_g[0],
                                   gmlp_ln_b[0], gmlp_ws[0], gmlp_bs[0], gmlp_norm_g[0], w_out[0],
                                   norm_ffn_g[0], router_w[0], router_b[0])
    route = route.transpose(0, 2, 1).reshape(n_tok, ROUTE_ROWS)
    dest, pad_dest, blk0, nblk, p_rows = _routing(route, counts, n_tok)
    xs = _dispatch(h2.reshape(n_tok, d), dest, pad_dest, p_rows)
    ys = _moe(xs, blk0, nblk, w_gate_up[0], b_gate_up[0], w_down[0], b_down[0])
    return _combine(x1, mod8, route, dest, ys, final_g)
```

```python
import functools

import jax
import jax.numpy as jnp
from jax import lax
from jax.experimental import pallas as pl
from jax.experimental.pallas import tpu as pltpu

F32 = jnp.float32
BF16 = jnp.bfloat16

HGRN_HEADS = 4
HEAD_DIM = 128
HGRN_WIDTH = HGRN_HEADS * HEAD_DIM
HGRN_CHUNK = 64
HGRN_SUB = 16
GMLP_GROUPS = 4
GROUP_DIM = 128
GMLP_WIDTH = GMLP_GROUPS * GROUP_DIM
GMLP_CHUNK = 128
N_EXPERTS = 32
TOP_K = 4
SWIGLU_LIMIT = 7.0
SWIGLU_ALPHA = 1.702
EPS = 1e-6
LANES = 128
SUBLANES = 8
N_DMA_QUEUES = 2
BLOCK_DMA_PRIORITY = 1
ROUTE_ROWS = 16
W_CHUNKS = 8
DECAY_EXP_CLAMP = 60.0
TM = 512
MOE_BLK = 256
TD = 256
VMEM_LIMIT = 56 * 1024 * 1024


def _dot(a, b):
    return jnp.dot(a, b, preferred_element_type=F32)


def _dot_nt(a, b):
    return lax.dot_general(a, b, (((1,), (1,)), ((), ())), preferred_element_type=F32)


def _dot_tn(a, b):
    return lax.dot_general(a, b, (((0,), (0,)), ((), ())), preferred_element_type=F32)


def _rms(x):
    return x * lax.rsqrt(jnp.mean(x * x, axis=-1, keepdims=True) + EPS)


def _gelu(x):
    return 0.5 * x * (1.0 + lax.erf(x * 0.7071067811865476))


def _by_parity(i, fn):
    @pl.when(i % 2 == 0)
    def _():
        fn(0)

    @pl.when(i % 2 == 1)
    def _():
        fn(1)


def _mod_kernel(c_ref, w_ref, b_ref, o_ref):
    c = c_ref[...]
    ca = c * jax.nn.sigmoid(c)
    o_ref[...] = jnp.dot(ca, w_ref[...], precision=lax.Precision.HIGHEST,
                         preferred_element_type=F32) + b_ref[...]


def _modulation(c, ada_w, ada_b):
    bsz, d = c.shape
    n_out = ada_w.shape[1]
    rows = 8
    c_pad = jnp.zeros((rows, d), F32).at[:bsz].set(c)
    tn = 1024
    out = pl.pallas_call(
        _mod_kernel,
        grid=(n_out // tn,),
        in_specs=[pl.BlockSpec((rows, d), lambda j: (0, 0)),
                  pl.BlockSpec((d, tn), lambda j: (0, j)),
                  pl.BlockSpec((1, tn), lambda j: (0, j))],
        out_specs=pl.BlockSpec((rows, tn), lambda j: (0, j)),
        out_shape=jax.ShapeDtypeStruct((rows, n_out), F32),
        name="adaln_mod",
    )(c_pad, ada_w, ada_b.reshape(1, n_out))
    return out[:bsz]


def _mixer_kernel(x_ref, mod_ref, g1_ref, win_ref, lbp_ref, hg_ref, lng_ref, lnb_ref, ws_ref, bs_ref,
                  gng_ref, wout_ref, g2_ref, rwh_ref, rwl_ref, rb_ref,
                  x1_ref, h2_ref, route_ref, cnt_out_ref,
                  z_ref, y_ref, st_ref, cnt_ref):
    @pl.when(pl.program_id(1) == 0)
    def _():
        st_ref[...] = jnp.zeros_like(st_ref)

    x = x_ref[...]
    mod = mod_ref[...]
    h = _rms(x) * g1_ref[...]
    h = h * (1.0 + mod[1:2]) + mod[0:1]
    z_ref[...] = _dot(h.astype(BF16), win_ref[...])

    lbp = lbp_ref[...]
    lbe = jnp.exp(lbp - jnp.max(lbp, axis=0, keepdims=True))
    lb = lbe[0:1] / jnp.sum(lbe, axis=0, keepdims=True)
    hg = hg_ref[...]
    row = lax.broadcasted_iota(jnp.int32, (HGRN_CHUNK, HGRN_CHUNK), 0)
    col = lax.broadcasted_iota(jnp.int32, (HGRN_CHUNK, HGRN_CHUNK), 1)
    causal = col <= row
    crow = lax.broadcasted_iota(jnp.int32, (HGRN_CHUNK, HGRN_WIDTH), 0)
    n_sub = HGRN_CHUNK // HGRN_SUB

    def chunk_scores(c):
        rows = slice(c * HGRN_CHUNK, (c + 1) * HGRN_CHUNK)
        zq = z_ref[rows, 0:HGRN_WIDTH]
        zf = z_ref[rows, HGRN_WIDTH:2 * HGRN_WIDTH]
        zi = z_ref[rows, 2 * HGRN_WIDTH:3 * HGRN_WIDTH]
        zg = z_ref[rows, 3 * HGRN_WIDTH:4 * HGRN_WIDTH]
        f = lb + (1.0 - lb) * jax.nn.sigmoid(zf)
        logf = jnp.log(f)
        kk = 1.0 - f
        b = logf
        for sh in (1, 2, 4, 8, 16, 32):
            b = b + jnp.where(crow >= sh, pltpu.roll(b, sh, axis=0), 0.0)
        b_last = b[HGRN_CHUNK - 1:HGRN_CHUNK, :]
        qe = (zq * jnp.exp(b)).astype(BF16)
        kdec = (kk * jnp.exp(b_last - b)).astype(BF16)
        v = zi.astype(BF16)
        dec_last = jnp.exp(b_last)
        a_sub, k_sub = [], []
        for i in range(n_sub):
            lo, hi = i * HGRN_SUB, (i + 1) * HGRN_SUB
            bref = b[lo - 1:lo, :] if i > 0 else jnp.zeros((1, HGRN_WIDTH), F32)
            a_sub.append((zq[lo:hi] * jnp.exp(b[lo:hi] - bref)).astype(BF16))
            k_sub.append((kk * jnp.exp(jnp.minimum(bref - b, DECAY_EXP_CLAMP))).astype(BF16))
        sc = []
        for hd in range(HGRN_HEADS):
            ls = slice(hd * HEAD_DIM, (hd + 1) * HEAD_DIM)
            s_h = jnp.concatenate([_dot_nt(a_sub[i][:, ls], k_sub[i][:, ls]) for i in range(n_sub)], axis=0)
            sc.append(jnp.where(causal, s_h, 0.0).astype(BF16))
        return qe, kdec, v, dec_last, sc, zg * jax.nn.sigmoid(zg)

    def chunk_output(c, parts):
        rows = slice(c * HGRN_CHUNK, (c + 1) * HGRN_CHUNK)
        qe, kdec, v, dec_last, sc, silu_g = parts
        for hd in range(HGRN_HEADS):
            ls = slice(hd * HEAD_DIM, (hd + 1) * HEAD_DIM)
            st = st_ref[hd]
            o = _dot_nt(qe[:, ls], st.astype(BF16)) + _dot(sc[hd], v[:, ls])
            st_ref[hd] = st * dec_last[:, ls] + _dot_tn(v[:, ls], kdec[:, ls])
            o = _rms(o) * hg[:, ls]
            y_ref[rows, ls] = (o * silu_g[:, ls]).astype(BF16)

    row_g = lax.broadcasted_iota(jnp.int32, (GMLP_CHUNK, GMLP_CHUNK), 0)
    col_g = lax.broadcasted_iota(jnp.int32, (GMLP_CHUNK, GMLP_CHUNK), 1)
    gng = gng_ref[...]
    ws_c = [jnp.where(col_g <= row_g, ws_ref[g], 0.0).astype(BF16) for g in range(GMLP_GROUPS)]

    def gmlp_block(n):
        rs = slice(n * GMLP_CHUNK, (n + 1) * GMLP_CHUNK)
        u = _gelu(z_ref[rs, 4 * HGRN_WIDTH:4 * HGRN_WIDTH + GMLP_WIDTH])
        gv = _gelu(z_ref[rs, 4 * HGRN_WIDTH + GMLP_WIDTH:])
        mu = jnp.mean(gv, axis=-1, keepdims=True)
        gc = gv - mu
        var = jnp.mean(gc * gc, axis=-1, keepdims=True)
        vn = (gc * lax.rsqrt(var + EPS) * lng_ref[...] + lnb_ref[...]).astype(BF16)
        for g in range(GMLP_GROUPS):
            ls = slice(g * GROUP_DIM, (g + 1) * GROUP_DIM)
            sv = _dot(ws_c[g], vn[:, ls]) + bs_ref[g]
            yy = _rms(u[:, ls] * sv) * gng[:, ls]
            y_ref[rs, HGRN_WIDTH + g * GROUP_DIM:HGRN_WIDTH + (g + 1) * GROUP_DIM] = yy.astype(BF16)

    n_chunks = TM // HGRN_CHUNK
    per_blk = GMLP_CHUNK // HGRN_CHUNK
    parts = [chunk_scores(c) for c in range(n_chunks)]
    for c in range(n_chunks):
        chunk_output(c, parts[c])
        if c % per_blk == per_blk - 1:
            gmlp_block(c // per_blk)

    x1 = x + mod[2:3] * _dot(y_ref[...], wout_ref[...])
    x1_ref[...] = x1
    h2 = _rms(x1) * g2_ref[...]
    h2 = h2 * (1.0 + mod[4:5]) + mod[3:4]
    h2_ref[...] = h2

    hh = h2.astype(BF16)
    hl = (h2 - hh.astype(F32)).astype(BF16)
    rwh, rwl = rwh_ref[...], rwl_ref[...]
    logits = _dot_nt(rwh, hh) + (_dot_nt(rwh, hl) + _dot_nt(rwl, hh)) + rb_ref[...]
    erow = lax.broadcasted_iota(jnp.int32, (N_EXPERTS, TM), 0)
    vals, idxs = [], []
    for _ in range(TOP_K):
        m = jnp.max(logits, axis=0, keepdims=True)
        idx = jnp.min(jnp.where(logits == m, erow, N_EXPERTS), axis=0, keepdims=True)
        vals.append(m)
        idxs.append(idx)
        logits = jnp.where(erow == idx, -jnp.inf, logits)
    es = [jnp.exp(v - vals[0]) for v in vals]
    tot = (es[0] + es[1]) + (es[2] + es[3])

    @pl.when((pl.program_id(0) == 0) & (pl.program_id(1) == 0))
    def _():
        cnt_ref[...] = jnp.zeros_like(cnt_ref)

    hot = [erow == idxs[k] for k in range(TOP_K)]
    picked = jnp.where((hot[0] | hot[1]) | (hot[2] | hot[3]), 1.0, 0.0)
    row_t = lax.broadcasted_iota(jnp.int32, (TM, TM), 0)
    col_t = lax.broadcasted_iota(jnp.int32, (TM, TM), 1)
    earlier = jnp.where(row_t < col_t, 1.0, 0.0).astype(BF16)
    seen = _dot(picked.astype(BF16), earlier) + cnt_ref[:, 0:1]
    cnt_ref[...] = cnt_ref[...] + jnp.sum(picked, axis=1, keepdims=True)
    cnt_out_ref[...] = cnt_ref[...]
    ranks = [jnp.sum(jnp.where(hot[k], seen, 0.0), axis=0, keepdims=True) for k in range(TOP_K)]
    route_ref[...] = jnp.concatenate([i.astype(F32) for i in idxs] + [e / tot for e in es] + ranks +
                                     [jnp.zeros((ROUTE_ROWS - 3 * TOP_K, TM), F32)], axis=0)


def _mixer(x, mod8, norm_mix_g, w_in, lb_params, hgrn_norm_g, gmlp_ln_g, gmlp_ln_b, gmlp_ws, gmlp_bs,
           gmlp_norm_g, w_out, norm_ffn_g, router_w, router_b):
    bsz, seq, d = x.shape
    n_in = w_in.shape[1]
    rw = router_w.T
    rwh = rw.astype(BF16)
    rwl = (rw - rwh.astype(F32)).astype(BF16)
    rb = router_b.reshape(N_EXPERTS, 1)
    const = lambda *shape: pl.BlockSpec(shape, lambda b, i: (0,) * len(shape))
    tile = lambda w: pl.BlockSpec((None, TM, w), lambda b, i: (b, i, 0))
    per_seq = seq // TM
    return pl.pallas_call(
        _mixer_kernel,
        grid=(bsz, seq // TM),
        in_specs=[tile(d),
                  pl.BlockSpec((None, 8, d), lambda b, i: (b, 0, 0)),
                  const(1, d), const(d, n_in), const(2, HGRN_WIDTH), const(1, HGRN_WIDTH),
                  const(1, GMLP_WIDTH), const(1, GMLP_WIDTH),
                  const(GMLP_GROUPS, GMLP_CHUNK, GMLP_CHUNK), const(GMLP_GROUPS, GMLP_CHUNK, 1),
                  const(1, GMLP_WIDTH), const(d, d), const(1, d),
                  const(N_EXPERTS, d), const(N_EXPERTS, d), const(N_EXPERTS, 1)],
        out_specs=[tile(d), tile(d),
                   pl.BlockSpec((None, ROUTE_ROWS, TM), lambda b, i: (b * per_seq + i, 0, 0)),
                   const(N_EXPERTS, LANES)],
        out_shape=[jax.ShapeDtypeStruct((bsz, seq, d), F32),
                   jax.ShapeDtypeStruct((bsz, seq, d), F32),
                   jax.ShapeDtypeStruct((bsz * per_seq, ROUTE_ROWS, TM), F32),
                   jax.ShapeDtypeStruct((N_EXPERTS, LANES), F32)],
        scratch_shapes=[pltpu.VMEM((TM, n_in), F32),
                        pltpu.VMEM((TM, d), BF16),
                        pltpu.VMEM((HGRN_HEADS, HEAD_DIM, HEAD_DIM), F32),
                        pltpu.VMEM((N_EXPERTS, LANES), F32)],
        compiler_params=pltpu.CompilerParams(dimension_semantics=("arbitrary", "arbitrary"),
                                             vmem_limit_bytes=VMEM_LIMIT),
        name="mixer",
    )(x, mod8, norm_mix_g.reshape(1, d), w_in.astype(BF16), lb_params, hgrn_norm_g.reshape(1, -1),
      gmlp_ln_g.reshape(1, -1), gmlp_ln_b.reshape(1, -1), gmlp_ws, gmlp_bs[:, :, None],
      gmlp_norm_g.reshape(1, -1), w_out.astype(BF16), norm_ffn_g.reshape(1, d), rwh, rwl, rb)


def _tile(r):
    return pl.ds(r * SUBLANES, SUBLANES)


def _tile8(r8):
    return pl.ds(pl.multiple_of(r8, SUBLANES), SUBLANES)


def _chunk(j, n):
    return pl.ds(j, n, stride=SUBLANES)


def _dispatch_kernel(n_pad_step, dest_ref, pad_ref, h2_ref, xs_hbm, rows, zero_tile, sem):
    i = pl.program_id(0)
    n_steps = pl.num_programs(0)

    def wait_step(s):
        for _ in range(TOP_K):
            pltpu.make_async_copy(rows.at[s], xs_hbm.at[pl.ds(0, TD * SUBLANES)], sem.at[s]).wait()
        n = n_pad_step * SUBLANES
        pltpu.make_async_copy(rows.at[s, pl.ds(0, n)], xs_hbm.at[pl.ds(0, n)], sem.at[s]).wait()

    @pl.when(i == 0)
    def _():
        zero_tile[...] = jnp.zeros_like(zero_tile)

    def step(s):
        @pl.when(i >= 2)
        def _():
            wait_step(s)

        for j in range(SUBLANES):
            rows[s, _chunk(j, TD), :] = h2_ref[:, j * LANES:(j + 1) * LANES]
        for t in range(TD):
            for k in range(TOP_K):
                pltpu.make_async_copy(rows.at[s, _tile(t)], xs_hbm.at[_tile8(dest_ref[0, 0, t * TOP_K + k])],
                                      sem.at[s]).start(priority=k % N_DMA_QUEUES)
        for q in range(n_pad_step):
            pltpu.make_async_copy(zero_tile, xs_hbm.at[_tile8(pad_ref[0, 0, q])],
                                  sem.at[s]).start(priority=q % N_DMA_QUEUES)

        @pl.when(i == n_steps - 1)
        def _():
            wait_step(s)

            @pl.when(i >= 1)
            def _():
                wait_step(1 - s)

    _by_parity(i, step)


def _dispatch(h2, dest, pad_dest, p_rows):
    n_tok, d = h2.shape
    n_steps = n_tok // TD
    n_pad_step = pad_dest.shape[0] // n_steps
    return pl.pallas_call(
        functools.partial(_dispatch_kernel, n_pad_step),
        grid=(n_steps,),
        in_specs=[pl.BlockSpec((1, 1, TD * TOP_K), lambda i: (i, 0, 0), memory_space=pltpu.SMEM),
                  pl.BlockSpec((1, 1, n_pad_step), lambda i: (i, 0, 0), memory_space=pltpu.SMEM),
                  pl.BlockSpec((TD, d), lambda i: (i, 0))],
        out_specs=pl.BlockSpec(memory_space=pl.ANY),
        out_shape=jax.ShapeDtypeStruct((p_rows * SUBLANES, LANES), F32),
        scratch_shapes=[pltpu.VMEM((2, TD * SUBLANES, LANES), F32),
                        pltpu.VMEM((SUBLANES, LANES), F32),
                        pltpu.SemaphoreType.DMA((2,))],
        compiler_params=pltpu.CompilerParams(dimension_semantics=("arbitrary",)),
        name="dispatch",
    )(dest.reshape(n_steps, 1, TD * TOP_K), pad_dest.reshape(n_steps, 1, n_pad_step), h2)


def _moe_kernel(n_blocks, blk0_ref, nblk_ref, xs_hbm, wgu_hbm, bgu_ref, wd_hbm, bd_ref, ys_hbm,
                xbuf, ybuf, wgu_f, wd_f, wgu_bf, wd_bf, isem, osem, wsem):
    e = pl.program_id(0)
    n_exp = pl.num_programs(0)
    nb = nblk_ref[e]
    b0 = blk0_ref[e]
    d, d_ff = wd_f.shape[2], wd_f.shape[1]
    blk_rows = MOE_BLK * SUBLANES
    ws = e % 2
    gu_rows, d_rows = d // W_CHUNKS, d_ff // W_CHUNKS

    npairs = nb // 2
    odd = nb - 2 * npairs

    def rows_of(b, n):
        return pl.ds(pl.multiple_of((b0 + b) * blk_rows, blk_rows), n * blk_rows)

    def x_copy(b, n, s):
        return pltpu.make_async_copy(xs_hbm.at[rows_of(b, n)], xbuf.at[s, pl.ds(0, n * blk_rows)], isem.at[s])

    def y_copy(b, n, s):
        return pltpu.make_async_copy(ybuf.at[s, pl.ds(0, n * blk_rows)], ys_hbm.at[rows_of(b, n)], osem.at[s])

    def w_start(ex, slot, c):
        r_gu = pl.ds(pl.multiple_of(c * gu_rows, gu_rows), gu_rows)
        r_d = pl.ds(pl.multiple_of(c * d_rows, d_rows), d_rows)
        pltpu.make_async_copy(wgu_hbm.at[ex, r_gu], wgu_f.at[slot, r_gu], wsem.at[slot]).start()
        pltpu.make_async_copy(wd_hbm.at[ex, r_d], wd_f.at[slot, r_d], wsem.at[slot]).start()

    def w_wait(slot):
        pltpu.make_async_copy(wgu_hbm.at[0], wgu_f.at[slot], wsem.at[slot]).wait()
        pltpu.make_async_copy(wd_hbm.at[0], wd_f.at[slot], wsem.at[slot]).wait()

    @pl.when(e == 0)
    def _():
        for c in range(W_CHUNKS):
            w_start(0, 0, c)

    @pl.when(npairs > 0)
    def _():
        x_copy(0, 2, 0).start(priority=BLOCK_DMA_PRIORITY)

    @pl.when((npairs == 0) & (odd == 1))
    def _():
        x_copy(0, 1, 0).start(priority=BLOCK_DMA_PRIORITY)

    w_wait(ws)
    wgu_bf[...] = wgu_f[ws].astype(BF16)
    wd_bf[...] = wd_f[ws].astype(BF16)
    has_next = e + 1 < n_exp

    def ffn(n, s):
        rows = n * MOE_BLK
        xb = jnp.concatenate([xbuf[s, _chunk(j, rows), :] for j in range(SUBLANES)], axis=-1).astype(BF16)
        gu = _dot(xb, wgu_bf[...]) + bgu_ref[...]
        gate = jnp.minimum(gu[:, :d_ff], SWIGLU_LIMIT)
        up = jnp.clip(gu[:, d_ff:], -SWIGLU_LIMIT, SWIGLU_LIMIT)
        glu = gate * jax.nn.sigmoid(SWIGLU_ALPHA * gate)
        yb = _dot(((up + 1.0) * glu).astype(BF16), wd_bf[...]) + bd_ref[...]
        for j in range(SUBLANES):
            ybuf[s, _chunk(j, rows), :] = yb[:, j * LANES:(j + 1) * LANES]

    def pair_body(p, carry):
        s = p % 2

        @pl.when(p + 1 < npairs)
        def _():
            x_copy(2 * p + 2, 2, 1 - s).start(priority=BLOCK_DMA_PRIORITY)

        @pl.when((p + 1 == npairs) & (odd == 1))
        def _():
            x_copy(2 * p + 2, 1, 1 - s).start(priority=BLOCK_DMA_PRIORITY)

        @pl.when(has_next & (2 * p < W_CHUNKS))
        def _():
            w_start(e + 1, 1 - ws, 2 * p)
            w_start(e + 1, 1 - ws, 2 * p + 1)

        x_copy(2 * p, 2, s).wait()

        @pl.when(p >= 2)
        def _():
            y_copy(2 * p - 4, 2, s).wait()

        ffn(2, s)
        y_copy(2 * p, 2, s).start(priority=BLOCK_DMA_PRIORITY)
        return carry

    lax.fori_loop(0, npairs, pair_body, 0)

    @pl.when(odd == 1)
    def _():
        s = npairs % 2
        x_copy(2 * npairs, 1, s).wait()

        @pl.when(npairs >= 2)
        def _():
            y_copy(2 * npairs - 4, 2, s).wait()

        ffn(1, s)
        cp = y_copy(2 * npairs, 1, s)
        cp.start(priority=BLOCK_DMA_PRIORITY)
        cp.wait()

    @pl.when(has_next)
    def _():
        def rest(c, carry):
            w_start(e + 1, 1 - ws, c)
            return carry

        lax.fori_loop(jnp.minimum(2 * npairs, W_CHUNKS), W_CHUNKS, rest, 0)

    @pl.when(npairs >= 1)
    def _():
        y_copy(2 * npairs - 2, 2, (npairs - 1) % 2).wait()

    @pl.when((npairs >= 2) & (odd == 0))
    def _():
        y_copy(2 * npairs - 4, 2, npairs % 2).wait()

    @pl.when(e == pl.num_programs(0) - 1)
    def _():
        ybuf[0, pl.ds(0, blk_rows), :] = jnp.zeros((blk_rows, LANES), F32)

        def fill(b, carry):
            cp = y_copy(b, 1, 0)
            cp.start()
            cp.wait()
            return carry

        lax.fori_loop(nb, n_blocks - b0, fill, 0)


def _moe(xs, blk0, nblk, w_gate_up, b_gate_up, w_down, b_down):
    n_blocks = xs.shape[0] // (MOE_BLK * SUBLANES)
    n_exp, d_ff, d = w_down.shape
    grid_spec = pltpu.PrefetchScalarGridSpec(
        num_scalar_prefetch=2,
        grid=(n_exp,),
        in_specs=[pl.BlockSpec(memory_space=pl.ANY),
                  pl.BlockSpec(memory_space=pl.ANY),
                  pl.BlockSpec((None, 1, 2 * d_ff), lambda e, b0, nb: (e, 0, 0)),
                  pl.BlockSpec(memory_space=pl.ANY),
                  pl.BlockSpec((None, 1, d), lambda e, b0, nb: (e, 0, 0))],
        out_specs=pl.BlockSpec(memory_space=pl.ANY),
        scratch_shapes=[pltpu.VMEM((2, 2 * MOE_BLK * SUBLANES, LANES), F32),
                        pltpu.VMEM((2, 2 * MOE_BLK * SUBLANES, LANES), F32),
                        pltpu.VMEM((2, d, 2 * d_ff), F32),
                        pltpu.VMEM((2, d_ff, d), F32),
                        pltpu.VMEM((d, 2 * d_ff), BF16),
                        pltpu.VMEM((d_ff, d), BF16),
                        pltpu.SemaphoreType.DMA((2,)),
                        pltpu.SemaphoreType.DMA((2,)),
                        pltpu.SemaphoreType.DMA((2,))],
    )
    return pl.pallas_call(
        functools.partial(_moe_kernel, n_blocks),
        grid_spec=grid_spec,
        out_shape=jax.ShapeDtypeStruct(xs.shape, F32),
        compiler_params=pltpu.CompilerParams(dimension_semantics=("arbitrary",),
                                             vmem_limit_bytes=VMEM_LIMIT),
        name="moe_ffn",
    )(blk0, nblk, xs, w_gate_up, b_gate_up[:, None, :], w_down, b_down[:, None, :])


def _combine_kernel(destc_ref, destn_ref, x1_ref, mod_ref, route_ref, fg_ref, ys_hbm, o_ref, buf0, buf1, sem):
    buf = (buf0, buf1)
    i = pl.program_id(0)
    n_steps = pl.num_programs(0)
    d = x1_ref.shape[-1]

    def start_gather(dest_ref, s, t0=0, t1=TD):
        for t in range(t0, t1):
            for k in range(TOP_K):
                pltpu.make_async_copy(ys_hbm.at[_tile8(dest_ref[0, 0, t * TOP_K + k])], buf[s].at[k, _tile(t)],
                                      sem.at[s]).start(priority=k % N_DMA_QUEUES)

    def wait_gather(s):
        for k in range(TOP_K):
            pltpu.make_async_copy(ys_hbm.at[pl.ds(0, TD * SUBLANES)], buf[s].at[k], sem.at[s]).wait()

    @pl.when(i == 0)
    def _():
        start_gather(destc_ref, 0)

    def step(s):
        wait_gather(s)
        gates = route_ref[...]
        gk = [gates[:, TOP_K + k:TOP_K + k + 1] for k in range(TOP_K)]
        gate2 = mod_ref[5:6, :]

        def residual(j):
            ls = slice(j * LANES, (j + 1) * LANES)
            rj = _chunk(j, TD)
            y = (gk[0] * buf[s][0, rj, :] + gk[1] * buf[s][1, rj, :]) + \
                (gk[2] * buf[s][2, rj, :] + gk[3] * buf[s][3, rj, :])
            return x1_ref[:, ls] + gate2[:, ls] * y

        ss = jnp.zeros((TD, 1), F32)
        per = TD // SUBLANES
        for j in range(SUBLANES):
            start_gather(destn_ref, 1 - s, j * per, (j + 1) * per)
            xj = residual(j)
            ss = ss + jnp.sum(xj * xj, axis=-1, keepdims=True)
        inv = lax.rsqrt(ss / d + EPS)
        for j in range(SUBLANES):
            ls = slice(j * LANES, (j + 1) * LANES)
            o_ref[:, ls] = residual(j) * inv * fg_ref[:, ls]

        @pl.when(i == n_steps - 1)
        def _():
            wait_gather(1 - s)

    _by_parity(i, step)


def _combine(x1, mod8, route, dest, ys, final_g):
    bsz, seq, d = x1.shape
    n_tok = bsz * seq
    n_steps = n_tok // TD
    per_seq = seq // TD
    dest3 = dest.reshape(n_steps, 1, TD * TOP_K)
    smem_blk = lambda f: pl.BlockSpec((1, 1, TD * TOP_K), f, memory_space=pltpu.SMEM)
    return pl.pallas_call(
        _combine_kernel,
        grid=(n_steps,),
        in_specs=[smem_blk(lambda i: (i, 0, 0)),
                  smem_blk(lambda i: (jnp.minimum(i + 1, n_steps - 1), 0, 0)),
                  pl.BlockSpec((TD, d), lambda i: (i, 0)),
                  pl.BlockSpec((None, 8, d), lambda i: (i // per_seq, 0, 0)),
                  pl.BlockSpec((TD, ROUTE_ROWS), lambda i: (i, 0)),
                  pl.BlockSpec((1, d), lambda i: (0, 0)),
                  pl.BlockSpec(memory_space=pl.ANY)],
        out_specs=pl.BlockSpec((TD, d), lambda i: (i, 0)),
        out_shape=jax.ShapeDtypeStruct((n_tok, d), F32),
        scratch_shapes=[pltpu.VMEM((TOP_K, TD * SUBLANES, LANES), F32),
                        pltpu.VMEM((TOP_K, TD * SUBLANES, LANES), F32),
                        pltpu.SemaphoreType.DMA((2,))],
        compiler_params=pltpu.CompilerParams(dimension_semantics=("arbitrary",),
                                             vmem_limit_bytes=VMEM_LIMIT),
        name="combine",
    )(dest3, dest3, x1.reshape(n_tok, d), mod8, route, final_g.reshape(1, d), ys).reshape(bsz, seq, d)


def _routing(route, counts_f, n_tok):
    nk = n_tok * TOP_K
    experts = jnp.arange(N_EXPERTS, dtype=jnp.int32)
    topi = route[:, :TOP_K].astype(jnp.int32)
    rank = route[:, 2 * TOP_K:3 * TOP_K].astype(jnp.int32)
    counts = counts_f[:, 0].astype(jnp.int32)
    padded = ((counts + MOE_BLK - 1) // MOE_BLK) * MOE_BLK
    pad_end = jnp.cumsum(padded)
    start_pad = pad_end - padded
    dest = rank + jnp.sum(jnp.where(topi[:, :, None] == experts[None, None, :], start_pad[None, None, :], 0), axis=-1)
    p_rows = ((nk + N_EXPERTS * (MOE_BLK - 1) + MOE_BLK - 1) // MOE_BLK) * MOE_BLK
    n_pad = padded - counts
    pad_cum = jnp.cumsum(n_pad)
    j = jnp.arange(p_rows - nk, dtype=jnp.int32)
    first_pad = pad_cum - n_pad
    mine = (j[:, None] >= first_pad[None, :]) & (j[:, None] < pad_cum[None, :])
    in_tail = jnp.sum(jnp.where(mine, (start_pad + counts - first_pad)[None, :], 0), axis=1) + j
    pad_dest = jnp.where(j < pad_cum[-1], in_tail, pad_end[-1] + (j - pad_cum[-1])).astype(jnp.int32)
    return (dest.reshape(-1) * SUBLANES, pad_dest * SUBLANES, (start_pad // MOE_BLK).astype(jnp.int32),
            (padded // MOE_BLK).astype(jnp.int32), p_rows)


def kernel(x, c, ada_w, ada_b, norm_mix_g, w_in, lb_params, hgrn_norm_g, gmlp_ln_g, gmlp_ln_b, gmlp_ws, gmlp_bs,
           gmlp_norm_g, w_out, norm_ffn_g, router_w, router_b, w_gate_up, b_gate_up, w_down, b_down, final_g):
    assert ada_w.shape[0] == 1, "single-layer block"
    bsz, seq, d = x.shape
    assert d == SUBLANES * LANES and seq % TM == 0 and (bsz * seq) % TD == 0
    n_tok = bsz * seq
    mod = _modulation(c, ada_w[0], ada_b[0])
    mod8 = jnp.zeros((bsz, 8, d), F32).at[:, :6].set(mod.reshape(bsz, 6, d))
    x1, h2, route, counts = _mixer(x, mod8, norm_mix_g[0], w_in[0], lb_params, hgrn_norm_g[0], gmlp_ln_g[0],
                                   gmlp_ln_b[0], gmlp_ws[0], gmlp_bs[0], gmlp_norm_g[0], w_out[0],
                                   norm_ffn_g[0], router_w[0], router_b[0])
    route = route.transpose(0, 2, 1).reshape(n_tok, ROUTE_ROWS)
    dest, pad_dest, blk0, nblk, p_rows = _routing(route, counts, n_tok)
    xs = _dispatch(h2.reshape(n_tok, d), dest, pad_dest, p_rows)
    ys = _moe(xs, blk0, nblk, w_gate_up[0], b_gate_up[0], w_down[0], b_down[0])
    return _combine(x1, mod8, route, dest, ys, final_g)
```

```python
import functools

import jax
import jax.numpy as jnp
from jax import lax
from jax.experimental import pallas as pl
from jax.experimental.pallas import tpu as pltpu

F32 = jnp.float32
BF16 = jnp.bfloat16

HGRN_HEADS = 4
HEAD_DIM = 128
HGRN_WIDTH = HGRN_HEADS * HEAD_DIM
HGRN_CHUNK = 64
HGRN_SUB = 16
GMLP_GROUPS = 4
GROUP_DIM = 128
GMLP_WIDTH = GMLP_GROUPS * GROUP_DIM
GMLP_CHUNK = 128
N_EXPERTS = 32
TOP_K = 4
SWIGLU_LIMIT = 7.0
SWIGLU_ALPHA = 1.702
EPS = 1e-6
LANES = 128
SUBLANES = 8
N_DMA_QUEUES = 2
BLOCK_DMA_PRIORITY = 1
ROUTE_ROWS = 16
W_CHUNKS = 8
DECAY_EXP_CLAMP = 60.0
TM = 512
MOE_BLK = 256
TD = 256
VMEM_LIMIT = 56 * 1024 * 1024


def _dot(a, b):
    return jnp.dot(a, b, preferred_element_type=F32)


def _dot_nt(a, b):
    return lax.dot_general(a, b, (((1,), (1,)), ((), ())), preferred_element_type=F32)


def _dot_tn(a, b):
    return lax.dot_general(a, b, (((0,), (0,)), ((), ())), preferred_element_type=F32)


def _rms(x):
    return x * lax.rsqrt(jnp.mean(x * x, axis=-1, keepdims=True) + EPS)


def _gelu(x):
    return 0.5 * x * (1.0 + lax.erf(x * 0.7071067811865476))


def _by_parity(i, fn):
    @pl.when(i % 2 == 0)
    def _():
        fn(0)

    @pl.when(i % 2 == 1)
    def _():
        fn(1)


def _mod_kernel(c_ref, w_ref, b_ref, o_ref):
    c = c_ref[...]
    ca = c * jax.nn.sigmoid(c)
    o_ref[...] = jnp.dot(ca, w_ref[...], precision=lax.Precision.HIGHEST,
                         preferred_element_type=F32) + b_ref[...]


def _modulation(c, ada_w, ada_b):
    bsz, d = c.shape
    n_out = ada_w.shape[1]
    rows = 8
    c_pad = jnp.zeros((rows, d), F32).at[:bsz].set(c)
    tn = 1024
    out = pl.pallas_call(
        _mod_kernel,
        grid=(n_out // tn,),
        in_specs=[pl.BlockSpec((rows, d), lambda j: (0, 0)),
                  pl.BlockSpec((d, tn), lambda j: (0, j)),
                  pl.BlockSpec((1, tn), lambda j: (0, j))],
        out_specs=pl.BlockSpec((rows, tn), lambda j: (0, j)),
        out_shape=jax.ShapeDtypeStruct((rows, n_out), F32),
        name="adaln_mod",
    )(c_pad, ada_w, ada_b.reshape(1, n_out))
    return out[:bsz]


def _mixer_kernel(x_ref, mod_ref, g1_ref, win_ref, lbp_ref, hg_ref, lng_ref, lnb_ref, ws_ref, bs_ref,
                  gng_ref, wout_ref, g2_ref, rwh_ref, rwl_ref, rb_ref,
                  x1_ref, h2_ref, route_ref, cnt_out_ref,
                  z_ref, y_ref, st_ref, cnt_ref):
    @pl.when(pl.program_id(1) == 0)
    def _():
        st_ref[...] = jnp.zeros_like(st_ref)

    x = x_ref[...]
    mod = mod_ref[...]
    h = _rms(x) * g1_ref[...]
    h = h * (1.0 + mod[1:2]) + mod[0:1]
    z_ref[...] = _dot(h.astype(BF16), win_ref[...])

    lbp = lbp_ref[...]
    lbe = jnp.exp(lbp - jnp.max(lbp, axis=0, keepdims=True))
    lb = lbe[0:1] / jnp.sum(lbe, axis=0, keepdims=True)
    hg = hg_ref[...]
    row = lax.broadcasted_iota(jnp.int32, (HGRN_CHUNK, HGRN_CHUNK), 0)
    col = lax.broadcasted_iota(jnp.int32, (HGRN_CHUNK, HGRN_CHUNK), 1)
    causal = col <= row
    crow = lax.broadcasted_iota(jnp.int32, (HGRN_CHUNK, HGRN_WIDTH), 0)
    n_sub = HGRN_CHUNK // HGRN_SUB

    def chunk_scores(c):
        rows = slice(c * HGRN_CHUNK, (c + 1) * HGRN_CHUNK)
        zq = z_ref[rows, 0:HGRN_WIDTH]
        zf = z_ref[rows, HGRN_WIDTH:2 * HGRN_WIDTH]
        zi = z_ref[rows, 2 * HGRN_WIDTH:3 * HGRN_WIDTH]
        zg = z_ref[rows, 3 * HGRN_WIDTH:4 * HGRN_WIDTH]
        f = lb + (1.0 - lb) * jax.nn.sigmoid(zf)
        logf = jnp.log(f)
        kk = 1.0 - f
        b = logf
        for sh in (1, 2, 4, 8, 16, 32):
            b = b + jnp.where(crow >= sh, pltpu.roll(b, sh, axis=0), 0.0)
        b_last = b[HGRN_CHUNK - 1:HGRN_CHUNK, :]
        qe = (zq * jnp.exp(b)).astype(BF16)
        kdec = (kk * jnp.exp(b_last - b)).astype(BF16)
        v = zi.astype(BF16)
        dec_last = jnp.exp(b_last)
        a_sub, k_sub = [], []
        for i in range(n_sub):
            lo, hi = i * HGRN_SUB, (i + 1) * HGRN_SUB
            bref = b[lo - 1:lo, :] if i > 0 else jnp.zeros((1, HGRN_WIDTH), F32)
            a_sub.append((zq[lo:hi] * jnp.exp(b[lo:hi] - bref)).astype(BF16))
            k_sub.append((kk * jnp.exp(jnp.minimum(bref - b, DECAY_EXP_CLAMP))).astype(BF16))
        sc = []
        for hd in range(HGRN_HEADS):
            ls = slice(hd * HEAD_DIM, (hd + 1) * HEAD_DIM)
            s_h = jnp.concatenate([_dot_nt(a_sub[i][:, ls], k_sub[i][:, ls]) for i in range(n_sub)], axis=0)
            sc.append(jnp.where(causal, s_h, 0.0).astype(BF16))
        return qe, kdec, v, dec_last, sc, zg * jax.nn.sigmoid(zg)

    def chunk_output(c, parts):
        rows = slice(c * HGRN_CHUNK, (c + 1) * HGRN_CHUNK)
        qe, kdec, v, dec_last, sc, silu_g = parts
        for hd in range(HGRN_HEADS):
            ls = slice(hd * HEAD_DIM, (hd + 1) * HEAD_DIM)
            st = st_ref[hd]
            o = _dot_nt(qe[:, ls], st.astype(BF16)) + _dot(sc[hd], v[:, ls])
            st_ref[hd] = st * dec_last[:, ls] + _dot_tn(v[:, ls], kdec[:, ls])
            o = _rms(o) * hg[:, ls]
            y_ref[rows, ls] = (o * silu_g[:, ls]).astype(BF16)

    row_g = lax.broadcasted_iota(jnp.int32, (GMLP_CHUNK, GMLP_CHUNK), 0)
    col_g = lax.broadcasted_iota(jnp.int32, (GMLP_CHUNK, GMLP_CHUNK), 1)
    gng = gng_ref[...]
    ws_c = [jnp.where(col_g <= row_g, ws_ref[g], 0.0).astype(BF16) for g in range(GMLP_GROUPS)]

    def gmlp_block(n):
        rs = slice(n * GMLP_CHUNK, (n + 1) * GMLP_CHUNK)
        u = _gelu(z_ref[rs, 4 * HGRN_WIDTH:4 * HGRN_WIDTH + GMLP_WIDTH])
        gv = _gelu(z_ref[rs, 4 * HGRN_WIDTH + GMLP_WIDTH:])
        mu = jnp.mean(gv, axis=-1, keepdims=True)
        gc = gv - mu
        var = jnp.mean(gc * gc, axis=-1, keepdims=True)
        vn = (gc * lax.rsqrt(var + EPS) * lng_ref[...] + lnb_ref[...]).astype(BF16)
        for g in range(GMLP_GROUPS):
            ls = slice(g * GROUP_DIM, (g + 1) * GROUP_DIM)
            sv = _dot(ws_c[g], vn[:, ls]) + bs_ref[g]
            yy = _rms(u[:, ls] * sv) * gng[:, ls]
            y_ref[rs, HGRN_WIDTH + g * GROUP_DIM:HGRN_WIDTH + (g + 1) * GROUP_DIM] = yy.astype(BF16)

    n_chunks = TM // HGRN_CHUNK
    per_blk = GMLP_CHUNK // HGRN_CHUNK
    parts = [chunk_scores(c) for c in range(n_chunks)]
    for c in range(n_chunks):
        chunk_output(c, parts[c])
        if c % per_blk == per_blk - 1:
            gmlp_block(c // per_blk)

    x1 = x + mod[2:3] * _dot(y_ref[...], wout_ref[...])
    x1_ref[...] = x1
    h2 = _rms(x1) * g2_ref[...]
    h2 = h2 * (1.0 + mod[4:5]) + mod[3:4]
    h2_ref[...] = h2

    hh = h2.astype(BF16)
    hl = (h2 - hh.astype(F32)).astype(BF16)
    rwh, rwl = rwh_ref[...], rwl_ref[...]
    logits = _dot_nt(rwh, hh) + (_dot_nt(rwh, hl) + _dot_nt(rwl, hh)) + rb_ref[...]
    erow = lax.broadcasted_iota(jnp.int32, (N_EXPERTS, TM), 0)
    vals, idxs = [], []
    for _ in range(TOP_K):
        m = jnp.max(logits, axis=0, keepdims=True)
        idx = jnp.min(jnp.where(logits == m, erow, N_EXPERTS), axis=0, keepdims=True)
        vals.append(m)
        idxs.append(idx)
        logits = jnp.where(erow == idx, -jnp.inf, logits)
    es = [jnp.exp(v - vals[0]) for v in vals]
    tot = (es[0] + es[1]) + (es[2] + es[3])

    @pl.when((pl.program_id(0) == 0) & (pl.program_id(1) == 0))
    def _():
        cnt_ref[...] = jnp.zeros_like(cnt_ref)

    hot = [erow == idxs[k] for k in range(TOP_K)]
    picked = jnp.where((hot[0] | hot[1]) | (hot[2] | hot[3]), 1.0, 0.0)
    row_t = lax.broadcasted_iota(jnp.int32, (TM, TM), 0)
    col_t = lax.broadcasted_iota(jnp.int32, (TM, TM), 1)
    earlier = jnp.where(row_t < col_t, 1.0, 0.0).astype(BF16)
    seen = _dot(picked.astype(BF16), earlier) + cnt_ref[:, 0:1]
    cnt_ref[...] = cnt_ref[...] + jnp.sum(picked, axis=1, keepdims=True)
    cnt_out_ref[...] = cnt_ref[...]
    ranks = [jnp.sum(jnp.where(hot[k], seen, 0.0), axis=0, keepdims=True) for k in range(TOP_K)]
    route_ref[...] = jnp.concatenate([i.astype(F32) for i in idxs] + [e / tot for e in es] + ranks +
                                     [jnp.zeros((ROUTE_ROWS - 3 * TOP_K, TM), F32)], axis=0)


def _mixer(x, mod8, norm_mix_g, w_in, lb_params, hgrn_norm_g, gmlp_ln_g, gmlp_ln_b, gmlp_ws, gmlp_bs,
           gmlp_norm_g, w_out, norm_ffn_g, router_w, router_b):
    bsz, seq, d = x.shape
    n_in = w_in.shape[1]
    rw = router_w.T
    rwh = rw.astype(BF16)
    rwl = (rw - rwh.astype(F32)).astype(BF16)
    rb = router_b.reshape(N_EXPERTS, 1)
    const = lambda *shape: pl.BlockSpec(shape, lambda b, i: (0,) * len(shape))
    tile = lambda w: pl.BlockSpec((None, TM, w), lambda b, i: (b, i, 0))
    per_seq = seq // TM
    return pl.pallas_call(
        _mixer_kernel,
        grid=(bsz, seq // TM),
        in_specs=[tile(d),
                  pl.BlockSpec((None, 8, d), lambda b, i: (b, 0, 0)),
                  const(1, d), const(d, n_in), const(2, HGRN_WIDTH), const(1, HGRN_WIDTH),
                  const(1, GMLP_WIDTH), const(1, GMLP_WIDTH),
                  const(GMLP_GROUPS, GMLP_CHUNK, GMLP_CHUNK), const(GMLP_GROUPS, GMLP_CHUNK, 1),
                  const(1, GMLP_WIDTH), const(d, d), const(1, d),
                  const(N_EXPERTS, d), const(N_EXPERTS, d), const(N_EXPERTS, 1)],
        out_specs=[tile(d), tile(d),
                   pl.BlockSpec((None, ROUTE_ROWS, TM), lambda b, i: (b * per_seq + i, 0, 0)),
                   const(N_EXPERTS, LANES)],
        out_shape=[jax.ShapeDtypeStruct((bsz, seq, d), F32),
                   jax.ShapeDtypeStruct((bsz, seq, d), F32),
                   jax.ShapeDtypeStruct((bsz * per_seq, ROUTE_ROWS, TM), F32),
                   jax.ShapeDtypeStruct((N_EXPERTS, LANES), F32)],
        scratch_shapes=[pltpu.VMEM((TM, n_in), F32),
                        pltpu.VMEM((TM, d), BF16),
                        pltpu.VMEM((HGRN_HEADS, HEAD_DIM, HEAD_DIM), F32),
                        pltpu.VMEM((N_EXPERTS, LANES), F32)],
        compiler_params=pltpu.CompilerParams(dimension_semantics=("arbitrary", "arbitrary"),
                                             vmem_limit_bytes=VMEM_LIMIT),
        name="mixer",
    )(x, mod8, norm_mix_g.reshape(1, d), w_in.astype(BF16), lb_params, hgrn_norm_g.reshape(1, -1),
      gmlp_ln_g.reshape(1, -1), gmlp_ln_b.reshape(1, -1), gmlp_ws, gmlp_bs[:, :, None],
      gmlp_norm_g.reshape(1, -1), w_out.astype(BF16), norm_ffn_g.reshape(1, d), rwh, rwl, rb)


def _tile(r):
    return pl.ds(r * SUBLANES, SUBLANES)


def _tile8(r8):
    return pl.ds(pl.multiple_of(r8, SUBLANES), SUBLANES)


def _chunk(j, n):
    return pl.ds(j, n, stride=SUBLANES)


def _dispatch_kernel(n_pad_step, dest_ref, pad_ref, h2_ref, xs_hbm, rows, zero_tile, sem):
    i = pl.program_id(0)
    n_steps = pl.num_programs(0)

    def wait_step(s):
        for _ in range(TOP_K):
            pltpu.make_async_copy(rows.at[s], xs_hbm.at[pl.ds(0, TD * SUBLANES)], sem.at[s]).wait()
        n = n_pad_step * SUBLANES
        pltpu.make_async_copy(rows.at[s, pl.ds(0, n)], xs_hbm.at[pl.ds(0, n)], sem.at[s]).wait()

    @pl.when(i == 0)
    def _():
        zero_tile[...] = jnp.zeros_like(zero_tile)

    def step(s):
        @pl.when(i >= 2)
        def _():
            wait_step(s)

        for j in range(SUBLANES):
            rows[s, _chunk(j, TD), :] = h2_ref[:, j * LANES:(j + 1) * LANES]
        for t in range(TD):
            for k in range(TOP_K):
                pltpu.make_async_copy(rows.at[s, _tile(t)], xs_hbm.at[_tile8(dest_ref[0, k, t])],
                                      sem.at[s]).start(priority=k % N_DMA_QUEUES)
        for q in range(n_pad_step):
            pltpu.make_async_copy(zero_tile, xs_hbm.at[_tile8(pad_ref[0, 0, q])],
                                  sem.at[s]).start(priority=q % N_DMA_QUEUES)

        @pl.when(i == n_steps - 1)
        def _():
            wait_step(s)

            @pl.when(i >= 1)
            def _():
                wait_step(1 - s)

    _by_parity(i, step)


def _dest_block(step_of):
    per_tile = TM // TD
    return pl.BlockSpec((1, TOP_K, TD), lambda i: (step_of(i) // per_tile, 0, step_of(i) % per_tile),
                        memory_space=pltpu.SMEM)


def _dispatch(h2, dest, pad_dest, p_rows):
    n_tok, d = h2.shape
    n_steps = n_tok // TD
    n_pad_step = pad_dest.shape[0] // n_steps
    return pl.pallas_call(
        functools.partial(_dispatch_kernel, n_pad_step),
        grid=(n_steps,),
        in_specs=[_dest_block(lambda i: i),
                  pl.BlockSpec((1, 1, n_pad_step), lambda i: (i, 0, 0), memory_space=pltpu.SMEM),
                  pl.BlockSpec((TD, d), lambda i: (i, 0))],
        out_specs=pl.BlockSpec(memory_space=pl.ANY),
        out_shape=jax.ShapeDtypeStruct((p_rows * SUBLANES, LANES), F32),
        scratch_shapes=[pltpu.VMEM((2, TD * SUBLANES, LANES), F32),
                        pltpu.VMEM((SUBLANES, LANES), F32),
                        pltpu.SemaphoreType.DMA((2,))],
        compiler_params=pltpu.CompilerParams(dimension_semantics=("arbitrary",)),
        name="dispatch",
    )(dest, pad_dest.reshape(n_steps, 1, n_pad_step), h2)


def _moe_kernel(n_blocks, blk0_ref, nblk_ref, xs_hbm, wgu_hbm, bgu_ref, wd_hbm, bd_ref, ys_hbm,
                xbuf, ybuf, wgu_f, wd_f, wgu_bf, wd_bf, isem, osem, wsem):
    e = pl.program_id(0)
    n_exp = pl.num_programs(0)
    nb = nblk_ref[e]
    b0 = blk0_ref[e]
    d, d_ff = wd_f.shape[2], wd_f.shape[1]
    blk_rows = MOE_BLK * SUBLANES
    ws = e % 2
    gu_rows, d_rows = d // W_CHUNKS, d_ff // W_CHUNKS

    npairs = nb // 2
    odd = nb - 2 * npairs

    def rows_of(b, n):
        return pl.ds(pl.multiple_of((b0 + b) * blk_rows, blk_rows), n * blk_rows)

    def x_copy(b, n, s):
        return pltpu.make_async_copy(xs_hbm.at[rows_of(b, n)], xbuf.at[s, pl.ds(0, n * blk_rows)], isem.at[s])

    def y_copy(b, n, s):
        return pltpu.make_async_copy(ybuf.at[s, pl.ds(0, n * blk_rows)], ys_hbm.at[rows_of(b, n)], osem.at[s])

    def w_start(ex, slot, c):
        r_gu = pl.ds(pl.multiple_of(c * gu_rows, gu_rows), gu_rows)
        r_d = pl.ds(pl.multiple_of(c * d_rows, d_rows), d_rows)
        pltpu.make_async_copy(wgu_hbm.at[ex, r_gu], wgu_f.at[slot, r_gu], wsem.at[slot]).start()
        pltpu.make_async_copy(wd_hbm.at[ex, r_d], wd_f.at[slot, r_d], wsem.at[slot]).start()

    def w_wait(slot):
        pltpu.make_async_copy(wgu_hbm.at[0], wgu_f.at[slot], wsem.at[slot]).wait()
        pltpu.make_async_copy(wd_hbm.at[0], wd_f.at[slot], wsem.at[slot]).wait()

    @pl.when(e == 0)
    def _():
        for c in range(W_CHUNKS):
            w_start(0, 0, c)

    @pl.when(npairs > 0)
    def _():
        x_copy(0, 2, 0).start(priority=BLOCK_DMA_PRIORITY)

    @pl.when((npairs == 0) & (odd == 1))
    def _():
        x_copy(0, 1, 0).start(priority=BLOCK_DMA_PRIORITY)

    w_wait(ws)
    wgu_bf[...] = wgu_f[ws].astype(BF16)
    wd_bf[...] = wd_f[ws].astype(BF16)
    has_next = e + 1 < n_exp

    def ffn(n, s):
        rows = n * MOE_BLK
        xb = jnp.concatenate([xbuf[s, _chunk(j, rows), :] for j in range(SUBLANES)], axis=-1).astype(BF16)
        gu = _dot(xb, wgu_bf[...]) + bgu_ref[...]
        gate = jnp.minimum(gu[:, :d_ff], SWIGLU_LIMIT)
        up = jnp.clip(gu[:, d_ff:], -SWIGLU_LIMIT, SWIGLU_LIMIT)
        glu = gate * jax.nn.sigmoid(SWIGLU_ALPHA * gate)
        yb = _dot(((up + 1.0) * glu).astype(BF16), wd_bf[...]) + bd_ref[...]
        for j in range(SUBLANES):
            ybuf[s, _chunk(j, rows), :] = yb[:, j * LANES:(j + 1) * LANES]

    def pair_body(p, carry):
        s = p % 2

        @pl.when(p + 1 < npairs)
        def _():
            x_copy(2 * p + 2, 2, 1 - s).start(priority=BLOCK_DMA_PRIORITY)

        @pl.when((p + 1 == npairs) & (odd == 1))
        def _():
            x_copy(2 * p + 2, 1, 1 - s).start(priority=BLOCK_DMA_PRIORITY)

        @pl.when(has_next & (2 * p < W_CHUNKS))
        def _():
            w_start(e + 1, 1 - ws, 2 * p)
            w_start(e + 1, 1 - ws, 2 * p + 1)

        x_copy(2 * p, 2, s).wait()

        @pl.when(p >= 2)
        def _():
            y_copy(2 * p - 4, 2, s).wait()

        ffn(2, s)
        y_copy(2 * p, 2, s).start(priority=BLOCK_DMA_PRIORITY)
        return carry

    lax.fori_loop(0, npairs, pair_body, 0)

    @pl.when(odd == 1)
    def _():
        s = npairs % 2
        x_copy(2 * npairs, 1, s).wait()

        @pl.when(npairs >= 2)
        def _():
            y_copy(2 * npairs - 4, 2, s).wait()

        ffn(1, s)
        cp = y_copy(2 * npairs, 1, s)
        cp.start(priority=BLOCK_DMA_PRIORITY)
        cp.wait()

    @pl.when(has_next)
    def _():
        def rest(c, carry):
            w_start(e + 1, 1 - ws, c)
            return carry

        lax.fori_loop(jnp.minimum(2 * npairs, W_CHUNKS), W_CHUNKS, rest, 0)

    @pl.when(npairs >= 1)
    def _():
        y_copy(2 * npairs - 2, 2, (npairs - 1) % 2).wait()

    @pl.when((npairs >= 2) & (odd == 0))
    def _():
        y_copy(2 * npairs - 4, 2, npairs % 2).wait()

    @pl.when(e == pl.num_programs(0) - 1)
    def _():
        ybuf[0, pl.ds(0, blk_rows), :] = jnp.zeros((blk_rows, LANES), F32)

        def fill(b, carry):
            cp = y_copy(b, 1, 0)
            cp.start()
            cp.wait()
            return carry

        lax.fori_loop(nb, n_blocks - b0, fill, 0)


def _moe(xs, blk0, nblk, w_gate_up, b_gate_up, w_down, b_down):
    n_blocks = xs.shape[0] // (MOE_BLK * SUBLANES)
    n_exp, d_ff, d = w_down.shape
    grid_spec = pltpu.PrefetchScalarGridSpec(
        num_scalar_prefetch=2,
        grid=(n_exp,),
        in_specs=[pl.BlockSpec(memory_space=pl.ANY),
                  pl.BlockSpec(memory_space=pl.ANY),
                  pl.BlockSpec((None, 1, 2 * d_ff), lambda e, b0, nb: (e, 0, 0)),
                  pl.BlockSpec(memory_space=pl.ANY),
                  pl.BlockSpec((None, 1, d), lambda e, b0, nb: (e, 0, 0))],
        out_specs=pl.BlockSpec(memory_space=pl.ANY),
        scratch_shapes=[pltpu.VMEM((2, 2 * MOE_BLK * SUBLANES, LANES), F32),
                        pltpu.VMEM((2, 2 * MOE_BLK * SUBLANES, LANES), F32),
                        pltpu.VMEM((2, d, 2 * d_ff), F32),
                        pltpu.VMEM((2, d_ff, d), F32),
                        pltpu.VMEM((d, 2 * d_ff), BF16),
                        pltpu.VMEM((d_ff, d), BF16),
                        pltpu.SemaphoreType.DMA((2,)),
                        pltpu.SemaphoreType.DMA((2,)),
                        pltpu.SemaphoreType.DMA((2,))],
    )
    return pl.pallas_call(
        functools.partial(_moe_kernel, n_blocks),
        grid_spec=grid_spec,
        out_shape=jax.ShapeDtypeStruct(xs.shape, F32),
        compiler_params=pltpu.CompilerParams(dimension_semantics=("arbitrary",),
                                             vmem_limit_bytes=VMEM_LIMIT),
        name="moe_ffn",
    )(blk0, nblk, xs, w_gate_up, b_gate_up[:, None, :], w_down, b_down[:, None, :])


def _combine_kernel(destc_ref, destn_ref, x1_ref, mod_ref, route_ref, fg_ref, ys_hbm, o_ref, buf0, buf1, sem):
    buf = (buf0, buf1)
    i = pl.program_id(0)
    n_steps = pl.num_programs(0)
    d = x1_ref.shape[-1]

    def start_gather(dest_ref, s, t0=0, t1=TD):
        for t in range(t0, t1):
            for k in range(TOP_K):
                pltpu.make_async_copy(ys_hbm.at[_tile8(dest_ref[0, k, t])], buf[s].at[k, _tile(t)],
                                      sem.at[s]).start(priority=k % N_DMA_QUEUES)

    def wait_gather(s):
        for k in range(TOP_K):
            pltpu.make_async_copy(ys_hbm.at[pl.ds(0, TD * SUBLANES)], buf[s].at[k], sem.at[s]).wait()

    @pl.when(i == 0)
    def _():
        start_gather(destc_ref, 0)

    def step(s):
        wait_gather(s)
        gates = route_ref[...]
        gk = [gates[:, TOP_K + k:TOP_K + k + 1] for k in range(TOP_K)]
        gate2 = mod_ref[5:6, :]

        def residual(j):
            ls = slice(j * LANES, (j + 1) * LANES)
            rj = _chunk(j, TD)
            y = (gk[0] * buf[s][0, rj, :] + gk[1] * buf[s][1, rj, :]) + \
                (gk[2] * buf[s][2, rj, :] + gk[3] * buf[s][3, rj, :])
            return x1_ref[:, ls] + gate2[:, ls] * y

        ss = jnp.zeros((TD, 1), F32)
        per = TD // SUBLANES
        for j in range(SUBLANES):
            start_gather(destn_ref, 1 - s, j * per, (j + 1) * per)
            xj = residual(j)
            ss = ss + jnp.sum(xj * xj, axis=-1, keepdims=True)
        inv = lax.rsqrt(ss / d + EPS)
        for j in range(SUBLANES):
            ls = slice(j * LANES, (j + 1) * LANES)
            o_ref[:, ls] = residual(j) * inv * fg_ref[:, ls]

        @pl.when(i == n_steps - 1)
        def _():
            wait_gather(1 - s)

    _by_parity(i, step)


def _combine(x1, mod8, route, dest, ys, final_g):
    bsz, seq, d = x1.shape
    n_tok = bsz * seq
    n_steps = n_tok // TD
    per_seq = seq // TD
    return pl.pallas_call(
        _combine_kernel,
        grid=(n_steps,),
        in_specs=[_dest_block(lambda i: i),
                  _dest_block(lambda i: jnp.minimum(i + 1, n_steps - 1)),
                  pl.BlockSpec((TD, d), lambda i: (i, 0)),
                  pl.BlockSpec((None, 8, d), lambda i: (i // per_seq, 0, 0)),
                  pl.BlockSpec((TD, ROUTE_ROWS), lambda i: (i, 0)),
                  pl.BlockSpec((1, d), lambda i: (0, 0)),
                  pl.BlockSpec(memory_space=pl.ANY)],
        out_specs=pl.BlockSpec((TD, d), lambda i: (i, 0)),
        out_shape=jax.ShapeDtypeStruct((n_tok, d), F32),
        scratch_shapes=[pltpu.VMEM((TOP_K, TD * SUBLANES, LANES), F32),
                        pltpu.VMEM((TOP_K, TD * SUBLANES, LANES), F32),
                        pltpu.SemaphoreType.DMA((2,))],
        compiler_params=pltpu.CompilerParams(dimension_semantics=("arbitrary",),
                                             vmem_limit_bytes=VMEM_LIMIT),
        name="combine",
    )(dest, dest, x1.reshape(n_tok, d), mod8, route, final_g.reshape(1, d), ys).reshape(bsz, seq, d)


def _dest_kernel(route_ref, start_ref, dest_ref):
    r = route_ref[...]
    erow = lax.broadcasted_iota(jnp.int32, (N_EXPERTS, TM), 0).astype(F32)
    rows = []
    for k in range(TOP_K):
        base = jnp.sum(jnp.where(erow == r[k:k + 1, :], start_ref[...], 0.0), axis=0, keepdims=True)
        rows.append((base + r[2 * TOP_K + k:2 * TOP_K + k + 1, :]) * float(SUBLANES))
    dest_ref[...] = jnp.concatenate(rows, axis=0).astype(jnp.int32)


def _destinations(route_t, start_pad):
    n_tiles = route_t.shape[0]
    return pl.pallas_call(
        _dest_kernel,
        grid=(n_tiles,),
        in_specs=[pl.BlockSpec((None, ROUTE_ROWS, TM), lambda i: (i, 0, 0)),
                  pl.BlockSpec((N_EXPERTS, 1), lambda i: (0, 0))],
        out_specs=pl.BlockSpec((None, TOP_K, TM), lambda i: (i, 0, 0)),
        out_shape=jax.ShapeDtypeStruct((n_tiles, TOP_K, TM), jnp.int32),
        name="destinations",
    )(route_t, start_pad.astype(F32).reshape(N_EXPERTS, 1))


def _routing(counts_f, n_tok):
    nk = n_tok * TOP_K
    counts = counts_f[:, 0].astype(jnp.int32)
    padded = ((counts + MOE_BLK - 1) // MOE_BLK) * MOE_BLK
    pad_end = jnp.cumsum(padded)
    start_pad = pad_end - padded
    p_rows = ((nk + N_EXPERTS * (MOE_BLK - 1) + MOE_BLK - 1) // MOE_BLK) * MOE_BLK
    n_pad = padded - counts
    pad_cum = jnp.cumsum(n_pad)
    j = jnp.arange(p_rows - nk, dtype=jnp.int32)
    first_pad = pad_cum - n_pad
    mine = (j[:, None] >= first_pad[None, :]) & (j[:, None] < pad_cum[None, :])
    in_tail = jnp.sum(jnp.where(mine, (start_pad + counts - first_pad)[None, :], 0), axis=1) + j
    pad_dest = jnp.where(j < pad_cum[-1], in_tail, pad_end[-1] + (j - pad_cum[-1])).astype(jnp.int32)
    return (start_pad, pad_dest * SUBLANES, (start_pad // MOE_BLK).astype(jnp.int32),
            (padded // MOE_BLK).astype(jnp.int32), p_rows)


def kernel(x, c, ada_w, ada_b, norm_mix_g, w_in, lb_params, hgrn_norm_g, gmlp_ln_g, gmlp_ln_b, gmlp_ws, gmlp_bs,
           gmlp_norm_g, w_out, norm_ffn_g, router_w, router_b, w_gate_up, b_gate_up, w_down, b_down, final_g):
    assert ada_w.shape[0] == 1, "single-layer block"
    bsz, seq, d = x.shape
    assert d == SUBLANES * LANES and seq % TM == 0 and (bsz * seq) % TD == 0
    n_tok = bsz * seq
    mod = _modulation(c, ada_w[0], ada_b[0])
    mod8 = jnp.zeros((bsz, 8, d), F32).at[:, :6].set(mod.reshape(bsz, 6, d))
    x1, h2, route, counts = _mixer(x, mod8, norm_mix_g[0], w_in[0], lb_params, hgrn_norm_g[0], gmlp_ln_g[0],
                                   gmlp_ln_b[0], gmlp_ws[0], gmlp_bs[0], gmlp_norm_g[0], w_out[0],
                                   norm_ffn_g[0], router_w[0], router_b[0])
    start_pad, pad_dest, blk0, nblk, p_rows = _routing(counts, n_tok)
    dest = _destinations(route, start_pad)
    route = route.transpose(0, 2, 1).reshape(n_tok, ROUTE_ROWS)
    xs = _dispatch(h2.reshape(n_tok, d), dest, pad_dest, p_rows)
    ys = _moe(xs, blk0, nblk, w_gate_up[0], b_gate_up[0], w_down[0], b_down[0])
    return _combine(x1, mod8, route, dest, ys, final_g)
```

```python
import functools

import jax
import jax.numpy as jnp
from jax import lax
from jax.experimental import pallas as pl
from jax.experimental.pallas import tpu as pltpu

F32 = jnp.float32
BF16 = jnp.bfloat16

HGRN_HEADS = 4
HEAD_DIM = 128
HGRN_WIDTH = HGRN_HEADS * HEAD_DIM
HGRN_CHUNK = 64
HGRN_SUB = 16
GMLP_GROUPS = 4
GROUP_DIM = 128
GMLP_WIDTH = GMLP_GROUPS * GROUP_DIM
GMLP_CHUNK = 128
N_EXPERTS = 32
TOP_K = 4
SWIGLU_LIMIT = 7.0
SWIGLU_ALPHA = 1.702
EPS = 1e-6
LANES = 128
SUBLANES = 8
N_DMA_QUEUES = 2
BLOCK_DMA_PRIORITY = 1
ROUTE_ROWS = 16
W_CHUNKS = 8
DECAY_EXP_CLAMP = 60.0
TM = 512
MOE_BLK = 256
TD = 256
VMEM_LIMIT = 56 * 1024 * 1024


def _dot(a, b):
    return jnp.dot(a, b, preferred_element_type=F32)


def _dot_nt(a, b):
    return lax.dot_general(a, b, (((1,), (1,)), ((), ())), preferred_element_type=F32)


def _dot_tn(a, b):
    return lax.dot_general(a, b, (((0,), (0,)), ((), ())), preferred_element_type=F32)


def _rms(x):
    return x * lax.rsqrt(jnp.mean(x * x, axis=-1, keepdims=True) + EPS)


def _gelu(x):
    return 0.5 * x * (1.0 + lax.erf(x * 0.7071067811865476))


def _by_parity(i, fn):
    @pl.when(i % 2 == 0)
    def _():
        fn(0)

    @pl.when(i % 2 == 1)
    def _():
        fn(1)


def _mod_kernel(c_ref, w_ref, b_ref, o_ref):
    c = c_ref[...]
    ca = c * jax.nn.sigmoid(c)
    o_ref[...] = jnp.dot(ca, w_ref[...], precision=lax.Precision.HIGHEST,
                         preferred_element_type=F32) + b_ref[...]


def _modulation(c, ada_w, ada_b):
    bsz, d = c.shape
    n_out = ada_w.shape[1]
    rows = 8
    c_pad = jnp.zeros((rows, d), F32).at[:bsz].set(c)
    tn = 2048
    out = pl.pallas_call(
        _mod_kernel,
        grid=(n_out // tn,),
        in_specs=[pl.BlockSpec((rows, d), lambda j: (0, 0)),
                  pl.BlockSpec((d, tn), lambda j: (0, j)),
                  pl.BlockSpec((1, tn), lambda j: (0, j))],
        out_specs=pl.BlockSpec((rows, tn), lambda j: (0, j)),
        out_shape=jax.ShapeDtypeStruct((rows, n_out), F32),
        name="adaln_mod",
    )(c_pad, ada_w, ada_b.reshape(1, n_out))
    return out[:bsz]


def _mixer_kernel(x_ref, mod_ref, g1_ref, win_ref, lbp_ref, hg_ref, lng_ref, lnb_ref, ws_ref, bs_ref,
                  gng_ref, wout_ref, g2_ref, rwh_ref, rwl_ref, rb_ref,
                  x1_ref, h2_ref, route_ref, cnt_out_ref,
                  z_ref, y_ref, st_ref, cnt_ref):
    @pl.when(pl.program_id(1) == 0)
    def _():
        st_ref[...] = jnp.zeros_like(st_ref)

    x = x_ref[...]
    mod = mod_ref[...]
    h = _rms(x) * g1_ref[...]
    h = h * (1.0 + mod[1:2]) + mod[0:1]
    z_ref[...] = _dot(h.astype(BF16), win_ref[...])

    lbp = lbp_ref[...]
    lbe = jnp.exp(lbp - jnp.max(lbp, axis=0, keepdims=True))
    lb = lbe[0:1] / jnp.sum(lbe, axis=0, keepdims=True)
    hg = hg_ref[...]
    row = lax.broadcasted_iota(jnp.int32, (HGRN_CHUNK, HGRN_CHUNK), 0)
    col = lax.broadcasted_iota(jnp.int32, (HGRN_CHUNK, HGRN_CHUNK), 1)
    causal = col <= row
    crow = lax.broadcasted_iota(jnp.int32, (HGRN_CHUNK, HGRN_WIDTH), 0)
    n_sub = HGRN_CHUNK // HGRN_SUB

    def chunk_scores(c):
        rows = slice(c * HGRN_CHUNK, (c + 1) * HGRN_CHUNK)
        zq = z_ref[rows, 0:HGRN_WIDTH]
        zf = z_ref[rows, HGRN_WIDTH:2 * HGRN_WIDTH]
        zi = z_ref[rows, 2 * HGRN_WIDTH:3 * HGRN_WIDTH]
        zg = z_ref[rows, 3 * HGRN_WIDTH:4 * HGRN_WIDTH]
        f = lb + (1.0 - lb) * jax.nn.sigmoid(zf)
        logf = jnp.log(f)
        kk = 1.0 - f
        b = logf
        for sh in (1, 2, 4, 8, 16, 32):
            b = b + jnp.where(crow >= sh, pltpu.roll(b, sh, axis=0), 0.0)
        b_last = b[HGRN_CHUNK - 1:HGRN_CHUNK, :]
        qe = (zq * jnp.exp(b)).astype(BF16)
        kdec = (kk * jnp.exp(b_last - b)).astype(BF16)
        v = zi.astype(BF16)
        dec_last = jnp.exp(b_last)
        a_sub, k_sub = [], []
        for i in range(n_sub):
            lo, hi = i * HGRN_SUB, (i + 1) * HGRN_SUB
            bref = b[lo - 1:lo, :] if i > 0 else jnp.zeros((1, HGRN_WIDTH), F32)
            a_sub.append((zq[lo:hi] * jnp.exp(b[lo:hi] - bref)).astype(BF16))
            k_sub.append((kk * jnp.exp(jnp.minimum(bref - b, DECAY_EXP_CLAMP))).astype(BF16))
        sc = []
        for hd in range(HGRN_HEADS):
            ls = slice(hd * HEAD_DIM, (hd + 1) * HEAD_DIM)
            s_h = jnp.concatenate([_dot_nt(a_sub[i][:, ls], k_sub[i][:, ls]) for i in range(n_sub)], axis=0)
            sc.append(jnp.where(causal, s_h, 0.0).astype(BF16))
        return qe, kdec, v, dec_last, sc, zg * jax.nn.sigmoid(zg)

    def chunk_output(c, parts):
        rows = slice(c * HGRN_CHUNK, (c + 1) * HGRN_CHUNK)
        qe, kdec, v, dec_last, sc, silu_g = parts
        for hd in range(HGRN_HEADS):
            ls = slice(hd * HEAD_DIM, (hd + 1) * HEAD_DIM)
            st = st_ref[hd]
            o = _dot_nt(qe[:, ls], st.astype(BF16)) + _dot(sc[hd], v[:, ls])
            st_ref[hd] = st * dec_last[:, ls] + _dot_tn(v[:, ls], kdec[:, ls])
            o = _rms(o) * hg[:, ls]
            y_ref[rows, ls] = (o * silu_g[:, ls]).astype(BF16)

    row_g = lax.broadcasted_iota(jnp.int32, (GMLP_CHUNK, GMLP_CHUNK), 0)
    col_g = lax.broadcasted_iota(jnp.int32, (GMLP_CHUNK, GMLP_CHUNK), 1)
    gng = gng_ref[...]
    ws_c = [jnp.where(col_g <= row_g, ws_ref[g], 0.0).astype(BF16) for g in range(GMLP_GROUPS)]

    def gmlp_block(n):
        rs = slice(n * GMLP_CHUNK, (n + 1) * GMLP_CHUNK)
        u = _gelu(z_ref[rs, 4 * HGRN_WIDTH:4 * HGRN_WIDTH + GMLP_WIDTH])
        gv = _gelu(z_ref[rs, 4 * HGRN_WIDTH + GMLP_WIDTH:])
        mu = jnp.mean(gv, axis=-1, keepdims=True)
        gc = gv - mu
        var = jnp.mean(gc * gc, axis=-1, keepdims=True)
        vn = (gc * lax.rsqrt(var + EPS) * lng_ref[...] + lnb_ref[...]).astype(BF16)
        for g in range(GMLP_GROUPS):
            ls = slice(g * GROUP_DIM, (g + 1) * GROUP_DIM)
            sv = _dot(ws_c[g], vn[:, ls]) + bs_ref[g]
            yy = _rms(u[:, ls] * sv) * gng[:, ls]
            y_ref[rs, HGRN_WIDTH + g * GROUP_DIM:HGRN_WIDTH + (g + 1) * GROUP_DIM] = yy.astype(BF16)

    n_chunks = TM // HGRN_CHUNK
    per_blk = GMLP_CHUNK // HGRN_CHUNK
    parts = [chunk_scores(c) for c in range(n_chunks)]
    for c in range(n_chunks):
        chunk_output(c, parts[c])
        if c % per_blk == per_blk - 1:
            gmlp_block(c // per_blk)

    x1 = x + mod[2:3] * _dot(y_ref[...], wout_ref[...])
    x1_ref[...] = x1
    h2 = _rms(x1) * g2_ref[...]
    h2 = h2 * (1.0 + mod[4:5]) + mod[3:4]
    h2_ref[...] = h2

    hh = h2.astype(BF16)
    hl = (h2 - hh.astype(F32)).astype(BF16)
    rwh, rwl = rwh_ref[...], rwl_ref[...]
    logits = _dot_nt(rwh, hh) + (_dot_nt(rwh, hl) + _dot_nt(rwl, hh)) + rb_ref[...]
    erow = lax.broadcasted_iota(jnp.int32, (N_EXPERTS, TM), 0)
    vals, idxs = [], []
    for _ in range(TOP_K):
        m = jnp.max(logits, axis=0, keepdims=True)
        idx = jnp.min(jnp.where(logits == m, erow, N_EXPERTS), axis=0, keepdims=True)
        vals.append(m)
        idxs.append(idx)
        logits = jnp.where(erow == idx, -jnp.inf, logits)
    es = [jnp.exp(v - vals[0]) for v in vals]
    tot = (es[0] + es[1]) + (es[2] + es[3])

    @pl.when((pl.program_id(0) == 0) & (pl.program_id(1) == 0))
    def _():
        cnt_ref[...] = jnp.zeros_like(cnt_ref)

    hot = [erow == idxs[k] for k in range(TOP_K)]
    picked = jnp.where((hot[0] | hot[1]) | (hot[2] | hot[3]), 1.0, 0.0)
    row_t = lax.broadcasted_iota(jnp.int32, (TM, TM), 0)
    col_t = lax.broadcasted_iota(jnp.int32, (TM, TM), 1)
    earlier = jnp.where(row_t < col_t, 1.0, 0.0).astype(BF16)
    seen = _dot(picked.astype(BF16), earlier) + cnt_ref[:, 0:1]
    cnt_ref[...] = cnt_ref[...] + jnp.sum(picked, axis=1, keepdims=True)
    cnt_out_ref[...] = cnt_ref[...]
    ranks = [jnp.sum(jnp.where(hot[k], seen, 0.0), axis=0, keepdims=True) for k in range(TOP_K)]
    route_ref[...] = jnp.concatenate([i.astype(F32) for i in idxs] + [e / tot for e in es] + ranks +
                                     [jnp.zeros((ROUTE_ROWS - 3 * TOP_K, TM), F32)], axis=0)


def _mixer(x, mod8, norm_mix_g, w_in, lb_params, hgrn_norm_g, gmlp_ln_g, gmlp_ln_b, gmlp_ws, gmlp_bs,
           gmlp_norm_g, w_out, norm_ffn_g, router_w, router_b):
    bsz, seq, d = x.shape
    n_in = w_in.shape[1]
    rw = router_w.T
    rwh = rw.astype(BF16)
    rwl = (rw - rwh.astype(F32)).astype(BF16)
    rb = router_b.reshape(N_EXPERTS, 1)
    const = lambda *shape: pl.BlockSpec(shape, lambda b, i: (0,) * len(shape))
    tile = lambda w: pl.BlockSpec((None, TM, w), lambda b, i: (b, i, 0))
    per_seq = seq // TM
    return pl.pallas_call(
        _mixer_kernel,
        grid=(bsz, seq // TM),
        in_specs=[tile(d),
                  pl.BlockSpec((None, 8, d), lambda b, i: (b, 0, 0)),
                  const(1, d), const(d, n_in), const(2, HGRN_WIDTH), const(1, HGRN_WIDTH),
                  const(1, GMLP_WIDTH), const(1, GMLP_WIDTH),
                  const(GMLP_GROUPS, GMLP_CHUNK, GMLP_CHUNK), const(GMLP_GROUPS, GMLP_CHUNK, 1),
                  const(1, GMLP_WIDTH), const(d, d), const(1, d),
                  const(N_EXPERTS, d), const(N_EXPERTS, d), const(N_EXPERTS, 1)],
        out_specs=[tile(d), tile(d),
                   pl.BlockSpec((None, ROUTE_ROWS, TM), lambda b, i: (b * per_seq + i, 0, 0)),
                   const(N_EXPERTS, LANES)],
        out_shape=[jax.ShapeDtypeStruct((bsz, seq, d), F32),
                   jax.ShapeDtypeStruct((bsz, seq, d), F32),
                   jax.ShapeDtypeStruct((bsz * per_seq, ROUTE_ROWS, TM), F32),
                   jax.ShapeDtypeStruct((N_EXPERTS, LANES), F32)],
        scratch_shapes=[pltpu.VMEM((TM, n_in), F32),
                        pltpu.VMEM((TM, d), BF16),
                        pltpu.VMEM((HGRN_HEADS, HEAD_DIM, HEAD_DIM), F32),
                        pltpu.VMEM((N_EXPERTS, LANES), F32)],
        compiler_params=pltpu.CompilerParams(dimension_semantics=("arbitrary", "arbitrary"),
                                             vmem_limit_bytes=VMEM_LIMIT),
        name="mixer",
    )(x, mod8, norm_mix_g.reshape(1, d), w_in.astype(BF16), lb_params, hgrn_norm_g.reshape(1, -1),
      gmlp_ln_g.reshape(1, -1), gmlp_ln_b.reshape(1, -1), gmlp_ws, gmlp_bs[:, :, None],
      gmlp_norm_g.reshape(1, -1), w_out.astype(BF16), norm_ffn_g.reshape(1, d), rwh, rwl, rb)


def _tile(r):
    return pl.ds(r * SUBLANES, SUBLANES)


def _tile8(r8):
    return pl.ds(pl.multiple_of(r8, SUBLANES), SUBLANES)


def _chunk(j, n):
    return pl.ds(j, n, stride=SUBLANES)


def _dispatch_kernel(n_pad_step, dest_ref, pad_ref, h2_ref, xs_hbm, rows, zero_tile, sem):
    i = pl.program_id(0)
    n_steps = pl.num_programs(0)

    def wait_step(s):
        for _ in range(TOP_K):
            pltpu.make_async_copy(rows.at[s], xs_hbm.at[pl.ds(0, TD * SUBLANES)], sem.at[s]).wait()
        n = n_pad_step * SUBLANES
        pltpu.make_async_copy(rows.at[s, pl.ds(0, n)], xs_hbm.at[pl.ds(0, n)], sem.at[s]).wait()

    @pl.when(i == 0)
    def _():
        zero_tile[...] = jnp.zeros_like(zero_tile)

    def step(s):
        @pl.when(i >= 2)
        def _():
            wait_step(s)

        for j in range(SUBLANES):
            rows[s, _chunk(j, TD), :] = h2_ref[:, j * LANES:(j + 1) * LANES]
        for t in range(TD):
            for k in range(TOP_K):
                pltpu.make_async_copy(rows.at[s, _tile(t)], xs_hbm.at[_tile8(dest_ref[0, k, t])],
                                      sem.at[s]).start(priority=k % N_DMA_QUEUES)
        for q in range(n_pad_step):
            pltpu.make_async_copy(zero_tile, xs_hbm.at[_tile8(pad_ref[0, 0, q])],
                                  sem.at[s]).start(priority=q % N_DMA_QUEUES)

        @pl.when(i == n_steps - 1)
        def _():
            wait_step(s)

            @pl.when(i >= 1)
            def _():
                wait_step(1 - s)

    _by_parity(i, step)


def _dest_block(step_of):
    per_tile = TM // TD
    return pl.BlockSpec((1, TOP_K, TD), lambda i: (step_of(i) // per_tile, 0, step_of(i) % per_tile),
                        memory_space=pltpu.SMEM)


def _dispatch(h2, dest, pad_dest, p_rows):
    n_tok, d = h2.shape
    n_steps = n_tok // TD
    n_pad_step = pad_dest.shape[0] // n_steps
    return pl.pallas_call(
        functools.partial(_dispatch_kernel, n_pad_step),
        grid=(n_steps,),
        in_specs=[_dest_block(lambda i: i),
                  pl.BlockSpec((1, 1, n_pad_step), lambda i: (i, 0, 0), memory_space=pltpu.SMEM),
                  pl.BlockSpec((TD, d), lambda i: (i, 0))],
        out_specs=pl.BlockSpec(memory_space=pl.ANY),
        out_shape=jax.ShapeDtypeStruct((p_rows * SUBLANES, LANES), F32),
        scratch_shapes=[pltpu.VMEM((2, TD * SUBLANES, LANES), F32),
                        pltpu.VMEM((SUBLANES, LANES), F32),
                        pltpu.SemaphoreType.DMA((2,))],
        compiler_params=pltpu.CompilerParams(dimension_semantics=("arbitrary",)),
        name="dispatch",
    )(dest, pad_dest.reshape(n_steps, 1, n_pad_step), h2)


def _moe_kernel(n_blocks, blk0_ref, nblk_ref, xs_hbm, wgu_hbm, bgu_ref, wd_hbm, bd_ref, ys_hbm,
                xbuf, ybuf, wgu_f, wd_f, wgu_bf, wd_bf, isem, osem, wsem):
    e = pl.program_id(0)
    n_exp = pl.num_programs(0)
    nb = nblk_ref[e]
    b0 = blk0_ref[e]
    d, d_ff = wd_f.shape[2], wd_f.shape[1]
    blk_rows = MOE_BLK * SUBLANES
    ws = e % 2
    gu_rows, d_rows = d // W_CHUNKS, d_ff // W_CHUNKS

    npairs = nb // 2
    odd = nb - 2 * npairs

    def rows_of(b, n):
        return pl.ds(pl.multiple_of((b0 + b) * blk_rows, blk_rows), n * blk_rows)

    def x_copy(b, n, s):
        return pltpu.make_async_copy(xs_hbm.at[rows_of(b, n)], xbuf.at[s, pl.ds(0, n * blk_rows)], isem.at[s])

    def y_copy(b, n, s):
        return pltpu.make_async_copy(ybuf.at[s, pl.ds(0, n * blk_rows)], ys_hbm.at[rows_of(b, n)], osem.at[s])

    def w_start(ex, slot, c):
        r_gu = pl.ds(pl.multiple_of(c * gu_rows, gu_rows), gu_rows)
        r_d = pl.ds(pl.multiple_of(c * d_rows, d_rows), d_rows)
        pltpu.make_async_copy(wgu_hbm.at[ex, r_gu], wgu_f.at[slot, r_gu], wsem.at[slot]).start()
        pltpu.make_async_copy(wd_hbm.at[ex, r_d], wd_f.at[slot, r_d], wsem.at[slot]).start()

    def w_wait(slot):
        pltpu.make_async_copy(wgu_hbm.at[0], wgu_f.at[slot], wsem.at[slot]).wait()
        pltpu.make_async_copy(wd_hbm.at[0], wd_f.at[slot], wsem.at[slot]).wait()

    @pl.when(e == 0)
    def _():
        for c in range(W_CHUNKS):
            w_start(0, 0, c)

    @pl.when(npairs > 0)
    def _():
        x_copy(0, 2, 0).start(priority=BLOCK_DMA_PRIORITY)

    @pl.when((npairs == 0) & (odd == 1))
    def _():
        x_copy(0, 1, 0).start(priority=BLOCK_DMA_PRIORITY)

    w_wait(ws)
    wgu_bf[...] = wgu_f[ws].astype(BF16)
    wd_bf[...] = wd_f[ws].astype(BF16)
    has_next = e + 1 < n_exp

    def ffn(n, s):
        rows = n * MOE_BLK
        xb = jnp.concatenate([xbuf[s, _chunk(j, rows), :] for j in range(SUBLANES)], axis=-1).astype(BF16)
        gu = _dot(xb, wgu_bf[...]) + bgu_ref[...]
        gate = jnp.minimum(gu[:, :d_ff], SWIGLU_LIMIT)
        up = jnp.clip(gu[:, d_ff:], -SWIGLU_LIMIT, SWIGLU_LIMIT)
        glu = gate * jax.nn.sigmoid(SWIGLU_ALPHA * gate)
        yb = _dot(((up + 1.0) * glu).astype(BF16), wd_bf[...]) + bd_ref[...]
        for j in range(SUBLANES):
            ybuf[s, _chunk(j, rows), :] = yb[:, j * LANES:(j + 1) * LANES]

    def pair_body(p, carry):
        s = p % 2

        @pl.when(p + 1 < npairs)
        def _():
            x_copy(2 * p + 2, 2, 1 - s).start(priority=BLOCK_DMA_PRIORITY)

        @pl.when((p + 1 == npairs) & (odd == 1))
        def _():
            x_copy(2 * p + 2, 1, 1 - s).start(priority=BLOCK_DMA_PRIORITY)

        @pl.when(has_next & (2 * p < W_CHUNKS))
        def _():
            w_start(e + 1, 1 - ws, 2 * p)
            w_start(e + 1, 1 - ws, 2 * p + 1)

        x_copy(2 * p, 2, s).wait()

        @pl.when(p >= 2)
        def _():
            y_copy(2 * p - 4, 2, s).wait()

        ffn(2, s)
        y_copy(2 * p, 2, s).start(priority=BLOCK_DMA_PRIORITY)
        return carry

    lax.fori_loop(0, npairs, pair_body, 0)

    @pl.when(odd == 1)
    def _():
        s = npairs % 2
        x_copy(2 * npairs, 1, s).wait()

        @pl.when(npairs >= 2)
        def _():
            y_copy(2 * npairs - 4, 2, s).wait()

        ffn(1, s)
        cp = y_copy(2 * npairs, 1, s)
        cp.start(priority=BLOCK_DMA_PRIORITY)
        cp.wait()

    @pl.when(has_next)
    def _():
        def rest(c, carry):
            w_start(e + 1, 1 - ws, c)
            return carry

        lax.fori_loop(jnp.minimum(2 * npairs, W_CHUNKS), W_CHUNKS, rest, 0)

    @pl.when(npairs >= 1)
    def _():
        y_copy(2 * npairs - 2, 2, (npairs - 1) % 2).wait()

    @pl.when((npairs >= 2) & (odd == 0))
    def _():
        y_copy(2 * npairs - 4, 2, npairs % 2).wait()

    @pl.when(e == pl.num_programs(0) - 1)
    def _():
        ybuf[0, pl.ds(0, blk_rows), :] = jnp.zeros((blk_rows, LANES), F32)

        def fill(b, carry):
            cp = y_copy(b, 1, 0)
            cp.start()
            cp.wait()
            return carry

        lax.fori_loop(nb, n_blocks - b0, fill, 0)


def _moe(xs, blk0, nblk, w_gate_up, b_gate_up, w_down, b_down):
    n_blocks = xs.shape[0] // (MOE_BLK * SUBLANES)
    n_exp, d_ff, d = w_down.shape
    grid_spec = pltpu.PrefetchScalarGridSpec(
        num_scalar_prefetch=2,
        grid=(n_exp,),
        in_specs=[pl.BlockSpec(memory_space=pl.ANY),
                  pl.BlockSpec(memory_space=pl.ANY),
                  pl.BlockSpec((None, 1, 2 * d_ff), lambda e, b0, nb: (e, 0, 0)),
                  pl.BlockSpec(memory_space=pl.ANY),
                  pl.BlockSpec((None, 1, d), lambda e, b0, nb: (e, 0, 0))],
        out_specs=pl.BlockSpec(memory_space=pl.ANY),
        scratch_shapes=[pltpu.VMEM((2, 2 * MOE_BLK * SUBLANES, LANES), F32),
                        pltpu.VMEM((2, 2 * MOE_BLK * SUBLANES, LANES), F32),
                        pltpu.VMEM((2, d, 2 * d_ff), F32),
                        pltpu.VMEM((2, d_ff, d), F32),
                        pltpu.VMEM((d, 2 * d_ff), BF16),
                        pltpu.VMEM((d_ff, d), BF16),
                        pltpu.SemaphoreType.DMA((2,)),
                        pltpu.SemaphoreType.DMA((2,)),
                        pltpu.SemaphoreType.DMA((2,))],
    )
    return pl.pallas_call(
        functools.partial(_moe_kernel, n_blocks),
        grid_spec=grid_spec,
        out_shape=jax.ShapeDtypeStruct(xs.shape, F32),
        compiler_params=pltpu.CompilerParams(dimension_semantics=("arbitrary",),
                                             vmem_limit_bytes=VMEM_LIMIT),
        name="moe_ffn",
    )(blk0, nblk, xs, w_gate_up, b_gate_up[:, None, :], w_down, b_down[:, None, :])


def _combine_kernel(destc_ref, destn_ref, x1_ref, mod_ref, route_ref, fg_ref, ys_hbm, o_ref, buf0, buf1, sem):
    buf = (buf0, buf1)
    i = pl.program_id(0)
    n_steps = pl.num_programs(0)
    d = x1_ref.shape[-1]

    def start_gather(dest_ref, s, t0=0, t1=TD):
        for t in range(t0, t1):
            for k in range(TOP_K):
                pltpu.make_async_copy(ys_hbm.at[_tile8(dest_ref[0, k, t])], buf[s].at[k, _tile(t)],
                                      sem.at[s]).start(priority=k % N_DMA_QUEUES)

    def wait_gather(s):
        for k in range(TOP_K):
            pltpu.make_async_copy(ys_hbm.at[pl.ds(0, TD * SUBLANES)], buf[s].at[k], sem.at[s]).wait()

    @pl.when(i == 0)
    def _():
        start_gather(destc_ref, 0)

    def step(s):
        wait_gather(s)
        gates = route_ref[...]
        gk = [gates[:, TOP_K + k:TOP_K + k + 1] for k in range(TOP_K)]
        gate2 = mod_ref[5:6, :]

        def residual(j):
            ls = slice(j * LANES, (j + 1) * LANES)
            rj = _chunk(j, TD)
            y = (gk[0] * buf[s][0, rj, :] + gk[1] * buf[s][1, rj, :]) + \
                (gk[2] * buf[s][2, rj, :] + gk[3] * buf[s][3, rj, :])
            return x1_ref[:, ls] + gate2[:, ls] * y

        ss = jnp.zeros((TD, 1), F32)
        per = TD // SUBLANES
        for j in range(SUBLANES):
            start_gather(destn_ref, 1 - s, j * per, (j + 1) * per)
            xj = residual(j)
            ss = ss + jnp.sum(xj * xj, axis=-1, keepdims=True)
        inv = lax.rsqrt(ss / d + EPS)
        for j in range(SUBLANES):
            ls = slice(j * LANES, (j + 1) * LANES)
            o_ref[:, ls] = residual(j) * inv * fg_ref[:, ls]

        @pl.when(i == n_steps - 1)
        def _():
            wait_gather(1 - s)

    _by_parity(i, step)


def _combine(x1, mod8, route, dest, ys, final_g):
    bsz, seq, d = x1.shape
    n_tok = bsz * seq
    n_steps = n_tok // TD
    per_seq = seq // TD
    return pl.pallas_call(
        _combine_kernel,
        grid=(n_steps,),
        in_specs=[_dest_block(lambda i: i),
                  _dest_block(lambda i: jnp.minimum(i + 1, n_steps - 1)),
                  pl.BlockSpec((TD, d), lambda i: (i, 0)),
                  pl.BlockSpec((None, 8, d), lambda i: (i // per_seq, 0, 0)),
                  pl.BlockSpec((TD, ROUTE_ROWS), lambda i: (i, 0)),
                  pl.BlockSpec((1, d), lambda i: (0, 0)),
                  pl.BlockSpec(memory_space=pl.ANY)],
        out_specs=pl.BlockSpec((TD, d), lambda i: (i, 0)),
        out_shape=jax.ShapeDtypeStruct((n_tok, d), F32),
        scratch_shapes=[pltpu.VMEM((TOP_K, TD * SUBLANES, LANES), F32),
                        pltpu.VMEM((TOP_K, TD * SUBLANES, LANES), F32),
                        pltpu.SemaphoreType.DMA((2,))],
        compiler_params=pltpu.CompilerParams(dimension_semantics=("arbitrary",),
                                             vmem_limit_bytes=VMEM_LIMIT),
        name="combine",
    )(dest, dest, x1.reshape(n_tok, d), mod8, route, final_g.reshape(1, d), ys).reshape(bsz, seq, d)


def _dest_kernel(route_ref, start_ref, dest_ref):
    erow = lax.broadcasted_iota(jnp.int32, (N_EXPERTS, TM), 0).astype(F32)
    for i in range(route_ref.shape[0]):
        r = route_ref[i]
        rows = []
        for k in range(TOP_K):
            base = jnp.sum(jnp.where(erow == r[k:k + 1, :], start_ref[...], 0.0), axis=0, keepdims=True)
            rows.append((base + r[2 * TOP_K + k:2 * TOP_K + k + 1, :]) * float(SUBLANES))
        dest_ref[i] = jnp.concatenate(rows, axis=0).astype(jnp.int32)


def _destinations(route_t, start_pad):
    n_tiles = route_t.shape[0]
    per_step = 8 if n_tiles % 8 == 0 else 1
    return pl.pallas_call(
        _dest_kernel,
        grid=(n_tiles // per_step,),
        in_specs=[pl.BlockSpec((per_step, ROUTE_ROWS, TM), lambda i: (i, 0, 0)),
                  pl.BlockSpec((N_EXPERTS, 1), lambda i: (0, 0))],
        out_specs=pl.BlockSpec((per_step, TOP_K, TM), lambda i: (i, 0, 0)),
        out_shape=jax.ShapeDtypeStruct((n_tiles, TOP_K, TM), jnp.int32),
        name="destinations",
    )(route_t, start_pad.astype(F32).reshape(N_EXPERTS, 1))


def _routing(counts_f, n_tok):
    nk = n_tok * TOP_K
    counts = counts_f[:, 0].astype(jnp.int32)
    padded = ((counts + MOE_BLK - 1) // MOE_BLK) * MOE_BLK
    pad_end = jnp.cumsum(padded)
    start_pad = pad_end - padded
    p_rows = ((nk + N_EXPERTS * (MOE_BLK - 1) + MOE_BLK - 1) // MOE_BLK) * MOE_BLK
    n_pad = padded - counts
    pad_cum = jnp.cumsum(n_pad)
    j = jnp.arange(p_rows - nk, dtype=jnp.int32)
    first_pad = pad_cum - n_pad
    mine = (j[:, None] >= first_pad[None, :]) & (j[:, None] < pad_cum[None, :])
    in_tail = jnp.sum(jnp.where(mine, (start_pad + counts - first_pad)[None, :], 0), axis=1) + j
    pad_dest = jnp.where(j < pad_cum[-1], in_tail, pad_end[-1] + (j - pad_cum[-1])).astype(jnp.int32)
    return (start_pad, pad_dest * SUBLANES, (start_pad // MOE_BLK).astype(jnp.int32),
            (padded // MOE_BLK).astype(jnp.int32), p_rows)


def kernel(x, c, ada_w, ada_b, norm_mix_g, w_in, lb_params, hgrn_norm_g, gmlp_ln_g, gmlp_ln_b, gmlp_ws, gmlp_bs,
           gmlp_norm_g, w_out, norm_ffn_g, router_w, router_b, w_gate_up, b_gate_up, w_down, b_down, final_g):
    assert ada_w.shape[0] == 1, "single-layer block"
    bsz, seq, d = x.shape
    assert d == SUBLANES * LANES and seq % TM == 0 and (bsz * seq) % TD == 0
    n_tok = bsz * seq
    mod = _modulation(c, ada_w[0], ada_b[0])
    mod8 = jnp.zeros((bsz, 8, d), F32).at[:, :6].set(mod.reshape(bsz, 6, d))
    x1, h2, route, counts = _mixer(x, mod8, norm_mix_g[0], w_in[0], lb_params, hgrn_norm_g[0], gmlp_ln_g[0],
                                   gmlp_ln_b[0], gmlp_ws[0], gmlp_bs[0], gmlp_norm_g[0], w_out[0],
                                   norm_ffn_g[0], router_w[0], router_b[0])
    start_pad, pad_dest, blk0, nblk, p_rows = _routing(counts, n_tok)
    dest = _destinations(route, start_pad)
    route = route.transpose(0, 2, 1).reshape(n_tok, ROUTE_ROWS)
    xs = _dispatch(h2.reshape(n_tok, d), dest, pad_dest, p_rows)
    ys = _moe(xs, blk0, nblk, w_gate_up[0], b_gate_up[0], w_down[0], b_down[0])
    return _combine(x1, mod8, route, dest, ys, final_g)
```

```python
import functools

import jax
import jax.numpy as jnp
from jax import lax
from jax.experimental import pallas as pl
from jax.experimental.pallas import tpu as pltpu

F32 = jnp.float32
BF16 = jnp.bfloat16

HGRN_HEADS = 4
HEAD_DIM = 128
HGRN_WIDTH = HGRN_HEADS * HEAD_DIM
HGRN_CHUNK = 64
HGRN_SUB = 16
GMLP_GROUPS = 4
GROUP_DIM = 128
GMLP_WIDTH = GMLP_GROUPS * GROUP_DIM
GMLP_CHUNK = 128
N_EXPERTS = 32
TOP_K = 4
SWIGLU_LIMIT = 7.0
SWIGLU_ALPHA = 1.702
EPS = 1e-6
LANES = 128
SUBLANES = 8
N_DMA_QUEUES = 2
BLOCK_DMA_PRIORITY = 1
ROUTE_ROWS = 16
W_CHUNKS = 8
DECAY_EXP_CLAMP = 60.0
TM = 512
MOE_BLK = 256
TD = 256
VMEM_LIMIT = 56 * 1024 * 1024


def _dot(a, b):
    return jnp.dot(a, b, preferred_element_type=F32)


def _dot_nt(a, b):
    return lax.dot_general(a, b, (((1,), (1,)), ((), ())), preferred_element_type=F32)


def _dot_tn(a, b):
    return lax.dot_general(a, b, (((0,), (0,)), ((), ())), preferred_element_type=F32)


def _rms(x):
    return x * lax.rsqrt(jnp.mean(x * x, axis=-1, keepdims=True) + EPS)


def _gelu(x):
    return 0.5 * x * (1.0 + lax.erf(x * 0.7071067811865476))


def _by_parity(i, fn):
    @pl.when(i % 2 == 0)
    def _():
        fn(0)

    @pl.when(i % 2 == 1)
    def _():
        fn(1)


def _mod_kernel(c_ref, w_ref, b_ref, o_ref):
    c = c_ref[...]
    ca = c * jax.nn.sigmoid(c)
    o_ref[...] = jnp.dot(ca, w_ref[...], precision=lax.Precision.HIGHEST,
                         preferred_element_type=F32) + b_ref[...]


def _modulation(c, ada_w, ada_b):
    bsz, d = c.shape
    n_out = ada_w.shape[1]
    rows = 8
    c_pad = jnp.zeros((rows, d), F32).at[:bsz].set(c)
    tn = 2048
    out = pl.pallas_call(
        _mod_kernel,
        grid=(n_out // tn,),
        in_specs=[pl.BlockSpec((rows, d), lambda j: (0, 0)),
                  pl.BlockSpec((d, tn), lambda j: (0, j)),
                  pl.BlockSpec((1, tn), lambda j: (0, j))],
        out_specs=pl.BlockSpec((rows, tn), lambda j: (0, j)),
        out_shape=jax.ShapeDtypeStruct((rows, n_out), F32),
        name="adaln_mod",
    )(c_pad, ada_w, ada_b.reshape(1, n_out))
    return out[:bsz]


def _mixer_kernel(x_ref, mod_ref, g1_ref, win_ref, lbp_ref, hg_ref, lng_ref, lnb_ref, ws_ref, bs_ref,
                  gng_ref, wout_ref, g2_ref, rwh_ref, rb_ref, earlier_ref,
                  x1_ref, h2_ref, route_ref, cnt_out_ref,
                  z_ref, y_ref, st_ref, cnt_ref):
    @pl.when(pl.program_id(1) == 0)
    def _():
        st_ref[...] = jnp.zeros_like(st_ref)

    x = x_ref[...]
    mod = mod_ref[...]
    h = _rms(x) * g1_ref[...]
    h = h * (1.0 + mod[1:2]) + mod[0:1]
    z_ref[...] = _dot(h.astype(BF16), win_ref[...])

    lbp = lbp_ref[...]
    lbe = jnp.exp(lbp - jnp.max(lbp, axis=0, keepdims=True))
    lb = lbe[0:1] / jnp.sum(lbe, axis=0, keepdims=True)
    hg = hg_ref[...]
    row = lax.broadcasted_iota(jnp.int32, (HGRN_CHUNK, HGRN_CHUNK), 0)
    col = lax.broadcasted_iota(jnp.int32, (HGRN_CHUNK, HGRN_CHUNK), 1)
    causal = col <= row
    crow = lax.broadcasted_iota(jnp.int32, (HGRN_CHUNK, HGRN_WIDTH), 0)
    n_sub = HGRN_CHUNK // HGRN_SUB

    def chunk_scores(c):
        rows = slice(c * HGRN_CHUNK, (c + 1) * HGRN_CHUNK)
        zq = z_ref[rows, 0:HGRN_WIDTH]
        zf = z_ref[rows, HGRN_WIDTH:2 * HGRN_WIDTH]
        zi = z_ref[rows, 2 * HGRN_WIDTH:3 * HGRN_WIDTH]
        zg = z_ref[rows, 3 * HGRN_WIDTH:4 * HGRN_WIDTH]
        f = lb + (1.0 - lb) * jax.nn.sigmoid(zf)
        logf = jnp.log(f)
        kk = 1.0 - f
        b = logf
        for sh in (1, 2, 4, 8, 16, 32):
            b = b + jnp.where(crow >= sh, pltpu.roll(b, sh, axis=0), 0.0)
        b_last = b[HGRN_CHUNK - 1:HGRN_CHUNK, :]
        qe = (zq * jnp.exp(b)).astype(BF16)
        kdec = (kk * jnp.exp(b_last - b)).astype(BF16)
        v = zi.astype(BF16)
        dec_last = jnp.exp(b_last)
        a_sub, k_sub = [], []
        for i in range(n_sub):
            lo, hi = i * HGRN_SUB, (i + 1) * HGRN_SUB
            bref = b[lo - 1:lo, :] if i > 0 else jnp.zeros((1, HGRN_WIDTH), F32)
            a_sub.append((zq[lo:hi] * jnp.exp(b[lo:hi] - bref)).astype(BF16))
            k_sub.append((kk * jnp.exp(jnp.minimum(bref - b, DECAY_EXP_CLAMP))).astype(BF16))
        sc = []
        for hd in range(HGRN_HEADS):
            ls = slice(hd * HEAD_DIM, (hd + 1) * HEAD_DIM)
            s_h = jnp.concatenate([_dot_nt(a_sub[i][:, ls], k_sub[i][:, ls]) for i in range(n_sub)], axis=0)
            sc.append(jnp.where(causal, s_h, 0.0).astype(BF16))
        return qe, kdec, v, dec_last, sc, zg * jax.nn.sigmoid(zg)

    def chunk_output(c, parts):
        rows = slice(c * HGRN_CHUNK, (c + 1) * HGRN_CHUNK)
        qe, kdec, v, dec_last, sc, silu_g = parts
        for hd in range(HGRN_HEADS):
            ls = slice(hd * HEAD_DIM, (hd + 1) * HEAD_DIM)
            st = st_ref[hd]
            o = _dot_nt(qe[:, ls], st.astype(BF16)) + _dot(sc[hd], v[:, ls])
            st_ref[hd] = st * dec_last[:, ls] + _dot_tn(v[:, ls], kdec[:, ls])
            o = _rms(o) * hg[:, ls]
            y_ref[rows, ls] = (o * silu_g[:, ls]).astype(BF16)

    row_g = lax.broadcasted_iota(jnp.int32, (GMLP_CHUNK, GMLP_CHUNK), 0)
    col_g = lax.broadcasted_iota(jnp.int32, (GMLP_CHUNK, GMLP_CHUNK), 1)
    gng = gng_ref[...]
    ws_c = [jnp.where(col_g <= row_g, ws_ref[g], 0.0).astype(BF16) for g in range(GMLP_GROUPS)]

    def gmlp_block(n):
        rs = slice(n * GMLP_CHUNK, (n + 1) * GMLP_CHUNK)
        u = _gelu(z_ref[rs, 4 * HGRN_WIDTH:4 * HGRN_WIDTH + GMLP_WIDTH])
        gv = _gelu(z_ref[rs, 4 * HGRN_WIDTH + GMLP_WIDTH:])
        mu = jnp.mean(gv, axis=-1, keepdims=True)
        gc = gv - mu
        var = jnp.mean(gc * gc, axis=-1, keepdims=True)
        vn = (gc * lax.rsqrt(var + EPS) * lng_ref[...] + lnb_ref[...]).astype(BF16)
        for g in range(GMLP_GROUPS):
            ls = slice(g * GROUP_DIM, (g + 1) * GROUP_DIM)
            sv = _dot(ws_c[g], vn[:, ls]) + bs_ref[g]
            yy = _rms(u[:, ls] * sv) * gng[:, ls]
            y_ref[rs, HGRN_WIDTH + g * GROUP_DIM:HGRN_WIDTH + (g + 1) * GROUP_DIM] = yy.astype(BF16)

    n_chunks = TM // HGRN_CHUNK
    per_blk = GMLP_CHUNK // HGRN_CHUNK
    parts = [chunk_scores(c) for c in range(n_chunks)]
    for c in range(n_chunks):
        chunk_output(c, parts[c])
        if c % per_blk == per_blk - 1:
            gmlp_block(c // per_blk)

    x1 = x + mod[2:3] * _dot(y_ref[...], wout_ref[...])
    x1_ref[...] = x1
    h2 = _rms(x1) * g2_ref[...]
    h2 = h2 * (1.0 + mod[4:5]) + mod[3:4]
    h2_ref[...] = h2

    hh = h2.astype(BF16)
    hl = (h2 - hh.astype(F32)).astype(BF16)
    rw2 = rwh_ref[...]
    by_hh = _dot_nt(rw2, hh)
    logits = by_hh[:N_EXPERTS] + (_dot_nt(rw2[:N_EXPERTS], hl) + by_hh[N_EXPERTS:]) + rb_ref[...]
    erow = lax.broadcasted_iota(jnp.int32, (N_EXPERTS, TM), 0)
    vals, idxs = [], []
    for _ in range(TOP_K):
        m = jnp.max(logits, axis=0, keepdims=True)
        idx = jnp.min(jnp.where(logits == m, erow, N_EXPERTS), axis=0, keepdims=True)
        vals.append(m)
        idxs.append(idx)
        logits = jnp.where(erow == idx, -jnp.inf, logits)
    es = [jnp.exp(v - vals[0]) for v in vals]
    tot = (es[0] + es[1]) + (es[2] + es[3])

    @pl.when((pl.program_id(0) == 0) & (pl.program_id(1) == 0))
    def _():
        cnt_ref[...] = jnp.zeros_like(cnt_ref)

    hot = [erow == idxs[k] for k in range(TOP_K)]
    picked = jnp.where((hot[0] | hot[1]) | (hot[2] | hot[3]), 1.0, 0.0)
    seen = _dot(picked.astype(BF16), earlier_ref[...]) + cnt_ref[:, 0:1]
    cnt_ref[...] = cnt_ref[...] + jnp.sum(picked, axis=1, keepdims=True)
    cnt_out_ref[...] = cnt_ref[...]
    ranks = [jnp.sum(jnp.where(hot[k], seen, 0.0), axis=0, keepdims=True) for k in range(TOP_K)]
    route_ref[...] = jnp.concatenate([i.astype(F32) for i in idxs] + [e / tot for e in es] + ranks +
                                     [jnp.zeros((ROUTE_ROWS - 3 * TOP_K, TM), F32)], axis=0)


def _mixer(x, mod8, norm_mix_g, w_in, lb_params, hgrn_norm_g, gmlp_ln_g, gmlp_ln_b, gmlp_ws, gmlp_bs,
           gmlp_norm_g, w_out, norm_ffn_g, router_w, router_b):
    bsz, seq, d = x.shape
    n_in = w_in.shape[1]
    rw = router_w.T
    rwh = rw.astype(BF16)
    rw2 = jnp.concatenate([rwh, (rw - rwh.astype(F32)).astype(BF16)], axis=0)
    rb = router_b.reshape(N_EXPERTS, 1)
    pos = jnp.arange(TM, dtype=jnp.int32)
    earlier = (pos[:, None] < pos[None, :]).astype(BF16)
    const = lambda *shape: pl.BlockSpec(shape, lambda b, i: (0,) * len(shape))
    tile = lambda w: pl.BlockSpec((None, TM, w), lambda b, i: (b, i, 0))
    per_seq = seq // TM
    return pl.pallas_call(
        _mixer_kernel,
        grid=(bsz, seq // TM),
        in_specs=[tile(d),
                  pl.BlockSpec((None, 8, d), lambda b, i: (b, 0, 0)),
                  const(1, d), const(d, n_in), const(2, HGRN_WIDTH), const(1, HGRN_WIDTH),
                  const(1, GMLP_WIDTH), const(1, GMLP_WIDTH),
                  const(GMLP_GROUPS, GMLP_CHUNK, GMLP_CHUNK), const(GMLP_GROUPS, GMLP_CHUNK, 1),
                  const(1, GMLP_WIDTH), const(d, d), const(1, d),
                  const(2 * N_EXPERTS, d), const(N_EXPERTS, 1), const(TM, TM)],
        out_specs=[tile(d), tile(d),
                   pl.BlockSpec((None, ROUTE_ROWS, TM), lambda b, i: (b * per_seq + i, 0, 0)),
                   const(N_EXPERTS, LANES)],
        out_shape=[jax.ShapeDtypeStruct((bsz, seq, d), F32),
                   jax.ShapeDtypeStruct((bsz, seq, d), F32),
                   jax.ShapeDtypeStruct((bsz * per_seq, ROUTE_ROWS, TM), F32),
                   jax.ShapeDtypeStruct((N_EXPERTS, LANES), F32)],
        scratch_shapes=[pltpu.VMEM((TM, n_in), F32),
                        pltpu.VMEM((TM, d), BF16),
                        pltpu.VMEM((HGRN_HEADS, HEAD_DIM, HEAD_DIM), F32),
                        pltpu.VMEM((N_EXPERTS, LANES), F32)],
        compiler_params=pltpu.CompilerParams(dimension_semantics=("arbitrary", "arbitrary"),
                                             vmem_limit_bytes=VMEM_LIMIT),
        name="mixer",
    )(x, mod8, norm_mix_g.reshape(1, d), w_in.astype(BF16), lb_params, hgrn_norm_g.reshape(1, -1),
      gmlp_ln_g.reshape(1, -1), gmlp_ln_b.reshape(1, -1), gmlp_ws, gmlp_bs[:, :, None],
      gmlp_norm_g.reshape(1, -1), w_out.astype(BF16), norm_ffn_g.reshape(1, d), rw2, rb, earlier)


def _tile(r):
    return pl.ds(r * SUBLANES, SUBLANES)


def _tile8(r8):
    return pl.ds(pl.multiple_of(r8, SUBLANES), SUBLANES)


def _chunk(j, n):
    return pl.ds(j, n, stride=SUBLANES)


def _dispatch_kernel(n_pad_step, dest_ref, pad_ref, h2_ref, xs_hbm, rows, zero_tile, sem):
    i = pl.program_id(0)
    n_steps = pl.num_programs(0)

    def wait_step(s):
        for _ in range(TOP_K):
            pltpu.make_async_copy(rows.at[s], xs_hbm.at[pl.ds(0, TD * SUBLANES)], sem.at[s]).wait()
        n = n_pad_step * SUBLANES
        pltpu.make_async_copy(rows.at[s, pl.ds(0, n)], xs_hbm.at[pl.ds(0, n)], sem.at[s]).wait()

    @pl.when(i == 0)
    def _():
        zero_tile[...] = jnp.zeros_like(zero_tile)

    def step(s):
        @pl.when(i >= 2)
        def _():
            wait_step(s)

        for j in range(SUBLANES):
            rows[s, _chunk(j, TD), :] = h2_ref[:, j * LANES:(j + 1) * LANES]
        for t in range(TD):
            for k in range(TOP_K):
                pltpu.make_async_copy(rows.at[s, _tile(t)], xs_hbm.at[_tile8(dest_ref[0, k, t])],
                                      sem.at[s]).start(priority=k % N_DMA_QUEUES)
        for q in range(n_pad_step):
            pltpu.make_async_copy(zero_tile, xs_hbm.at[_tile8(pad_ref[0, 0, q])],
                                  sem.at[s]).start(priority=q % N_DMA_QUEUES)

        @pl.when(i == n_steps - 1)
        def _():
            wait_step(s)

            @pl.when(i >= 1)
            def _():
                wait_step(1 - s)

    _by_parity(i, step)


def _dest_block(step_of):
    per_tile = TM // TD
    return pl.BlockSpec((1, TOP_K, TD), lambda i: (step_of(i) // per_tile, 0, step_of(i) % per_tile),
                        memory_space=pltpu.SMEM)


def _dispatch(h2, dest, pad_dest, p_rows):
    n_tok, d = h2.shape
    n_steps = n_tok // TD
    n_pad_step = pad_dest.shape[0] // n_steps
    return pl.pallas_call(
        functools.partial(_dispatch_kernel, n_pad_step),
        grid=(n_steps,),
        in_specs=[_dest_block(lambda i: i),
                  pl.BlockSpec((1, 1, n_pad_step), lambda i: (i, 0, 0), memory_space=pltpu.SMEM),
                  pl.BlockSpec((TD, d), lambda i: (i, 0))],
        out_specs=pl.BlockSpec(memory_space=pl.ANY),
        out_shape=jax.ShapeDtypeStruct((p_rows * SUBLANES, LANES), F32),
        scratch_shapes=[pltpu.VMEM((2, TD * SUBLANES, LANES), F32),
                        pltpu.VMEM((SUBLANES, LANES), F32),
                        pltpu.SemaphoreType.DMA((2,))],
        compiler_params=pltpu.CompilerParams(dimension_semantics=("arbitrary",)),
        name="dispatch",
    )(dest, pad_dest.reshape(n_steps, 1, n_pad_step), h2)


def _moe_kernel(n_blocks, blk0_ref, nblk_ref, xs_hbm, wgu_hbm, bgu_ref, wd_hbm, bd_ref, ys_hbm,
                xbuf, ybuf, wgu_f, wd_f, wgu_bf, wd_bf, isem, osem, wsem):
    e = pl.program_id(0)
    n_exp = pl.num_programs(0)
    nb = nblk_ref[e]
    b0 = blk0_ref[e]
    d, d_ff = wd_f.shape[2], wd_f.shape[1]
    blk_rows = MOE_BLK * SUBLANES
    ws = e % 2
    gu_rows, d_rows = d // W_CHUNKS, d_ff // W_CHUNKS

    npairs = nb // 2
    odd = nb - 2 * npairs

    def rows_of(b, n):
        return pl.ds(pl.multiple_of((b0 + b) * blk_rows, blk_rows), n * blk_rows)

    def x_copy(b, n, s):
        return pltpu.make_async_copy(xs_hbm.at[rows_of(b, n)], xbuf.at[s, pl.ds(0, n * blk_rows)], isem.at[s])

    def y_copy(b, n, s):
        return pltpu.make_async_copy(ybuf.at[s, pl.ds(0, n * blk_rows)], ys_hbm.at[rows_of(b, n)], osem.at[s])

    def w_start(ex, slot, c):
        r_gu = pl.ds(pl.multiple_of(c * gu_rows, gu_rows), gu_rows)
        r_d = pl.ds(pl.multiple_of(c * d_rows, d_rows), d_rows)
        pltpu.make_async_copy(wgu_hbm.at[ex, r_gu], wgu_f.at[slot, r_gu], wsem.at[slot]).start()
        pltpu.make_async_copy(wd_hbm.at[ex, r_d], wd_f.at[slot, r_d], wsem.at[slot]).start()

    def w_wait(slot):
        pltpu.make_async_copy(wgu_hbm.at[0], wgu_f.at[slot], wsem.at[slot]).wait()
        pltpu.make_async_copy(wd_hbm.at[0], wd_f.at[slot], wsem.at[slot]).wait()

    @pl.when(e == 0)
    def _():
        for c in range(W_CHUNKS):
            w_start(0, 0, c)

    @pl.when(npairs > 0)
    def _():
        x_copy(0, 2, 0).start(priority=BLOCK_DMA_PRIORITY)

    @pl.when((npairs == 0) & (odd == 1))
    def _():
        x_copy(0, 1, 0).start(priority=BLOCK_DMA_PRIORITY)

    w_wait(ws)
    wgu_bf[...] = wgu_f[ws].astype(BF16)
    wd_bf[...] = wd_f[ws].astype(BF16)
    has_next = e + 1 < n_exp

    def ffn(n, s):
        rows = n * MOE_BLK
        xb = jnp.concatenate([xbuf[s, _chunk(j, rows), :] for j in range(SUBLANES)], axis=-1).astype(BF16)
        gu = _dot(xb, wgu_bf[...]) + bgu_ref[...]
        gate = jnp.minimum(gu[:, :d_ff], SWIGLU_LIMIT)
        up = jnp.clip(gu[:, d_ff:], -SWIGLU_LIMIT, SWIGLU_LIMIT)
        glu = gate * jax.nn.sigmoid(SWIGLU_ALPHA * gate)
        yb = _dot(((up + 1.0) * glu).astype(BF16), wd_bf[...]) + bd_ref[...]
        for j in range(SUBLANES):
            ybuf[s, _chunk(j, rows), :] = yb[:, j * LANES:(j + 1) * LANES]

    def pair_body(p, carry):
        s = p % 2

        @pl.when(p + 1 < npairs)
        def _():
            x_copy(2 * p + 2, 2, 1 - s).start(priority=BLOCK_DMA_PRIORITY)

        @pl.when((p + 1 == npairs) & (odd == 1))
        def _():
            x_copy(2 * p + 2, 1, 1 - s).start(priority=BLOCK_DMA_PRIORITY)

        @pl.when(has_next & (2 * p < W_CHUNKS))
        def _():
            w_start(e + 1, 1 - ws, 2 * p)
            w_start(e + 1, 1 - ws, 2 * p + 1)

        x_copy(2 * p, 2, s).wait()

        @pl.when(p >= 2)
        def _():
            y_copy(2 * p - 4, 2, s).wait()

        ffn(2, s)
        y_copy(2 * p, 2, s).start(priority=BLOCK_DMA_PRIORITY)
        return carry

    lax.fori_loop(0, npairs, pair_body, 0)

    @pl.when(odd == 1)
    def _():
        s = npairs % 2
        x_copy(2 * npairs, 1, s).wait()

        @pl.when(npairs >= 2)
        def _():
            y_copy(2 * npairs - 4, 2, s).wait()

        ffn(1, s)
        cp = y_copy(2 * npairs, 1, s)
        cp.start(priority=BLOCK_DMA_PRIORITY)
        cp.wait()

    @pl.when(has_next)
    def _():
        def rest(c, carry):
            w_start(e + 1, 1 - ws, c)
            return carry

        lax.fori_loop(jnp.minimum(2 * npairs, W_CHUNKS), W_CHUNKS, rest, 0)

    @pl.when(npairs >= 1)
    def _():
        y_copy(2 * npairs - 2, 2, (npairs - 1) % 2).wait()

    @pl.when((npairs >= 2) & (odd == 0))
    def _():
        y_copy(2 * npairs - 4, 2, npairs % 2).wait()

    @pl.when(e == pl.num_programs(0) - 1)
    def _():
        ybuf[0, pl.ds(0, blk_rows), :] = jnp.zeros((blk_rows, LANES), F32)

        def fill(b, carry):
            cp = y_copy(b, 1, 0)
            cp.start()
            cp.wait()
            return carry

        lax.fori_loop(nb, n_blocks - b0, fill, 0)


def _moe(xs, blk0, nblk, w_gate_up, b_gate_up, w_down, b_down):
    n_blocks = xs.shape[0] // (MOE_BLK * SUBLANES)
    n_exp, d_ff, d = w_down.shape
    grid_spec = pltpu.PrefetchScalarGridSpec(
        num_scalar_prefetch=2,
        grid=(n_exp,),
        in_specs=[pl.BlockSpec(memory_space=pl.ANY),
                  pl.BlockSpec(memory_space=pl.ANY),
                  pl.BlockSpec((None, 1, 2 * d_ff), lambda e, b0, nb: (e, 0, 0)),
                  pl.BlockSpec(memory_space=pl.ANY),
                  pl.BlockSpec((None, 1, d), lambda e, b0, nb: (e, 0, 0))],
        out_specs=pl.BlockSpec(memory_space=pl.ANY),
        scratch_shapes=[pltpu.VMEM((2, 2 * MOE_BLK * SUBLANES, LANES), F32),
                        pltpu.VMEM((2, 2 * MOE_BLK * SUBLANES, LANES), F32),
                        pltpu.VMEM((2, d, 2 * d_ff), F32),
                        pltpu.VMEM((2, d_ff, d), F32),
                        pltpu.VMEM((d, 2 * d_ff), BF16),
                        pltpu.VMEM((d_ff, d), BF16),
                        pltpu.SemaphoreType.DMA((2,)),
                        pltpu.SemaphoreType.DMA((2,)),
                        pltpu.SemaphoreType.DMA((2,))],
    )
    return pl.pallas_call(
        functools.partial(_moe_kernel, n_blocks),
        grid_spec=grid_spec,
        out_shape=jax.ShapeDtypeStruct(xs.shape, F32),
        compiler_params=pltpu.CompilerParams(dimension_semantics=("arbitrary",),
                                             vmem_limit_bytes=VMEM_LIMIT),
        name="moe_ffn",
    )(blk0, nblk, xs, w_gate_up, b_gate_up[:, None, :], w_down, b_down[:, None, :])


def _combine_kernel(destc_ref, destn_ref, x1_ref, mod_ref, route_ref, fg_ref, ys_hbm, o_ref, buf0, buf1, sem):
    buf = (buf0, buf1)
    i = pl.program_id(0)
    n_steps = pl.num_programs(0)
    d = x1_ref.shape[-1]

    def start_gather(dest_ref, s, t0=0, t1=TD):
        for t in range(t0, t1):
            for k in range(TOP_K):
                pltpu.make_async_copy(ys_hbm.at[_tile8(dest_ref[0, k, t])], buf[s].at[k, _tile(t)],
                                      sem.at[s]).start(priority=k % N_DMA_QUEUES)

    def wait_gather(s):
        for k in range(TOP_K):
            pltpu.make_async_copy(ys_hbm.at[pl.ds(0, TD * SUBLANES)], buf[s].at[k], sem.at[s]).wait()

    @pl.when(i == 0)
    def _():
        start_gather(destc_ref, 0)

    def step(s):
        wait_gather(s)
        gates = route_ref[...]
        gk = [gates[:, TOP_K + k:TOP_K + k + 1] for k in range(TOP_K)]
        gate2 = mod_ref[5:6, :]

        def residual(j):
            ls = slice(j * LANES, (j + 1) * LANES)
            rj = _chunk(j, TD)
            y = (gk[0] * buf[s][0, rj, :] + gk[1] * buf[s][1, rj, :]) + \
                (gk[2] * buf[s][2, rj, :] + gk[3] * buf[s][3, rj, :])
            return x1_ref[:, ls] + gate2[:, ls] * y

        ss = jnp.zeros((TD, 1), F32)
        per = TD // SUBLANES
        for j in range(SUBLANES):
            start_gather(destn_ref, 1 - s, j * per, (j + 1) * per)
            xj = residual(j)
            ss = ss + jnp.sum(xj * xj, axis=-1, keepdims=True)
        inv = lax.rsqrt(ss / d + EPS)
        for j in range(SUBLANES):
            ls = slice(j * LANES, (j + 1) * LANES)
            o_ref[:, ls] = residual(j) * inv * fg_ref[:, ls]

        @pl.when(i == n_steps - 1)
        def _():
            wait_gather(1 - s)

    _by_parity(i, step)


def _combine(x1, mod8, route, dest, ys, final_g):
    bsz, seq, d = x1.shape
    n_tok = bsz * seq
    n_steps = n_tok // TD
    per_seq = seq // TD
    return pl.pallas_call(
        _combine_kernel,
        grid=(n_steps,),
        in_specs=[_dest_block(lambda i: i),
                  _dest_block(lambda i: jnp.minimum(i + 1, n_steps - 1)),
                  pl.BlockSpec((TD, d), lambda i: (i, 0)),
                  pl.BlockSpec((None, 8, d), lambda i: (i // per_seq, 0, 0)),
                  pl.BlockSpec((TD, ROUTE_ROWS), lambda i: (i, 0)),
                  pl.BlockSpec((1, d), lambda i: (0, 0)),
                  pl.BlockSpec(memory_space=pl.ANY)],
        out_specs=pl.BlockSpec((TD, d), lambda i: (i, 0)),
        out_shape=jax.ShapeDtypeStruct((n_tok, d), F32),
        scratch_shapes=[pltpu.VMEM((TOP_K, TD * SUBLANES, LANES), F32),
                        pltpu.VMEM((TOP_K, TD * SUBLANES, LANES), F32),
                        pltpu.SemaphoreType.DMA((2,))],
        compiler_params=pltpu.CompilerParams(dimension_semantics=("arbitrary",),
                                             vmem_limit_bytes=VMEM_LIMIT),
        name="combine",
    )(dest, dest, x1.reshape(n_tok, d), mod8, route, final_g.reshape(1, d), ys).reshape(bsz, seq, d)


def _dest_kernel(route_ref, start_ref, dest_ref):
    erow = lax.broadcasted_iota(jnp.int32, (N_EXPERTS, TM), 0).astype(F32)
    for i in range(route_ref.shape[0]):
        r = route_ref[i]
        rows = []
        for k in range(TOP_K):
            base = jnp.sum(jnp.where(erow == r[k:k + 1, :], start_ref[...], 0.0), axis=0, keepdims=True)
            rows.append((base + r[2 * TOP_K + k:2 * TOP_K + k + 1, :]) * float(SUBLANES))
        dest_ref[i] = jnp.concatenate(rows, axis=0).astype(jnp.int32)


def _destinations(route_t, start_pad):
    n_tiles = route_t.shape[0]
    per_step = 8 if n_tiles % 8 == 0 else 1
    return pl.pallas_call(
        _dest_kernel,
        grid=(n_tiles // per_step,),
        in_specs=[pl.BlockSpec((per_step, ROUTE_ROWS, TM), lambda i: (i, 0, 0)),
                  pl.BlockSpec((N_EXPERTS, 1), lambda i: (0, 0))],
        out_specs=pl.BlockSpec((per_step, TOP_K, TM), lambda i: (i, 0, 0)),
        out_shape=jax.ShapeDtypeStruct((n_tiles, TOP_K, TM), jnp.int32),
        name="destinations",
    )(route_t, start_pad.astype(F32).reshape(N_EXPERTS, 1))


def _routing(counts_f, n_tok):
    nk = n_tok * TOP_K
    counts = counts_f[:, 0].astype(jnp.int32)
    padded = ((counts + MOE_BLK - 1) // MOE_BLK) * MOE_BLK
    pad_end = jnp.cumsum(padded)
    start_pad = pad_end - padded
    p_rows = ((nk + N_EXPERTS * (MOE_BLK - 1) + MOE_BLK - 1) // MOE_BLK) * MOE_BLK
    n_pad = padded - counts
    pad_cum = jnp.cumsum(n_pad)
    j = jnp.arange(p_rows - nk, dtype=jnp.int32)
    first_pad = pad_cum - n_pad
    mine = (j[:, None] >= first_pad[None, :]) & (j[:, None] < pad_cum[None, :])
    in_tail = jnp.sum(jnp.where(mine, (start_pad + counts - first_pad)[None, :], 0), axis=1) + j
    pad_dest = jnp.where(j < pad_cum[-1], in_tail, pad_end[-1] + (j - pad_cum[-1])).astype(jnp.int32)
    return (start_pad, pad_dest * SUBLANES, (start_pad // MOE_BLK).astype(jnp.int32),
            (padded // MOE_BLK).astype(jnp.int32), p_rows)


def kernel(x, c, ada_w, ada_b, norm_mix_g, w_in, lb_params, hgrn_norm_g, gmlp_ln_g, gmlp_ln_b, gmlp_ws, gmlp_bs,
           gmlp_norm_g, w_out, norm_ffn_g, router_w, router_b, w_gate_up, b_gate_up, w_down, b_down, final_g):
    assert ada_w.shape[0] == 1, "single-layer block"
    bsz, seq, d = x.shape
    assert d == SUBLANES * LANES and seq % TM == 0 and (bsz * seq) % TD == 0
    n_tok = bsz * seq
    mod = _modulation(c, ada_w[0], ada_b[0])
    mod8 = jnp.zeros((bsz, 8, d), F32).at[:, :6].set(mod.reshape(bsz, 6, d))
    x1, h2, route, counts = _mixer(x, mod8, norm_mix_g[0], w_in[0], lb_params, hgrn_norm_g[0], gmlp_ln_g[0],
                                   gmlp_ln_b[0], gmlp_ws[0], gmlp_bs[0], gmlp_norm_g[0], w_out[0],
                                   norm_ffn_g[0], router_w[0], router_b[0])
    start_pad, pad_dest, blk0, nblk, p_rows = _routing(counts, n_tok)
    dest = _destinations(route, start_pad)
    route = route.transpose(0, 2, 1).reshape(n_tok, ROUTE_ROWS)
    xs = _dispatch(h2.reshape(n_tok, d), dest, pad_dest, p_rows)
    ys = _moe(xs, blk0, nblk, w_gate_up[0], b_gate_up[0], w_down[0], b_down[0])
    return _combine(x1, mod8, route, dest, ys, final_g)
```

```python
import functools

import jax
import jax.numpy as jnp
from jax import lax
from jax.experimental import pallas as pl
from jax.experimental.pallas import tpu as pltpu

F32 = jnp.float32
BF16 = jnp.bfloat16

HGRN_HEADS = 4
HEAD_DIM = 128
HGRN_WIDTH = HGRN_HEADS * HEAD_DIM
HGRN_CHUNK = 64
HGRN_SUB = 16
GMLP_GROUPS = 4
GROUP_DIM = 128
GMLP_WIDTH = GMLP_GROUPS * GROUP_DIM
GMLP_CHUNK = 128
N_EXPERTS = 32
TOP_K = 4
SWIGLU_LIMIT = 7.0
SWIGLU_ALPHA = 1.702
EPS = 1e-6
LANES = 128
SUBLANES = 8
N_DMA_QUEUES = 2
BLOCK_DMA_PRIORITY = 1
ROUTE_ROWS = 16
W_CHUNKS = 8
DECAY_EXP_CLAMP = 60.0
TM = 512
MOE_BLK = 256
TD = 256
VMEM_LIMIT = 56 * 1024 * 1024


def _dot(a, b):
    return jnp.dot(a, b, preferred_element_type=F32)


def _dot_nt(a, b):
    return lax.dot_general(a, b, (((1,), (1,)), ((), ())), preferred_element_type=F32)


def _dot_tn(a, b):
    return lax.dot_general(a, b, (((0,), (0,)), ((), ())), preferred_element_type=F32)


def _rms(x):
    return x * lax.rsqrt(jnp.mean(x * x, axis=-1, keepdims=True) + EPS)


def _gelu(x):
    return 0.5 * x * (1.0 + lax.erf(x * 0.7071067811865476))


def _by_parity(i, fn):
    @pl.when(i % 2 == 0)
    def _():
        fn(0)

    @pl.when(i % 2 == 1)
    def _():
        fn(1)


def _mod_kernel(c_ref, w_ref, b_ref, o_ref):
    c = c_ref[...]
    ca = c * jax.nn.sigmoid(c)
    o_ref[...] = jnp.dot(ca, w_ref[...], precision=lax.Precision.HIGHEST,
                         preferred_element_type=F32) + b_ref[...]


def _modulation(c, ada_w, ada_b):
    bsz, d = c.shape
    n_out = ada_w.shape[1]
    rows = 8
    c_pad = jnp.zeros((rows, d), F32).at[:bsz].set(c)
    tn = 2048
    out = pl.pallas_call(
        _mod_kernel,
        grid=(n_out // tn,),
        in_specs=[pl.BlockSpec((rows, d), lambda j: (0, 0)),
                  pl.BlockSpec((d, tn), lambda j: (0, j)),
                  pl.BlockSpec((1, tn), lambda j: (0, j))],
        out_specs=pl.BlockSpec((rows, tn), lambda j: (0, j)),
        out_shape=jax.ShapeDtypeStruct((rows, n_out), F32),
        name="adaln_mod",
    )(c_pad, ada_w, ada_b.reshape(1, n_out))
    return out[:bsz]


def _mixer_kernel(x_ref, mod_ref, g1_ref, win_ref, lbp_ref, hg_ref, lng_ref, lnb_ref, ws_ref, bs_ref,
                  gng_ref, wout_ref, g2_ref, rwh_ref, rb_ref, earlier_ref,
                  x1_ref, h2_ref, route_ref, cnt_out_ref,
                  z_ref, y_ref, st_ref, cnt_ref):
    @pl.when(pl.program_id(1) == 0)
    def _():
        st_ref[...] = jnp.zeros_like(st_ref)

    x = x_ref[...]
    mod = mod_ref[...]
    h = _rms(x) * g1_ref[...]
    h = h * (1.0 + mod[1:2]) + mod[0:1]
    z_ref[...] = _dot(h.astype(BF16), win_ref[...])

    lbp = lbp_ref[...]
    lbe = jnp.exp(lbp - jnp.max(lbp, axis=0, keepdims=True))
    lb = lbe[0:1] / jnp.sum(lbe, axis=0, keepdims=True)
    hg = hg_ref[...]
    row = lax.broadcasted_iota(jnp.int32, (HGRN_CHUNK, HGRN_CHUNK), 0)
    col = lax.broadcasted_iota(jnp.int32, (HGRN_CHUNK, HGRN_CHUNK), 1)
    causal = col <= row
    crow = lax.broadcasted_iota(jnp.int32, (HGRN_CHUNK, HGRN_WIDTH), 0)
    n_sub = HGRN_CHUNK // HGRN_SUB

    def chunk_scores(c):
        rows = slice(c * HGRN_CHUNK, (c + 1) * HGRN_CHUNK)
        zq = z_ref[rows, 0:HGRN_WIDTH]
        zf = z_ref[rows, HGRN_WIDTH:2 * HGRN_WIDTH]
        zi = z_ref[rows, 2 * HGRN_WIDTH:3 * HGRN_WIDTH]
        zg = z_ref[rows, 3 * HGRN_WIDTH:4 * HGRN_WIDTH]
        f = lb + (1.0 - lb) * jax.nn.sigmoid(zf)
        logf = jnp.log(f)
        kk = 1.0 - f
        b = logf
        for sh in (1, 2, 4, 8, 16, 32):
            b = b + jnp.where(crow >= sh, pltpu.roll(b, sh, axis=0), 0.0)
        b_last = b[HGRN_CHUNK - 1:HGRN_CHUNK, :]
        qe = (zq * jnp.exp(b)).astype(BF16)
        kdec = (kk * jnp.exp(b_last - b)).astype(BF16)
        v = zi.astype(BF16)
        dec_last = jnp.exp(b_last)
        a_sub, k_sub = [], []
        for i in range(n_sub):
            lo, hi = i * HGRN_SUB, (i + 1) * HGRN_SUB
            bref = b[lo - 1:lo, :] if i > 0 else jnp.zeros((1, HGRN_WIDTH), F32)
            a_sub.append((zq[lo:hi] * jnp.exp(b[lo:hi] - bref)).astype(BF16))
            k_sub.append((kk * jnp.exp(jnp.minimum(bref - b, DECAY_EXP_CLAMP))).astype(BF16))
        sc = []
        for hd in range(HGRN_HEADS):
            ls = slice(hd * HEAD_DIM, (hd + 1) * HEAD_DIM)
            s_h = jnp.concatenate([_dot_nt(a_sub[i][:, ls], k_sub[i][:, ls]) for i in range(n_sub)], axis=0)
            sc.append(jnp.where(causal, s_h, 0.0).astype(BF16))
        return qe, kdec, v, dec_last, sc, zg * jax.nn.sigmoid(zg)

    def chunk_output(c, parts):
        rows = slice(c * HGRN_CHUNK, (c + 1) * HGRN_CHUNK)
        qe, kdec, v, dec_last, sc, silu_g = parts
        for hd in range(HGRN_HEADS):
            ls = slice(hd * HEAD_DIM, (hd + 1) * HEAD_DIM)
            st = st_ref[hd]
            o = _dot_nt(qe[:, ls], st.astype(BF16)) + _dot(sc[hd], v[:, ls])
            st_ref[hd] = st * dec_last[:, ls] + _dot_tn(v[:, ls], kdec[:, ls])
            o = _rms(o) * hg[:, ls]
            y_ref[rows, ls] = (o * silu_g[:, ls]).astype(BF16)

    row_g = lax.broadcasted_iota(jnp.int32, (GMLP_CHUNK, GMLP_CHUNK), 0)
    col_g = lax.broadcasted_iota(jnp.int32, (GMLP_CHUNK, GMLP_CHUNK), 1)
    gng = gng_ref[...]
    ws_c = [jnp.where(col_g <= row_g, ws_ref[g], 0.0).astype(BF16) for g in range(GMLP_GROUPS)]

    def gmlp_block(n):
        rs = slice(n * GMLP_CHUNK, (n + 1) * GMLP_CHUNK)
        u = _gelu(z_ref[rs, 4 * HGRN_WIDTH:4 * HGRN_WIDTH + GMLP_WIDTH])
        gv = _gelu(z_ref[rs, 4 * HGRN_WIDTH + GMLP_WIDTH:])
        mu = jnp.mean(gv, axis=-1, keepdims=True)
        gc = gv - mu
        var = jnp.mean(gc * gc, axis=-1, keepdims=True)
        vn = (gc * lax.rsqrt(var + EPS) * lng_ref[...] + lnb_ref[...]).astype(BF16)
        for g in range(GMLP_GROUPS):
            ls = slice(g * GROUP_DIM, (g + 1) * GROUP_DIM)
            sv = _dot(ws_c[g], vn[:, ls]) + bs_ref[g]
            yy = _rms(u[:, ls] * sv) * gng[:, ls]
            y_ref[rs, HGRN_WIDTH + g * GROUP_DIM:HGRN_WIDTH + (g + 1) * GROUP_DIM] = yy.astype(BF16)

    n_chunks = TM // HGRN_CHUNK
    per_blk = GMLP_CHUNK // HGRN_CHUNK
    parts = [chunk_scores(c) for c in range(n_chunks)]
    for c in range(n_chunks):
        chunk_output(c, parts[c])
        if c % per_blk == per_blk - 1:
            gmlp_block(c // per_blk)

    x1 = x + mod[2:3] * _dot(y_ref[...], wout_ref[...])
    x1_ref[...] = x1
    h2 = _rms(x1) * g2_ref[...]
    h2 = h2 * (1.0 + mod[4:5]) + mod[3:4]
    h2_ref[...] = h2

    hh = h2.astype(BF16)
    hl = (h2 - hh.astype(F32)).astype(BF16)
    rw2 = rwh_ref[...]
    by_hh = _dot_nt(rw2, hh)
    logits = by_hh[:N_EXPERTS] + (_dot_nt(rw2[:N_EXPERTS], hl) + by_hh[N_EXPERTS:]) + rb_ref[...]
    erow = lax.broadcasted_iota(jnp.int32, (N_EXPERTS, TM), 0)
    vals, idxs = [], []
    for _ in range(TOP_K):
        m = jnp.max(logits, axis=0, keepdims=True)
        idx = jnp.min(jnp.where(logits == m, erow, N_EXPERTS), axis=0, keepdims=True)
        vals.append(m)
        idxs.append(idx)
        logits = jnp.where(erow == idx, -jnp.inf, logits)
    es = [jnp.exp(v - vals[0]) for v in vals]
    tot = (es[0] + es[1]) + (es[2] + es[3])

    @pl.when((pl.program_id(0) == 0) & (pl.program_id(1) == 0))
    def _():
        cnt_ref[...] = jnp.zeros_like(cnt_ref)

    hot = [erow == idxs[k] for k in range(TOP_K)]
    picked = jnp.where((hot[0] | hot[1]) | (hot[2] | hot[3]), 1.0, 0.0)
    seen = _dot(picked.astype(BF16), earlier_ref[...]) + cnt_ref[:, 0:1]
    cnt_ref[...] = cnt_ref[...] + jnp.sum(picked, axis=1, keepdims=True)
    cnt_out_ref[...] = cnt_ref[...]
    ranks = [jnp.sum(jnp.where(hot[k], seen, 0.0), axis=0, keepdims=True) for k in range(TOP_K)]
    route_ref[...] = jnp.concatenate([i.astype(F32) for i in idxs] + [e / tot for e in es] + ranks +
                                     [jnp.zeros((ROUTE_ROWS - 3 * TOP_K, TM), F32)], axis=0)


def _mixer(x, mod8, norm_mix_g, w_in, lb_params, hgrn_norm_g, gmlp_ln_g, gmlp_ln_b, gmlp_ws, gmlp_bs,
           gmlp_norm_g, w_out, norm_ffn_g, router_w, router_b):
    bsz, seq, d = x.shape
    n_in = w_in.shape[1]
    rw = router_w.T
    rwh = rw.astype(BF16)
    rw2 = jnp.concatenate([rwh, (rw - rwh.astype(F32)).astype(BF16)], axis=0)
    rb = router_b.reshape(N_EXPERTS, 1)
    pos = jnp.arange(TM, dtype=jnp.int32)
    earlier = (pos[:, None] < pos[None, :]).astype(BF16)
    const = lambda *shape: pl.BlockSpec(shape, lambda b, i: (0,) * len(shape))
    tile = lambda w: pl.BlockSpec((None, TM, w), lambda b, i: (b, i, 0))
    per_seq = seq // TM
    return pl.pallas_call(
        _mixer_kernel,
        grid=(bsz, seq // TM),
        in_specs=[tile(d),
                  pl.BlockSpec((None, 8, d), lambda b, i: (b, 0, 0)),
                  const(1, d), const(d, n_in), const(2, HGRN_WIDTH), const(1, HGRN_WIDTH),
                  const(1, GMLP_WIDTH), const(1, GMLP_WIDTH),
                  const(GMLP_GROUPS, GMLP_CHUNK, GMLP_CHUNK), const(GMLP_GROUPS, GMLP_CHUNK, 1),
                  const(1, GMLP_WIDTH), const(d, d), const(1, d),
                  const(2 * N_EXPERTS, d), const(N_EXPERTS, 1), const(TM, TM)],
        out_specs=[tile(d), tile(d),
                   pl.BlockSpec((None, ROUTE_ROWS, TM), lambda b, i: (b * per_seq + i, 0, 0)),
                   const(N_EXPERTS, LANES)],
        out_shape=[jax.ShapeDtypeStruct((bsz, seq, d), F32),
                   jax.ShapeDtypeStruct((bsz, seq, d), F32),
                   jax.ShapeDtypeStruct((bsz * per_seq, ROUTE_ROWS, TM), F32),
                   jax.ShapeDtypeStruct((N_EXPERTS, LANES), F32)],
        scratch_shapes=[pltpu.VMEM((TM, n_in), F32),
                        pltpu.VMEM((TM, d), BF16),
                        pltpu.VMEM((HGRN_HEADS, HEAD_DIM, HEAD_DIM), F32),
                        pltpu.VMEM((N_EXPERTS, LANES), F32)],
        compiler_params=pltpu.CompilerParams(dimension_semantics=("arbitrary", "arbitrary"),
                                             vmem_limit_bytes=VMEM_LIMIT),
        name="mixer",
    )(x, mod8, norm_mix_g.reshape(1, d), w_in.astype(BF16), lb_params, hgrn_norm_g.reshape(1, -1),
      gmlp_ln_g.reshape(1, -1), gmlp_ln_b.reshape(1, -1), gmlp_ws, gmlp_bs[:, :, None],
      gmlp_norm_g.reshape(1, -1), w_out.astype(BF16), norm_ffn_g.reshape(1, d), rw2, rb, earlier)


def _tile(r):
    return pl.ds(r * SUBLANES, SUBLANES)


def _tile8(r8):
    return pl.ds(pl.multiple_of(r8, SUBLANES), SUBLANES)


def _chunk(j, n):
    return pl.ds(j, n, stride=SUBLANES)


def _dispatch_kernel(n_blocks, n_pad_rows, pad0_ref, padn_ref, used_ref, dest_ref, h2_ref, xs_hbm,
                     rows, zeros, sem, zsem):
    i = pl.program_id(0)
    n_steps = pl.num_programs(0)
    blk_rows = MOE_BLK * SUBLANES

    def wait_step(s):
        for _ in range(TOP_K):
            pltpu.make_async_copy(rows.at[s], xs_hbm.at[pl.ds(0, TD * SUBLANES)], sem.at[s]).wait()

    def zero_fill(first_row, n_rows):
        n = n_rows * SUBLANES
        pltpu.make_async_copy(zeros.at[pl.ds(0, n)], xs_hbm.at[pl.ds(pl.multiple_of(first_row * SUBLANES, SUBLANES), n)],
                              zsem.at[0]).start()

    @pl.when(i == 0)
    def _():
        zeros[...] = jnp.zeros_like(zeros)
        for e in range(N_EXPERTS):
            first, count = pad0_ref[e], padn_ref[e]
            for bit in reversed(range(MOE_BLK.bit_length() - 1)):
                @pl.when((count >> bit) & 1 == 1)
                def _():
                    zero_fill(first + ((count >> (bit + 1)) << (bit + 1)), 1 << bit)

        def fill_block(b, carry):
            zero_fill(b * MOE_BLK, MOE_BLK)
            return carry

        lax.fori_loop(used_ref[0], n_blocks, fill_block, 0)

    def step(s):
        @pl.when(i >= 2)
        def _():
            wait_step(s)

        for j in range(SUBLANES):
            rows[s, _chunk(j, TD), :] = h2_ref[:, j * LANES:(j + 1) * LANES]
        for t in range(TD):
            for k in range(TOP_K):
                pltpu.make_async_copy(rows.at[s, _tile(t)], xs_hbm.at[_tile8(dest_ref[0, k, t])],
                                      sem.at[s]).start(priority=k % N_DMA_QUEUES)

        @pl.when(i == n_steps - 1)
        def _():
            wait_step(s)

            @pl.when(i >= 1)
            def _():
                wait_step(1 - s)

            for _ in range(n_pad_rows // MOE_BLK):
                pltpu.make_async_copy(zeros, xs_hbm.at[pl.ds(0, blk_rows)], zsem.at[0]).wait()

    _by_parity(i, step)


def _dest_block(step_of):
    per_tile = TM // TD
    return pl.BlockSpec((1, TOP_K, TD), lambda i, *_: (step_of(i) // per_tile, 0, step_of(i) % per_tile),
                        memory_space=pltpu.SMEM)


def _dispatch(h2, dest, pad_first, pad_count, blocks_used, p_rows):
    n_tok, d = h2.shape
    n_steps = n_tok // TD
    n_pad_rows = p_rows - n_tok * TOP_K
    assert n_pad_rows % MOE_BLK == 0
    grid_spec = pltpu.PrefetchScalarGridSpec(
        num_scalar_prefetch=3,
        grid=(n_steps,),
        in_specs=[_dest_block(lambda i: i),
                  pl.BlockSpec((TD, d), lambda i, *_: (i, 0))],
        out_specs=pl.BlockSpec(memory_space=pl.ANY),
        scratch_shapes=[pltpu.VMEM((2, TD * SUBLANES, LANES), F32),
                        pltpu.VMEM((MOE_BLK * SUBLANES, LANES), F32),
                        pltpu.SemaphoreType.DMA((2,)),
                        pltpu.SemaphoreType.DMA((1,))],
    )
    return pl.pallas_call(
        functools.partial(_dispatch_kernel, p_rows // MOE_BLK, n_pad_rows),
        grid_spec=grid_spec,
        out_shape=jax.ShapeDtypeStruct((p_rows * SUBLANES, LANES), F32),
        compiler_params=pltpu.CompilerParams(dimension_semantics=("arbitrary",)),
        name="dispatch",
    )(pad_first, pad_count, blocks_used, dest, h2)


def _moe_kernel(n_blocks, blk0_ref, nblk_ref, xs_hbm, wgu_hbm, bgu_ref, wd_hbm, bd_ref, ys_hbm,
                xbuf, ybuf, wgu_f, wd_f, wgu_bf, wd_bf, isem, osem, wsem):
    e = pl.program_id(0)
    n_exp = pl.num_programs(0)
    nb = nblk_ref[e]
    b0 = blk0_ref[e]
    d, d_ff = wd_f.shape[2], wd_f.shape[1]
    blk_rows = MOE_BLK * SUBLANES
    ws = e % 2
    gu_rows, d_rows = d // W_CHUNKS, d_ff // W_CHUNKS

    npairs = nb // 2
    odd = nb - 2 * npairs

    def rows_of(b, n):
        return pl.ds(pl.multiple_of((b0 + b) * blk_rows, blk_rows), n * blk_rows)

    def x_copy(b, n, s):
        return pltpu.make_async_copy(xs_hbm.at[rows_of(b, n)], xbuf.at[s, pl.ds(0, n * blk_rows)], isem.at[s])

    def y_copy(b, n, s):
        return pltpu.make_async_copy(ybuf.at[s, pl.ds(0, n * blk_rows)], ys_hbm.at[rows_of(b, n)], osem.at[s])

    def w_start(ex, slot, c):
        r_gu = pl.ds(pl.multiple_of(c * gu_rows, gu_rows), gu_rows)
        r_d = pl.ds(pl.multiple_of(c * d_rows, d_rows), d_rows)
        pltpu.make_async_copy(wgu_hbm.at[ex, r_gu], wgu_f.at[slot, r_gu], wsem.at[slot]).start()
        pltpu.make_async_copy(wd_hbm.at[ex, r_d], wd_f.at[slot, r_d], wsem.at[slot]).start()

    def w_wait(slot):
        pltpu.make_async_copy(wgu_hbm.at[0], wgu_f.at[slot], wsem.at[slot]).wait()
        pltpu.make_async_copy(wd_hbm.at[0], wd_f.at[slot], wsem.at[slot]).wait()

    @pl.when(e == 0)
    def _():
        for c in range(W_CHUNKS):
            w_start(0, 0, c)

    @pl.when(npairs > 0)
    def _():
        x_copy(0, 2, 0).start(priority=BLOCK_DMA_PRIORITY)

    @pl.when((npairs == 0) & (odd == 1))
    def _():
        x_copy(0, 1, 0).start(priority=BLOCK_DMA_PRIORITY)

    w_wait(ws)
    wgu_bf[...] = wgu_f[ws].astype(BF16)
    wd_bf[...] = wd_f[ws].astype(BF16)
    has_next = e + 1 < n_exp

    def ffn(n, s):
        rows = n * MOE_BLK
        xb = jnp.concatenate([xbuf[s, _chunk(j, rows), :] for j in range(SUBLANES)], axis=-1).astype(BF16)
        gu = _dot(xb, wgu_bf[...]) + bgu_ref[...]
        gate = jnp.minimum(gu[:, :d_ff], SWIGLU_LIMIT)
        up = jnp.clip(gu[:, d_ff:], -SWIGLU_LIMIT, SWIGLU_LIMIT)
        glu = gate * jax.nn.sigmoid(SWIGLU_ALPHA * gate)
        yb = _dot(((up + 1.0) * glu).astype(BF16), wd_bf[...]) + bd_ref[...]
        for j in range(SUBLANES):
            ybuf[s, _chunk(j, rows), :] = yb[:, j * LANES:(j + 1) * LANES]

    def pair_body(p, carry):
        s = p % 2

        @pl.when(p + 1 < npairs)
        def _():
            x_copy(2 * p + 2, 2, 1 - s).start(priority=BLOCK_DMA_PRIORITY)

        @pl.when((p + 1 == npairs) & (odd == 1))
        def _():
            x_copy(2 * p + 2, 1, 1 - s).start(priority=BLOCK_DMA_PRIORITY)

        @pl.when(has_next & (2 * p < W_CHUNKS))
        def _():
            w_start(e + 1, 1 - ws, 2 * p)
            w_start(e + 1, 1 - ws, 2 * p + 1)

        x_copy(2 * p, 2, s).wait()

        @pl.when(p >= 2)
        def _():
            y_copy(2 * p - 4, 2, s).wait()

        ffn(2, s)
        y_copy(2 * p, 2, s).start(priority=BLOCK_DMA_PRIORITY)
        return carry

    lax.fori_loop(0, npairs, pair_body, 0)

    @pl.when(odd == 1)
    def _():
        s = npairs % 2
        x_copy(2 * npairs, 1, s).wait()

        @pl.when(npairs >= 2)
        def _():
            y_copy(2 * npairs - 4, 2, s).wait()

        ffn(1, s)
        cp = y_copy(2 * npairs, 1, s)
        cp.start(priority=BLOCK_DMA_PRIORITY)
        cp.wait()

    @pl.when(has_next)
    def _():
        def rest(c, carry):
            w_start(e + 1, 1 - ws, c)
            return carry

        lax.fori_loop(jnp.minimum(2 * npairs, W_CHUNKS), W_CHUNKS, rest, 0)

    @pl.when(npairs >= 1)
    def _():
        y_copy(2 * npairs - 2, 2, (npairs - 1) % 2).wait()

    @pl.when((npairs >= 2) & (odd == 0))
    def _():
        y_copy(2 * npairs - 4, 2, npairs % 2).wait()

    @pl.when(e == pl.num_programs(0) - 1)
    def _():
        ybuf[0, pl.ds(0, blk_rows), :] = jnp.zeros((blk_rows, LANES), F32)

        def fill(b, carry):
            cp = y_copy(b, 1, 0)
            cp.start()
            cp.wait()
            return carry

        lax.fori_loop(nb, n_blocks - b0, fill, 0)


def _moe(xs, blk0, nblk, w_gate_up, b_gate_up, w_down, b_down):
    n_blocks = xs.shape[0] // (MOE_BLK * SUBLANES)
    n_exp, d_ff, d = w_down.shape
    grid_spec = pltpu.PrefetchScalarGridSpec(
        num_scalar_prefetch=2,
        grid=(n_exp,),
        in_specs=[pl.BlockSpec(memory_space=pl.ANY),
                  pl.BlockSpec(memory_space=pl.ANY),
                  pl.BlockSpec((None, 1, 2 * d_ff), lambda e, b0, nb: (e, 0, 0)),
                  pl.BlockSpec(memory_space=pl.ANY),
                  pl.BlockSpec((None, 1, d), lambda e, b0, nb: (e, 0, 0))],
        out_specs=pl.BlockSpec(memory_space=pl.ANY),
        scratch_shapes=[pltpu.VMEM((2, 2 * MOE_BLK * SUBLANES, LANES), F32),
                        pltpu.VMEM((2, 2 * MOE_BLK * SUBLANES, LANES), F32),
                        pltpu.VMEM((2, d, 2 * d_ff), F32),
                        pltpu.VMEM((2, d_ff, d), F32),
                        pltpu.VMEM((d, 2 * d_ff), BF16),
                        pltpu.VMEM((d_ff, d), BF16),
                        pltpu.SemaphoreType.DMA((2,)),
                        pltpu.SemaphoreType.DMA((2,)),
                        pltpu.SemaphoreType.DMA((2,))],
    )
    return pl.pallas_call(
        functools.partial(_moe_kernel, n_blocks),
        grid_spec=grid_spec,
        out_shape=jax.ShapeDtypeStruct(xs.shape, F32),
        compiler_params=pltpu.CompilerParams(dimension_semantics=("arbitrary",),
                                             vmem_limit_bytes=VMEM_LIMIT),
        name="moe_ffn",
    )(blk0, nblk, xs, w_gate_up, b_gate_up[:, None, :], w_down, b_down[:, None, :])


def _combine_kernel(destc_ref, destn_ref, x1_ref, mod_ref, route_ref, fg_ref, ys_hbm, o_ref, buf0, buf1, sem):
    buf = (buf0, buf1)
    i = pl.program_id(0)
    n_steps = pl.num_programs(0)
    d = x1_ref.shape[-1]

    def start_gather(dest_ref, s, t0=0, t1=TD):
        for t in range(t0, t1):
            for k in range(TOP_K):
                pltpu.make_async_copy(ys_hbm.at[_tile8(dest_ref[0, k, t])], buf[s].at[k, _tile(t)],
                                      sem.at[s]).start(priority=k % N_DMA_QUEUES)

    def wait_gather(s):
        for k in range(TOP_K):
            pltpu.make_async_copy(ys_hbm.at[pl.ds(0, TD * SUBLANES)], buf[s].at[k], sem.at[s]).wait()

    @pl.when(i == 0)
    def _():
        start_gather(destc_ref, 0)

    def step(s):
        wait_gather(s)
        gates = route_ref[...]
        gk = [gates[:, TOP_K + k:TOP_K + k + 1] for k in range(TOP_K)]
        gate2 = mod_ref[5:6, :]

        def residual(j):
            ls = slice(j * LANES, (j + 1) * LANES)
            rj = _chunk(j, TD)
            y = (gk[0] * buf[s][0, rj, :] + gk[1] * buf[s][1, rj, :]) + \
                (gk[2] * buf[s][2, rj, :] + gk[3] * buf[s][3, rj, :])
            return x1_ref[:, ls] + gate2[:, ls] * y

        ss = jnp.zeros((TD, 1), F32)
        per = TD // SUBLANES
        for j in range(SUBLANES):
            start_gather(destn_ref, 1 - s, j * per, (j + 1) * per)
            xj = residual(j)
            ss = ss + jnp.sum(xj * xj, axis=-1, keepdims=True)
        inv = lax.rsqrt(ss / d + EPS)
        for j in range(SUBLANES):
            ls = slice(j * LANES, (j + 1) * LANES)
            o_ref[:, ls] = residual(j) * inv * fg_ref[:, ls]

        @pl.when(i == n_steps - 1)
        def _():
            wait_gather(1 - s)

    _by_parity(i, step)


def _combine(x1, mod8, route, dest, ys, final_g):
    bsz, seq, d = x1.shape
    n_tok = bsz * seq
    n_steps = n_tok // TD
    per_seq = seq // TD
    return pl.pallas_call(
        _combine_kernel,
        grid=(n_steps,),
        in_specs=[_dest_block(lambda i: i),
                  _dest_block(lambda i: jnp.minimum(i + 1, n_steps - 1)),
                  pl.BlockSpec((TD, d), lambda i: (i, 0)),
                  pl.BlockSpec((None, 8, d), lambda i: (i // per_seq, 0, 0)),
                  pl.BlockSpec((TD, ROUTE_ROWS), lambda i: (i, 0)),
                  pl.BlockSpec((1, d), lambda i: (0, 0)),
                  pl.BlockSpec(memory_space=pl.ANY)],
        out_specs=pl.BlockSpec((TD, d), lambda i: (i, 0)),
        out_shape=jax.ShapeDtypeStruct((n_tok, d), F32),
        scratch_shapes=[pltpu.VMEM((TOP_K, TD * SUBLANES, LANES), F32),
                        pltpu.VMEM((TOP_K, TD * SUBLANES, LANES), F32),
                        pltpu.SemaphoreType.DMA((2,))],
        compiler_params=pltpu.CompilerParams(dimension_semantics=("arbitrary",),
                                             vmem_limit_bytes=VMEM_LIMIT),
        name="combine",
    )(dest, dest, x1.reshape(n_tok, d), mod8, route, final_g.reshape(1, d), ys).reshape(bsz, seq, d)


def _dest_kernel(route_ref, start_ref, dest_ref):
    erow = lax.broadcasted_iota(jnp.int32, (N_EXPERTS, TM), 0).astype(F32)
    for i in range(route_ref.shape[0]):
        r = route_ref[i]
        rows = []
        for k in range(TOP_K):
            base = jnp.sum(jnp.where(erow == r[k:k + 1, :], start_ref[...], 0.0), axis=0, keepdims=True)
            rows.append((base + r[2 * TOP_K + k:2 * TOP_K + k + 1, :]) * float(SUBLANES))
        dest_ref[i] = jnp.concatenate(rows, axis=0).astype(jnp.int32)


def _destinations(route_t, start_pad):
    n_tiles = route_t.shape[0]
    per_step = 8 if n_tiles % 8 == 0 else 1
    return pl.pallas_call(
        _dest_kernel,
        grid=(n_tiles // per_step,),
        in_specs=[pl.BlockSpec((per_step, ROUTE_ROWS, TM), lambda i: (i, 0, 0)),
                  pl.BlockSpec((N_EXPERTS, 1), lambda i: (0, 0))],
        out_specs=pl.BlockSpec((per_step, TOP_K, TM), lambda i: (i, 0, 0)),
        out_shape=jax.ShapeDtypeStruct((n_tiles, TOP_K, TM), jnp.int32),
        name="destinations",
    )(route_t, start_pad.astype(F32).reshape(N_EXPERTS, 1))


def _routing(counts_f, n_tok):
    nk = n_tok * TOP_K
    counts = counts_f[:, 0].astype(jnp.int32)
    padded = ((counts + MOE_BLK - 1) // MOE_BLK) * MOE_BLK
    pad_end = jnp.cumsum(padded)
    start_pad = pad_end - padded
    p_rows = ((nk + N_EXPERTS * (MOE_BLK - 1) + MOE_BLK - 1) // MOE_BLK) * MOE_BLK
    blocks_used = (pad_end[-1:] // MOE_BLK).astype(jnp.int32)
    return (start_pad, start_pad + counts, padded - counts, blocks_used,
            (start_pad // MOE_BLK).astype(jnp.int32), (padded // MOE_BLK).astype(jnp.int32), p_rows)


def kernel(x, c, ada_w, ada_b, norm_mix_g, w_in, lb_params, hgrn_norm_g, gmlp_ln_g, gmlp_ln_b, gmlp_ws, gmlp_bs,
           gmlp_norm_g, w_out, norm_ffn_g, router_w, router_b, w_gate_up, b_gate_up, w_down, b_down, final_g):
    assert ada_w.shape[0] == 1, "single-layer block"
    bsz, seq, d = x.shape
    assert d == SUBLANES * LANES and seq % TM == 0 and (bsz * seq) % TD == 0
    n_tok = bsz * seq
    mod = _modulation(c, ada_w[0], ada_b[0])
    mod8 = jnp.zeros((bsz, 8, d), F32).at[:, :6].set(mod.reshape(bsz, 6, d))
    x1, h2, route, counts = _mixer(x, mod8, norm_mix_g[0], w_in[0], lb_params, hgrn_norm_g[0], gmlp_ln_g[0],
                                   gmlp_ln_b[0], gmlp_ws[0], gmlp_bs[0], gmlp_norm_g[0], w_out[0],
                                   norm_ffn_g[0], router_w[0], router_b[0])
    start_pad, pad_first, pad_count, blocks_used, blk0, nblk, p_rows = _routing(counts, n_tok)
    dest = _destinations(route, start_pad)
    route = route.transpose(0, 2, 1).reshape(n_tok, ROUTE_ROWS)
    xs = _dispatch(h2.reshape(n_tok, d), dest, pad_first, pad_count, blocks_used, p_rows)
    ys = _moe(xs, blk0, nblk, w_gate_up[0], b_gate_up[0], w_down[0], b_down[0])
    return _combine(x1, mod8, route, dest, ys, final_g)
```

```python
import functools

import jax
import jax.numpy as jnp
from jax import lax
from jax.experimental import pallas as pl
from jax.experimental.pallas import tpu as pltpu

F32 = jnp.float32
BF16 = jnp.bfloat16

HGRN_HEADS = 4
HEAD_DIM = 128
HGRN_WIDTH = HGRN_HEADS * HEAD_DIM
HGRN_CHUNK = 64
HGRN_SUB = 16
GMLP_GROUPS = 4
GROUP_DIM = 128
GMLP_WIDTH = GMLP_GROUPS * GROUP_DIM
GMLP_CHUNK = 128
N_EXPERTS = 32
TOP_K = 4
SWIGLU_LIMIT = 7.0
SWIGLU_ALPHA = 1.702
EPS = 1e-6
LANES = 128
SUBLANES = 8
N_DMA_QUEUES = 2
BLOCK_DMA_PRIORITY = 1
ROUTE_ROWS = 16
W_CHUNKS = 8
DECAY_EXP_CLAMP = 60.0
TM = 512
MOE_BLK = 256
TD = 512
VMEM_LIMIT = 56 * 1024 * 1024


def _dot(a, b):
    return jnp.dot(a, b, preferred_element_type=F32)


def _dot_nt(a, b):
    return lax.dot_general(a, b, (((1,), (1,)), ((), ())), preferred_element_type=F32)


def _dot_tn(a, b):
    return lax.dot_general(a, b, (((0,), (0,)), ((), ())), preferred_element_type=F32)


def _rms(x):
    return x * lax.rsqrt(jnp.mean(x * x, axis=-1, keepdims=True) + EPS)


def _gelu(x):
    return 0.5 * x * (1.0 + lax.erf(x * 0.7071067811865476))


def _by_parity(i, fn):
    @pl.when(i % 2 == 0)
    def _():
        fn(0)

    @pl.when(i % 2 == 1)
    def _():
        fn(1)


def _mod_kernel(c_ref, w_ref, b_ref, o_ref):
    c = c_ref[...]
    ca = c * jax.nn.sigmoid(c)
    o_ref[...] = jnp.dot(ca, w_ref[...], precision=lax.Precision.HIGHEST,
                         preferred_element_type=F32) + b_ref[...]


def _modulation(c, ada_w, ada_b):
    bsz, d = c.shape
    n_out = ada_w.shape[1]
    rows = 8
    c_pad = jnp.zeros((rows, d), F32).at[:bsz].set(c)
    tn = 2048
    out = pl.pallas_call(
        _mod_kernel,
        grid=(n_out // tn,),
        in_specs=[pl.BlockSpec((rows, d), lambda j: (0, 0)),
                  pl.BlockSpec((d, tn), lambda j: (0, j)),
                  pl.BlockSpec((1, tn), lambda j: (0, j))],
        out_specs=pl.BlockSpec((rows, tn), lambda j: (0, j)),
        out_shape=jax.ShapeDtypeStruct((rows, n_out), F32),
        name="adaln_mod",
    )(c_pad, ada_w, ada_b.reshape(1, n_out))
    return out[:bsz]


def _mixer_kernel(x_ref, mod_ref, g1_ref, win_ref, lbp_ref, hg_ref, lng_ref, lnb_ref, ws_ref, bs_ref,
                  gng_ref, wout_ref, g2_ref, rwh_ref, rb_ref, earlier_ref,
                  x1_ref, h2_ref, route_ref, cnt_out_ref,
                  z_ref, y_ref, st_ref, cnt_ref):
    @pl.when(pl.program_id(1) == 0)
    def _():
        st_ref[...] = jnp.zeros_like(st_ref)

    x = x_ref[...]
    mod = mod_ref[...]
    h = _rms(x) * g1_ref[...]
    h = h * (1.0 + mod[1:2]) + mod[0:1]
    z_ref[...] = _dot(h.astype(BF16), win_ref[...])

    lbp = lbp_ref[...]
    lbe = jnp.exp(lbp - jnp.max(lbp, axis=0, keepdims=True))
    lb = lbe[0:1] / jnp.sum(lbe, axis=0, keepdims=True)
    hg = hg_ref[...]
    row = lax.broadcasted_iota(jnp.int32, (HGRN_CHUNK, HGRN_CHUNK), 0)
    col = lax.broadcasted_iota(jnp.int32, (HGRN_CHUNK, HGRN_CHUNK), 1)
    causal = col <= row
    crow = lax.broadcasted_iota(jnp.int32, (HGRN_CHUNK, HGRN_WIDTH), 0)
    n_sub = HGRN_CHUNK // HGRN_SUB

    def chunk_scores(c):
        rows = slice(c * HGRN_CHUNK, (c + 1) * HGRN_CHUNK)
        zq = z_ref[rows, 0:HGRN_WIDTH]
        zf = z_ref[rows, HGRN_WIDTH:2 * HGRN_WIDTH]
        zi = z_ref[rows, 2 * HGRN_WIDTH:3 * HGRN_WIDTH]
        zg = z_ref[rows, 3 * HGRN_WIDTH:4 * HGRN_WIDTH]
        f = lb + (1.0 - lb) * jax.nn.sigmoid(zf)
        logf = jnp.log(f)
        kk = 1.0 - f
        b = logf
        for sh in (1, 2, 4, 8, 16, 32):
            b = b + jnp.where(crow >= sh, pltpu.roll(b, sh, axis=0), 0.0)
        b_last = b[HGRN_CHUNK - 1:HGRN_CHUNK, :]
        qe = (zq * jnp.exp(b)).astype(BF16)
        kdec = (kk * jnp.exp(b_last - b)).astype(BF16)
        v = zi.astype(BF16)
        dec_last = jnp.exp(b_last)
        a_sub, k_sub = [], []
        for i in range(n_sub):
            lo, hi = i * HGRN_SUB, (i + 1) * HGRN_SUB
            bref = b[lo - 1:lo, :] if i > 0 else jnp.zeros((1, HGRN_WIDTH), F32)
            a_sub.append((zq[lo:hi] * jnp.exp(b[lo:hi] - bref)).astype(BF16))
            k_sub.append((kk * jnp.exp(jnp.minimum(bref - b, DECAY_EXP_CLAMP))).astype(BF16))
        sc = []
        for hd in range(HGRN_HEADS):
            ls = slice(hd * HEAD_DIM, (hd + 1) * HEAD_DIM)
            s_h = jnp.concatenate([_dot_nt(a_sub[i][:, ls], k_sub[i][:, ls]) for i in range(n_sub)], axis=0)
            sc.append(jnp.where(causal, s_h, 0.0).astype(BF16))
        return qe, kdec, v, dec_last, sc, zg * jax.nn.sigmoid(zg)

    def chunk_output(c, parts):
        rows = slice(c * HGRN_CHUNK, (c + 1) * HGRN_CHUNK)
        qe, kdec, v, dec_last, sc, silu_g = parts
        for hd in range(HGRN_HEADS):
            ls = slice(hd * HEAD_DIM, (hd + 1) * HEAD_DIM)
            st = st_ref[hd]
            o = _dot_nt(qe[:, ls], st.astype(BF16)) + _dot(sc[hd], v[:, ls])
            st_ref[hd] = st * dec_last[:, ls] + _dot_tn(v[:, ls], kdec[:, ls])
            o = _rms(o) * hg[:, ls]
            y_ref[rows, ls] = (o * silu_g[:, ls]).astype(BF16)

    row_g = lax.broadcasted_iota(jnp.int32, (GMLP_CHUNK, GMLP_CHUNK), 0)
    col_g = lax.broadcasted_iota(jnp.int32, (GMLP_CHUNK, GMLP_CHUNK), 1)
    gng = gng_ref[...]
    ws_c = [jnp.where(col_g <= row_g, ws_ref[g], 0.0).astype(BF16) for g in range(GMLP_GROUPS)]

    def gmlp_block(n):
        rs = slice(n * GMLP_CHUNK, (n + 1) * GMLP_CHUNK)
        u = _gelu(z_ref[rs, 4 * HGRN_WIDTH:4 * HGRN_WIDTH + GMLP_WIDTH])
        gv = _gelu(z_ref[rs, 4 * HGRN_WIDTH + GMLP_WIDTH:])
        mu = jnp.mean(gv, axis=-1, keepdims=True)
        gc = gv - mu
        var = jnp.mean(gc * gc, axis=-1, keepdims=True)
        vn = (gc * lax.rsqrt(var + EPS) * lng_ref[...] + lnb_ref[...]).astype(BF16)
        for g in range(GMLP_GROUPS):
            ls = slice(g * GROUP_DIM, (g + 1) * GROUP_DIM)
            sv = _dot(ws_c[g], vn[:, ls]) + bs_ref[g]
            yy = _rms(u[:, ls] * sv) * gng[:, ls]
            y_ref[rs, HGRN_WIDTH + g * GROUP_DIM:HGRN_WIDTH + (g + 1) * GROUP_DIM] = yy.astype(BF16)

    n_chunks = TM // HGRN_CHUNK
    per_blk = GMLP_CHUNK // HGRN_CHUNK
    parts = [chunk_scores(c) for c in range(n_chunks)]
    for c in range(n_chunks):
        chunk_output(c, parts[c])
        if c % per_blk == per_blk - 1:
            gmlp_block(c // per_blk)

    x1 = x + mod[2:3] * _dot(y_ref[...], wout_ref[...])
    x1_ref[...] = x1
    h2 = _rms(x1) * g2_ref[...]
    h2 = h2 * (1.0 + mod[4:5]) + mod[3:4]
    h2_ref[...] = h2

    hh = h2.astype(BF16)
    hl = (h2 - hh.astype(F32)).astype(BF16)
    rw2 = rwh_ref[...]
    by_hh = _dot_nt(rw2, hh)
    logits = by_hh[:N_EXPERTS] + (_dot_nt(rw2[:N_EXPERTS], hl) + by_hh[N_EXPERTS:]) + rb_ref[...]
    erow = lax.broadcasted_iota(jnp.int32, (N_EXPERTS, TM), 0)
    vals, idxs = [], []
    for _ in range(TOP_K):
        m = jnp.max(logits, axis=0, keepdims=True)
        idx = jnp.min(jnp.where(logits == m, erow, N_EXPERTS), axis=0, keepdims=True)
        vals.append(m)
        idxs.append(idx)
        logits = jnp.where(erow == idx, -jnp.inf, logits)
    es = [jnp.exp(v - vals[0]) for v in vals]
    tot = (es[0] + es[1]) + (es[2] + es[3])

    @pl.when((pl.program_id(0) == 0) & (pl.program_id(1) == 0))
    def _():
        cnt_ref[...] = jnp.zeros_like(cnt_ref)

    hot = [erow == idxs[k] for k in range(TOP_K)]
    picked = jnp.where((hot[0] | hot[1]) | (hot[2] | hot[3]), 1.0, 0.0)
    seen = _dot(picked.astype(BF16), earlier_ref[...]) + cnt_ref[:, 0:1]
    cnt_ref[...] = cnt_ref[...] + jnp.sum(picked, axis=1, keepdims=True)
    cnt_out_ref[...] = cnt_ref[...]
    ranks = [jnp.sum(jnp.where(hot[k], seen, 0.0), axis=0, keepdims=True) for k in range(TOP_K)]
    route_ref[...] = jnp.concatenate([i.astype(F32) for i in idxs] + [e / tot for e in es] + ranks +
                                     [jnp.zeros((ROUTE_ROWS - 3 * TOP_K, TM), F32)], axis=0)


def _mixer(x, mod8, norm_mix_g, w_in, lb_params, hgrn_norm_g, gmlp_ln_g, gmlp_ln_b, gmlp_ws, gmlp_bs,
           gmlp_norm_g, w_out, norm_ffn_g, router_w, router_b):
    bsz, seq, d = x.shape
    n_in = w_in.shape[1]
    rw = router_w.T
    rwh = rw.astype(BF16)
    rw2 = jnp.concatenate([rwh, (rw - rwh.astype(F32)).astype(BF16)], axis=0)
    rb = router_b.reshape(N_EXPERTS, 1)
    pos = jnp.arange(TM, dtype=jnp.int32)
    earlier = (pos[:, None] < pos[None, :]).astype(BF16)
    const = lambda *shape: pl.BlockSpec(shape, lambda b, i: (0,) * len(shape))
    tile = lambda w: pl.BlockSpec((None, TM, w), lambda b, i: (b, i, 0))
    per_seq = seq // TM
    return pl.pallas_call(
        _mixer_kernel,
        grid=(bsz, seq // TM),
        in_specs=[tile(d),
                  pl.BlockSpec((None, 8, d), lambda b, i: (b, 0, 0)),
                  const(1, d), const(d, n_in), const(2, HGRN_WIDTH), const(1, HGRN_WIDTH),
                  const(1, GMLP_WIDTH), const(1, GMLP_WIDTH),
                  const(GMLP_GROUPS, GMLP_CHUNK, GMLP_CHUNK), const(GMLP_GROUPS, GMLP_CHUNK, 1),
                  const(1, GMLP_WIDTH), const(d, d), const(1, d),
                  const(2 * N_EXPERTS, d), const(N_EXPERTS, 1), const(TM, TM)],
        out_specs=[tile(d), tile(d),
                   pl.BlockSpec((None, ROUTE_ROWS, TM), lambda b, i: (b * per_seq + i, 0, 0)),
                   const(N_EXPERTS, LANES)],
        out_shape=[jax.ShapeDtypeStruct((bsz, seq, d), F32),
                   jax.ShapeDtypeStruct((bsz, seq, d), F32),
                   jax.ShapeDtypeStruct((bsz * per_seq, ROUTE_ROWS, TM), F32),
                   jax.ShapeDtypeStruct((N_EXPERTS, LANES), F32)],
        scratch_shapes=[pltpu.VMEM((TM, n_in), F32),
                        pltpu.VMEM((TM, d), BF16),
                        pltpu.VMEM((HGRN_HEADS, HEAD_DIM, HEAD_DIM), F32),
                        pltpu.VMEM((N_EXPERTS, LANES), F32)],
        compiler_params=pltpu.CompilerParams(dimension_semantics=("arbitrary", "arbitrary"),
                                             vmem_limit_bytes=VMEM_LIMIT),
        name="mixer",
    )(x, mod8, norm_mix_g.reshape(1, d), w_in.astype(BF16), lb_params, hgrn_norm_g.reshape(1, -1),
      gmlp_ln_g.reshape(1, -1), gmlp_ln_b.reshape(1, -1), gmlp_ws, gmlp_bs[:, :, None],
      gmlp_norm_g.reshape(1, -1), w_out.astype(BF16), norm_ffn_g.reshape(1, d), rw2, rb, earlier)


def _tile(r):
    return pl.ds(r * SUBLANES, SUBLANES)


def _tile8(r8):
    return pl.ds(pl.multiple_of(r8, SUBLANES), SUBLANES)


def _chunk(j, n):
    return pl.ds(j, n, stride=SUBLANES)


def _dispatch_kernel(n_blocks, n_pad_rows, pad0_ref, padn_ref, used_ref, dest_ref, h2_ref, xs_hbm,
                     rows, zeros, sem, zsem):
    i = pl.program_id(0)
    n_steps = pl.num_programs(0)
    blk_rows = MOE_BLK * SUBLANES

    def wait_step(s):
        for _ in range(TOP_K):
            pltpu.make_async_copy(rows.at[s], xs_hbm.at[pl.ds(0, TD * SUBLANES)], sem.at[s]).wait()

    def zero_fill(first_row, n_rows):
        n = n_rows * SUBLANES
        pltpu.make_async_copy(zeros.at[pl.ds(0, n)], xs_hbm.at[pl.ds(pl.multiple_of(first_row * SUBLANES, SUBLANES), n)],
                              zsem.at[0]).start()

    @pl.when(i == 0)
    def _():
        zeros[...] = jnp.zeros_like(zeros)
        for e in range(N_EXPERTS):
            first, count = pad0_ref[e], padn_ref[e]
            for bit in reversed(range(MOE_BLK.bit_length() - 1)):
                @pl.when((count >> bit) & 1 == 1)
                def _():
                    zero_fill(first + ((count >> (bit + 1)) << (bit + 1)), 1 << bit)

        def fill_block(b, carry):
            zero_fill(b * MOE_BLK, MOE_BLK)
            return carry

        lax.fori_loop(used_ref[0], n_blocks, fill_block, 0)

    def step(s):
        @pl.when(i >= 2)
        def _():
            wait_step(s)

        for j in range(SUBLANES):
            rows[s, _chunk(j, TD), :] = h2_ref[:, j * LANES:(j + 1) * LANES]
        for t in range(TD):
            for k in range(TOP_K):
                pltpu.make_async_copy(rows.at[s, _tile(t)], xs_hbm.at[_tile8(dest_ref[0, k, t])],
                                      sem.at[s]).start(priority=k % N_DMA_QUEUES)

        @pl.when(i == n_steps - 1)
        def _():
            wait_step(s)

            @pl.when(i >= 1)
            def _():
                wait_step(1 - s)

            for _ in range(n_pad_rows // MOE_BLK):
                pltpu.make_async_copy(zeros, xs_hbm.at[pl.ds(0, blk_rows)], zsem.at[0]).wait()

    _by_parity(i, step)


def _dest_block(step_of):
    per_tile = TM // TD
    return pl.BlockSpec((1, TOP_K, TD), lambda i, *_: (step_of(i) // per_tile, 0, step_of(i) % per_tile),
                        memory_space=pltpu.SMEM)


def _dispatch(h2, dest, pad_first, pad_count, blocks_used, p_rows):
    n_tok, d = h2.shape
    n_steps = n_tok // TD
    n_pad_rows = p_rows - n_tok * TOP_K
    assert n_pad_rows % MOE_BLK == 0
    grid_spec = pltpu.PrefetchScalarGridSpec(
        num_scalar_prefetch=3,
        grid=(n_steps,),
        in_specs=[_dest_block(lambda i: i),
                  pl.BlockSpec((TD, d), lambda i, *_: (i, 0))],
        out_specs=pl.BlockSpec(memory_space=pl.ANY),
        scratch_shapes=[pltpu.VMEM((2, TD * SUBLANES, LANES), F32),
                        pltpu.VMEM((MOE_BLK * SUBLANES, LANES), F32),
                        pltpu.SemaphoreType.DMA((2,)),
                        pltpu.SemaphoreType.DMA((1,))],
    )
    return pl.pallas_call(
        functools.partial(_dispatch_kernel, p_rows // MOE_BLK, n_pad_rows),
        grid_spec=grid_spec,
        out_shape=jax.ShapeDtypeStruct((p_rows * SUBLANES, LANES), F32),
        compiler_params=pltpu.CompilerParams(dimension_semantics=("arbitrary",)),
        name="dispatch",
    )(pad_first, pad_count, blocks_used, dest, h2)


def _moe_kernel(n_blocks, blk0_ref, nblk_ref, xs_hbm, wgu_hbm, bgu_ref, wd_hbm, bd_ref, ys_hbm,
                xbuf, ybuf, wgu_f, wd_f, wgu_bf, wd_bf, isem, osem, wsem):
    e = pl.program_id(0)
    n_exp = pl.num_programs(0)
    nb = nblk_ref[e]
    b0 = blk0_ref[e]
    d, d_ff = wd_f.shape[2], wd_f.shape[1]
    blk_rows = MOE_BLK * SUBLANES
    ws = e % 2
    gu_rows, d_rows = d // W_CHUNKS, d_ff // W_CHUNKS

    npairs = nb // 2
    odd = nb - 2 * npairs

    def rows_of(b, n):
        return pl.ds(pl.multiple_of((b0 + b) * blk_rows, blk_rows), n * blk_rows)

    def x_copy(b, n, s):
        return pltpu.make_async_copy(xs_hbm.at[rows_of(b, n)], xbuf.at[s, pl.ds(0, n * blk_rows)], isem.at[s])

    def y_copy(b, n, s):
        return pltpu.make_async_copy(ybuf.at[s, pl.ds(0, n * blk_rows)], ys_hbm.at[rows_of(b, n)], osem.at[s])

    def w_start(ex, slot, c):
        r_gu = pl.ds(pl.multiple_of(c * gu_rows, gu_rows), gu_rows)
        r_d = pl.ds(pl.multiple_of(c * d_rows, d_rows), d_rows)
        pltpu.make_async_copy(wgu_hbm.at[ex, r_gu], wgu_f.at[slot, r_gu], wsem.at[slot]).start()
        pltpu.make_async_copy(wd_hbm.at[ex, r_d], wd_f.at[slot, r_d], wsem.at[slot]).start()

    def w_wait(slot):
        pltpu.make_async_copy(wgu_hbm.at[0], wgu_f.at[slot], wsem.at[slot]).wait()
        pltpu.make_async_copy(wd_hbm.at[0], wd_f.at[slot], wsem.at[slot]).wait()

    @pl.when(e == 0)
    def _():
        for c in range(W_CHUNKS):
            w_start(0, 0, c)

    @pl.when(npairs > 0)
    def _():
        x_copy(0, 2, 0).start(priority=BLOCK_DMA_PRIORITY)

    @pl.when((npairs == 0) & (odd == 1))
    def _():
        x_copy(0, 1, 0).start(priority=BLOCK_DMA_PRIORITY)

    w_wait(ws)
    wgu_bf[...] = wgu_f[ws].astype(BF16)
    wd_bf[...] = wd_f[ws].astype(BF16)
    has_next = e + 1 < n_exp

    def ffn(n, s):
        rows = n * MOE_BLK
        xb = jnp.concatenate([xbuf[s, _chunk(j, rows), :] for j in range(SUBLANES)], axis=-1).astype(BF16)
        gu = _dot(xb, wgu_bf[...]) + bgu_ref[...]
        gate = jnp.minimum(gu[:, :d_ff], SWIGLU_LIMIT)
        up = jnp.clip(gu[:, d_ff:], -SWIGLU_LIMIT, SWIGLU_LIMIT)
        glu = gate * jax.nn.sigmoid(SWIGLU_ALPHA * gate)
        yb = _dot(((up + 1.0) * glu).astype(BF16), wd_bf[...]) + bd_ref[...]
        for j in range(SUBLANES):
            ybuf[s, _chunk(j, rows), :] = yb[:, j * LANES:(j + 1) * LANES]

    def pair_body(p, carry):
        s = p % 2

        @pl.when(p + 1 < npairs)
        def _():
            x_copy(2 * p + 2, 2, 1 - s).start(priority=BLOCK_DMA_PRIORITY)

        @pl.when((p + 1 == npairs) & (odd == 1))
        def _():
            x_copy(2 * p + 2, 1, 1 - s).start(priority=BLOCK_DMA_PRIORITY)

        @pl.when(has_next & (2 * p < W_CHUNKS))
        def _():
            w_start(e + 1, 1 - ws, 2 * p)
            w_start(e + 1, 1 - ws, 2 * p + 1)

        x_copy(2 * p, 2, s).wait()

        @pl.when(p >= 2)
        def _():
            y_copy(2 * p - 4, 2, s).wait()

        ffn(2, s)
        y_copy(2 * p, 2, s).start(priority=BLOCK_DMA_PRIORITY)
        return carry

    lax.fori_loop(0, npairs, pair_body, 0)

    @pl.when(odd == 1)
    def _():
        s = npairs % 2
        x_copy(2 * npairs, 1, s).wait()

        @pl.when(npairs >= 2)
        def _():
            y_copy(2 * npairs - 4, 2, s).wait()

        ffn(1, s)
        cp = y_copy(2 * npairs, 1, s)
        cp.start(priority=BLOCK_DMA_PRIORITY)
        cp.wait()

    @pl.when(has_next)
    def _():
        def rest(c, carry):
            w_start(e + 1, 1 - ws, c)
            return carry

        lax.fori_loop(jnp.minimum(2 * npairs, W_CHUNKS), W_CHUNKS, rest, 0)

    @pl.when(npairs >= 1)
    def _():
        y_copy(2 * npairs - 2, 2, (npairs - 1) % 2).wait()

    @pl.when((npairs >= 2) & (odd == 0))
    def _():
        y_copy(2 * npairs - 4, 2, npairs % 2).wait()

    @pl.when(e == pl.num_programs(0) - 1)
    def _():
        ybuf[0, pl.ds(0, blk_rows), :] = jnp.zeros((blk_rows, LANES), F32)

        def fill(b, carry):
            cp = y_copy(b, 1, 0)
            cp.start()
            cp.wait()
            return carry

        lax.fori_loop(nb, n_blocks - b0, fill, 0)


def _moe(xs, blk0, nblk, w_gate_up, b_gate_up, w_down, b_down):
    n_blocks = xs.shape[0] // (MOE_BLK * SUBLANES)
    n_exp, d_ff, d = w_down.shape
    grid_spec = pltpu.PrefetchScalarGridSpec(
        num_scalar_prefetch=2,
        grid=(n_exp,),
        in_specs=[pl.BlockSpec(memory_space=pl.ANY),
                  pl.BlockSpec(memory_space=pl.ANY),
                  pl.BlockSpec((None, 1, 2 * d_ff), lambda e, b0, nb: (e, 0, 0)),
                  pl.BlockSpec(memory_space=pl.ANY),
                  pl.BlockSpec((None, 1, d), lambda e, b0, nb: (e, 0, 0))],
        out_specs=pl.BlockSpec(memory_space=pl.ANY),
        scratch_shapes=[pltpu.VMEM((2, 2 * MOE_BLK * SUBLANES, LANES), F32),
                        pltpu.VMEM((2, 2 * MOE_BLK * SUBLANES, LANES), F32),
                        pltpu.VMEM((2, d, 2 * d_ff), F32),
                        pltpu.VMEM((2, d_ff, d), F32),
                        pltpu.VMEM((d, 2 * d_ff), BF16),
                        pltpu.VMEM((d_ff, d), BF16),
                        pltpu.SemaphoreType.DMA((2,)),
                        pltpu.SemaphoreType.DMA((2,)),
                        pltpu.SemaphoreType.DMA((2,))],
    )
    return pl.pallas_call(
        functools.partial(_moe_kernel, n_blocks),
        grid_spec=grid_spec,
        out_shape=jax.ShapeDtypeStruct(xs.shape, F32),
        compiler_params=pltpu.CompilerParams(dimension_semantics=("arbitrary",),
                                             vmem_limit_bytes=VMEM_LIMIT),
        name="moe_ffn",
    )(blk0, nblk, xs, w_gate_up, b_gate_up[:, None, :], w_down, b_down[:, None, :])


def _combine_kernel(destc_ref, destn_ref, x1_ref, mod_ref, route_ref, fg_ref, ys_hbm, o_ref, buf0, buf1, sem):
    buf = (buf0, buf1)
    i = pl.program_id(0)
    n_steps = pl.num_programs(0)
    d = x1_ref.shape[-1]

    def start_gather(dest_ref, s, t0=0, t1=TD):
        for t in range(t0, t1):
            for k in range(TOP_K):
                pltpu.make_async_copy(ys_hbm.at[_tile8(dest_ref[0, k, t])], buf[s].at[k, _tile(t)],
                                      sem.at[s]).start(priority=k % N_DMA_QUEUES)

    def wait_gather(s):
        for k in range(TOP_K):
            pltpu.make_async_copy(ys_hbm.at[pl.ds(0, TD * SUBLANES)], buf[s].at[k], sem.at[s]).wait()

    @pl.when(i == 0)
    def _():
        start_gather(destc_ref, 0)

    def step(s):
        wait_gather(s)
        gates = route_ref[...]
        gk = [gates[:, TOP_K + k:TOP_K + k + 1] for k in range(TOP_K)]
        gate2 = mod_ref[5:6, :]

        def residual(j):
            ls = slice(j * LANES, (j + 1) * LANES)
            rj = _chunk(j, TD)
            y = (gk[0] * buf[s][0, rj, :] + gk[1] * buf[s][1, rj, :]) + \
                (gk[2] * buf[s][2, rj, :] + gk[3] * buf[s][3, rj, :])
            return x1_ref[:, ls] + gate2[:, ls] * y

        ss = jnp.zeros((TD, 1), F32)
        per = TD // SUBLANES
        for j in range(SUBLANES):
            start_gather(destn_ref, 1 - s, j * per, (j + 1) * per)
            xj = residual(j)
            ss = ss + jnp.sum(xj * xj, axis=-1, keepdims=True)
        inv = lax.rsqrt(ss / d + EPS)
        for j in range(SUBLANES):
            ls = slice(j * LANES, (j + 1) * LANES)
            o_ref[:, ls] = residual(j) * inv * fg_ref[:, ls]

        @pl.when(i == n_steps - 1)
        def _():
            wait_gather(1 - s)

    _by_parity(i, step)


def _combine(x1, mod8, route, dest, ys, final_g):
    bsz, seq, d = x1.shape
    n_tok = bsz * seq
    n_steps = n_tok // TD
    per_seq = seq // TD
    return pl.pallas_call(
        _combine_kernel,
        grid=(n_steps,),
        in_specs=[_dest_block(lambda i: i),
                  _dest_block(lambda i: jnp.minimum(i + 1, n_steps - 1)),
                  pl.BlockSpec((TD, d), lambda i: (i, 0)),
                  pl.BlockSpec((None, 8, d), lambda i: (i // per_seq, 0, 0)),
                  pl.BlockSpec((TD, ROUTE_ROWS), lambda i: (i, 0)),
                  pl.BlockSpec((1, d), lambda i: (0, 0)),
                  pl.BlockSpec(memory_space=pl.ANY)],
        out_specs=pl.BlockSpec((TD, d), lambda i: (i, 0)),
        out_shape=jax.ShapeDtypeStruct((n_tok, d), F32),
        scratch_shapes=[pltpu.VMEM((TOP_K, TD * SUBLANES, LANES), F32),
                        pltpu.VMEM((TOP_K, TD * SUBLANES, LANES), F32),
                        pltpu.SemaphoreType.DMA((2,))],
        compiler_params=pltpu.CompilerParams(dimension_semantics=("arbitrary",),
                                             vmem_limit_bytes=VMEM_LIMIT),
        name="combine",
    )(dest, dest, x1.reshape(n_tok, d), mod8, route, final_g.reshape(1, d), ys).reshape(bsz, seq, d)


def _dest_kernel(route_ref, start_ref, dest_ref):
    erow = lax.broadcasted_iota(jnp.int32, (N_EXPERTS, TM), 0).astype(F32)
    for i in range(route_ref.shape[0]):
        r = route_ref[i]
        rows = []
        for k in range(TOP_K):
            base = jnp.sum(jnp.where(erow == r[k:k + 1, :], start_ref[...], 0.0), axis=0, keepdims=True)
            rows.append((base + r[2 * TOP_K + k:2 * TOP_K + k + 1, :]) * float(SUBLANES))
        dest_ref[i] = jnp.concatenate(rows, axis=0).astype(jnp.int32)


def _destinations(route_t, start_pad):
    n_tiles = route_t.shape[0]
    per_step = 8 if n_tiles % 8 == 0 else 1
    return pl.pallas_call(
        _dest_kernel,
        grid=(n_tiles // per_step,),
        in_specs=[pl.BlockSpec((per_step, ROUTE_ROWS, TM), lambda i: (i, 0, 0)),
                  pl.BlockSpec((N_EXPERTS, 1), lambda i: (0, 0))],
        out_specs=pl.BlockSpec((per_step, TOP_K, TM), lambda i: (i, 0, 0)),
        out_shape=jax.ShapeDtypeStruct((n_tiles, TOP_K, TM), jnp.int32),
        name="destinations",
    )(route_t, start_pad.astype(F32).reshape(N_EXPERTS, 1))


def _routing(counts_f, n_tok):
    nk = n_tok * TOP_K
    counts = counts_f[:, 0].astype(jnp.int32)
    padded = ((counts + MOE_BLK - 1) // MOE_BLK) * MOE_BLK
    pad_end = jnp.cumsum(padded)
    start_pad = pad_end - padded
    p_rows = ((nk + N_EXPERTS * (MOE_BLK - 1) + MOE_BLK - 1) // MOE_BLK) * MOE_BLK
    blocks_used = (pad_end[-1:] // MOE_BLK).astype(jnp.int32)
    return (start_pad, start_pad + counts, padded - counts, blocks_used,
            (start_pad // MOE_BLK).astype(jnp.int32), (padded // MOE_BLK).astype(jnp.int32), p_rows)


def kernel(x, c, ada_w, ada_b, norm_mix_g, w_in, lb_params, hgrn_norm_g, gmlp_ln_g, gmlp_ln_b, gmlp_ws, gmlp_bs,
           gmlp_norm_g, w_out, norm_ffn_g, router_w, router_b, w_gate_up, b_gate_up, w_down, b_down, final_g):
    assert ada_w.shape[0] == 1, "single-layer block"
    bsz, seq, d = x.shape
    assert d == SUBLANES * LANES and seq % TM == 0 and (bsz * seq) % TD == 0
    n_tok = bsz * seq
    mod = _modulation(c, ada_w[0], ada_b[0])
    mod8 = jnp.zeros((bsz, 8, d), F32).at[:, :6].set(mod.reshape(bsz, 6, d))
    x1, h2, route, counts = _mixer(x, mod8, norm_mix_g[0], w_in[0], lb_params, hgrn_norm_g[0], gmlp_ln_g[0],
                                   gmlp_ln_b[0], gmlp_ws[0], gmlp_bs[0], gmlp_norm_g[0], w_out[0],
                                   norm_ffn_g[0], router_w[0], router_b[0])
    start_pad, pad_first, pad_count, blocks_used, blk0, nblk, p_rows = _routing(counts, n_tok)
    dest = _destinations(route, start_pad)
    route = route.transpose(0, 2, 1).reshape(n_tok, ROUTE_ROWS)
    xs = _dispatch(h2.reshape(n_tok, d), dest, pad_first, pad_count, blocks_used, p_rows)
    ys = _moe(xs, blk0, nblk, w_gate_up[0], b_gate_up[0], w_down[0], b_down[0])
    return _combine(x1, mod8, route, dest, ys, final_g)
```

```python
import functools

import jax
import jax.numpy as jnp
from jax import lax
from jax.experimental import pallas as pl
from jax.experimental.pallas import tpu as pltpu

F32 = jnp.float32
BF16 = jnp.bfloat16

HGRN_HEADS = 4
HEAD_DIM = 128
HGRN_WIDTH = HGRN_HEADS * HEAD_DIM
HGRN_CHUNK = 64
HGRN_SUB = 16
GMLP_GROUPS = 4
GROUP_DIM = 128
GMLP_WIDTH = GMLP_GROUPS * GROUP_DIM
GMLP_CHUNK = 128
N_EXPERTS = 32
TOP_K = 4
SWIGLU_LIMIT = 7.0
SWIGLU_ALPHA = 1.702
EPS = 1e-6
LANES = 128
SUBLANES = 8
N_DMA_QUEUES = 2
BLOCK_DMA_PRIORITY = 1
ROUTE_ROWS = 16
W_CHUNKS = 8
DECAY_EXP_CLAMP = 60.0
TM = 512
MOE_BLK = 256
TD = 512
VMEM_LIMIT = 56 * 1024 * 1024


def _dot(a, b):
    return jnp.dot(a, b, preferred_element_type=F32)


def _dot_nt(a, b):
    return lax.dot_general(a, b, (((1,), (1,)), ((), ())), preferred_element_type=F32)


def _dot_tn(a, b):
    return lax.dot_general(a, b, (((0,), (0,)), ((), ())), preferred_element_type=F32)


def _rms(x):
    return x * lax.rsqrt(jnp.mean(x * x, axis=-1, keepdims=True) + EPS)


def _gelu(x):
    return 0.5 * x * (1.0 + lax.erf(x * 0.7071067811865476))


def _by_parity(i, fn):
    @pl.when(i % 2 == 0)
    def _():
        fn(0)

    @pl.when(i % 2 == 1)
    def _():
        fn(1)


def _mod_kernel(c_ref, w_ref, b_ref, o_ref):
    c = c_ref[...]
    ca = c * jax.nn.sigmoid(c)
    o_ref[...] = jnp.dot(ca, w_ref[...], precision=lax.Precision.HIGHEST,
                         preferred_element_type=F32) + b_ref[...]


def _modulation(c, ada_w, ada_b):
    bsz, d = c.shape
    n_out = ada_w.shape[1]
    rows = 8
    c_pad = jnp.zeros((rows, d), F32).at[:bsz].set(c)
    tn = 2048
    out = pl.pallas_call(
        _mod_kernel,
        grid=(n_out // tn,),
        in_specs=[pl.BlockSpec((rows, d), lambda j: (0, 0)),
                  pl.BlockSpec((d, tn), lambda j: (0, j)),
                  pl.BlockSpec((1, tn), lambda j: (0, j))],
        out_specs=pl.BlockSpec((rows, tn), lambda j: (0, j)),
        out_shape=jax.ShapeDtypeStruct((rows, n_out), F32),
        name="adaln_mod",
    )(c_pad, ada_w, ada_b.reshape(1, n_out))
    return out[:bsz]


def _mixer_kernel(x_ref, mod_ref, g1_ref, win_ref, lbp_ref, hg_ref, lng_ref, lnb_ref, ws_ref, bs_ref,
                  gng_ref, wout_ref, g2_ref, rwh_ref, rb_ref, earlier_ref,
                  x1_ref, h2_ref, route_ref, cnt_out_ref,
                  z_ref, y_ref, st_ref, cnt_ref):
    @pl.when(pl.program_id(1) == 0)
    def _():
        st_ref[...] = jnp.zeros_like(st_ref)

    x = x_ref[...]
    mod = mod_ref[...]
    h = _rms(x) * g1_ref[...]
    h = h * (1.0 + mod[1:2]) + mod[0:1]
    z_ref[...] = _dot(h.astype(BF16), win_ref[...])

    lbp = lbp_ref[...]
    lbe = jnp.exp(lbp - jnp.max(lbp, axis=0, keepdims=True))
    lb = lbe[0:1] / jnp.sum(lbe, axis=0, keepdims=True)
    hg = hg_ref[...]
    row = lax.broadcasted_iota(jnp.int32, (HGRN_CHUNK, HGRN_CHUNK), 0)
    col = lax.broadcasted_iota(jnp.int32, (HGRN_CHUNK, HGRN_CHUNK), 1)
    causal = col <= row
    crow = lax.broadcasted_iota(jnp.int32, (HGRN_CHUNK, HGRN_WIDTH), 0)
    n_sub = HGRN_CHUNK // HGRN_SUB

    def chunk_scores(c):
        rows = slice(c * HGRN_CHUNK, (c + 1) * HGRN_CHUNK)
        zq = z_ref[rows, 0:HGRN_WIDTH]
        zf = z_ref[rows, HGRN_WIDTH:2 * HGRN_WIDTH]
        zi = z_ref[rows, 2 * HGRN_WIDTH:3 * HGRN_WIDTH]
        zg = z_ref[rows, 3 * HGRN_WIDTH:4 * HGRN_WIDTH]
        f = lb + (1.0 - lb) * jax.nn.sigmoid(zf)
        logf = jnp.log(f)
        kk = 1.0 - f
        b = logf
        for sh in (1, 2, 4, 8, 16, 32):
            b = b + jnp.where(crow >= sh, pltpu.roll(b, sh, axis=0), 0.0)
        b_last = b[HGRN_CHUNK - 1:HGRN_CHUNK, :]
        qe = (zq * jnp.exp(b)).astype(BF16)
        kdec = (kk * jnp.exp(b_last - b)).astype(BF16)
        v = zi.astype(BF16)
        dec_last = jnp.exp(b_last)
        a_sub, k_sub = [], []
        for i in range(n_sub):
            lo, hi = i * HGRN_SUB, (i + 1) * HGRN_SUB
            bref = b[lo - 1:lo, :] if i > 0 else jnp.zeros((1, HGRN_WIDTH), F32)
            a_sub.append((zq[lo:hi] * jnp.exp(b[lo:hi] - bref)).astype(BF16))
            k_sub.append((kk * jnp.exp(jnp.minimum(bref - b, DECAY_EXP_CLAMP))).astype(BF16))
        sc = []
        for hd in range(HGRN_HEADS):
            ls = slice(hd * HEAD_DIM, (hd + 1) * HEAD_DIM)
            s_h = jnp.concatenate([_dot_nt(a_sub[i][:, ls], k_sub[i][:, ls]) for i in range(n_sub)], axis=0)
            sc.append(jnp.where(causal, s_h, 0.0).astype(BF16))
        return qe, kdec, v, dec_last, sc, zg * jax.nn.sigmoid(zg)

    def chunk_output(c, parts):
        rows = slice(c * HGRN_CHUNK, (c + 1) * HGRN_CHUNK)
        qe, kdec, v, dec_last, sc, silu_g = parts
        for hd in range(HGRN_HEADS):
            ls = slice(hd * HEAD_DIM, (hd + 1) * HEAD_DIM)
            st = st_ref[hd]
            o = _dot_nt(qe[:, ls], st.astype(BF16)) + _dot(sc[hd], v[:, ls])
            st_ref[hd] = st * dec_last[:, ls] + _dot_tn(v[:, ls], kdec[:, ls])
            o = _rms(o) * hg[:, ls]
            y_ref[rows, ls] = (o * silu_g[:, ls]).astype(BF16)

    row_g = lax.broadcasted_iota(jnp.int32, (GMLP_CHUNK, GMLP_CHUNK), 0)
    col_g = lax.broadcasted_iota(jnp.int32, (GMLP_CHUNK, GMLP_CHUNK), 1)
    gng = gng_ref[...]
    ws_c = [jnp.where(col_g <= row_g, ws_ref[g], 0.0).astype(BF16) for g in range(GMLP_GROUPS)]

    def gmlp_block(n):
        rs = slice(n * GMLP_CHUNK, (n + 1) * GMLP_CHUNK)
        u = _gelu(z_ref[rs, 4 * HGRN_WIDTH:4 * HGRN_WIDTH + GMLP_WIDTH])
        gv = _gelu(z_ref[rs, 4 * HGRN_WIDTH + GMLP_WIDTH:])
        mu = jnp.mean(gv, axis=-1, keepdims=True)
        gc = gv - mu
        var = jnp.mean(gc * gc, axis=-1, keepdims=True)
        vn = (gc * lax.rsqrt(var + EPS) * lng_ref[...] + lnb_ref[...]).astype(BF16)
        for g in range(GMLP_GROUPS):
            ls = slice(g * GROUP_DIM, (g + 1) * GROUP_DIM)
            sv = _dot(ws_c[g], vn[:, ls]) + bs_ref[g]
            yy = _rms(u[:, ls] * sv) * gng[:, ls]
            y_ref[rs, HGRN_WIDTH + g * GROUP_DIM:HGRN_WIDTH + (g + 1) * GROUP_DIM] = yy.astype(BF16)

    n_chunks = TM // HGRN_CHUNK
    per_blk = GMLP_CHUNK // HGRN_CHUNK
    parts = [chunk_scores(c) for c in range(n_chunks)]
    for c in range(n_chunks):
        chunk_output(c, parts[c])
        if c % per_blk == per_blk - 1:
            gmlp_block(c // per_blk)

    hh, hl = [], []
    for r in range(2):
        rs = slice(r * (TM // 2), (r + 1) * (TM // 2))
        x1 = x_ref[rs, :] + mod[2:3] * _dot(y_ref[rs, :], wout_ref[...])
        x1_ref[rs, :] = x1
        h2 = _rms(x1) * g2_ref[...]
        h2 = h2 * (1.0 + mod[4:5]) + mod[3:4]
        h2_ref[rs, :] = h2
        hh.append(h2.astype(BF16))
        hl.append((h2 - hh[-1].astype(F32)).astype(BF16))
    hh = jnp.concatenate(hh, axis=0)
    hl = jnp.concatenate(hl, axis=0)

    rw2 = rwh_ref[...]
    by_hh = _dot_nt(rw2, hh)
    logits = by_hh[:N_EXPERTS] + (_dot_nt(rw2[:N_EXPERTS], hl) + by_hh[N_EXPERTS:]) + rb_ref[...]
    erow = lax.broadcasted_iota(jnp.int32, (N_EXPERTS, TM), 0)
    vals, idxs = [], []
    for _ in range(TOP_K):
        m = jnp.max(logits, axis=0, keepdims=True)
        idx = jnp.min(jnp.where(logits == m, erow, N_EXPERTS), axis=0, keepdims=True)
        vals.append(m)
        idxs.append(idx)
        logits = jnp.where(erow == idx, -jnp.inf, logits)
    es = [jnp.exp(v - vals[0]) for v in vals]
    tot = (es[0] + es[1]) + (es[2] + es[3])

    @pl.when((pl.program_id(0) == 0) & (pl.program_id(1) == 0))
    def _():
        cnt_ref[...] = jnp.zeros_like(cnt_ref)

    hot = [erow == idxs[k] for k in range(TOP_K)]
    picked = jnp.where((hot[0] | hot[1]) | (hot[2] | hot[3]), 1.0, 0.0)
    seen = _dot(picked.astype(BF16), earlier_ref[...]) + cnt_ref[:, 0:1]
    cnt_ref[...] = cnt_ref[...] + jnp.sum(picked, axis=1, keepdims=True)
    cnt_out_ref[...] = cnt_ref[...]
    ranks = [jnp.sum(jnp.where(hot[k], seen, 0.0), axis=0, keepdims=True) for k in range(TOP_K)]
    route_ref[...] = jnp.concatenate([i.astype(F32) for i in idxs] + [e / tot for e in es] + ranks +
                                     [jnp.zeros((ROUTE_ROWS - 3 * TOP_K, TM), F32)], axis=0)


def _mixer(x, mod8, norm_mix_g, w_in, lb_params, hgrn_norm_g, gmlp_ln_g, gmlp_ln_b, gmlp_ws, gmlp_bs,
           gmlp_norm_g, w_out, norm_ffn_g, router_w, router_b):
    bsz, seq, d = x.shape
    n_in = w_in.shape[1]
    rw = router_w.T
    rwh = rw.astype(BF16)
    rw2 = jnp.concatenate([rwh, (rw - rwh.astype(F32)).astype(BF16)], axis=0)
    rb = router_b.reshape(N_EXPERTS, 1)
    pos = jnp.arange(TM, dtype=jnp.int32)
    earlier = (pos[:, None] < pos[None, :]).astype(BF16)
    const = lambda *shape: pl.BlockSpec(shape, lambda b, i: (0,) * len(shape))
    tile = lambda w: pl.BlockSpec((None, TM, w), lambda b, i: (b, i, 0))
    per_seq = seq // TM
    return pl.pallas_call(
        _mixer_kernel,
        grid=(bsz, seq // TM),
        in_specs=[tile(d),
                  pl.BlockSpec((None, 8, d), lambda b, i: (b, 0, 0)),
                  const(1, d), const(d, n_in), const(2, HGRN_WIDTH), const(1, HGRN_WIDTH),
                  const(1, GMLP_WIDTH), const(1, GMLP_WIDTH),
                  const(GMLP_GROUPS, GMLP_CHUNK, GMLP_CHUNK), const(GMLP_GROUPS, GMLP_CHUNK, 1),
                  const(1, GMLP_WIDTH), const(d, d), const(1, d),
                  const(2 * N_EXPERTS, d), const(N_EXPERTS, 1), const(TM, TM)],
        out_specs=[tile(d), tile(d),
                   pl.BlockSpec((None, ROUTE_ROWS, TM), lambda b, i: (b * per_seq + i, 0, 0)),
                   const(N_EXPERTS, LANES)],
        out_shape=[jax.ShapeDtypeStruct((bsz, seq, d), F32),
                   jax.ShapeDtypeStruct((bsz, seq, d), F32),
                   jax.ShapeDtypeStruct((bsz * per_seq, ROUTE_ROWS, TM), F32),
                   jax.ShapeDtypeStruct((N_EXPERTS, LANES), F32)],
        scratch_shapes=[pltpu.VMEM((TM, n_in), F32),
                        pltpu.VMEM((TM, d), BF16),
                        pltpu.VMEM((HGRN_HEADS, HEAD_DIM, HEAD_DIM), F32),
                        pltpu.VMEM((N_EXPERTS, LANES), F32)],
        compiler_params=pltpu.CompilerParams(dimension_semantics=("arbitrary", "arbitrary"),
                                             vmem_limit_bytes=VMEM_LIMIT),
        name="mixer",
    )(x, mod8, norm_mix_g.reshape(1, d), w_in.astype(BF16), lb_params, hgrn_norm_g.reshape(1, -1),
      gmlp_ln_g.reshape(1, -1), gmlp_ln_b.reshape(1, -1), gmlp_ws, gmlp_bs[:, :, None],
      gmlp_norm_g.reshape(1, -1), w_out.astype(BF16), norm_ffn_g.reshape(1, d), rw2, rb, earlier)


def _tile(r):
    return pl.ds(r * SUBLANES, SUBLANES)


def _tile8(r8):
    return pl.ds(pl.multiple_of(r8, SUBLANES), SUBLANES)


def _chunk(j, n):
    return pl.ds(j, n, stride=SUBLANES)


def _dispatch_kernel(n_blocks, n_pad_rows, pad0_ref, padn_ref, used_ref, dest_ref, h2_ref, xs_hbm,
                     rows, zeros, sem, zsem):
    i = pl.program_id(0)
    n_steps = pl.num_programs(0)
    blk_rows = MOE_BLK * SUBLANES

    def wait_step(s):
        for _ in range(TOP_K):
            pltpu.make_async_copy(rows.at[s], xs_hbm.at[pl.ds(0, TD * SUBLANES)], sem.at[s]).wait()

    def zero_fill(first_row, n_rows):
        n = n_rows * SUBLANES
        pltpu.make_async_copy(zeros.at[pl.ds(0, n)], xs_hbm.at[pl.ds(pl.multiple_of(first_row * SUBLANES, SUBLANES), n)],
                              zsem.at[0]).start()

    @pl.when(i == 0)
    def _():
        zeros[...] = jnp.zeros_like(zeros)
        for e in range(N_EXPERTS):
            first, count = pad0_ref[e], padn_ref[e]
            for bit in reversed(range(MOE_BLK.bit_length() - 1)):
                @pl.when((count >> bit) & 1 == 1)
                def _():
                    zero_fill(first + ((count >> (bit + 1)) << (bit + 1)), 1 << bit)

        def fill_block(b, carry):
            zero_fill(b * MOE_BLK, MOE_BLK)
            return carry

        lax.fori_loop(used_ref[0], n_blocks, fill_block, 0)

    def step(s):
        @pl.when(i >= 2)
        def _():
            wait_step(s)

        for j in range(SUBLANES):
            rows[s, _chunk(j, TD), :] = h2_ref[:, j * LANES:(j + 1) * LANES]
        for t in range(TD):
            for k in range(TOP_K):
                pltpu.make_async_copy(rows.at[s, _tile(t)], xs_hbm.at[_tile8(dest_ref[0, k, t])],
                                      sem.at[s]).start(priority=k % N_DMA_QUEUES)

        @pl.when(i == n_steps - 1)
        def _():
            wait_step(s)

            @pl.when(i >= 1)
            def _():
                wait_step(1 - s)

            for _ in range(n_pad_rows // MOE_BLK):
                pltpu.make_async_copy(zeros, xs_hbm.at[pl.ds(0, blk_rows)], zsem.at[0]).wait()

    _by_parity(i, step)


def _dest_block(step_of):
    per_tile = TM // TD
    return pl.BlockSpec((1, TOP_K, TD), lambda i, *_: (step_of(i) // per_tile, 0, step_of(i) % per_tile),
                        memory_space=pltpu.SMEM)


def _dispatch(h2, dest, pad_first, pad_count, blocks_used, p_rows):
    n_tok, d = h2.shape
    n_steps = n_tok // TD
    n_pad_rows = p_rows - n_tok * TOP_K
    assert n_pad_rows % MOE_BLK == 0
    grid_spec = pltpu.PrefetchScalarGridSpec(
        num_scalar_prefetch=3,
        grid=(n_steps,),
        in_specs=[_dest_block(lambda i: i),
                  pl.BlockSpec((TD, d), lambda i, *_: (i, 0))],
        out_specs=pl.BlockSpec(memory_space=pl.ANY),
        scratch_shapes=[pltpu.VMEM((2, TD * SUBLANES, LANES), F32),
                        pltpu.VMEM((MOE_BLK * SUBLANES, LANES), F32),
                        pltpu.SemaphoreType.DMA((2,)),
                        pltpu.SemaphoreType.DMA((1,))],
    )
    return pl.pallas_call(
        functools.partial(_dispatch_kernel, p_rows // MOE_BLK, n_pad_rows),
        grid_spec=grid_spec,
        out_shape=jax.ShapeDtypeStruct((p_rows * SUBLANES, LANES), F32),
        compiler_params=pltpu.CompilerParams(dimension_semantics=("arbitrary",)),
        name="dispatch",
    )(pad_first, pad_count, blocks_used, dest, h2)


def _moe_kernel(n_blocks, blk0_ref, nblk_ref, xs_hbm, wgu_hbm, bgu_ref, wd_hbm, bd_ref, ys_hbm,
                xbuf, ybuf, wgu_f, wd_f, wgu_bf, wd_bf, isem, osem, wsem):
    e = pl.program_id(0)
    n_exp = pl.num_programs(0)
    nb = nblk_ref[e]
    b0 = blk0_ref[e]
    d, d_ff = wd_f.shape[2], wd_f.shape[1]
    blk_rows = MOE_BLK * SUBLANES
    ws = e % 2
    gu_rows, d_rows = d // W_CHUNKS, d_ff // W_CHUNKS

    npairs = nb // 2
    odd = nb - 2 * npairs

    def rows_of(b, n):
        return pl.ds(pl.multiple_of((b0 + b) * blk_rows, blk_rows), n * blk_rows)

    def x_copy(b, n, s):
        return pltpu.make_async_copy(xs_hbm.at[rows_of(b, n)], xbuf.at[s, pl.ds(0, n * blk_rows)], isem.at[s])

    def y_copy(b, n, s):
        return pltpu.make_async_copy(ybuf.at[s, pl.ds(0, n * blk_rows)], ys_hbm.at[rows_of(b, n)], osem.at[s])

    def w_start(ex, slot, c):
        r_gu = pl.ds(pl.multiple_of(c * gu_rows, gu_rows), gu_rows)
        r_d = pl.ds(pl.multiple_of(c * d_rows, d_rows), d_rows)
        pltpu.make_async_copy(wgu_hbm.at[ex, r_gu], wgu_f.at[slot, r_gu], wsem.at[slot]).start()
        pltpu.make_async_copy(wd_hbm.at[ex, r_d], wd_f.at[slot, r_d], wsem.at[slot]).start()

    def w_wait(slot):
        pltpu.make_async_copy(wgu_hbm.at[0], wgu_f.at[slot], wsem.at[slot]).wait()
        pltpu.make_async_copy(wd_hbm.at[0], wd_f.at[slot], wsem.at[slot]).wait()

    @pl.when(e == 0)
    def _():
        for c in range(W_CHUNKS):
            w_start(0, 0, c)

    @pl.when(npairs > 0)
    def _():
        x_copy(0, 2, 0).start(priority=BLOCK_DMA_PRIORITY)

    @pl.when((npairs == 0) & (odd == 1))
    def _():
        x_copy(0, 1, 0).start(priority=BLOCK_DMA_PRIORITY)

    w_wait(ws)
    wgu_bf[...] = wgu_f[ws].astype(BF16)
    wd_bf[...] = wd_f[ws].astype(BF16)
    has_next = e + 1 < n_exp

    def ffn(n, s):
        rows = n * MOE_BLK
        xb = jnp.concatenate([xbuf[s, _chunk(j, rows), :] for j in range(SUBLANES)], axis=-1).astype(BF16)
        gu = _dot(xb, wgu_bf[...]) + bgu_ref[...]
        gate = jnp.minimum(gu[:, :d_ff], SWIGLU_LIMIT)
        up = jnp.clip(gu[:, d_ff:], -SWIGLU_LIMIT, SWIGLU_LIMIT)
        glu = gate * jax.nn.sigmoid(SWIGLU_ALPHA * gate)
        yb = _dot(((up + 1.0) * glu).astype(BF16), wd_bf[...]) + bd_ref[...]
        for j in range(SUBLANES):
            ybuf[s, _chunk(j, rows), :] = yb[:, j * LANES:(j + 1) * LANES]

    def pair_body(p, carry):
        s = p % 2

        @pl.when(p + 1 < npairs)
        def _():
            x_copy(2 * p + 2, 2, 1 - s).start(priority=BLOCK_DMA_PRIORITY)

        @pl.when((p + 1 == npairs) & (odd == 1))
        def _():
            x_copy(2 * p + 2, 1, 1 - s).start(priority=BLOCK_DMA_PRIORITY)

        @pl.when(has_next & (2 * p < W_CHUNKS))
        def _():
            w_start(e + 1, 1 - ws, 2 * p)
            w_start(e + 1, 1 - ws, 2 * p + 1)

        x_copy(2 * p, 2, s).wait()

        @pl.when(p >= 2)
        def _():
            y_copy(2 * p - 4, 2, s).wait()

        ffn(2, s)
        y_copy(2 * p, 2, s).start(priority=BLOCK_DMA_PRIORITY)
        return carry

    lax.fori_loop(0, npairs, pair_body, 0)

    @pl.when(odd == 1)
    def _():
        s = npairs % 2
        x_copy(2 * npairs, 1, s).wait()

        @pl.when(npairs >= 2)
        def _():
            y_copy(2 * npairs - 4, 2, s).wait()

        ffn(1, s)
        cp = y_copy(2 * npairs, 1, s)
        cp.start(priority=BLOCK_DMA_PRIORITY)
        cp.wait()

    @pl.when(has_next)
    def _():
        def rest(c, carry):
            w_start(e + 1, 1 - ws, c)
            return carry

        lax.fori_loop(jnp.minimum(2 * npairs, W_CHUNKS), W_CHUNKS, rest, 0)

    @pl.when(npairs >= 1)
    def _():
        y_copy(2 * npairs - 2, 2, (npairs - 1) % 2).wait()

    @pl.when((npairs >= 2) & (odd == 0))
    def _():
        y_copy(2 * npairs - 4, 2, npairs % 2).wait()

    @pl.when(e == pl.num_programs(0) - 1)
    def _():
        ybuf[0, pl.ds(0, blk_rows), :] = jnp.zeros((blk_rows, LANES), F32)

        def fill(b, carry):
            cp = y_copy(b, 1, 0)
            cp.start()
            cp.wait()
            return carry

        lax.fori_loop(nb, n_blocks - b0, fill, 0)


def _moe(xs, blk0, nblk, w_gate_up, b_gate_up, w_down, b_down):
    n_blocks = xs.shape[0] // (MOE_BLK * SUBLANES)
    n_exp, d_ff, d = w_down.shape
    grid_spec = pltpu.PrefetchScalarGridSpec(
        num_scalar_prefetch=2,
        grid=(n_exp,),
        in_specs=[pl.BlockSpec(memory_space=pl.ANY),
                  pl.BlockSpec(memory_space=pl.ANY),
                  pl.BlockSpec((None, 1, 2 * d_ff), lambda e, b0, nb: (e, 0, 0)),
                  pl.BlockSpec(memory_space=pl.ANY),
                  pl.BlockSpec((None, 1, d), lambda e, b0, nb: (e, 0, 0))],
        out_specs=pl.BlockSpec(memory_space=pl.ANY),
        scratch_shapes=[pltpu.VMEM((2, 2 * MOE_BLK * SUBLANES, LANES), F32),
                        pltpu.VMEM((2, 2 * MOE_BLK * SUBLANES, LANES), F32),
                        pltpu.VMEM((2, d, 2 * d_ff), F32),
                        pltpu.VMEM((2, d_ff, d), F32),
                        pltpu.VMEM((d, 2 * d_ff), BF16),
                        pltpu.VMEM((d_ff, d), BF16),
                        pltpu.SemaphoreType.DMA((2,)),
                        pltpu.SemaphoreType.DMA((2,)),
                        pltpu.SemaphoreType.DMA((2,))],
    )
    return pl.pallas_call(
        functools.partial(_moe_kernel, n_blocks),
        grid_spec=grid_spec,
        out_shape=jax.ShapeDtypeStruct(xs.shape, F32),
        compiler_params=pltpu.CompilerParams(dimension_semantics=("arbitrary",),
                                             vmem_limit_bytes=VMEM_LIMIT),
        name="moe_ffn",
    )(blk0, nblk, xs, w_gate_up, b_gate_up[:, None, :], w_down, b_down[:, None, :])


def _combine_kernel(destc_ref, destn_ref, x1_ref, mod_ref, route_ref, fg_ref, ys_hbm, o_ref, buf0, buf1, sem):
    buf = (buf0, buf1)
    i = pl.program_id(0)
    n_steps = pl.num_programs(0)
    d = x1_ref.shape[-1]

    def start_gather(dest_ref, s, t0=0, t1=TD):
        for t in range(t0, t1):
            for k in range(TOP_K):
                pltpu.make_async_copy(ys_hbm.at[_tile8(dest_ref[0, k, t])], buf[s].at[k, _tile(t)],
                                      sem.at[s]).start(priority=k % N_DMA_QUEUES)

    def wait_gather(s):
        for k in range(TOP_K):
            pltpu.make_async_copy(ys_hbm.at[pl.ds(0, TD * SUBLANES)], buf[s].at[k], sem.at[s]).wait()

    @pl.when(i == 0)
    def _():
        start_gather(destc_ref, 0)

    def step(s):
        wait_gather(s)
        gates = route_ref[...]
        gk = [gates[:, TOP_K + k:TOP_K + k + 1] for k in range(TOP_K)]
        gate2 = mod_ref[5:6, :]

        def residual(j):
            ls = slice(j * LANES, (j + 1) * LANES)
            rj = _chunk(j, TD)
            y = (gk[0] * buf[s][0, rj, :] + gk[1] * buf[s][1, rj, :]) + \
                (gk[2] * buf[s][2, rj, :] + gk[3] * buf[s][3, rj, :])
            return x1_ref[:, ls] + gate2[:, ls] * y

        ss = jnp.zeros((TD, 1), F32)
        per = TD // SUBLANES
        for j in range(SUBLANES):
            start_gather(destn_ref, 1 - s, j * per, (j + 1) * per)
            xj = residual(j)
            ss = ss + jnp.sum(xj * xj, axis=-1, keepdims=True)
        inv = lax.rsqrt(ss / d + EPS)
        for j in range(SUBLANES):
            ls = slice(j * LANES, (j + 1) * LANES)
            o_ref[:, ls] = residual(j) * inv * fg_ref[:, ls]

        @pl.when(i == n_steps - 1)
        def _():
            wait_gather(1 - s)

    _by_parity(i, step)


def _combine(x1, mod8, route, dest, ys, final_g):
    bsz, seq, d = x1.shape
    n_tok = bsz * seq
    n_steps = n_tok // TD
    per_seq = seq // TD
    return pl.pallas_call(
        _combine_kernel,
        grid=(n_steps,),
        in_specs=[_dest_block(lambda i: i),
                  _dest_block(lambda i: jnp.minimum(i + 1, n_steps - 1)),
                  pl.BlockSpec((TD, d), lambda i: (i, 0)),
                  pl.BlockSpec((None, 8, d), lambda i: (i // per_seq, 0, 0)),
                  pl.BlockSpec((TD, ROUTE_ROWS), lambda i: (i, 0)),
                  pl.BlockSpec((1, d), lambda i: (0, 0)),
                  pl.BlockSpec(memory_space=pl.ANY)],
        out_specs=pl.BlockSpec((TD, d), lambda i: (i, 0)),
        out_shape=jax.ShapeDtypeStruct((n_tok, d), F32),
        scratch_shapes=[pltpu.VMEM((TOP_K, TD * SUBLANES, LANES), F32),
                        pltpu.VMEM((TOP_K, TD * SUBLANES, LANES), F32),
                        pltpu.SemaphoreType.DMA((2,))],
        compiler_params=pltpu.CompilerParams(dimension_semantics=("arbitrary",),
                                             vmem_limit_bytes=VMEM_LIMIT),
        name="combine",
    )(dest, dest, x1.reshape(n_tok, d), mod8, route, final_g.reshape(1, d), ys).reshape(bsz, seq, d)


def _dest_kernel(route_ref, start_ref, dest_ref):
    erow = lax.broadcasted_iota(jnp.int32, (N_EXPERTS, TM), 0).astype(F32)
    for i in range(route_ref.shape[0]):
        r = route_ref[i]
        rows = []
        for k in range(TOP_K):
            base = jnp.sum(jnp.where(erow == r[k:k + 1, :], start_ref[...], 0.0), axis=0, keepdims=True)
            rows.append((base + r[2 * TOP_K + k:2 * TOP_K + k + 1, :]) * float(SUBLANES))
        dest_ref[i] = jnp.concatenate(rows, axis=0).astype(jnp.int32)


def _destinations(route_t, start_pad):
    n_tiles = route_t.shape[0]
    per_step = 8 if n_tiles % 8 == 0 else 1
    return pl.pallas_call(
        _dest_kernel,
        grid=(n_tiles // per_step,),
        in_specs=[pl.BlockSpec((per_step, ROUTE_ROWS, TM), lambda i: (i, 0, 0)),
                  pl.BlockSpec((N_EXPERTS, 1), lambda i: (0, 0))],
        out_specs=pl.BlockSpec((per_step, TOP_K, TM), lambda i: (i, 0, 0)),
        out_shape=jax.ShapeDtypeStruct((n_tiles, TOP_K, TM), jnp.int32),
        name="destinations",
    )(route_t, start_pad.astype(F32).reshape(N_EXPERTS, 1))


def _routing(counts_f, n_tok):
    nk = n_tok * TOP_K
    counts = counts_f[:, 0].astype(jnp.int32)
    padded = ((counts + MOE_BLK - 1) // MOE_BLK) * MOE_BLK
    pad_end = jnp.cumsum(padded)
    start_pad = pad_end - padded
    p_rows = ((nk + N_EXPERTS * (MOE_BLK - 1) + MOE_BLK - 1) // MOE_BLK) * MOE_BLK
    blocks_used = (pad_end[-1:] // MOE_BLK).astype(jnp.int32)
    return (start_pad, start_pad + counts, padded - counts, blocks_used,
            (start_pad // MOE_BLK).astype(jnp.int32), (padded // MOE_BLK).astype(jnp.int32), p_rows)


def kernel(x, c, ada_w, ada_b, norm_mix_g, w_in, lb_params, hgrn_norm_g, gmlp_ln_g, gmlp_ln_b, gmlp_ws, gmlp_bs,
           gmlp_norm_g, w_out, norm_ffn_g, router_w, router_b, w_gate_up, b_gate_up, w_down, b_down, final_g):
    assert ada_w.shape[0] == 1, "single-layer block"
    bsz, seq, d = x.shape
    assert d == SUBLANES * LANES and seq % TM == 0 and (bsz * seq) % TD == 0
    n_tok = bsz * seq
    mod = _modulation(c, ada_w[0], ada_b[0])
    mod8 = jnp.zeros((bsz, 8, d), F32).at[:, :6].set(mod.reshape(bsz, 6, d))
    x1, h2, route, counts = _mixer(x, mod8, norm_mix_g[0], w_in[0], lb_params, hgrn_norm_g[0], gmlp_ln_g[0],
                                   gmlp_ln_b[0], gmlp_ws[0], gmlp_bs[0], gmlp_norm_g[0], w_out[0],
                                   norm_ffn_g[0], router_w[0], router_b[0])
    start_pad, pad_first, pad_count, blocks_used, blk0, nblk, p_rows = _routing(counts, n_tok)
    dest = _destinations(route, start_pad)
    route = route.transpose(0, 2, 1).reshape(n_tok, ROUTE_ROWS)
    xs = _dispatch(h2.reshape(n_tok, d), dest, pad_first, pad_count, blocks_used, p_rows)
    ys = _moe(xs, blk0, nblk, w_gate_up[0], b_gate_up[0], w_down[0], b_down[0])
    return _combine(x1, mod8, route, dest, ys, final_g)
```

```python
import functools

import jax
import jax.numpy as jnp
from jax import lax
from jax.experimental import pallas as pl
from jax.experimental.pallas import tpu as pltpu

F32 = jnp.float32
BF16 = jnp.bfloat16

HGRN_HEADS = 4
HEAD_DIM = 128
HGRN_WIDTH = HGRN_HEADS * HEAD_DIM
HGRN_CHUNK = 64
HGRN_SUB = 16
GMLP_GROUPS = 4
GROUP_DIM = 128
GMLP_WIDTH = GMLP_GROUPS * GROUP_DIM
GMLP_CHUNK = 128
N_EXPERTS = 32
TOP_K = 4
SWIGLU_LIMIT = 7.0
SWIGLU_ALPHA = 1.702
EPS = 1e-6
LANES = 128
SUBLANES = 8
N_DMA_QUEUES = 2
BLOCK_DMA_PRIORITY = 1
ROUTE_ROWS = 16
W_CHUNKS = 8
DECAY_EXP_CLAMP = 60.0
TM = 512
MOE_BLK = 256
TD = 512
VMEM_LIMIT = 56 * 1024 * 1024


def _dot(a, b):
    return jnp.dot(a, b, preferred_element_type=F32)


def _dot_nt(a, b):
    return lax.dot_general(a, b, (((1,), (1,)), ((), ())), preferred_element_type=F32)


def _dot_tn(a, b):
    return lax.dot_general(a, b, (((0,), (0,)), ((), ())), preferred_element_type=F32)


def _rms(x):
    return x * lax.rsqrt(jnp.mean(x * x, axis=-1, keepdims=True) + EPS)


def _gelu(x):
    return 0.5 * x * (1.0 + lax.erf(x * 0.7071067811865476))


def _by_parity(i, fn):
    @pl.when(i % 2 == 0)
    def _():
        fn(0)

    @pl.when(i % 2 == 1)
    def _():
        fn(1)


def _mod_kernel(bsz, ct_ref, w_ref, b_ref, o_ref):
    ct = ct_ref[...]
    ca = ct * jax.nn.sigmoid(ct)
    w = w_ref[...]
    rows = [jnp.sum(w * ca[:, b:b + 1], axis=0, keepdims=True) for b in range(bsz)]
    rows.append(jnp.zeros((o_ref.shape[0] - bsz, w.shape[1]), F32))
    o_ref[...] = jnp.concatenate(rows, axis=0) + b_ref[...]


def _modulation(c, ada_w, ada_b):
    bsz, d = c.shape
    n_out = ada_w.shape[1]
    rows = SUBLANES
    assert bsz <= rows
    ct = jnp.zeros((d, LANES), F32).at[:, :bsz].set(c.T)
    tn = 1024
    out = pl.pallas_call(
        functools.partial(_mod_kernel, bsz),
        grid=(n_out // tn,),
        in_specs=[pl.BlockSpec((d, LANES), lambda j: (0, 0)),
                  pl.BlockSpec((d, tn), lambda j: (0, j)),
                  pl.BlockSpec((1, tn), lambda j: (0, j))],
        out_specs=pl.BlockSpec((rows, tn), lambda j: (0, j)),
        out_shape=jax.ShapeDtypeStruct((rows, n_out), F32),
        name="adaln_mod",
    )(ct, ada_w, ada_b.reshape(1, n_out))
    return out[:bsz]


def _mixer_kernel(x_ref, mod_ref, g1_ref, win_ref, lbp_ref, hg_ref, lng_ref, lnb_ref, ws_ref, bs_ref,
                  gng_ref, wout_ref, g2_ref, rwh_ref, rb_ref, earlier_ref,
                  x1_ref, h2_ref, route_ref, cnt_out_ref,
                  z_ref, y_ref, st_ref, cnt_ref):
    @pl.when(pl.program_id(1) == 0)
    def _():
        st_ref[...] = jnp.zeros_like(st_ref)

    x = x_ref[...]
    mod = mod_ref[...]
    h = _rms(x) * g1_ref[...]
    h = h * (1.0 + mod[1:2]) + mod[0:1]
    z_ref[...] = _dot(h.astype(BF16), win_ref[...])

    lbp = lbp_ref[...]
    lbe = jnp.exp(lbp - jnp.max(lbp, axis=0, keepdims=True))
    lb = lbe[0:1] / jnp.sum(lbe, axis=0, keepdims=True)
    hg = hg_ref[...]
    row = lax.broadcasted_iota(jnp.int32, (HGRN_CHUNK, HGRN_CHUNK), 0)
    col = lax.broadcasted_iota(jnp.int32, (HGRN_CHUNK, HGRN_CHUNK), 1)
    causal = col <= row
    crow = lax.broadcasted_iota(jnp.int32, (HGRN_CHUNK, HGRN_WIDTH), 0)
    n_sub = HGRN_CHUNK // HGRN_SUB

    def chunk_scores(c):
        rows = slice(c * HGRN_CHUNK, (c + 1) * HGRN_CHUNK)
        zq = z_ref[rows, 0:HGRN_WIDTH]
        zf = z_ref[rows, HGRN_WIDTH:2 * HGRN_WIDTH]
        zi = z_ref[rows, 2 * HGRN_WIDTH:3 * HGRN_WIDTH]
        zg = z_ref[rows, 3 * HGRN_WIDTH:4 * HGRN_WIDTH]
        f = lb + (1.0 - lb) * jax.nn.sigmoid(zf)
        logf = jnp.log(f)
        kk = 1.0 - f
        b = logf
        for sh in (1, 2, 4, 8, 16, 32):
            b = b + jnp.where(crow >= sh, pltpu.roll(b, sh, axis=0), 0.0)
        b_last = b[HGRN_CHUNK - 1:HGRN_CHUNK, :]
        qe = (zq * jnp.exp(b)).astype(BF16)
        kdec = (kk * jnp.exp(b_last - b)).astype(BF16)
        v = zi.astype(BF16)
        dec_last = jnp.exp(b_last)
        a_sub, k_sub = [], []
        for i in range(n_sub):
            lo, hi = i * HGRN_SUB, (i + 1) * HGRN_SUB
            bref = b[lo - 1:lo, :] if i > 0 else jnp.zeros((1, HGRN_WIDTH), F32)
            a_sub.append((zq[lo:hi] * jnp.exp(b[lo:hi] - bref)).astype(BF16))
            k_sub.append((kk * jnp.exp(jnp.minimum(bref - b, DECAY_EXP_CLAMP))).astype(BF16))
        sc = []
        for hd in range(HGRN_HEADS):
            ls = slice(hd * HEAD_DIM, (hd + 1) * HEAD_DIM)
            s_h = jnp.concatenate([_dot_nt(a_sub[i][:, ls], k_sub[i][:, ls]) for i in range(n_sub)], axis=0)
            sc.append(jnp.where(causal, s_h, 0.0).astype(BF16))
        return qe, kdec, v, dec_last, sc, zg * jax.nn.sigmoid(zg)

    def chunk_output(c, parts):
        rows = slice(c * HGRN_CHUNK, (c + 1) * HGRN_CHUNK)
        qe, kdec, v, dec_last, sc, silu_g = parts
        for hd in range(HGRN_HEADS):
            ls = slice(hd * HEAD_DIM, (hd + 1) * HEAD_DIM)
            st = st_ref[hd]
            o = _dot_nt(qe[:, ls], st.astype(BF16)) + _dot(sc[hd], v[:, ls])
            st_ref[hd] = st * dec_last[:, ls] + _dot_tn(v[:, ls], kdec[:, ls])
            o = _rms(o) * hg[:, ls]
            y_ref[rows, ls] = (o * silu_g[:, ls]).astype(BF16)

    row_g = lax.broadcasted_iota(jnp.int32, (GMLP_CHUNK, GMLP_CHUNK), 0)
    col_g = lax.broadcasted_iota(jnp.int32, (GMLP_CHUNK, GMLP_CHUNK), 1)
    gng = gng_ref[...]
    ws_c = [jnp.where(col_g <= row_g, ws_ref[g], 0.0).astype(BF16) for g in range(GMLP_GROUPS)]

    def gmlp_block(n):
        rs = slice(n * GMLP_CHUNK, (n + 1) * GMLP_CHUNK)
        u = _gelu(z_ref[rs, 4 * HGRN_WIDTH:4 * HGRN_WIDTH + GMLP_WIDTH])
        gv = _gelu(z_ref[rs, 4 * HGRN_WIDTH + GMLP_WIDTH:])
        mu = jnp.mean(gv, axis=-1, keepdims=True)
        gc = gv - mu
        var = jnp.mean(gc * gc, axis=-1, keepdims=True)
        vn = (gc * lax.rsqrt(var + EPS) * lng_ref[...] + lnb_ref[...]).astype(BF16)
        for g in range(GMLP_GROUPS):
            ls = slice(g * GROUP_DIM, (g + 1) * GROUP_DIM)
            sv = _dot(ws_c[g], vn[:, ls]) + bs_ref[g]
            yy = _rms(u[:, ls] * sv) * gng[:, ls]
            y_ref[rs, HGRN_WIDTH + g * GROUP_DIM:HGRN_WIDTH + (g + 1) * GROUP_DIM] = yy.astype(BF16)

    n_chunks = TM // HGRN_CHUNK
    per_blk = GMLP_CHUNK // HGRN_CHUNK
    parts = [chunk_scores(c) for c in range(n_chunks)]
    for c in range(n_chunks):
        chunk_output(c, parts[c])
        if c % per_blk == per_blk - 1:
            gmlp_block(c // per_blk)

    hh, hl = [], []
    for r in range(2):
        rs = slice(r * (TM // 2), (r + 1) * (TM // 2))
        x1 = x_ref[rs, :] + mod[2:3] * _dot(y_ref[rs, :], wout_ref[...])
        x1_ref[rs, :] = x1
        h2 = _rms(x1) * g2_ref[...]
        h2 = h2 * (1.0 + mod[4:5]) + mod[3:4]
        h2_ref[rs, :] = h2
        hh.append(h2.astype(BF16))
        hl.append((h2 - hh[-1].astype(F32)).astype(BF16))
    hh = jnp.concatenate(hh, axis=0)
    hl = jnp.concatenate(hl, axis=0)

    rw2 = rwh_ref[...]
    by_hh = _dot_nt(rw2, hh)
    logits = by_hh[:N_EXPERTS] + (_dot_nt(rw2[:N_EXPERTS], hl) + by_hh[N_EXPERTS:]) + rb_ref[...]
    erow = lax.broadcasted_iota(jnp.int32, (N_EXPERTS, TM), 0)
    vals, idxs = [], []
    for _ in range(TOP_K):
        m = jnp.max(logits, axis=0, keepdims=True)
        idx = jnp.min(jnp.where(logits == m, erow, N_EXPERTS), axis=0, keepdims=True)
        vals.append(m)
        idxs.append(idx)
        logits = jnp.where(erow == idx, -jnp.inf, logits)
    es = [jnp.exp(v - vals[0]) for v in vals]
    tot = (es[0] + es[1]) + (es[2] + es[3])

    @pl.when((pl.program_id(0) == 0) & (pl.program_id(1) == 0))
    def _():
        cnt_ref[...] = jnp.zeros_like(cnt_ref)

    hot = [erow == idxs[k] for k in range(TOP_K)]
    picked = jnp.where((hot[0] | hot[1]) | (hot[2] | hot[3]), 1.0, 0.0)
    seen = _dot(picked.astype(BF16), earlier_ref[...]) + cnt_ref[:, 0:1]
    cnt_ref[...] = cnt_ref[...] + jnp.sum(picked, axis=1, keepdims=True)
    cnt_out_ref[...] = cnt_ref[...]
    ranks = [jnp.sum(jnp.where(hot[k], seen, 0.0), axis=0, keepdims=True) for k in range(TOP_K)]
    route_ref[...] = jnp.concatenate([i.astype(F32) for i in idxs] + [e / tot for e in es] + ranks +
                                     [jnp.zeros((ROUTE_ROWS - 3 * TOP_K, TM), F32)], axis=0)


def _mixer(x, mod8, norm_mix_g, w_in, lb_params, hgrn_norm_g, gmlp_ln_g, gmlp_ln_b, gmlp_ws, gmlp_bs,
           gmlp_norm_g, w_out, norm_ffn_g, router_w, router_b):
    bsz, seq, d = x.shape
    n_in = w_in.shape[1]
    rw = router_w.T
    rwh = rw.astype(BF16)
    rw2 = jnp.concatenate([rwh, (rw - rwh.astype(F32)).astype(BF16)], axis=0)
    rb = router_b.reshape(N_EXPERTS, 1)
    pos = jnp.arange(TM, dtype=jnp.int32)
    earlier = (pos[:, None] < pos[None, :]).astype(BF16)
    const = lambda *shape: pl.BlockSpec(shape, lambda b, i: (0,) * len(shape))
    tile = lambda w: pl.BlockSpec((None, TM, w), lambda b, i: (b, i, 0))
    per_seq = seq // TM
    return pl.pallas_call(
        _mixer_kernel,
        grid=(bsz, seq // TM),
        in_specs=[tile(d),
                  pl.BlockSpec((None, 8, d), lambda b, i: (b, 0, 0)),
                  const(1, d), const(d, n_in), const(2, HGRN_WIDTH), const(1, HGRN_WIDTH),
                  const(1, GMLP_WIDTH), const(1, GMLP_WIDTH),
                  const(GMLP_GROUPS, GMLP_CHUNK, GMLP_CHUNK), const(GMLP_GROUPS, GMLP_CHUNK, 1),
                  const(1, GMLP_WIDTH), const(d, d), const(1, d),
                  const(2 * N_EXPERTS, d), const(N_EXPERTS, 1), const(TM, TM)],
        out_specs=[tile(d), tile(d),
                   pl.BlockSpec((None, ROUTE_ROWS, TM), lambda b, i: (b * per_seq + i, 0, 0)),
                   const(N_EXPERTS, LANES)],
        out_shape=[jax.ShapeDtypeStruct((bsz, seq, d), F32),
                   jax.ShapeDtypeStruct((bsz, seq, d), F32),
                   jax.ShapeDtypeStruct((bsz * per_seq, ROUTE_ROWS, TM), F32),
                   jax.ShapeDtypeStruct((N_EXPERTS, LANES), F32)],
        scratch_shapes=[pltpu.VMEM((TM, n_in), F32),
                        pltpu.VMEM((TM, d), BF16),
                        pltpu.VMEM((HGRN_HEADS, HEAD_DIM, HEAD_DIM), F32),
                        pltpu.VMEM((N_EXPERTS, LANES), F32)],
        compiler_params=pltpu.CompilerParams(dimension_semantics=("arbitrary", "arbitrary"),
                                             vmem_limit_bytes=VMEM_LIMIT),
        name="mixer",
    )(x, mod8, norm_mix_g.reshape(1, d), w_in.astype(BF16), lb_params, hgrn_norm_g.reshape(1, -1),
      gmlp_ln_g.reshape(1, -1), gmlp_ln_b.reshape(1, -1), gmlp_ws, gmlp_bs[:, :, None],
      gmlp_norm_g.reshape(1, -1), w_out.astype(BF16), norm_ffn_g.reshape(1, d), rw2, rb, earlier)


def _tile(r):
    return pl.ds(r * SUBLANES, SUBLANES)


def _tile8(r8):
    return pl.ds(pl.multiple_of(r8, SUBLANES), SUBLANES)


def _chunk(j, n):
    return pl.ds(j, n, stride=SUBLANES)


def _dispatch_kernel(n_blocks, n_pad_rows, pad0_ref, padn_ref, used_ref, dest_ref, h2_ref, xs_hbm,
                     rows, zeros, sem, zsem):
    i = pl.program_id(0)
    n_steps = pl.num_programs(0)
    blk_rows = MOE_BLK * SUBLANES

    def wait_step(s):
        for _ in range(TOP_K):
            pltpu.make_async_copy(rows.at[s], xs_hbm.at[pl.ds(0, TD * SUBLANES)], sem.at[s]).wait()

    def zero_fill(first_row, n_rows):
        n = n_rows * SUBLANES
        pltpu.make_async_copy(zeros.at[pl.ds(0, n)], xs_hbm.at[pl.ds(pl.multiple_of(first_row * SUBLANES, SUBLANES), n)],
                              zsem.at[0]).start()

    @pl.when(i == 0)
    def _():
        zeros[...] = jnp.zeros_like(zeros)
        for e in range(N_EXPERTS):
            first, count = pad0_ref[e], padn_ref[e]
            for bit in reversed(range(MOE_BLK.bit_length() - 1)):
                @pl.when((count >> bit) & 1 == 1)
                def _():
                    zero_fill(first + ((count >> (bit + 1)) << (bit + 1)), 1 << bit)

        def fill_block(b, carry):
            zero_fill(b * MOE_BLK, MOE_BLK)
            return carry

        lax.fori_loop(used_ref[0], n_blocks, fill_block, 0)

    def step(s):
        @pl.when(i >= 2)
        def _():
            wait_step(s)

        for j in range(SUBLANES):
            rows[s, _chunk(j, TD), :] = h2_ref[:, j * LANES:(j + 1) * LANES]
        for t in range(TD):
            for k in range(TOP_K):
                pltpu.make_async_copy(rows.at[s, _tile(t)], xs_hbm.at[_tile8(dest_ref[0, k, t])],
                                      sem.at[s]).start(priority=k % N_DMA_QUEUES)

        @pl.when(i == n_steps - 1)
        def _():
            wait_step(s)

            @pl.when(i >= 1)
            def _():
                wait_step(1 - s)

            for _ in range(n_pad_rows // MOE_BLK):
                pltpu.make_async_copy(zeros, xs_hbm.at[pl.ds(0, blk_rows)], zsem.at[0]).wait()

    _by_parity(i, step)


def _dest_block(step_of):
    per_tile = TM // TD
    return pl.BlockSpec((1, TOP_K, TD), lambda i, *_: (step_of(i) // per_tile, 0, step_of(i) % per_tile),
                        memory_space=pltpu.SMEM)


def _dispatch(h2, dest, pad_first, pad_count, blocks_used, p_rows):
    n_tok, d = h2.shape
    n_steps = n_tok // TD
    n_pad_rows = p_rows - n_tok * TOP_K
    assert n_pad_rows % MOE_BLK == 0
    grid_spec = pltpu.PrefetchScalarGridSpec(
        num_scalar_prefetch=3,
        grid=(n_steps,),
        in_specs=[_dest_block(lambda i: i),
                  pl.BlockSpec((TD, d), lambda i, *_: (i, 0))],
        out_specs=pl.BlockSpec(memory_space=pl.ANY),
        scratch_shapes=[pltpu.VMEM((2, TD * SUBLANES, LANES), F32),
                        pltpu.VMEM((MOE_BLK * SUBLANES, LANES), F32),
                        pltpu.SemaphoreType.DMA((2,)),
                        pltpu.SemaphoreType.DMA((1,))],
    )
    return pl.pallas_call(
        functools.partial(_dispatch_kernel, p_rows // MOE_BLK, n_pad_rows),
        grid_spec=grid_spec,
        out_shape=jax.ShapeDtypeStruct((p_rows * SUBLANES, LANES), F32),
        compiler_params=pltpu.CompilerParams(dimension_semantics=("arbitrary",)),
        name="dispatch",
    )(pad_first, pad_count, blocks_used, dest, h2)


def _moe_kernel(n_blocks, blk0_ref, nblk_ref, xs_hbm, wgu_hbm, bgu_ref, wd_hbm, bd_ref, ys_hbm,
                xbuf, ybuf, wgu_f, wd_f, wgu_bf, wd_bf, isem, osem, wsem):
    e = pl.program_id(0)
    n_exp = pl.num_programs(0)
    nb = nblk_ref[e]
    b0 = blk0_ref[e]
    d, d_ff = wd_f.shape[2], wd_f.shape[1]
    blk_rows = MOE_BLK * SUBLANES
    ws = e % 2
    gu_rows, d_rows = d // W_CHUNKS, d_ff // W_CHUNKS

    npairs = nb // 2
    odd = nb - 2 * npairs

    def rows_of(b, n):
        return pl.ds(pl.multiple_of((b0 + b) * blk_rows, blk_rows), n * blk_rows)

    def x_copy(b, n, s):
        return pltpu.make_async_copy(xs_hbm.at[rows_of(b, n)], xbuf.at[s, pl.ds(0, n * blk_rows)], isem.at[s])

    def y_copy(b, n, s):
        return pltpu.make_async_copy(ybuf.at[s, pl.ds(0, n * blk_rows)], ys_hbm.at[rows_of(b, n)], osem.at[s])

    def w_start(ex, slot, c):
        r_gu = pl.ds(pl.multiple_of(c * gu_rows, gu_rows), gu_rows)
        r_d = pl.ds(pl.multiple_of(c * d_rows, d_rows), d_rows)
        pltpu.make_async_copy(wgu_hbm.at[ex, r_gu], wgu_f.at[slot, r_gu], wsem.at[slot]).start()
        pltpu.make_async_copy(wd_hbm.at[ex, r_d], wd_f.at[slot, r_d], wsem.at[slot]).start()

    def w_wait(slot):
        pltpu.make_async_copy(wgu_hbm.at[0], wgu_f.at[slot], wsem.at[slot]).wait()
        pltpu.make_async_copy(wd_hbm.at[0], wd_f.at[slot], wsem.at[slot]).wait()

    @pl.when(e == 0)
    def _():
        for c in range(W_CHUNKS):
            w_start(0, 0, c)

    @pl.when(npairs > 0)
    def _():
        x_copy(0, 2, 0).start(priority=BLOCK_DMA_PRIORITY)

    @pl.when((npairs == 0) & (odd == 1))
    def _():
        x_copy(0, 1, 0).start(priority=BLOCK_DMA_PRIORITY)

    w_wait(ws)
    wgu_bf[...] = wgu_f[ws].astype(BF16)
    wd_bf[...] = wd_f[ws].astype(BF16)
    has_next = e + 1 < n_exp

    def ffn(n, s):
        rows = n * MOE_BLK
        xb = jnp.concatenate([xbuf[s, _chunk(j, rows), :] for j in range(SUBLANES)], axis=-1).astype(BF16)
        gu = _dot(xb, wgu_bf[...]) + bgu_ref[...]
        gate = jnp.minimum(gu[:, :d_ff], SWIGLU_LIMIT)
        up = jnp.clip(gu[:, d_ff:], -SWIGLU_LIMIT, SWIGLU_LIMIT)
        glu = gate * jax.nn.sigmoid(SWIGLU_ALPHA * gate)
        yb = _dot(((up + 1.0) * glu).astype(BF16), wd_bf[...]) + bd_ref[...]
        for j in range(SUBLANES):
            ybuf[s, _chunk(j, rows), :] = yb[:, j * LANES:(j + 1) * LANES]

    def pair_body(p, carry):
        s = p % 2

        @pl.when(p + 1 < npairs)
        def _():
            x_copy(2 * p + 2, 2, 1 - s).start(priority=BLOCK_DMA_PRIORITY)

        @pl.when((p + 1 == npairs) & (odd == 1))
        def _():
            x_copy(2 * p + 2, 1, 1 - s).start(priority=BLOCK_DMA_PRIORITY)

        @pl.when(has_next & (2 * p < W_CHUNKS))
        def _():
            w_start(e + 1, 1 - ws, 2 * p)
            w_start(e + 1, 1 - ws, 2 * p + 1)

        x_copy(2 * p, 2, s).wait()

        @pl.when(p >= 2)
        def _():
            y_copy(2 * p - 4, 2, s).wait()

        ffn(2, s)
        y_copy(2 * p, 2, s).start(priority=BLOCK_DMA_PRIORITY)
        return carry

    lax.fori_loop(0, npairs, pair_body, 0)

    @pl.when(odd == 1)
    def _():
        s = npairs % 2
        x_copy(2 * npairs, 1, s).wait()

        @pl.when(npairs >= 2)
        def _():
            y_copy(2 * npairs - 4, 2, s).wait()

        ffn(1, s)
        cp = y_copy(2 * npairs, 1, s)
        cp.start(priority=BLOCK_DMA_PRIORITY)
        cp.wait()

    @pl.when(has_next)
    def _():
        def rest(c, carry):
            w_start(e + 1, 1 - ws, c)
            return carry

        lax.fori_loop(jnp.minimum(2 * npairs, W_CHUNKS), W_CHUNKS, rest, 0)

    @pl.when(npairs >= 1)
    def _():
        y_copy(2 * npairs - 2, 2, (npairs - 1) % 2).wait()

    @pl.when((npairs >= 2) & (odd == 0))
    def _():
        y_copy(2 * npairs - 4, 2, npairs % 2).wait()

    @pl.when(e == pl.num_programs(0) - 1)
    def _():
        ybuf[0, pl.ds(0, blk_rows), :] = jnp.zeros((blk_rows, LANES), F32)

        def fill(b, carry):
            cp = y_copy(b, 1, 0)
            cp.start()
            cp.wait()
            return carry

        lax.fori_loop(nb, n_blocks - b0, fill, 0)


def _moe(xs, blk0, nblk, w_gate_up, b_gate_up, w_down, b_down):
    n_blocks = xs.shape[0] // (MOE_BLK * SUBLANES)
    n_exp, d_ff, d = w_down.shape
    grid_spec = pltpu.PrefetchScalarGridSpec(
        num_scalar_prefetch=2,
        grid=(n_exp,),
        in_specs=[pl.BlockSpec(memory_space=pl.ANY),
                  pl.BlockSpec(memory_space=pl.ANY),
                  pl.BlockSpec((None, 1, 2 * d_ff), lambda e, b0, nb: (e, 0, 0)),
                  pl.BlockSpec(memory_space=pl.ANY),
                  pl.BlockSpec((None, 1, d), lambda e, b0, nb: (e, 0, 0))],
        out_specs=pl.BlockSpec(memory_space=pl.ANY),
        scratch_shapes=[pltpu.VMEM((2, 2 * MOE_BLK * SUBLANES, LANES), F32),
                        pltpu.VMEM((2, 2 * MOE_BLK * SUBLANES, LANES), F32),
                        pltpu.VMEM((2, d, 2 * d_ff), F32),
                        pltpu.VMEM((2, d_ff, d), F32),
                        pltpu.VMEM((d, 2 * d_ff), BF16),
                        pltpu.VMEM((d_ff, d), BF16),
                        pltpu.SemaphoreType.DMA((2,)),
                        pltpu.SemaphoreType.DMA((2,)),
                        pltpu.SemaphoreType.DMA((2,))],
    )
    return pl.pallas_call(
        functools.partial(_moe_kernel, n_blocks),
        grid_spec=grid_spec,
        out_shape=jax.ShapeDtypeStruct(xs.shape, F32),
        compiler_params=pltpu.CompilerParams(dimension_semantics=("arbitrary",),
                                             vmem_limit_bytes=VMEM_LIMIT),
        name="moe_ffn",
    )(blk0, nblk, xs, w_gate_up, b_gate_up[:, None, :], w_down, b_down[:, None, :])


def _combine_kernel(destc_ref, destn_ref, x1_ref, mod_ref, route_ref, fg_ref, ys_hbm, o_ref, buf0, buf1, sem):
    buf = (buf0, buf1)
    i = pl.program_id(0)
    n_steps = pl.num_programs(0)
    d = x1_ref.shape[-1]

    def start_gather(dest_ref, s, t0=0, t1=TD):
        for t in range(t0, t1):
            for k in range(TOP_K):
                pltpu.make_async_copy(ys_hbm.at[_tile8(dest_ref[0, k, t])], buf[s].at[k, _tile(t)],
                                      sem.at[s]).start(priority=k % N_DMA_QUEUES)

    def wait_gather(s):
        for k in range(TOP_K):
            pltpu.make_async_copy(ys_hbm.at[pl.ds(0, TD * SUBLANES)], buf[s].at[k], sem.at[s]).wait()

    @pl.when(i == 0)
    def _():
        start_gather(destc_ref, 0)

    def step(s):
        wait_gather(s)
        gates = route_ref[...]
        gk = [gates[:, TOP_K + k:TOP_K + k + 1] for k in range(TOP_K)]
        gate2 = mod_ref[5:6, :]

        def residual(j):
            ls = slice(j * LANES, (j + 1) * LANES)
            rj = _chunk(j, TD)
            y = (gk[0] * buf[s][0, rj, :] + gk[1] * buf[s][1, rj, :]) + \
                (gk[2] * buf[s][2, rj, :] + gk[3] * buf[s][3, rj, :])
            return x1_ref[:, ls] + gate2[:, ls] * y

        ss = jnp.zeros((TD, 1), F32)
        per = TD // SUBLANES
        for j in range(SUBLANES):
            start_gather(destn_ref, 1 - s, j * per, (j + 1) * per)
            xj = residual(j)
            ss = ss + jnp.sum(xj * xj, axis=-1, keepdims=True)
        inv = lax.rsqrt(ss / d + EPS)
        for j in range(SUBLANES):
            ls = slice(j * LANES, (j + 1) * LANES)
            o_ref[:, ls] = residual(j) * inv * fg_ref[:, ls]

        @pl.when(i == n_steps - 1)
        def _():
            wait_gather(1 - s)

    _by_parity(i, step)


def _combine(x1, mod8, route, dest, ys, final_g):
    bsz, seq, d = x1.shape
    n_tok = bsz * seq
    n_steps = n_tok // TD
    per_seq = seq // TD
    return pl.pallas_call(
        _combine_kernel,
        grid=(n_steps,),
        in_specs=[_dest_block(lambda i: i),
                  _dest_block(lambda i: jnp.minimum(i + 1, n_steps - 1)),
                  pl.BlockSpec((TD, d), lambda i: (i, 0)),
                  pl.BlockSpec((None, 8, d), lambda i: (i // per_seq, 0, 0)),
                  pl.BlockSpec((TD, ROUTE_ROWS), lambda i: (i, 0)),
                  pl.BlockSpec((1, d), lambda i: (0, 0)),
                  pl.BlockSpec(memory_space=pl.ANY)],
        out_specs=pl.BlockSpec((TD, d), lambda i: (i, 0)),
        out_shape=jax.ShapeDtypeStruct((n_tok, d), F32),
        scratch_shapes=[pltpu.VMEM((TOP_K, TD * SUBLANES, LANES), F32),
                        pltpu.VMEM((TOP_K, TD * SUBLANES, LANES), F32),
                        pltpu.SemaphoreType.DMA((2,))],
        compiler_params=pltpu.CompilerParams(dimension_semantics=("arbitrary",),
                                             vmem_limit_bytes=VMEM_LIMIT),
        name="combine",
    )(dest, dest, x1.reshape(n_tok, d), mod8, route, final_g.reshape(1, d), ys).reshape(bsz, seq, d)


def _dest_kernel(route_ref, start_ref, dest_ref):
    erow = lax.broadcasted_iota(jnp.int32, (N_EXPERTS, TM), 0).astype(F32)
    for i in range(route_ref.shape[0]):
        r = route_ref[i]
        rows = []
        for k in range(TOP_K):
            base = jnp.sum(jnp.where(erow == r[k:k + 1, :], start_ref[...], 0.0), axis=0, keepdims=True)
            rows.append((base + r[2 * TOP_K + k:2 * TOP_K + k + 1, :]) * float(SUBLANES))
        dest_ref[i] = jnp.concatenate(rows, axis=0).astype(jnp.int32)


def _destinations(route_t, start_pad):
    n_tiles = route_t.shape[0]
    per_step = 8 if n_tiles % 8 == 0 else 1
    return pl.pallas_call(
        _dest_kernel,
        grid=(n_tiles // per_step,),
        in_specs=[pl.BlockSpec((per_step, ROUTE_ROWS, TM), lambda i: (i, 0, 0)),
                  pl.BlockSpec((N_EXPERTS, 1), lambda i: (0, 0))],
        out_specs=pl.BlockSpec((per_step, TOP_K, TM), lambda i: (i, 0, 0)),
        out_shape=jax.ShapeDtypeStruct((n_tiles, TOP_K, TM), jnp.int32),
        name="destinations",
    )(route_t, start_pad.astype(F32).reshape(N_EXPERTS, 1))


def _routing(counts_f, n_tok):
    nk = n_tok * TOP_K
    counts = counts_f[:, 0].astype(jnp.int32)
    padded = ((counts + MOE_BLK - 1) // MOE_BLK) * MOE_BLK
    pad_end = jnp.cumsum(padded)
    start_pad = pad_end - padded
    p_rows = ((nk + N_EXPERTS * (MOE_BLK - 1) + MOE_BLK - 1) // MOE_BLK) * MOE_BLK
    blocks_used = (pad_end[-1:] // MOE_BLK).astype(jnp.int32)
    return (start_pad, start_pad + counts, padded - counts, blocks_used,
            (start_pad // MOE_BLK).astype(jnp.int32), (padded // MOE_BLK).astype(jnp.int32), p_rows)


def kernel(x, c, ada_w, ada_b, norm_mix_g, w_in, lb_params, hgrn_norm_g, gmlp_ln_g, gmlp_ln_b, gmlp_ws, gmlp_bs,
           gmlp_norm_g, w_out, norm_ffn_g, router_w, router_b, w_gate_up, b_gate_up, w_down, b_down, final_g):
    assert ada_w.shape[0] == 1, "single-layer block"
    bsz, seq, d = x.shape
    assert d == SUBLANES * LANES and seq % TM == 0 and (bsz * seq) % TD == 0
    n_tok = bsz * seq
    mod = _modulation(c, ada_w[0], ada_b[0])
    mod8 = jnp.zeros((bsz, 8, d), F32).at[:, :6].set(mod.reshape(bsz, 6, d))
    x1, h2, route, counts = _mixer(x, mod8, norm_mix_g[0], w_in[0], lb_params, hgrn_norm_g[0], gmlp_ln_g[0],
                                   gmlp_ln_b[0], gmlp_ws[0], gmlp_bs[0], gmlp_norm_g[0], w_out[0],
                                   norm_ffn_g[0], router_w[0], router_b[0])
    start_pad, pad_first, pad_count, blocks_used, blk0, nblk, p_rows = _routing(counts, n_tok)
    dest = _destinations(route, start_pad)
    route = route.transpose(0, 2, 1).reshape(n_tok, ROUTE_ROWS)
    xs = _dispatch(h2.reshape(n_tok, d), dest, pad_first, pad_count, blocks_used, p_rows)
    ys = _moe(xs, blk0, nblk, w_gate_up[0], b_gate_up[0], w_down[0], b_down[0])
    return _combine(x1, mod8, route, dest, ys, final_g)
```

```python
import functools

import jax
import jax.numpy as jnp
from jax import lax
from jax.experimental import pallas as pl
from jax.experimental.pallas import tpu as pltpu

F32 = jnp.float32
BF16 = jnp.bfloat16

HGRN_HEADS = 4
HEAD_DIM = 128
HGRN_WIDTH = HGRN_HEADS * HEAD_DIM
HGRN_CHUNK = 64
HGRN_SUB = 16
GMLP_GROUPS = 4
GROUP_DIM = 128
GMLP_WIDTH = GMLP_GROUPS * GROUP_DIM
GMLP_CHUNK = 128
N_EXPERTS = 32
TOP_K = 4
SWIGLU_LIMIT = 7.0
SWIGLU_ALPHA = 1.702
EPS = 1e-6
LANES = 128
SUBLANES = 8
N_DMA_QUEUES = 2
BLOCK_DMA_PRIORITY = 1
ROUTE_ROWS = 16
W_CHUNKS = 8
DECAY_EXP_CLAMP = 60.0
TM = 1024
MOE_BLK = 256
TD = 512
VMEM_LIMIT = 56 * 1024 * 1024


def _dot(a, b):
    return jnp.dot(a, b, preferred_element_type=F32)


def _dot_nt(a, b):
    return lax.dot_general(a, b, (((1,), (1,)), ((), ())), preferred_element_type=F32)


def _dot_tn(a, b):
    return lax.dot_general(a, b, (((0,), (0,)), ((), ())), preferred_element_type=F32)


def _rms(x):
    return x * lax.rsqrt(jnp.mean(x * x, axis=-1, keepdims=True) + EPS)


def _gelu(x):
    return 0.5 * x * (1.0 + lax.erf(x * 0.7071067811865476))


def _by_parity(i, fn):
    @pl.when(i % 2 == 0)
    def _():
        fn(0)

    @pl.when(i % 2 == 1)
    def _():
        fn(1)


def _mod_kernel(c_ref, w_ref, b_ref, o_ref):
    c = c_ref[...]
    ca = c * jax.nn.sigmoid(c)
    o_ref[...] = jnp.dot(ca, w_ref[...], precision=lax.Precision.HIGHEST,
                         preferred_element_type=F32) + b_ref[...]


def _modulation(c, ada_w, ada_b):
    bsz, d = c.shape
    n_out = ada_w.shape[1]
    rows = 8
    c_pad = jnp.zeros((rows, d), F32).at[:bsz].set(c)
    tn = 2048
    out = pl.pallas_call(
        _mod_kernel,
        grid=(n_out // tn,),
        in_specs=[pl.BlockSpec((rows, d), lambda j: (0, 0)),
                  pl.BlockSpec((d, tn), lambda j: (0, j)),
                  pl.BlockSpec((1, tn), lambda j: (0, j))],
        out_specs=pl.BlockSpec((rows, tn), lambda j: (0, j)),
        out_shape=jax.ShapeDtypeStruct((rows, n_out), F32),
        name="adaln_mod",
    )(c_pad, ada_w, ada_b.reshape(1, n_out))
    return out[:bsz]


def _mixer_kernel(x_ref, mod_ref, g1_ref, win_ref, lbp_ref, hg_ref, lng_ref, lnb_ref, ws_ref, bs_ref,
                  gng_ref, wout_ref, g2_ref, rwh_ref, rb_ref, earlier_ref,
                  x1_ref, h2_ref, route_ref, cnt_out_ref,
                  z_ref, y_ref, st_ref, cnt_ref):
    @pl.when(pl.program_id(1) == 0)
    def _():
        st_ref[...] = jnp.zeros_like(st_ref)

    x = x_ref[...]
    mod = mod_ref[...]
    h = _rms(x) * g1_ref[...]
    h = h * (1.0 + mod[1:2]) + mod[0:1]
    z_ref[...] = _dot(h.astype(BF16), win_ref[...])

    lbp = lbp_ref[...]
    lbe = jnp.exp(lbp - jnp.max(lbp, axis=0, keepdims=True))
    lb = lbe[0:1] / jnp.sum(lbe, axis=0, keepdims=True)
    hg = hg_ref[...]
    row = lax.broadcasted_iota(jnp.int32, (HGRN_CHUNK, HGRN_CHUNK), 0)
    col = lax.broadcasted_iota(jnp.int32, (HGRN_CHUNK, HGRN_CHUNK), 1)
    causal = col <= row
    crow = lax.broadcasted_iota(jnp.int32, (HGRN_CHUNK, HGRN_WIDTH), 0)
    n_sub = HGRN_CHUNK // HGRN_SUB

    def chunk_scores(c):
        rows = slice(c * HGRN_CHUNK, (c + 1) * HGRN_CHUNK)
        zq = z_ref[rows, 0:HGRN_WIDTH]
        zf = z_ref[rows, HGRN_WIDTH:2 * HGRN_WIDTH]
        zi = z_ref[rows, 2 * HGRN_WIDTH:3 * HGRN_WIDTH]
        zg = z_ref[rows, 3 * HGRN_WIDTH:4 * HGRN_WIDTH]
        f = lb + (1.0 - lb) * jax.nn.sigmoid(zf)
        logf = jnp.log(f)
        kk = 1.0 - f
        b = logf
        for sh in (1, 2, 4, 8, 16, 32):
            b = b + jnp.where(crow >= sh, pltpu.roll(b, sh, axis=0), 0.0)
        b_last = b[HGRN_CHUNK - 1:HGRN_CHUNK, :]
        qe = (zq * jnp.exp(b)).astype(BF16)
        kdec = (kk * jnp.exp(b_last - b)).astype(BF16)
        v = zi.astype(BF16)
        dec_last = jnp.exp(b_last)
        a_sub, k_sub = [], []
        for i in range(n_sub):
            lo, hi = i * HGRN_SUB, (i + 1) * HGRN_SUB
            bref = b[lo - 1:lo, :] if i > 0 else jnp.zeros((1, HGRN_WIDTH), F32)
            a_sub.append((zq[lo:hi] * jnp.exp(b[lo:hi] - bref)).astype(BF16))
            k_sub.append((kk * jnp.exp(jnp.minimum(bref - b, DECAY_EXP_CLAMP))).astype(BF16))
        sc = []
        for hd in range(HGRN_HEADS):
            ls = slice(hd * HEAD_DIM, (hd + 1) * HEAD_DIM)
            s_h = jnp.concatenate([_dot_nt(a_sub[i][:, ls], k_sub[i][:, ls]) for i in range(n_sub)], axis=0)
            sc.append(jnp.where(causal, s_h, 0.0).astype(BF16))
        return qe, kdec, v, dec_last, sc, zg * jax.nn.sigmoid(zg)

    def chunk_output(c, parts):
        rows = slice(c * HGRN_CHUNK, (c + 1) * HGRN_CHUNK)
        qe, kdec, v, dec_last, sc, silu_g = parts
        for hd in range(HGRN_HEADS):
            ls = slice(hd * HEAD_DIM, (hd + 1) * HEAD_DIM)
            st = st_ref[hd]
            o = _dot_nt(qe[:, ls], st.astype(BF16)) + _dot(sc[hd], v[:, ls])
            st_ref[hd] = st * dec_last[:, ls] + _dot_tn(v[:, ls], kdec[:, ls])
            o = _rms(o) * hg[:, ls]
            y_ref[rows, ls] = (o * silu_g[:, ls]).astype(BF16)

    row_g = lax.broadcasted_iota(jnp.int32, (GMLP_CHUNK, GMLP_CHUNK), 0)
    col_g = lax.broadcasted_iota(jnp.int32, (GMLP_CHUNK, GMLP_CHUNK), 1)
    gng = gng_ref[...]
    ws_c = [jnp.where(col_g <= row_g, ws_ref[g], 0.0).astype(BF16) for g in range(GMLP_GROUPS)]

    def gmlp_block(n):
        rs = slice(n * GMLP_CHUNK, (n + 1) * GMLP_CHUNK)
        u = _gelu(z_ref[rs, 4 * HGRN_WIDTH:4 * HGRN_WIDTH + GMLP_WIDTH])
        gv = _gelu(z_ref[rs, 4 * HGRN_WIDTH + GMLP_WIDTH:])
        mu = jnp.mean(gv, axis=-1, keepdims=True)
        gc = gv - mu
        var = jnp.mean(gc * gc, axis=-1, keepdims=True)
        vn = (gc * lax.rsqrt(var + EPS) * lng_ref[...] + lnb_ref[...]).astype(BF16)
        for g in range(GMLP_GROUPS):
            ls = slice(g * GROUP_DIM, (g + 1) * GROUP_DIM)
            sv = _dot(ws_c[g], vn[:, ls]) + bs_ref[g]
            yy = _rms(u[:, ls] * sv) * gng[:, ls]
            y_ref[rs, HGRN_WIDTH + g * GROUP_DIM:HGRN_WIDTH + (g + 1) * GROUP_DIM] = yy.astype(BF16)

    n_chunks = TM // HGRN_CHUNK
    per_blk = GMLP_CHUNK // HGRN_CHUNK
    parts = [chunk_scores(c) for c in range(n_chunks)]
    for c in range(n_chunks):
        chunk_output(c, parts[c])
        if c % per_blk == per_blk - 1:
            gmlp_block(c // per_blk)

    hh, hl = [], []
    for r in range(2):
        rs = slice(r * (TM // 2), (r + 1) * (TM // 2))
        x1 = x_ref[rs, :] + mod[2:3] * _dot(y_ref[rs, :], wout_ref[...])
        x1_ref[rs, :] = x1
        h2 = _rms(x1) * g2_ref[...]
        h2 = h2 * (1.0 + mod[4:5]) + mod[3:4]
        h2_ref[rs, :] = h2
        hh.append(h2.astype(BF16))
        hl.append((h2 - hh[-1].astype(F32)).astype(BF16))
    hh = jnp.concatenate(hh, axis=0)
    hl = jnp.concatenate(hl, axis=0)

    rw2 = rwh_ref[...]
    by_hh = _dot_nt(rw2, hh)
    logits = by_hh[:N_EXPERTS] + (_dot_nt(rw2[:N_EXPERTS], hl) + by_hh[N_EXPERTS:]) + rb_ref[...]
    erow = lax.broadcasted_iota(jnp.int32, (N_EXPERTS, TM), 0)
    vals, idxs = [], []
    for _ in range(TOP_K):
        m = jnp.max(logits, axis=0, keepdims=True)
        idx = jnp.min(jnp.where(logits == m, erow, N_EXPERTS), axis=0, keepdims=True)
        vals.append(m)
        idxs.append(idx)
        logits = jnp.where(erow == idx, -jnp.inf, logits)
    es = [jnp.exp(v - vals[0]) for v in vals]
    tot = (es[0] + es[1]) + (es[2] + es[3])

    @pl.when((pl.program_id(0) == 0) & (pl.program_id(1) == 0))
    def _():
        cnt_ref[...] = jnp.zeros_like(cnt_ref)

    hot = [erow == idxs[k] for k in range(TOP_K)]
    picked = jnp.where((hot[0] | hot[1]) | (hot[2] | hot[3]), 1.0, 0.0)
    seen = _dot(picked.astype(BF16), earlier_ref[...]) + cnt_ref[:, 0:1]
    cnt_ref[...] = cnt_ref[...] + jnp.sum(picked, axis=1, keepdims=True)
    cnt_out_ref[...] = cnt_ref[...]
    ranks = [jnp.sum(jnp.where(hot[k], seen, 0.0), axis=0, keepdims=True) for k in range(TOP_K)]
    route_ref[...] = jnp.concatenate([i.astype(F32) for i in idxs] + [e / tot for e in es] + ranks +
                                     [jnp.zeros((ROUTE_ROWS - 3 * TOP_K, TM), F32)], axis=0)


def _mixer(x, mod8, norm_mix_g, w_in, lb_params, hgrn_norm_g, gmlp_ln_g, gmlp_ln_b, gmlp_ws, gmlp_bs,
           gmlp_norm_g, w_out, norm_ffn_g, router_w, router_b):
    bsz, seq, d = x.shape
    n_in = w_in.shape[1]
    rw = router_w.T
    rwh = rw.astype(BF16)
    rw2 = jnp.concatenate([rwh, (rw - rwh.astype(F32)).astype(BF16)], axis=0)
    rb = router_b.reshape(N_EXPERTS, 1)
    pos = jnp.arange(TM, dtype=jnp.int32)
    earlier = (pos[:, None] < pos[None, :]).astype(BF16)
    const = lambda *shape: pl.BlockSpec(shape, lambda b, i: (0,) * len(shape), pipeline_mode=pl.Buffered(1))
    tile = lambda w: pl.BlockSpec((None, TM, w), lambda b, i: (b, i, 0))
    per_seq = seq // TM
    return pl.pallas_call(
        _mixer_kernel,
        grid=(bsz, seq // TM),
        in_specs=[tile(d),
                  pl.BlockSpec((None, 8, d), lambda b, i: (b, 0, 0)),
                  const(1, d), const(d, n_in), const(2, HGRN_WIDTH), const(1, HGRN_WIDTH),
                  const(1, GMLP_WIDTH), const(1, GMLP_WIDTH),
                  const(GMLP_GROUPS, GMLP_CHUNK, GMLP_CHUNK), const(GMLP_GROUPS, GMLP_CHUNK, 1),
                  const(1, GMLP_WIDTH), const(d, d), const(1, d),
                  const(2 * N_EXPERTS, d), const(N_EXPERTS, 1), const(TM, TM)],
        out_specs=[tile(d), tile(d),
                   pl.BlockSpec((None, ROUTE_ROWS, TM), lambda b, i: (b * per_seq + i, 0, 0)),
                   pl.BlockSpec((N_EXPERTS, LANES), lambda b, i: (0, 0))],
        out_shape=[jax.ShapeDtypeStruct((bsz, seq, d), F32),
                   jax.ShapeDtypeStruct((bsz, seq, d), F32),
                   jax.ShapeDtypeStruct((bsz * per_seq, ROUTE_ROWS, TM), F32),
                   jax.ShapeDtypeStruct((N_EXPERTS, LANES), F32)],
        scratch_shapes=[pltpu.VMEM((TM, n_in), F32),
                        pltpu.VMEM((TM, d), BF16),
                        pltpu.VMEM((HGRN_HEADS, HEAD_DIM, HEAD_DIM), F32),
                        pltpu.VMEM((N_EXPERTS, LANES), F32)],
        compiler_params=pltpu.CompilerParams(dimension_semantics=("arbitrary", "arbitrary"),
                                             vmem_limit_bytes=VMEM_LIMIT),
        name="mixer",
    )(x, mod8, norm_mix_g.reshape(1, d), w_in.astype(BF16), lb_params, hgrn_norm_g.reshape(1, -1),
      gmlp_ln_g.reshape(1, -1), gmlp_ln_b.reshape(1, -1), gmlp_ws, gmlp_bs[:, :, None],
      gmlp_norm_g.reshape(1, -1), w_out.astype(BF16), norm_ffn_g.reshape(1, d), rw2, rb, earlier)


def _tile(r):
    return pl.ds(r * SUBLANES, SUBLANES)


def _tile8(r8):
    return pl.ds(pl.multiple_of(r8, SUBLANES), SUBLANES)


def _chunk(j, n):
    return pl.ds(j, n, stride=SUBLANES)


def _dispatch_kernel(n_blocks, n_pad_rows, pad0_ref, padn_ref, used_ref, dest_ref, h2_ref, xs_hbm,
                     rows, zeros, sem, zsem):
    i = pl.program_id(0)
    n_steps = pl.num_programs(0)
    blk_rows = MOE_BLK * SUBLANES

    def wait_step(s):
        for _ in range(TOP_K):
            pltpu.make_async_copy(rows.at[s], xs_hbm.at[pl.ds(0, TD * SUBLANES)], sem.at[s]).wait()

    def zero_fill(first_row, n_rows):
        n = n_rows * SUBLANES
        pltpu.make_async_copy(zeros.at[pl.ds(0, n)], xs_hbm.at[pl.ds(pl.multiple_of(first_row * SUBLANES, SUBLANES), n)],
                              zsem.at[0]).start()

    @pl.when(i == 0)
    def _():
        zeros[...] = jnp.zeros_like(zeros)
        for e in range(N_EXPERTS):
            first, count = pad0_ref[e], padn_ref[e]
            for bit in reversed(range(MOE_BLK.bit_length() - 1)):
                @pl.when((count >> bit) & 1 == 1)
                def _():
                    zero_fill(first + ((count >> (bit + 1)) << (bit + 1)), 1 << bit)

        def fill_block(b, carry):
            zero_fill(b * MOE_BLK, MOE_BLK)
            return carry

        lax.fori_loop(used_ref[0], n_blocks, fill_block, 0)

    def step(s):
        @pl.when(i >= 2)
        def _():
            wait_step(s)

        for j in range(SUBLANES):
            rows[s, _chunk(j, TD), :] = h2_ref[:, j * LANES:(j + 1) * LANES]
        for t in range(TD):
            for k in range(TOP_K):
                pltpu.make_async_copy(rows.at[s, _tile(t)], xs_hbm.at[_tile8(dest_ref[0, k, t])],
                                      sem.at[s]).start(priority=k % N_DMA_QUEUES)

        @pl.when(i == n_steps - 1)
        def _():
            wait_step(s)

            @pl.when(i >= 1)
            def _():
                wait_step(1 - s)

            for _ in range(n_pad_rows // MOE_BLK):
                pltpu.make_async_copy(zeros, xs_hbm.at[pl.ds(0, blk_rows)], zsem.at[0]).wait()

    _by_parity(i, step)


def _dest_block(step_of):
    per_tile = TM // TD
    return pl.BlockSpec((1, TOP_K, TD), lambda i, *_: (step_of(i) // per_tile, 0, step_of(i) % per_tile),
                        memory_space=pltpu.SMEM)


def _dispatch(h2, dest, pad_first, pad_count, blocks_used, p_rows):
    n_tok, d = h2.shape
    n_steps = n_tok // TD
    n_pad_rows = p_rows - n_tok * TOP_K
    assert n_pad_rows % MOE_BLK == 0
    grid_spec = pltpu.PrefetchScalarGridSpec(
        num_scalar_prefetch=3,
        grid=(n_steps,),
        in_specs=[_dest_block(lambda i: i),
                  pl.BlockSpec((TD, d), lambda i, *_: (i, 0))],
        out_specs=pl.BlockSpec(memory_space=pl.ANY),
        scratch_shapes=[pltpu.VMEM((2, TD * SUBLANES, LANES), F32),
                        pltpu.VMEM((MOE_BLK * SUBLANES, LANES), F32),
                        pltpu.SemaphoreType.DMA((2,)),
                        pltpu.SemaphoreType.DMA((1,))],
    )
    return pl.pallas_call(
        functools.partial(_dispatch_kernel, p_rows // MOE_BLK, n_pad_rows),
        grid_spec=grid_spec,
        out_shape=jax.ShapeDtypeStruct((p_rows * SUBLANES, LANES), F32),
        compiler_params=pltpu.CompilerParams(dimension_semantics=("arbitrary",)),
        name="dispatch",
    )(pad_first, pad_count, blocks_used, dest, h2)


def _moe_kernel(n_blocks, blk0_ref, nblk_ref, xs_hbm, wgu_hbm, bgu_ref, wd_hbm, bd_ref, ys_hbm,
                xbuf, ybuf, wgu_f, wd_f, wgu_bf, wd_bf, isem, osem, wsem):
    e = pl.program_id(0)
    n_exp = pl.num_programs(0)
    nb = nblk_ref[e]
    b0 = blk0_ref[e]
    d, d_ff = wd_f.shape[2], wd_f.shape[1]
    blk_rows = MOE_BLK * SUBLANES
    ws = e % 2
    gu_rows, d_rows = d // W_CHUNKS, d_ff // W_CHUNKS

    npairs = nb // 2
    odd = nb - 2 * npairs

    def rows_of(b, n):
        return pl.ds(pl.multiple_of((b0 + b) * blk_rows, blk_rows), n * blk_rows)

    def x_copy(b, n, s):
        return pltpu.make_async_copy(xs_hbm.at[rows_of(b, n)], xbuf.at[s, pl.ds(0, n * blk_rows)], isem.at[s])

    def y_copy(b, n, s):
        return pltpu.make_async_copy(ybuf.at[s, pl.ds(0, n * blk_rows)], ys_hbm.at[rows_of(b, n)], osem.at[s])

    def w_start(ex, slot, c):
        r_gu = pl.ds(pl.multiple_of(c * gu_rows, gu_rows), gu_rows)
        r_d = pl.ds(pl.multiple_of(c * d_rows, d_rows), d_rows)
        pltpu.make_async_copy(wgu_hbm.at[ex, r_gu], wgu_f.at[slot, r_gu], wsem.at[slot]).start()
        pltpu.make_async_copy(wd_hbm.at[ex, r_d], wd_f.at[slot, r_d], wsem.at[slot]).start()

    def w_wait(slot):
        pltpu.make_async_copy(wgu_hbm.at[0], wgu_f.at[slot], wsem.at[slot]).wait()
        pltpu.make_async_copy(wd_hbm.at[0], wd_f.at[slot], wsem.at[slot]).wait()

    @pl.when(e == 0)
    def _():
        for c in range(W_CHUNKS):
            w_start(0, 0, c)

    @pl.when(npairs > 0)
    def _():
        x_copy(0, 2, 0).start(priority=BLOCK_DMA_PRIORITY)

    @pl.when((npairs == 0) & (odd == 1))
    def _():
        x_copy(0, 1, 0).start(priority=BLOCK_DMA_PRIORITY)

    w_wait(ws)
    wgu_bf[...] = wgu_f[ws].astype(BF16)
    wd_bf[...] = wd_f[ws].astype(BF16)
    has_next = e + 1 < n_exp

    def ffn(n, s):
        rows = n * MOE_BLK
        xb = jnp.concatenate([xbuf[s, _chunk(j, rows), :] for j in range(SUBLANES)], axis=-1).astype(BF16)
        gu = _dot(xb, wgu_bf[...]) + bgu_ref[...]
        gate = jnp.minimum(gu[:, :d_ff], SWIGLU_LIMIT)
        up = jnp.clip(gu[:, d_ff:], -SWIGLU_LIMIT, SWIGLU_LIMIT)
        glu = gate * jax.nn.sigmoid(SWIGLU_ALPHA * gate)
        yb = _dot(((up + 1.0) * glu).astype(BF16), wd_bf[...]) + bd_ref[...]
        for j in range(SUBLANES):
            ybuf[s, _chunk(j, rows), :] = yb[:, j * LANES:(j + 1) * LANES]

    def pair_body(p, carry):
        s = p % 2

        @pl.when(p + 1 < npairs)
        def _():
            x_copy(2 * p + 2, 2, 1 - s).start(priority=BLOCK_DMA_PRIORITY)

        @pl.when((p + 1 == npairs) & (odd == 1))
        def _():
            x_copy(2 * p + 2, 1, 1 - s).start(priority=BLOCK_DMA_PRIORITY)

        @pl.when(has_next & (2 * p < W_CHUNKS))
        def _():
            w_start(e + 1, 1 - ws, 2 * p)
            w_start(e + 1, 1 - ws, 2 * p + 1)

        x_copy(2 * p, 2, s).wait()

        @pl.when(p >= 2)
        def _():
            y_copy(2 * p - 4, 2, s).wait()

        ffn(2, s)
        y_copy(2 * p, 2, s).start(priority=BLOCK_DMA_PRIORITY)
        return carry

    lax.fori_loop(0, npairs, pair_body, 0)

    @pl.when(odd == 1)
    def _():
        s = npairs % 2
        x_copy(2 * npairs, 1, s).wait()

        @pl.when(npairs >= 2)
        def _():
            y_copy(2 * npairs - 4, 2, s).wait()

        ffn(1, s)
        cp = y_copy(2 * npairs, 1, s)
        cp.start(priority=BLOCK_DMA_PRIORITY)
        cp.wait()

    @pl.when(has_next)
    def _():
        def rest(c, carry):
            w_start(e + 1, 1 - ws, c)
            return carry

        lax.fori_loop(jnp.minimum(2 * npairs, W_CHUNKS), W_CHUNKS, rest, 0)

    @pl.when(npairs >= 1)
    def _():
        y_copy(2 * npairs - 2, 2, (npairs - 1) % 2).wait()

    @pl.when((npairs >= 2) & (odd == 0))
    def _():
        y_copy(2 * npairs - 4, 2, npairs % 2).wait()

    @pl.when(e == pl.num_programs(0) - 1)
    def _():
        ybuf[0, pl.ds(0, blk_rows), :] = jnp.zeros((blk_rows, LANES), F32)

        def fill(b, carry):
            cp = y_copy(b, 1, 0)
            cp.start()
            cp.wait()
            return carry

        lax.fori_loop(nb, n_blocks - b0, fill, 0)


def _moe(xs, blk0, nblk, w_gate_up, b_gate_up, w_down, b_down):
    n_blocks = xs.shape[0] // (MOE_BLK * SUBLANES)
    n_exp, d_ff, d = w_down.shape
    grid_spec = pltpu.PrefetchScalarGridSpec(
        num_scalar_prefetch=2,
        grid=(n_exp,),
        in_specs=[pl.BlockSpec(memory_space=pl.ANY),
                  pl.BlockSpec(memory_space=pl.ANY),
                  pl.BlockSpec((None, 1, 2 * d_ff), lambda e, b0, nb: (e, 0, 0)),
                  pl.BlockSpec(memory_space=pl.ANY),
                  pl.BlockSpec((None, 1, d), lambda e, b0, nb: (e, 0, 0))],
        out_specs=pl.BlockSpec(memory_space=pl.ANY),
        scratch_shapes=[pltpu.VMEM((2, 2 * MOE_BLK * SUBLANES, LANES), F32),
                        pltpu.VMEM((2, 2 * MOE_BLK * SUBLANES, LANES), F32),
                        pltpu.VMEM((2, d, 2 * d_ff), F32),
                        pltpu.VMEM((2, d_ff, d), F32),
                        pltpu.VMEM((d, 2 * d_ff), BF16),
                        pltpu.VMEM((d_ff, d), BF16),
                        pltpu.SemaphoreType.DMA((2,)),
                        pltpu.SemaphoreType.DMA((2,)),
                        pltpu.SemaphoreType.DMA((2,))],
    )
    return pl.pallas_call(
        functools.partial(_moe_kernel, n_blocks),
        grid_spec=grid_spec,
        out_shape=jax.ShapeDtypeStruct(xs.shape, F32),
        compiler_params=pltpu.CompilerParams(dimension_semantics=("arbitrary",),
                                             vmem_limit_bytes=VMEM_LIMIT),
        name="moe_ffn",
    )(blk0, nblk, xs, w_gate_up, b_gate_up[:, None, :], w_down, b_down[:, None, :])


def _combine_kernel(destc_ref, destn_ref, x1_ref, mod_ref, route_ref, fg_ref, ys_hbm, o_ref, buf0, buf1, sem):
    buf = (buf0, buf1)
    i = pl.program_id(0)
    n_steps = pl.num_programs(0)
    d = x1_ref.shape[-1]

    def start_gather(dest_ref, s, t0=0, t1=TD):
        for t in range(t0, t1):
            for k in range(TOP_K):
                pltpu.make_async_copy(ys_hbm.at[_tile8(dest_ref[0, k, t])], buf[s].at[k, _tile(t)],
                                      sem.at[s]).start(priority=k % N_DMA_QUEUES)

    def wait_gather(s):
        for k in range(TOP_K):
            pltpu.make_async_copy(ys_hbm.at[pl.ds(0, TD * SUBLANES)], buf[s].at[k], sem.at[s]).wait()

    @pl.when(i == 0)
    def _():
        start_gather(destc_ref, 0)

    def step(s):
        wait_gather(s)
        gates = route_ref[...]
        gk = [gates[:, TOP_K + k:TOP_K + k + 1] for k in range(TOP_K)]
        gate2 = mod_ref[5:6, :]

        def residual(j):
            ls = slice(j * LANES, (j + 1) * LANES)
            rj = _chunk(j, TD)
            y = (gk[0] * buf[s][0, rj, :] + gk[1] * buf[s][1, rj, :]) + \
                (gk[2] * buf[s][2, rj, :] + gk[3] * buf[s][3, rj, :])
            return x1_ref[:, ls] + gate2[:, ls] * y

        ss = jnp.zeros((TD, 1), F32)
        per = TD // SUBLANES
        for j in range(SUBLANES):
            start_gather(destn_ref, 1 - s, j * per, (j + 1) * per)
            xj = residual(j)
            ss = ss + jnp.sum(xj * xj, axis=-1, keepdims=True)
        inv = lax.rsqrt(ss / d + EPS)
        for j in range(SUBLANES):
            ls = slice(j * LANES, (j + 1) * LANES)
            o_ref[:, ls] = residual(j) * inv * fg_ref[:, ls]

        @pl.when(i == n_steps - 1)
        def _():
            wait_gather(1 - s)

    _by_parity(i, step)


def _combine(x1, mod8, route, dest, ys, final_g):
    bsz, seq, d = x1.shape
    n_tok = bsz * seq
    n_steps = n_tok // TD
    per_seq = seq // TD
    return pl.pallas_call(
        _combine_kernel,
        grid=(n_steps,),
        in_specs=[_dest_block(lambda i: i),
                  _dest_block(lambda i: jnp.minimum(i + 1, n_steps - 1)),
                  pl.BlockSpec((TD, d), lambda i: (i, 0)),
                  pl.BlockSpec((None, 8, d), lambda i: (i // per_seq, 0, 0)),
                  pl.BlockSpec((TD, ROUTE_ROWS), lambda i: (i, 0)),
                  pl.BlockSpec((1, d), lambda i: (0, 0)),
                  pl.BlockSpec(memory_space=pl.ANY)],
        out_specs=pl.BlockSpec((TD, d), lambda i: (i, 0)),
        out_shape=jax.ShapeDtypeStruct((n_tok, d), F32),
        scratch_shapes=[pltpu.VMEM((TOP_K, TD * SUBLANES, LANES), F32),
                        pltpu.VMEM((TOP_K, TD * SUBLANES, LANES), F32),
                        pltpu.SemaphoreType.DMA((2,))],
        compiler_params=pltpu.CompilerParams(dimension_semantics=("arbitrary",),
                                             vmem_limit_bytes=VMEM_LIMIT),
        name="combine",
    )(dest, dest, x1.reshape(n_tok, d), mod8, route, final_g.reshape(1, d), ys).reshape(bsz, seq, d)


def _dest_kernel(route_ref, start_ref, dest_ref):
    erow = lax.broadcasted_iota(jnp.int32, (N_EXPERTS, TM), 0).astype(F32)
    for i in range(route_ref.shape[0]):
        r = route_ref[i]
        rows = []
        for k in range(TOP_K):
            base = jnp.sum(jnp.where(erow == r[k:k + 1, :], start_ref[...], 0.0), axis=0, keepdims=True)
            rows.append((base + r[2 * TOP_K + k:2 * TOP_K + k + 1, :]) * float(SUBLANES))
        dest_ref[i] = jnp.concatenate(rows, axis=0).astype(jnp.int32)


def _destinations(route_t, start_pad):
    n_tiles = route_t.shape[0]
    per_step = 8 if n_tiles % 8 == 0 else 1
    return pl.pallas_call(
        _dest_kernel,
        grid=(n_tiles // per_step,),
        in_specs=[pl.BlockSpec((per_step, ROUTE_ROWS, TM), lambda i: (i, 0, 0)),
                  pl.BlockSpec((N_EXPERTS, 1), lambda i: (0, 0))],
        out_specs=pl.BlockSpec((per_step, TOP_K, TM), lambda i: (i, 0, 0)),
        out_shape=jax.ShapeDtypeStruct((n_tiles, TOP_K, TM), jnp.int32),
        name="destinations",
    )(route_t, start_pad.astype(F32).reshape(N_EXPERTS, 1))


def _routing(counts_f, n_tok):
    nk = n_tok * TOP_K
    counts = counts_f[:, 0].astype(jnp.int32)
    padded = ((counts + MOE_BLK - 1) // MOE_BLK) * MOE_BLK
    pad_end = jnp.cumsum(padded)
    start_pad = pad_end - padded
    p_rows = ((nk + N_EXPERTS * (MOE_BLK - 1) + MOE_BLK - 1) // MOE_BLK) * MOE_BLK
    blocks_used = (pad_end[-1:] // MOE_BLK).astype(jnp.int32)
    return (start_pad, start_pad + counts, padded - counts, blocks_used,
            (start_pad // MOE_BLK).astype(jnp.int32), (padded // MOE_BLK).astype(jnp.int32), p_rows)


def kernel(x, c, ada_w, ada_b, norm_mix_g, w_in, lb_params, hgrn_norm_g, gmlp_ln_g, gmlp_ln_b, gmlp_ws, gmlp_bs,
           gmlp_norm_g, w_out, norm_ffn_g, router_w, router_b, w_gate_up, b_gate_up, w_down, b_down, final_g):
    assert ada_w.shape[0] == 1, "single-layer block"
    bsz, seq, d = x.shape
    assert d == SUBLANES * LANES and seq % TM == 0 and (bsz * seq) % TD == 0
    n_tok = bsz * seq
    mod = _modulation(c, ada_w[0], ada_b[0])
    mod8 = jnp.zeros((bsz, 8, d), F32).at[:, :6].set(mod.reshape(bsz, 6, d))
    x1, h2, route, counts = _mixer(x, mod8, norm_mix_g[0], w_in[0], lb_params, hgrn_norm_g[0], gmlp_ln_g[0],
                                   gmlp_ln_b[0], gmlp_ws[0], gmlp_bs[0], gmlp_norm_g[0], w_out[0],
                                   norm_ffn_g[0], router_w[0], router_b[0])
    start_pad, pad_first, pad_count, blocks_used, blk0, nblk, p_rows = _routing(counts, n_tok)
    dest = _destinations(route, start_pad)
    route = route.transpose(0, 2, 1).reshape(n_tok, ROUTE_ROWS)
    xs = _dispatch(h2.reshape(n_tok, d), dest, pad_first, pad_count, blocks_used, p_rows)
    ys = _moe(xs, blk0, nblk, w_gate_up[0], b_gate_up[0], w_down[0], b_down[0])
    return _combine(x1, mod8, route, dest, ys, final_g)
```

```python
import functools

import jax
import jax.numpy as jnp
from jax import lax
from jax.experimental import pallas as pl
from jax.experimental.pallas import tpu as pltpu

F32 = jnp.float32
BF16 = jnp.bfloat16

HGRN_HEADS = 4
HEAD_DIM = 128
HGRN_WIDTH = HGRN_HEADS * HEAD_DIM
HGRN_CHUNK = 64
HGRN_SUB = 16
GMLP_GROUPS = 4
GROUP_DIM = 128
GMLP_WIDTH = GMLP_GROUPS * GROUP_DIM
GMLP_CHUNK = 128
N_EXPERTS = 32
TOP_K = 4
SWIGLU_LIMIT = 7.0
SWIGLU_ALPHA = 1.702
EPS = 1e-6
LANES = 128
SUBLANES = 8
N_DMA_QUEUES = 2
BLOCK_DMA_PRIORITY = 1
ROUTE_ROWS = 16
W_CHUNKS = 8
DECAY_EXP_CLAMP = 60.0
TM = 512
MOE_BLK = 256
GROUP = 4
TD = 512
VMEM_LIMIT = 56 * 1024 * 1024


def _dot(a, b):
    return jnp.dot(a, b, preferred_element_type=F32)


def _dot_nt(a, b):
    return lax.dot_general(a, b, (((1,), (1,)), ((), ())), preferred_element_type=F32)


def _dot_tn(a, b):
    return lax.dot_general(a, b, (((0,), (0,)), ((), ())), preferred_element_type=F32)


def _rms(x):
    return x * lax.rsqrt(jnp.mean(x * x, axis=-1, keepdims=True) + EPS)


def _gelu(x):
    return 0.5 * x * (1.0 + lax.erf(x * 0.7071067811865476))


def _by_parity(i, fn):
    @pl.when(i % 2 == 0)
    def _():
        fn(0)

    @pl.when(i % 2 == 1)
    def _():
        fn(1)


def _mod_kernel(c_ref, w_ref, b_ref, o_ref):
    c = c_ref[...]
    ca = c * jax.nn.sigmoid(c)
    o_ref[...] = jnp.dot(ca, w_ref[...], precision=lax.Precision.HIGHEST,
                         preferred_element_type=F32) + b_ref[...]


def _modulation(c, ada_w, ada_b):
    bsz, d = c.shape
    n_out = ada_w.shape[1]
    rows = 8
    c_pad = jnp.zeros((rows, d), F32).at[:bsz].set(c)
    tn = 2048
    out = pl.pallas_call(
        _mod_kernel,
        grid=(n_out // tn,),
        in_specs=[pl.BlockSpec((rows, d), lambda j: (0, 0)),
                  pl.BlockSpec((d, tn), lambda j: (0, j)),
                  pl.BlockSpec((1, tn), lambda j: (0, j))],
        out_specs=pl.BlockSpec((rows, tn), lambda j: (0, j)),
        out_shape=jax.ShapeDtypeStruct((rows, n_out), F32),
        name="adaln_mod",
    )(c_pad, ada_w, ada_b.reshape(1, n_out))
    return out[:bsz]


def _mixer_kernel(x_ref, mod_ref, g1_ref, win_ref, lbp_ref, hg_ref, lng_ref, lnb_ref, ws_ref, bs_ref,
                  gng_ref, wout_ref, g2_ref, rwh_ref, rb_ref, earlier_ref,
                  x1_ref, h2_ref, route_ref, cnt_out_ref,
                  z_ref, y_ref, st_ref, cnt_ref):
    @pl.when(pl.program_id(1) == 0)
    def _():
        st_ref[...] = jnp.zeros_like(st_ref)

    x = x_ref[...]
    mod = mod_ref[...]
    h = _rms(x) * g1_ref[...]
    h = h * (1.0 + mod[1:2]) + mod[0:1]
    z_ref[...] = _dot(h.astype(BF16), win_ref[...])

    lbp = lbp_ref[...]
    lbe = jnp.exp(lbp - jnp.max(lbp, axis=0, keepdims=True))
    lb = lbe[0:1] / jnp.sum(lbe, axis=0, keepdims=True)
    hg = hg_ref[...]
    row = lax.broadcasted_iota(jnp.int32, (HGRN_CHUNK, HGRN_CHUNK), 0)
    col = lax.broadcasted_iota(jnp.int32, (HGRN_CHUNK, HGRN_CHUNK), 1)
    causal = col <= row
    crow = lax.broadcasted_iota(jnp.int32, (HGRN_CHUNK, HGRN_WIDTH), 0)
    n_sub = HGRN_CHUNK // HGRN_SUB

    def chunk_scores(c):
        rows = slice(c * HGRN_CHUNK, (c + 1) * HGRN_CHUNK)
        zq = z_ref[rows, 0:HGRN_WIDTH]
        zf = z_ref[rows, HGRN_WIDTH:2 * HGRN_WIDTH]
        zi = z_ref[rows, 2 * HGRN_WIDTH:3 * HGRN_WIDTH]
        zg = z_ref[rows, 3 * HGRN_WIDTH:4 * HGRN_WIDTH]
        f = lb + (1.0 - lb) * jax.nn.sigmoid(zf)
        logf = jnp.log(f)
        kk = 1.0 - f
        b = logf
        for sh in (1, 2, 4, 8, 16, 32):
            b = b + jnp.where(crow >= sh, pltpu.roll(b, sh, axis=0), 0.0)
        b_last = b[HGRN_CHUNK - 1:HGRN_CHUNK, :]
        qe = (zq * jnp.exp(b)).astype(BF16)
        kdec = (kk * jnp.exp(b_last - b)).astype(BF16)
        v = zi.astype(BF16)
        dec_last = jnp.exp(b_last)
        a_sub, k_sub = [], []
        for i in range(n_sub):
            lo, hi = i * HGRN_SUB, (i + 1) * HGRN_SUB
            bref = b[lo - 1:lo, :] if i > 0 else jnp.zeros((1, HGRN_WIDTH), F32)
            a_sub.append((zq[lo:hi] * jnp.exp(b[lo:hi] - bref)).astype(BF16))
            k_sub.append((kk * jnp.exp(jnp.minimum(bref - b, DECAY_EXP_CLAMP))).astype(BF16))
        sc = []
        for hd in range(HGRN_HEADS):
            ls = slice(hd * HEAD_DIM, (hd + 1) * HEAD_DIM)
            s_h = jnp.concatenate([_dot_nt(a_sub[i][:, ls], k_sub[i][:, ls]) for i in range(n_sub)], axis=0)
            sc.append(jnp.where(causal, s_h, 0.0).astype(BF16))
        return qe, kdec, v, dec_last, sc, zg * jax.nn.sigmoid(zg)

    def chunk_output(c, parts):
        rows = slice(c * HGRN_CHUNK, (c + 1) * HGRN_CHUNK)
        qe, kdec, v, dec_last, sc, silu_g = parts
        for hd in range(HGRN_HEADS):
            ls = slice(hd * HEAD_DIM, (hd + 1) * HEAD_DIM)
            st = st_ref[hd]
            o = _dot_nt(qe[:, ls], st.astype(BF16)) + _dot(sc[hd], v[:, ls])
            st_ref[hd] = st * dec_last[:, ls] + _dot_tn(v[:, ls], kdec[:, ls])
            o = _rms(o) * hg[:, ls]
            y_ref[rows, ls] = (o * silu_g[:, ls]).astype(BF16)

    row_g = lax.broadcasted_iota(jnp.int32, (GMLP_CHUNK, GMLP_CHUNK), 0)
    col_g = lax.broadcasted_iota(jnp.int32, (GMLP_CHUNK, GMLP_CHUNK), 1)
    gng = gng_ref[...]
    ws_c = [jnp.where(col_g <= row_g, ws_ref[g], 0.0).astype(BF16) for g in range(GMLP_GROUPS)]

    def gmlp_block(n):
        rs = slice(n * GMLP_CHUNK, (n + 1) * GMLP_CHUNK)
        u = _gelu(z_ref[rs, 4 * HGRN_WIDTH:4 * HGRN_WIDTH + GMLP_WIDTH])
        gv = _gelu(z_ref[rs, 4 * HGRN_WIDTH + GMLP_WIDTH:])
        mu = jnp.mean(gv, axis=-1, keepdims=True)
        gc = gv - mu
        var = jnp.mean(gc * gc, axis=-1, keepdims=True)
        vn = (gc * lax.rsqrt(var + EPS) * lng_ref[...] + lnb_ref[...]).astype(BF16)
        for g in range(GMLP_GROUPS):
            ls = slice(g * GROUP_DIM, (g + 1) * GROUP_DIM)
            sv = _dot(ws_c[g], vn[:, ls]) + bs_ref[g]
            yy = _rms(u[:, ls] * sv) * gng[:, ls]
            y_ref[rs, HGRN_WIDTH + g * GROUP_DIM:HGRN_WIDTH + (g + 1) * GROUP_DIM] = yy.astype(BF16)

    n_chunks = TM // HGRN_CHUNK
    per_blk = GMLP_CHUNK // HGRN_CHUNK
    parts = [chunk_scores(c) for c in range(n_chunks)]
    for c in range(n_chunks):
        chunk_output(c, parts[c])
        if c % per_blk == per_blk - 1:
            gmlp_block(c // per_blk)

    hh, hl = [], []
    for r in range(2):
        rs = slice(r * (TM // 2), (r + 1) * (TM // 2))
        x1 = x_ref[rs, :] + mod[2:3] * _dot(y_ref[rs, :], wout_ref[...])
        x1_ref[rs, :] = x1
        h2 = _rms(x1) * g2_ref[...]
        h2 = h2 * (1.0 + mod[4:5]) + mod[3:4]
        h2_ref[rs, :] = h2
        hh.append(h2.astype(BF16))
        hl.append((h2 - hh[-1].astype(F32)).astype(BF16))
    hh = jnp.concatenate(hh, axis=0)
    hl = jnp.concatenate(hl, axis=0)

    rw2 = rwh_ref[...]
    by_hh = _dot_nt(rw2, hh)
    logits = by_hh[:N_EXPERTS] + (_dot_nt(rw2[:N_EXPERTS], hl) + by_hh[N_EXPERTS:]) + rb_ref[...]
    erow = lax.broadcasted_iota(jnp.int32, (N_EXPERTS, TM), 0)
    vals, idxs = [], []
    for _ in range(TOP_K):
        m = jnp.max(logits, axis=0, keepdims=True)
        idx = jnp.min(jnp.where(logits == m, erow, N_EXPERTS), axis=0, keepdims=True)
        vals.append(m)
        idxs.append(idx)
        logits = jnp.where(erow == idx, -jnp.inf, logits)
    es = [jnp.exp(v - vals[0]) for v in vals]
    tot = (es[0] + es[1]) + (es[2] + es[3])

    @pl.when((pl.program_id(0) == 0) & (pl.program_id(1) == 0))
    def _():
        cnt_ref[...] = jnp.zeros_like(cnt_ref)

    hot = [erow == idxs[k] for k in range(TOP_K)]
    picked = jnp.where((hot[0] | hot[1]) | (hot[2] | hot[3]), 1.0, 0.0)
    seen = _dot(picked.astype(BF16), earlier_ref[...]) + cnt_ref[:, 0:1]
    cnt_ref[...] = cnt_ref[...] + jnp.sum(picked, axis=1, keepdims=True)
    cnt_out_ref[...] = cnt_ref[...]
    ranks = [jnp.sum(jnp.where(hot[k], seen, 0.0), axis=0, keepdims=True) for k in range(TOP_K)]
    route_ref[...] = jnp.concatenate([i.astype(F32) for i in idxs] + [e / tot for e in es] + ranks +
                                     [jnp.zeros((ROUTE_ROWS - 3 * TOP_K, TM), F32)], axis=0)


def _mixer(x, mod8, norm_mix_g, w_in, lb_params, hgrn_norm_g, gmlp_ln_g, gmlp_ln_b, gmlp_ws, gmlp_bs,
           gmlp_norm_g, w_out, norm_ffn_g, router_w, router_b):
    bsz, seq, d = x.shape
    n_in = w_in.shape[1]
    rw = router_w.T
    rwh = rw.astype(BF16)
    rw2 = jnp.concatenate([rwh, (rw - rwh.astype(F32)).astype(BF16)], axis=0)
    rb = router_b.reshape(N_EXPERTS, 1)
    pos = jnp.arange(TM, dtype=jnp.int32)
    earlier = (pos[:, None] < pos[None, :]).astype(BF16)
    const = lambda *shape: pl.BlockSpec(shape, lambda b, i: (0,) * len(shape))
    tile = lambda w: pl.BlockSpec((None, TM, w), lambda b, i: (b, i, 0))
    per_seq = seq // TM
    return pl.pallas_call(
        _mixer_kernel,
        grid=(bsz, seq // TM),
        in_specs=[tile(d),
                  pl.BlockSpec((None, 8, d), lambda b, i: (b, 0, 0)),
                  const(1, d), const(d, n_in), const(2, HGRN_WIDTH), const(1, HGRN_WIDTH),
                  const(1, GMLP_WIDTH), const(1, GMLP_WIDTH),
                  const(GMLP_GROUPS, GMLP_CHUNK, GMLP_CHUNK), const(GMLP_GROUPS, GMLP_CHUNK, 1),
                  const(1, GMLP_WIDTH), const(d, d), const(1, d),
                  const(2 * N_EXPERTS, d), const(N_EXPERTS, 1), const(TM, TM)],
        out_specs=[tile(d), tile(d),
                   pl.BlockSpec((None, ROUTE_ROWS, TM), lambda b, i: (b * per_seq + i, 0, 0)),
                   const(N_EXPERTS, LANES)],
        out_shape=[jax.ShapeDtypeStruct((bsz, seq, d), F32),
                   jax.ShapeDtypeStruct((bsz, seq, d), F32),
                   jax.ShapeDtypeStruct((bsz * per_seq, ROUTE_ROWS, TM), F32),
                   jax.ShapeDtypeStruct((N_EXPERTS, LANES), F32)],
        scratch_shapes=[pltpu.VMEM((TM, n_in), F32),
                        pltpu.VMEM((TM, d), BF16),
                        pltpu.VMEM((HGRN_HEADS, HEAD_DIM, HEAD_DIM), F32),
                        pltpu.VMEM((N_EXPERTS, LANES), F32)],
        compiler_params=pltpu.CompilerParams(dimension_semantics=("arbitrary", "arbitrary"),
                                             vmem_limit_bytes=VMEM_LIMIT),
        name="mixer",
    )(x, mod8, norm_mix_g.reshape(1, d), w_in.astype(BF16), lb_params, hgrn_norm_g.reshape(1, -1),
      gmlp_ln_g.reshape(1, -1), gmlp_ln_b.reshape(1, -1), gmlp_ws, gmlp_bs[:, :, None],
      gmlp_norm_g.reshape(1, -1), w_out.astype(BF16), norm_ffn_g.reshape(1, d), rw2, rb, earlier)


def _tile(r):
    return pl.ds(r * SUBLANES, SUBLANES)


def _tile8(r8):
    return pl.ds(pl.multiple_of(r8, SUBLANES), SUBLANES)


def _chunk(j, n):
    return pl.ds(j, n, stride=SUBLANES)


def _dispatch_kernel(n_blocks, n_pad_rows, pad0_ref, padn_ref, used_ref, dest_ref, h2_ref, xs_hbm,
                     rows, zeros, sem, zsem):
    i = pl.program_id(0)
    n_steps = pl.num_programs(0)
    blk_rows = MOE_BLK * SUBLANES

    def wait_step(s):
        for _ in range(TOP_K):
            pltpu.make_async_copy(rows.at[s], xs_hbm.at[pl.ds(0, TD * SUBLANES)], sem.at[s]).wait()

    def zero_fill(first_row, n_rows):
        n = n_rows * SUBLANES
        pltpu.make_async_copy(zeros.at[pl.ds(0, n)], xs_hbm.at[pl.ds(pl.multiple_of(first_row * SUBLANES, SUBLANES), n)],
                              zsem.at[0]).start()

    @pl.when(i == 0)
    def _():
        zeros[...] = jnp.zeros_like(zeros)
        for e in range(N_EXPERTS):
            first, count = pad0_ref[e], padn_ref[e]
            for bit in reversed(range(MOE_BLK.bit_length() - 1)):
                @pl.when((count >> bit) & 1 == 1)
                def _():
                    zero_fill(first + ((count >> (bit + 1)) << (bit + 1)), 1 << bit)

        def fill_block(b, carry):
            zero_fill(b * MOE_BLK, MOE_BLK)
            return carry

        lax.fori_loop(used_ref[0], n_blocks, fill_block, 0)

    def step(s):
        @pl.when(i >= 2)
        def _():
            wait_step(s)

        for j in range(SUBLANES):
            rows[s, _chunk(j, TD), :] = h2_ref[:, j * LANES:(j + 1) * LANES]
        for t in range(TD):
            for k in range(TOP_K):
                pltpu.make_async_copy(rows.at[s, _tile(t)], xs_hbm.at[_tile8(dest_ref[0, k, t])],
                                      sem.at[s]).start(priority=k % N_DMA_QUEUES)

        @pl.when(i == n_steps - 1)
        def _():
            wait_step(s)

            @pl.when(i >= 1)
            def _():
                wait_step(1 - s)

            for _ in range(n_pad_rows // MOE_BLK):
                pltpu.make_async_copy(zeros, xs_hbm.at[pl.ds(0, blk_rows)], zsem.at[0]).wait()

    _by_parity(i, step)


def _dest_block(step_of):
    per_tile = TM // TD
    return pl.BlockSpec((1, TOP_K, TD), lambda i, *_: (step_of(i) // per_tile, 0, step_of(i) % per_tile),
                        memory_space=pltpu.SMEM)


def _dispatch(h2, dest, pad_first, pad_count, blocks_used, p_rows):
    n_tok, d = h2.shape
    n_steps = n_tok // TD
    n_pad_rows = p_rows - n_tok * TOP_K
    assert n_pad_rows % MOE_BLK == 0
    grid_spec = pltpu.PrefetchScalarGridSpec(
        num_scalar_prefetch=3,
        grid=(n_steps,),
        in_specs=[_dest_block(lambda i: i),
                  pl.BlockSpec((TD, d), lambda i, *_: (i, 0))],
        out_specs=pl.BlockSpec(memory_space=pl.ANY),
        scratch_shapes=[pltpu.VMEM((2, TD * SUBLANES, LANES), F32),
                        pltpu.VMEM((MOE_BLK * SUBLANES, LANES), F32),
                        pltpu.SemaphoreType.DMA((2,)),
                        pltpu.SemaphoreType.DMA((1,))],
    )
    return pl.pallas_call(
        functools.partial(_dispatch_kernel, p_rows // MOE_BLK, n_pad_rows),
        grid_spec=grid_spec,
        out_shape=jax.ShapeDtypeStruct((p_rows * SUBLANES, LANES), F32),
        compiler_params=pltpu.CompilerParams(dimension_semantics=("arbitrary",)),
        name="dispatch",
    )(pad_first, pad_count, blocks_used, dest, h2)


def _moe_kernel(n_blocks, blk0_ref, nblk_ref, xs_hbm, wgu_hbm, bgu_ref, wd_hbm, bd_ref, ys_hbm,
                xbuf, ybuf, wgu_f, wd_f, wgu_bf, wd_bf, isem, osem, wsem):
    e = pl.program_id(0)
    n_exp = pl.num_programs(0)
    nb = nblk_ref[e]
    b0 = blk0_ref[e]
    d, d_ff = wd_f.shape[2], wd_f.shape[1]
    blk_rows = MOE_BLK * SUBLANES
    ws = e % 2
    gu_rows, d_rows = d // W_CHUNKS, d_ff // W_CHUNKS

    nq = nb // GROUP
    rem = nb - GROUP * nq
    has_pair = jnp.where(rem >= 2, 1, 0)
    has_one = rem % 2

    def rows_of(b, n):
        return pl.ds(pl.multiple_of((b0 + b) * blk_rows, blk_rows), n * blk_rows)

    def x_copy(b, n, s):
        return pltpu.make_async_copy(xs_hbm.at[rows_of(b, n)], xbuf.at[s, pl.ds(0, n * blk_rows)], isem.at[s])

    def y_copy(b, n, s):
        return pltpu.make_async_copy(ybuf.at[s, pl.ds(0, n * blk_rows)], ys_hbm.at[rows_of(b, n)], osem.at[s])

    def w_start(ex, slot, c):
        r_gu = pl.ds(pl.multiple_of(c * gu_rows, gu_rows), gu_rows)
        r_d = pl.ds(pl.multiple_of(c * d_rows, d_rows), d_rows)
        pltpu.make_async_copy(wgu_hbm.at[ex, r_gu], wgu_f.at[slot, r_gu], wsem.at[slot]).start()
        pltpu.make_async_copy(wd_hbm.at[ex, r_d], wd_f.at[slot, r_d], wsem.at[slot]).start()

    def w_wait(slot):
        pltpu.make_async_copy(wgu_hbm.at[0], wgu_f.at[slot], wsem.at[slot]).wait()
        pltpu.make_async_copy(wd_hbm.at[0], wd_f.at[slot], wsem.at[slot]).wait()

    @pl.when(e == 0)
    def _():
        for c in range(W_CHUNKS):
            w_start(0, 0, c)

    def start_x(b, n, s):
        x_copy(b, n, s).start(priority=BLOCK_DMA_PRIORITY)

    def start_tail_x(b, pair, one, s):
        @pl.when(pair == 1)
        def _():
            start_x(b, 2, s)

        @pl.when((pair == 0) & (one == 1))
        def _():
            start_x(b, 1, s)

    @pl.when(nq > 0)
    def _():
        start_x(0, GROUP, 0)

    @pl.when(nq == 0)
    def _():
        start_tail_x(0, has_pair, has_one, 0)

    w_wait(ws)
    wgu_bf[...] = wgu_f[ws].astype(BF16)
    wd_bf[...] = wd_f[ws].astype(BF16)
    has_next = e + 1 < n_exp

    def ffn(n, s, off=0):
        rows = n * MOE_BLK

        def chunk(j):
            return pl.ds(off * blk_rows + j, rows, stride=SUBLANES)

        xb = jnp.concatenate([xbuf[s, chunk(j), :] for j in range(SUBLANES)], axis=-1).astype(BF16)
        gu = _dot(xb, wgu_bf[...]) + bgu_ref[...]
        gate = jnp.minimum(gu[:, :d_ff], SWIGLU_LIMIT)
        up = jnp.clip(gu[:, d_ff:], -SWIGLU_LIMIT, SWIGLU_LIMIT)
        glu = gate * jax.nn.sigmoid(SWIGLU_ALPHA * gate)
        yb = _dot(((up + 1.0) * glu).astype(BF16), wd_bf[...]) + bd_ref[...]
        for j in range(SUBLANES):
            ybuf[s, chunk(j), :] = yb[:, j * LANES:(j + 1) * LANES]

    def wait_y_two_units_back(u, s):
        @pl.when(u >= 2)
        def _():
            y_copy(GROUP * (u - 2), GROUP, s).wait()

    chunks_per_group = W_CHUNKS // 2

    def group_body(p, carry):
        s = p % 2

        @pl.when(p + 1 < nq)
        def _():
            start_x(GROUP * (p + 1), GROUP, 1 - s)

        @pl.when(p + 1 == nq)
        def _():
            start_tail_x(GROUP * nq, has_pair, has_one, 1 - s)

        @pl.when(has_next & (p * chunks_per_group < W_CHUNKS))
        def _():
            for c in range(chunks_per_group):
                w_start(e + 1, 1 - ws, p * chunks_per_group + c)

        x_copy(GROUP * p, GROUP, s).wait()
        wait_y_two_units_back(p, s)
        for h in range(GROUP // 2):
            ffn(2, s, 2 * h)
        y_copy(GROUP * p, GROUP, s).start(priority=BLOCK_DMA_PRIORITY)
        return carry

    lax.fori_loop(0, nq, group_body, 0)

    @pl.when(has_next)
    def _():
        def rest(c, carry):
            w_start(e + 1, 1 - ws, c)
            return carry

        lax.fori_loop(jnp.minimum(nq * chunks_per_group, W_CHUNKS), W_CHUNKS, rest, 0)

    @pl.when(has_pair == 1)
    def _():
        u = nq
        s = u % 2
        b = GROUP * nq

        @pl.when(has_one == 1)
        def _():
            start_x(b + 2, 1, 1 - s)

        x_copy(b, 2, s).wait()
        wait_y_two_units_back(u, s)
        ffn(2, s)
        cp = y_copy(b, 2, s)
        cp.start(priority=BLOCK_DMA_PRIORITY)
        cp.wait()

    @pl.when(has_one == 1)
    def _():
        u = nq + has_pair
        s = u % 2
        b = GROUP * nq + 2 * has_pair
        x_copy(b, 1, s).wait()
        wait_y_two_units_back(u, s)
        ffn(1, s)
        cp = y_copy(b, 1, s)
        cp.start(priority=BLOCK_DMA_PRIORITY)
        cp.wait()

    @pl.when((nq >= 1) & (has_pair + has_one < 2))
    def _():
        y_copy(GROUP * (nq - 1), GROUP, (nq - 1) % 2).wait()

    @pl.when((nq >= 2) & (rem == 0))
    def _():
        y_copy(GROUP * (nq - 2), GROUP, nq % 2).wait()

    @pl.when(e == pl.num_programs(0) - 1)
    def _():
        ybuf[0, pl.ds(0, blk_rows), :] = jnp.zeros((blk_rows, LANES), F32)

        def fill(b, carry):
            cp = y_copy(b, 1, 0)
            cp.start()
            cp.wait()
            return carry

        lax.fori_loop(nb, n_blocks - b0, fill, 0)


def _moe(xs, blk0, nblk, w_gate_up, b_gate_up, w_down, b_down):
    n_blocks = xs.shape[0] // (MOE_BLK * SUBLANES)
    n_exp, d_ff, d = w_down.shape
    grid_spec = pltpu.PrefetchScalarGridSpec(
        num_scalar_prefetch=2,
        grid=(n_exp,),
        in_specs=[pl.BlockSpec(memory_space=pl.ANY),
                  pl.BlockSpec(memory_space=pl.ANY),
                  pl.BlockSpec((None, 1, 2 * d_ff), lambda e, b0, nb: (e, 0, 0)),
                  pl.BlockSpec(memory_space=pl.ANY),
                  pl.BlockSpec((None, 1, d), lambda e, b0, nb: (e, 0, 0))],
        out_specs=pl.BlockSpec(memory_space=pl.ANY),
        scratch_shapes=[pltpu.VMEM((2, GROUP * MOE_BLK * SUBLANES, LANES), F32),
                        pltpu.VMEM((2, GROUP * MOE_BLK * SUBLANES, LANES), F32),
                        pltpu.VMEM((2, d, 2 * d_ff), F32),
                        pltpu.VMEM((2, d_ff, d), F32),
                        pltpu.VMEM((d, 2 * d_ff), BF16),
                        pltpu.VMEM((d_ff, d), BF16),
                        pltpu.SemaphoreType.DMA((2,)),
                        pltpu.SemaphoreType.DMA((2,)),
                        pltpu.SemaphoreType.DMA((2,))],
    )
    return pl.pallas_call(
        functools.partial(_moe_kernel, n_blocks),
        grid_spec=grid_spec,
        out_shape=jax.ShapeDtypeStruct(xs.shape, F32),
        compiler_params=pltpu.CompilerParams(dimension_semantics=("arbitrary",),
                                             vmem_limit_bytes=VMEM_LIMIT),
        name="moe_ffn",
    )(blk0, nblk, xs, w_gate_up, b_gate_up[:, None, :], w_down, b_down[:, None, :])


def _combine_kernel(destc_ref, destn_ref, x1_ref, mod_ref, route_ref, fg_ref, ys_hbm, o_ref, buf0, buf1, sem):
    buf = (buf0, buf1)
    i = pl.program_id(0)
    n_steps = pl.num_programs(0)
    d = x1_ref.shape[-1]

    def start_gather(dest_ref, s, t0=0, t1=TD):
        for t in range(t0, t1):
            for k in range(TOP_K):
                pltpu.make_async_copy(ys_hbm.at[_tile8(dest_ref[0, k, t])], buf[s].at[k, _tile(t)],
                                      sem.at[s]).start(priority=k % N_DMA_QUEUES)

    def wait_gather(s):
        for k in range(TOP_K):
            pltpu.make_async_copy(ys_hbm.at[pl.ds(0, TD * SUBLANES)], buf[s].at[k], sem.at[s]).wait()

    @pl.when(i == 0)
    def _():
        start_gather(destc_ref, 0)

    def step(s):
        wait_gather(s)
        gates = route_ref[...]
        gk = [gates[:, TOP_K + k:TOP_K + k + 1] for k in range(TOP_K)]
        gate2 = mod_ref[5:6, :]

        def residual(j):
            ls = slice(j * LANES, (j + 1) * LANES)
            rj = _chunk(j, TD)
            y = (gk[0] * buf[s][0, rj, :] + gk[1] * buf[s][1, rj, :]) + \
                (gk[2] * buf[s][2, rj, :] + gk[3] * buf[s][3, rj, :])
            return x1_ref[:, ls] + gate2[:, ls] * y

        ss = jnp.zeros((TD, 1), F32)
        per = TD // SUBLANES
        for j in range(SUBLANES):
            start_gather(destn_ref, 1 - s, j * per, (j + 1) * per)
            xj = residual(j)
            ss = ss + jnp.sum(xj * xj, axis=-1, keepdims=True)
        inv = lax.rsqrt(ss / d + EPS)
        for j in range(SUBLANES):
            ls = slice(j * LANES, (j + 1) * LANES)
            o_ref[:, ls] = residual(j) * inv * fg_ref[:, ls]

        @pl.when(i == n_steps - 1)
        def _():
            wait_gather(1 - s)

    _by_parity(i, step)


def _combine(x1, mod8, route, dest, ys, final_g):
    bsz, seq, d = x1.shape
    n_tok = bsz * seq
    n_steps = n_tok // TD
    per_seq = seq // TD
    return pl.pallas_call(
        _combine_kernel,
        grid=(n_steps,),
        in_specs=[_dest_block(lambda i: i),
                  _dest_block(lambda i: jnp.minimum(i + 1, n_steps - 1)),
                  pl.BlockSpec((TD, d), lambda i: (i, 0)),
                  pl.BlockSpec((None, 8, d), lambda i: (i // per_seq, 0, 0)),
                  pl.BlockSpec((TD, ROUTE_ROWS), lambda i: (i, 0)),
                  pl.BlockSpec((1, d), lambda i: (0, 0)),
                  pl.BlockSpec(memory_space=pl.ANY)],
        out_specs=pl.BlockSpec((TD, d), lambda i: (i, 0)),
        out_shape=jax.ShapeDtypeStruct((n_tok, d), F32),
        scratch_shapes=[pltpu.VMEM((TOP_K, TD * SUBLANES, LANES), F32),
                        pltpu.VMEM((TOP_K, TD * SUBLANES, LANES), F32),
                        pltpu.SemaphoreType.DMA((2,))],
        compiler_params=pltpu.CompilerParams(dimension_semantics=("arbitrary",),
                                             vmem_limit_bytes=VMEM_LIMIT),
        name="combine",
    )(dest, dest, x1.reshape(n_tok, d), mod8, route, final_g.reshape(1, d), ys).reshape(bsz, seq, d)


def _dest_kernel(route_ref, start_ref, dest_ref):
    erow = lax.broadcasted_iota(jnp.int32, (N_EXPERTS, TM), 0).astype(F32)
    for i in range(route_ref.shape[0]):
        r = route_ref[i]
        rows = []
        for k in range(TOP_K):
            base = jnp.sum(jnp.where(erow == r[k:k + 1, :], start_ref[...], 0.0), axis=0, keepdims=True)
            rows.append((base + r[2 * TOP_K + k:2 * TOP_K + k + 1, :]) * float(SUBLANES))
        dest_ref[i] = jnp.concatenate(rows, axis=0).astype(jnp.int32)


def _destinations(route_t, start_pad):
    n_tiles = route_t.shape[0]
    per_step = 8 if n_tiles % 8 == 0 else 1
    return pl.pallas_call(
        _dest_kernel,
        grid=(n_tiles // per_step,),
        in_specs=[pl.BlockSpec((per_step, ROUTE_ROWS, TM), lambda i: (i, 0, 0)),
                  pl.BlockSpec((N_EXPERTS, 1), lambda i: (0, 0))],
        out_specs=pl.BlockSpec((per_step, TOP_K, TM), lambda i: (i, 0, 0)),
        out_shape=jax.ShapeDtypeStruct((n_tiles, TOP_K, TM), jnp.int32),
        name="destinations",
    )(route_t, start_pad.astype(F32).reshape(N_EXPERTS, 1))


def _routing(counts_f, n_tok):
    nk = n_tok * TOP_K
    counts = counts_f[:, 0].astype(jnp.int32)
    padded = ((counts + MOE_BLK - 1) // MOE_BLK) * MOE_BLK
    pad_end = jnp.cumsum(padded)
    start_pad = pad_end - padded
    p_rows = ((nk + N_EXPERTS * (MOE_BLK - 1) + MOE_BLK - 1) // MOE_BLK) * MOE_BLK
    blocks_used = (pad_end[-1:] // MOE_BLK).astype(jnp.int32)
    return (start_pad, start_pad + counts, padded - counts, blocks_used,
            (start_pad // MOE_BLK).astype(jnp.int32), (padded // MOE_BLK).astype(jnp.int32), p_rows)


def kernel(x, c, ada_w, ada_b, norm_mix_g, w_in, lb_params, hgrn_norm_g, gmlp_ln_g, gmlp_ln_b, gmlp_ws, gmlp_bs,
           gmlp_norm_g, w_out, norm_ffn_g, router_w, router_b, w_gate_up, b_gate_up, w_down, b_down, final_g):
    assert ada_w.shape[0] == 1, "single-layer block"
    bsz, seq, d = x.shape
    assert d == SUBLANES * LANES and seq % TM == 0 and (bsz * seq) % TD == 0
    n_tok = bsz * seq
    mod = _modulation(c, ada_w[0], ada_b[0])
    mod8 = jnp.zeros((bsz, 8, d), F32).at[:, :6].set(mod.reshape(bsz, 6, d))
    x1, h2, route, counts = _mixer(x, mod8, norm_mix_g[0], w_in[0], lb_params, hgrn_norm_g[0], gmlp_ln_g[0],
                                   gmlp_ln_b[0], gmlp_ws[0], gmlp_bs[0], gmlp_norm_g[0], w_out[0],
                                   norm_ffn_g[0], router_w[0], router_b[0])
    start_pad, pad_first, pad_count, blocks_used, blk0, nblk, p_rows = _routing(counts, n_tok)
    dest = _destinations(route, start_pad)
    route = route.transpose(0, 2, 1).reshape(n_tok, ROUTE_ROWS)
    xs = _dispatch(h2.reshape(n_tok, d), dest, pad_first, pad_count, blocks_used, p_rows)
    ys = _moe(xs, blk0, nblk, w_gate_up[0], b_gate_up[0], w_down[0], b_down[0])
    return _combine(x1, mod8, route, dest, ys, final_g)
```

```python
import functools

import jax
import jax.numpy as jnp
from jax import lax
from jax.experimental import pallas as pl
from jax.experimental.pallas import tpu as pltpu

F32 = jnp.float32
BF16 = jnp.bfloat16

HGRN_HEADS = 4
HEAD_DIM = 128
HGRN_WIDTH = HGRN_HEADS * HEAD_DIM
HGRN_CHUNK = 64
HGRN_SUB = 16
GMLP_GROUPS = 4
GROUP_DIM = 128
GMLP_WIDTH = GMLP_GROUPS * GROUP_DIM
GMLP_CHUNK = 128
N_EXPERTS = 32
TOP_K = 4
SWIGLU_LIMIT = 7.0
SWIGLU_ALPHA = 1.702
EPS = 1e-6
LANES = 128
SUBLANES = 8
N_DMA_QUEUES = 2
BLOCK_DMA_PRIORITY = 1
ROUTE_ROWS = 16
W_CHUNKS = 8
DECAY_EXP_CLAMP = 60.0
TM = 512
MOE_BLK = 256
X_SLOTS = 3
TD = 512
VMEM_LIMIT = 56 * 1024 * 1024


def _dot(a, b):
    return jnp.dot(a, b, preferred_element_type=F32)


def _dot_nt(a, b):
    return lax.dot_general(a, b, (((1,), (1,)), ((), ())), preferred_element_type=F32)


def _dot_tn(a, b):
    return lax.dot_general(a, b, (((0,), (0,)), ((), ())), preferred_element_type=F32)


def _rms(x):
    return x * lax.rsqrt(jnp.mean(x * x, axis=-1, keepdims=True) + EPS)


def _gelu(x):
    return 0.5 * x * (1.0 + lax.erf(x * 0.7071067811865476))


def _by_parity(i, fn):
    @pl.when(i % 2 == 0)
    def _():
        fn(0)

    @pl.when(i % 2 == 1)
    def _():
        fn(1)


def _mod_kernel(c_ref, w_ref, b_ref, o_ref):
    c = c_ref[...]
    ca = c * jax.nn.sigmoid(c)
    o_ref[...] = jnp.dot(ca, w_ref[...], precision=lax.Precision.HIGHEST,
                         preferred_element_type=F32) + b_ref[...]


def _modulation(c, ada_w, ada_b):
    bsz, d = c.shape
    n_out = ada_w.shape[1]
    rows = 8
    c_pad = jnp.zeros((rows, d), F32).at[:bsz].set(c)
    tn = 2048
    out = pl.pallas_call(
        _mod_kernel,
        grid=(n_out // tn,),
        in_specs=[pl.BlockSpec((rows, d), lambda j: (0, 0)),
                  pl.BlockSpec((d, tn), lambda j: (0, j)),
                  pl.BlockSpec((1, tn), lambda j: (0, j))],
        out_specs=pl.BlockSpec((rows, tn), lambda j: (0, j)),
        out_shape=jax.ShapeDtypeStruct((rows, n_out), F32),
        name="adaln_mod",
    )(c_pad, ada_w, ada_b.reshape(1, n_out))
    return out[:bsz]


def _mixer_kernel(x_ref, mod_ref, g1_ref, win_ref, lbp_ref, hg_ref, lng_ref, lnb_ref, ws_ref, bs_ref,
                  gng_ref, wout_ref, g2_ref, rwh_ref, rb_ref, earlier_ref,
                  x1_ref, h2_ref, route_ref, cnt_out_ref,
                  z_ref, y_ref, st_ref, cnt_ref):
    @pl.when(pl.program_id(1) == 0)
    def _():
        st_ref[...] = jnp.zeros_like(st_ref)

    x = x_ref[...]
    mod = mod_ref[...]
    h = _rms(x) * g1_ref[...]
    h = h * (1.0 + mod[1:2]) + mod[0:1]
    z_ref[...] = _dot(h.astype(BF16), win_ref[...])

    lbp = lbp_ref[...]
    lbe = jnp.exp(lbp - jnp.max(lbp, axis=0, keepdims=True))
    lb = lbe[0:1] / jnp.sum(lbe, axis=0, keepdims=True)
    hg = hg_ref[...]
    row = lax.broadcasted_iota(jnp.int32, (HGRN_CHUNK, HGRN_CHUNK), 0)
    col = lax.broadcasted_iota(jnp.int32, (HGRN_CHUNK, HGRN_CHUNK), 1)
    causal = col <= row
    crow = lax.broadcasted_iota(jnp.int32, (HGRN_CHUNK, HGRN_WIDTH), 0)
    n_sub = HGRN_CHUNK // HGRN_SUB

    def chunk_scores(c):
        rows = slice(c * HGRN_CHUNK, (c + 1) * HGRN_CHUNK)
        zq = z_ref[rows, 0:HGRN_WIDTH]
        zf = z_ref[rows, HGRN_WIDTH:2 * HGRN_WIDTH]
        zi = z_ref[rows, 2 * HGRN_WIDTH:3 * HGRN_WIDTH]
        zg = z_ref[rows, 3 * HGRN_WIDTH:4 * HGRN_WIDTH]
        f = lb + (1.0 - lb) * jax.nn.sigmoid(zf)
        logf = jnp.log(f)
        kk = 1.0 - f
        b = logf
        for sh in (1, 2, 4, 8, 16, 32):
            b = b + jnp.where(crow >= sh, pltpu.roll(b, sh, axis=0), 0.0)
        b_last = b[HGRN_CHUNK - 1:HGRN_CHUNK, :]
        qe = (zq * jnp.exp(b)).astype(BF16)
        kdec = (kk * jnp.exp(b_last - b)).astype(BF16)
        v = zi.astype(BF16)
        dec_last = jnp.exp(b_last)
        a_sub, k_sub = [], []
        for i in range(n_sub):
            lo, hi = i * HGRN_SUB, (i + 1) * HGRN_SUB
            bref = b[lo - 1:lo, :] if i > 0 else jnp.zeros((1, HGRN_WIDTH), F32)
            a_sub.append((zq[lo:hi] * jnp.exp(b[lo:hi] - bref)).astype(BF16))
            k_sub.append((kk * jnp.exp(jnp.minimum(bref - b, DECAY_EXP_CLAMP))).astype(BF16))
        sc = []
        for hd in range(HGRN_HEADS):
            ls = slice(hd * HEAD_DIM, (hd + 1) * HEAD_DIM)
            s_h = jnp.concatenate([_dot_nt(a_sub[i][:, ls], k_sub[i][:, ls]) for i in range(n_sub)], axis=0)
            sc.append(jnp.where(causal, s_h, 0.0).astype(BF16))
        return qe, kdec, v, dec_last, sc, zg * jax.nn.sigmoid(zg)

    def chunk_output(c, parts):
        rows = slice(c * HGRN_CHUNK, (c + 1) * HGRN_CHUNK)
        qe, kdec, v, dec_last, sc, silu_g = parts
        for hd in range(HGRN_HEADS):
            ls = slice(hd * HEAD_DIM, (hd + 1) * HEAD_DIM)
            st = st_ref[hd]
            o = _dot_nt(qe[:, ls], st.astype(BF16)) + _dot(sc[hd], v[:, ls])
            st_ref[hd] = st * dec_last[:, ls] + _dot_tn(v[:, ls], kdec[:, ls])
            o = _rms(o) * hg[:, ls]
            y_ref[rows, ls] = (o * silu_g[:, ls]).astype(BF16)

    row_g = lax.broadcasted_iota(jnp.int32, (GMLP_CHUNK, GMLP_CHUNK), 0)
    col_g = lax.broadcasted_iota(jnp.int32, (GMLP_CHUNK, GMLP_CHUNK), 1)
    gng = gng_ref[...]
    ws_c = [jnp.where(col_g <= row_g, ws_ref[g], 0.0).astype(BF16) for g in range(GMLP_GROUPS)]

    def gmlp_block(n):
        rs = slice(n * GMLP_CHUNK, (n + 1) * GMLP_CHUNK)
        u = _gelu(z_ref[rs, 4 * HGRN_WIDTH:4 * HGRN_WIDTH + GMLP_WIDTH])
        gv = _gelu(z_ref[rs, 4 * HGRN_WIDTH + GMLP_WIDTH:])
        mu = jnp.mean(gv, axis=-1, keepdims=True)
        gc = gv - mu
        var = jnp.mean(gc * gc, axis=-1, keepdims=True)
        vn = (gc * lax.rsqrt(var + EPS) * lng_ref[...] + lnb_ref[...]).astype(BF16)
        for g in range(GMLP_GROUPS):
            ls = slice(g * GROUP_DIM, (g + 1) * GROUP_DIM)
            sv = _dot(ws_c[g], vn[:, ls]) + bs_ref[g]
            yy = _rms(u[:, ls] * sv) * gng[:, ls]
            y_ref[rs, HGRN_WIDTH + g * GROUP_DIM:HGRN_WIDTH + (g + 1) * GROUP_DIM] = yy.astype(BF16)

    n_chunks = TM // HGRN_CHUNK
    per_blk = GMLP_CHUNK // HGRN_CHUNK
    parts = [chunk_scores(c) for c in range(n_chunks)]
    for c in range(n_chunks):
        chunk_output(c, parts[c])
        if c % per_blk == per_blk - 1:
            gmlp_block(c // per_blk)

    hh, hl = [], []
    for r in range(2):
        rs = slice(r * (TM // 2), (r + 1) * (TM // 2))
        x1 = x_ref[rs, :] + mod[2:3] * _dot(y_ref[rs, :], wout_ref[...])
        x1_ref[rs, :] = x1
        h2 = _rms(x1) * g2_ref[...]
        h2 = h2 * (1.0 + mod[4:5]) + mod[3:4]
        h2_ref[rs, :] = h2
        hh.append(h2.astype(BF16))
        hl.append((h2 - hh[-1].astype(F32)).astype(BF16))
    hh = jnp.concatenate(hh, axis=0)
    hl = jnp.concatenate(hl, axis=0)

    rw2 = rwh_ref[...]
    by_hh = _dot_nt(rw2, hh)
    logits = by_hh[:N_EXPERTS] + (_dot_nt(rw2[:N_EXPERTS], hl) + by_hh[N_EXPERTS:]) + rb_ref[...]
    erow = lax.broadcasted_iota(jnp.int32, (N_EXPERTS, TM), 0)
    vals, idxs = [], []
    for _ in range(TOP_K):
        m = jnp.max(logits, axis=0, keepdims=True)
        idx = jnp.min(jnp.where(logits == m, erow, N_EXPERTS), axis=0, keepdims=True)
        vals.append(m)
        idxs.append(idx)
        logits = jnp.where(erow == idx, -jnp.inf, logits)
    es = [jnp.exp(v - vals[0]) for v in vals]
    tot = (es[0] + es[1]) + (es[2] + es[3])

    @pl.when((pl.program_id(0) == 0) & (pl.program_id(1) == 0))
    def _():
        cnt_ref[...] = jnp.zeros_like(cnt_ref)

    hot = [erow == idxs[k] for k in range(TOP_K)]
    picked = jnp.where((hot[0] | hot[1]) | (hot[2] | hot[3]), 1.0, 0.0)
    seen = _dot(picked.astype(BF16), earlier_ref[...]) + cnt_ref[:, 0:1]
    cnt_ref[...] = cnt_ref[...] + jnp.sum(picked, axis=1, keepdims=True)
    cnt_out_ref[...] = cnt_ref[...]
    ranks = [jnp.sum(jnp.where(hot[k], seen, 0.0), axis=0, keepdims=True) for k in range(TOP_K)]
    route_ref[...] = jnp.concatenate([i.astype(F32) for i in idxs] + [e / tot for e in es] + ranks +
                                     [jnp.zeros((ROUTE_ROWS - 3 * TOP_K, TM), F32)], axis=0)


def _mixer(x, mod8, norm_mix_g, w_in, lb_params, hgrn_norm_g, gmlp_ln_g, gmlp_ln_b, gmlp_ws, gmlp_bs,
           gmlp_norm_g, w_out, norm_ffn_g, router_w, router_b):
    bsz, seq, d = x.shape
    n_in = w_in.shape[1]
    rw = router_w.T
    rwh = rw.astype(BF16)
    rw2 = jnp.concatenate([rwh, (rw - rwh.astype(F32)).astype(BF16)], axis=0)
    rb = router_b.reshape(N_EXPERTS, 1)
    pos = jnp.arange(TM, dtype=jnp.int32)
    earlier = (pos[:, None] < pos[None, :]).astype(BF16)
    const = lambda *shape: pl.BlockSpec(shape, lambda b, i: (0,) * len(shape))
    tile = lambda w: pl.BlockSpec((None, TM, w), lambda b, i: (b, i, 0))
    per_seq = seq // TM
    return pl.pallas_call(
        _mixer_kernel,
        grid=(bsz, seq // TM),
        in_specs=[tile(d),
                  pl.BlockSpec((None, 8, d), lambda b, i: (b, 0, 0)),
                  const(1, d), const(d, n_in), const(2, HGRN_WIDTH), const(1, HGRN_WIDTH),
                  const(1, GMLP_WIDTH), const(1, GMLP_WIDTH),
                  const(GMLP_GROUPS, GMLP_CHUNK, GMLP_CHUNK), const(GMLP_GROUPS, GMLP_CHUNK, 1),
                  const(1, GMLP_WIDTH), const(d, d), const(1, d),
                  const(2 * N_EXPERTS, d), const(N_EXPERTS, 1), const(TM, TM)],
        out_specs=[tile(d), tile(d),
                   pl.BlockSpec((None, ROUTE_ROWS, TM), lambda b, i: (b * per_seq + i, 0, 0)),
                   const(N_EXPERTS, LANES)],
        out_shape=[jax.ShapeDtypeStruct((bsz, seq, d), F32),
                   jax.ShapeDtypeStruct((bsz, seq, d), F32),
                   jax.ShapeDtypeStruct((bsz * per_seq, ROUTE_ROWS, TM), F32),
                   jax.ShapeDtypeStruct((N_EXPERTS, LANES), F32)],
        scratch_shapes=[pltpu.VMEM((TM, n_in), F32),
                        pltpu.VMEM((TM, d), BF16),
                        pltpu.VMEM((HGRN_HEADS, HEAD_DIM, HEAD_DIM), F32),
                        pltpu.VMEM((N_EXPERTS, LANES), F32)],
        compiler_params=pltpu.CompilerParams(dimension_semantics=("arbitrary", "arbitrary"),
                                             vmem_limit_bytes=VMEM_LIMIT),
        name="mixer",
    )(x, mod8, norm_mix_g.reshape(1, d), w_in.astype(BF16), lb_params, hgrn_norm_g.reshape(1, -1),
      gmlp_ln_g.reshape(1, -1), gmlp_ln_b.reshape(1, -1), gmlp_ws, gmlp_bs[:, :, None],
      gmlp_norm_g.reshape(1, -1), w_out.astype(BF16), norm_ffn_g.reshape(1, d), rw2, rb, earlier)


def _tile(r):
    return pl.ds(r * SUBLANES, SUBLANES)


def _tile8(r8):
    return pl.ds(pl.multiple_of(r8, SUBLANES), SUBLANES)


def _chunk(j, n):
    return pl.ds(j, n, stride=SUBLANES)


def _dispatch_kernel(n_blocks, n_pad_rows, pad0_ref, padn_ref, used_ref, dest_ref, h2_ref, xs_hbm,
                     rows, zeros, sem, zsem):
    i = pl.program_id(0)
    n_steps = pl.num_programs(0)
    blk_rows = MOE_BLK * SUBLANES

    def wait_step(s):
        for _ in range(TOP_K):
            pltpu.make_async_copy(rows.at[s], xs_hbm.at[pl.ds(0, TD * SUBLANES)], sem.at[s]).wait()

    def zero_fill(first_row, n_rows):
        n = n_rows * SUBLANES
        pltpu.make_async_copy(zeros.at[pl.ds(0, n)], xs_hbm.at[pl.ds(pl.multiple_of(first_row * SUBLANES, SUBLANES), n)],
                              zsem.at[0]).start()

    @pl.when(i == 0)
    def _():
        zeros[...] = jnp.zeros_like(zeros)
        for e in range(N_EXPERTS):
            first, count = pad0_ref[e], padn_ref[e]
            for bit in reversed(range(MOE_BLK.bit_length() - 1)):
                @pl.when((count >> bit) & 1 == 1)
                def _():
                    zero_fill(first + ((count >> (bit + 1)) << (bit + 1)), 1 << bit)

        def fill_block(b, carry):
            zero_fill(b * MOE_BLK, MOE_BLK)
            return carry

        lax.fori_loop(used_ref[0], n_blocks, fill_block, 0)

    def step(s):
        @pl.when(i >= 2)
        def _():
            wait_step(s)

        for j in range(SUBLANES):
            rows[s, _chunk(j, TD), :] = h2_ref[:, j * LANES:(j + 1) * LANES]
        for t in range(TD):
            for k in range(TOP_K):
                pltpu.make_async_copy(rows.at[s, _tile(t)], xs_hbm.at[_tile8(dest_ref[0, k, t])],
                                      sem.at[s]).start(priority=k % N_DMA_QUEUES)

        @pl.when(i == n_steps - 1)
        def _():
            wait_step(s)

            @pl.when(i >= 1)
            def _():
                wait_step(1 - s)

            for _ in range(n_pad_rows // MOE_BLK):
                pltpu.make_async_copy(zeros, xs_hbm.at[pl.ds(0, blk_rows)], zsem.at[0]).wait()

    _by_parity(i, step)


def _dest_block(step_of):
    per_tile = TM // TD
    return pl.BlockSpec((1, TOP_K, TD), lambda i, *_: (step_of(i) // per_tile, 0, step_of(i) % per_tile),
                        memory_space=pltpu.SMEM)


def _dispatch(h2, dest, pad_first, pad_count, blocks_used, p_rows):
    n_tok, d = h2.shape
    n_steps = n_tok // TD
    n_pad_rows = p_rows - n_tok * TOP_K
    assert n_pad_rows % MOE_BLK == 0
    grid_spec = pltpu.PrefetchScalarGridSpec(
        num_scalar_prefetch=3,
        grid=(n_steps,),
        in_specs=[_dest_block(lambda i: i),
                  pl.BlockSpec((TD, d), lambda i, *_: (i, 0))],
        out_specs=pl.BlockSpec(memory_space=pl.ANY),
        scratch_shapes=[pltpu.VMEM((2, TD * SUBLANES, LANES), F32),
                        pltpu.VMEM((MOE_BLK * SUBLANES, LANES), F32),
                        pltpu.SemaphoreType.DMA((2,)),
                        pltpu.SemaphoreType.DMA((1,))],
    )
    return pl.pallas_call(
        functools.partial(_dispatch_kernel, p_rows // MOE_BLK, n_pad_rows),
        grid_spec=grid_spec,
        out_shape=jax.ShapeDtypeStruct((p_rows * SUBLANES, LANES), F32),
        compiler_params=pltpu.CompilerParams(dimension_semantics=("arbitrary",)),
        name="dispatch",
    )(pad_first, pad_count, blocks_used, dest, h2)


def _moe_kernel(n_blocks, blk0_ref, nblk_ref, xs_hbm, wgu_hbm, bgu_ref, wd_hbm, bd_ref, ys_hbm,
                xbuf, ybuf, wgu_f, wd_f, wgu_bf, wd_bf, isem, osem, wsem):
    e = pl.program_id(0)
    n_exp = pl.num_programs(0)
    nb = nblk_ref[e]
    b0 = blk0_ref[e]
    d, d_ff = wd_f.shape[2], wd_f.shape[1]
    blk_rows = MOE_BLK * SUBLANES
    ws = e % 2
    gu_rows, d_rows = d // W_CHUNKS, d_ff // W_CHUNKS

    npairs = nb // 2
    odd = nb - 2 * npairs

    def rows_of(b, n):
        return pl.ds(pl.multiple_of((b0 + b) * blk_rows, blk_rows), n * blk_rows)

    def x_copy(b, n, s):
        return pltpu.make_async_copy(xs_hbm.at[rows_of(b, n)], xbuf.at[s, pl.ds(0, n * blk_rows)], isem.at[s])

    def y_copy(b, n, s):
        return pltpu.make_async_copy(ybuf.at[s, pl.ds(0, n * blk_rows)], ys_hbm.at[rows_of(b, n)], osem.at[s])

    def w_start(ex, slot, c):
        r_gu = pl.ds(pl.multiple_of(c * gu_rows, gu_rows), gu_rows)
        r_d = pl.ds(pl.multiple_of(c * d_rows, d_rows), d_rows)
        pltpu.make_async_copy(wgu_hbm.at[ex, r_gu], wgu_f.at[slot, r_gu], wsem.at[slot]).start()
        pltpu.make_async_copy(wd_hbm.at[ex, r_d], wd_f.at[slot, r_d], wsem.at[slot]).start()

    def w_wait(slot):
        pltpu.make_async_copy(wgu_hbm.at[0], wgu_f.at[slot], wsem.at[slot]).wait()
        pltpu.make_async_copy(wd_hbm.at[0], wd_f.at[slot], wsem.at[slot]).wait()

    @pl.when(e == 0)
    def _():
        for c in range(W_CHUNKS):
            w_start(0, 0, c)

    def start_unit(u):
        @pl.when(u < npairs)
        def _():
            x_copy(2 * u, 2, u % X_SLOTS).start(priority=BLOCK_DMA_PRIORITY)

        @pl.when((u == npairs) & (odd == 1))
        def _():
            x_copy(2 * u, 1, u % X_SLOTS).start(priority=BLOCK_DMA_PRIORITY)

    for u0 in range(X_SLOTS - 1):
        start_unit(u0)

    w_wait(ws)
    wgu_bf[...] = wgu_f[ws].astype(BF16)
    wd_bf[...] = wd_f[ws].astype(BF16)
    has_next = e + 1 < n_exp

    def ffn(n, sx, s):
        rows = n * MOE_BLK
        xb = jnp.concatenate([xbuf[sx, _chunk(j, rows), :] for j in range(SUBLANES)], axis=-1).astype(BF16)
        gu = _dot(xb, wgu_bf[...]) + bgu_ref[...]
        gate = jnp.minimum(gu[:, :d_ff], SWIGLU_LIMIT)
        up = jnp.clip(gu[:, d_ff:], -SWIGLU_LIMIT, SWIGLU_LIMIT)
        glu = gate * jax.nn.sigmoid(SWIGLU_ALPHA * gate)
        yb = _dot(((up + 1.0) * glu).astype(BF16), wd_bf[...]) + bd_ref[...]
        for j in range(SUBLANES):
            ybuf[s, _chunk(j, rows), :] = yb[:, j * LANES:(j + 1) * LANES]

    def pair_body(p, carry):
        s = p % 2

        start_unit(p + X_SLOTS - 1)

        @pl.when(has_next & (2 * p < W_CHUNKS))
        def _():
            w_start(e + 1, 1 - ws, 2 * p)
            w_start(e + 1, 1 - ws, 2 * p + 1)

        x_copy(2 * p, 2, p % X_SLOTS).wait()

        @pl.when(p >= 2)
        def _():
            y_copy(2 * p - 4, 2, s).wait()

        ffn(2, p % X_SLOTS, s)
        y_copy(2 * p, 2, s).start(priority=BLOCK_DMA_PRIORITY)
        return carry

    lax.fori_loop(0, npairs, pair_body, 0)

    @pl.when(odd == 1)
    def _():
        s = npairs % 2
        x_copy(2 * npairs, 1, npairs % X_SLOTS).wait()

        @pl.when(npairs >= 2)
        def _():
            y_copy(2 * npairs - 4, 2, s).wait()

        ffn(1, npairs % X_SLOTS, s)
        cp = y_copy(2 * npairs, 1, s)
        cp.start(priority=BLOCK_DMA_PRIORITY)
        cp.wait()

    @pl.when(has_next)
    def _():
        def rest(c, carry):
            w_start(e + 1, 1 - ws, c)
            return carry

        lax.fori_loop(jnp.minimum(2 * npairs, W_CHUNKS), W_CHUNKS, rest, 0)

    @pl.when(npairs >= 1)
    def _():
        y_copy(2 * npairs - 2, 2, (npairs - 1) % 2).wait()

    @pl.when((npairs >= 2) & (odd == 0))
    def _():
        y_copy(2 * npairs - 4, 2, npairs % 2).wait()

    @pl.when(e == pl.num_programs(0) - 1)
    def _():
        ybuf[0, pl.ds(0, blk_rows), :] = jnp.zeros((blk_rows, LANES), F32)

        def fill(b, carry):
            cp = y_copy(b, 1, 0)
            cp.start()
            cp.wait()
            return carry

        lax.fori_loop(nb, n_blocks - b0, fill, 0)


def _moe(xs, blk0, nblk, w_gate_up, b_gate_up, w_down, b_down):
    n_blocks = xs.shape[0] // (MOE_BLK * SUBLANES)
    n_exp, d_ff, d = w_down.shape
    grid_spec = pltpu.PrefetchScalarGridSpec(
        num_scalar_prefetch=2,
        grid=(n_exp,),
        in_specs=[pl.BlockSpec(memory_space=pl.ANY),
                  pl.BlockSpec(memory_space=pl.ANY),
                  pl.BlockSpec((None, 1, 2 * d_ff), lambda e, b0, nb: (e, 0, 0)),
                  pl.BlockSpec(memory_space=pl.ANY),
                  pl.BlockSpec((None, 1, d), lambda e, b0, nb: (e, 0, 0))],
        out_specs=pl.BlockSpec(memory_space=pl.ANY),
        scratch_shapes=[pltpu.VMEM((X_SLOTS, 2 * MOE_BLK * SUBLANES, LANES), F32),
                        pltpu.VMEM((2, 2 * MOE_BLK * SUBLANES, LANES), F32),
                        pltpu.VMEM((2, d, 2 * d_ff), F32),
                        pltpu.VMEM((2, d_ff, d), F32),
                        pltpu.VMEM((d, 2 * d_ff), BF16),
                        pltpu.VMEM((d_ff, d), BF16),
                        pltpu.SemaphoreType.DMA((X_SLOTS,)),
                        pltpu.SemaphoreType.DMA((2,)),
                        pltpu.SemaphoreType.DMA((2,))],
    )
    return pl.pallas_call(
        functools.partial(_moe_kernel, n_blocks),
        grid_spec=grid_spec,
        out_shape=jax.ShapeDtypeStruct(xs.shape, F32),
        compiler_params=pltpu.CompilerParams(dimension_semantics=("arbitrary",),
                                             vmem_limit_bytes=VMEM_LIMIT),
        name="moe_ffn",
    )(blk0, nblk, xs, w_gate_up, b_gate_up[:, None, :], w_down, b_down[:, None, :])


def _combine_kernel(destc_ref, destn_ref, x1_ref, mod_ref, route_ref, fg_ref, ys_hbm, o_ref, buf0, buf1, sem):
    buf = (buf0, buf1)
    i = pl.program_id(0)
    n_steps = pl.num_programs(0)
    d = x1_ref.shape[-1]

    def start_gather(dest_ref, s, t0=0, t1=TD):
        for t in range(t0, t1):
            for k in range(TOP_K):
                pltpu.make_async_copy(ys_hbm.at[_tile8(dest_ref[0, k, t])], buf[s].at[k, _tile(t)],
                                      sem.at[s]).start(priority=k % N_DMA_QUEUES)

    def wait_gather(s):
        for k in range(TOP_K):
            pltpu.make_async_copy(ys_hbm.at[pl.ds(0, TD * SUBLANES)], buf[s].at[k], sem.at[s]).wait()

    @pl.when(i == 0)
    def _():
        start_gather(destc_ref, 0)

    def step(s):
        wait_gather(s)
        gates = route_ref[...]
        gk = [gates[:, TOP_K + k:TOP_K + k + 1] for k in range(TOP_K)]
        gate2 = mod_ref[5:6, :]

        def residual(j):
            ls = slice(j * LANES, (j + 1) * LANES)
            rj = _chunk(j, TD)
            y = (gk[0] * buf[s][0, rj, :] + gk[1] * buf[s][1, rj, :]) + \
                (gk[2] * buf[s][2, rj, :] + gk[3] * buf[s][3, rj, :])
            return x1_ref[:, ls] + gate2[:, ls] * y

        ss = jnp.zeros((TD, 1), F32)
        per = TD // SUBLANES
        for j in range(SUBLANES):
            start_gather(destn_ref, 1 - s, j * per, (j + 1) * per)
            xj = residual(j)
            ss = ss + jnp.sum(xj * xj, axis=-1, keepdims=True)
        inv = lax.rsqrt(ss / d + EPS)
        for j in range(SUBLANES):
            ls = slice(j * LANES, (j + 1) * LANES)
            o_ref[:, ls] = residual(j) * inv * fg_ref[:, ls]

        @pl.when(i == n_steps - 1)
        def _():
            wait_gather(1 - s)

    _by_parity(i, step)


def _combine(x1, mod8, route, dest, ys, final_g):
    bsz, seq, d = x1.shape
    n_tok = bsz * seq
    n_steps = n_tok // TD
    per_seq = seq // TD
    return pl.pallas_call(
        _combine_kernel,
        grid=(n_steps,),
        in_specs=[_dest_block(lambda i: i),
                  _dest_block(lambda i: jnp.minimum(i + 1, n_steps - 1)),
                  pl.BlockSpec((TD, d), lambda i: (i, 0)),
                  pl.BlockSpec((None, 8, d), lambda i: (i // per_seq, 0, 0)),
                  pl.BlockSpec((TD, ROUTE_ROWS), lambda i: (i, 0)),
                  pl.BlockSpec((1, d), lambda i: (0, 0)),
                  pl.BlockSpec(memory_space=pl.ANY)],
        out_specs=pl.BlockSpec((TD, d), lambda i: (i, 0)),
        out_shape=jax.ShapeDtypeStruct((n_tok, d), F32),
        scratch_shapes=[pltpu.VMEM((TOP_K, TD * SUBLANES, LANES), F32),
                        pltpu.VMEM((TOP_K, TD * SUBLANES, LANES), F32),
                        pltpu.SemaphoreType.DMA((2,))],
        compiler_params=pltpu.CompilerParams(dimension_semantics=("arbitrary",),
                                             vmem_limit_bytes=VMEM_LIMIT),
        name="combine",
    )(dest, dest, x1.reshape(n_tok, d), mod8, route, final_g.reshape(1, d), ys).reshape(bsz, seq, d)


def _dest_kernel(route_ref, start_ref, dest_ref):
    erow = lax.broadcasted_iota(jnp.int32, (N_EXPERTS, TM), 0).astype(F32)
    for i in range(route_ref.shape[0]):
        r = route_ref[i]
        rows = []
        for k in range(TOP_K):
            base = jnp.sum(jnp.where(erow == r[k:k + 1, :], start_ref[...], 0.0), axis=0, keepdims=True)
            rows.append((base + r[2 * TOP_K + k:2 * TOP_K + k + 1, :]) * float(SUBLANES))
        dest_ref[i] = jnp.concatenate(rows, axis=0).astype(jnp.int32)


def _destinations(route_t, start_pad):
    n_tiles = route_t.shape[0]
    per_step = 8 if n_tiles % 8 == 0 else 1
    return pl.pallas_call(
        _dest_kernel,
        grid=(n_tiles // per_step,),
        in_specs=[pl.BlockSpec((per_step, ROUTE_ROWS, TM), lambda i: (i, 0, 0)),
                  pl.BlockSpec((N_EXPERTS, 1), lambda i: (0, 0))],
        out_specs=pl.BlockSpec((per_step, TOP_K, TM), lambda i: (i, 0, 0)),
        out_shape=jax.ShapeDtypeStruct((n_tiles, TOP_K, TM), jnp.int32),
        name="destinations",
    )(route_t, start_pad.astype(F32).reshape(N_EXPERTS, 1))


def _routing(counts_f, n_tok):
    nk = n_tok * TOP_K
    counts = counts_f[:, 0].astype(jnp.int32)
    padded = ((counts + MOE_BLK - 1) // MOE_BLK) * MOE_BLK
    pad_end = jnp.cumsum(padded)
    start_pad = pad_end - padded
    p_rows = ((nk + N_EXPERTS * (MOE_BLK - 1) + MOE_BLK - 1) // MOE_BLK) * MOE_BLK
    blocks_used = (pad_end[-1:] // MOE_BLK).astype(jnp.int32)
    return (start_pad, start_pad + counts, padded - counts, blocks_used,
            (start_pad // MOE_BLK).astype(jnp.int32), (padded // MOE_BLK).astype(jnp.int32), p_rows)


def kernel(x, c, ada_w, ada_b, norm_mix_g, w_in, lb_params, hgrn_norm_g, gmlp_ln_g, gmlp_ln_b, gmlp_ws, gmlp_bs,
           gmlp_norm_g, w_out, norm_ffn_g, router_w, router_b, w_gate_up, b_gate_up, w_down, b_down, final_g):
    assert ada_w.shape[0] == 1, "single-layer block"
    bsz, seq, d = x.shape
    assert d == SUBLANES * LANES and seq % TM == 0 and (bsz * seq) % TD == 0
    n_tok = bsz * seq
    mod = _modulation(c, ada_w[0], ada_b[0])
    mod8 = jnp.zeros((bsz, 8, d), F32).at[:, :6].set(mod.reshape(bsz, 6, d))
    x1, h2, route, counts = _mixer(x, mod8, norm_mix_g[0], w_in[0], lb_params, hgrn_norm_g[0], gmlp_ln_g[0],
                                   gmlp_ln_b[0], gmlp_ws[0], gmlp_bs[0], gmlp_norm_g[0], w_out[0],
                                   norm_ffn_g[0], router_w[0], router_b[0])
    start_pad, pad_first, pad_count, blocks_used, blk0, nblk, p_rows = _routing(counts, n_tok)
    dest = _destinations(route, start_pad)
    route = route.transpose(0, 2, 1).reshape(n_tok, ROUTE_ROWS)
    xs = _dispatch(h2.reshape(n_tok, d), dest, pad_first, pad_count, blocks_used, p_rows)
    ys = _moe(xs, blk0, nblk, w_gate_up[0], b_gate_up[0], w_down[0], b_down[0])
    return _combine(x1, mod8, route, dest, ys, final_g)
```
